```python
import math
import jax, jax.numpy as jnp
from jax import lax
import numpy as np

D_MODEL = 1024
BATCH = 2
SEQ = 8192
DEPTH = 4
DEC_BATCH = 128
DEC_SEQ = 1
PAST_LEN = 8192
PAGE_SIZE = 128

F32 = jnp.float32
A_HEADS = 4
A_QK = 64
A_V = 128
A_WIDTH = A_HEADS * A_V
B_HEADS = 8
B_P = 64
B_WIDTH = B_HEADS * B_P
B_GROUPS = 2
B_STATE = 128
CONV_W = 4
B_CONV_DIM = B_WIDTH + 2 * B_GROUPS * B_STATE
C_HEADS = 8
C_KV = 2
C_HD = 64
C_WIDTH = C_HEADS * C_HD
WINDOW = 128
ROPE_THETA = 10000.0
D_MIX = A_WIDTH + B_WIDTH + C_WIDTH
CHUNK = 64
NORM_EPS = 1e-6
IN_SIZES = (A_HEADS * A_QK, A_HEADS * A_QK, A_WIDTH, A_WIDTH, A_WIDTH, A_HEADS, A_HEADS,
            B_WIDTH, B_CONV_DIM, B_HEADS,
            C_WIDTH, C_KV * C_HD, C_KV * C_HD, C_WIDTH)
D_IN = 2 * A_HEADS * A_QK + 3 * A_WIDTH + 2 * A_HEADS + B_WIDTH + B_CONV_DIM + B_HEADS + 2 * C_WIDTH + 2 * C_KV * C_HD

kernel_name = 'hymba_mlstm_ssd_swa_step'


def _rmsnorm(x, w):
    xf = x.astype(F32)
    y = xf * lax.rsqrt(jnp.mean(xf * xf, axis=-1, keepdims=True) + NORM_EPS)
    return (y * w.astype(F32)).astype(x.dtype)


def _split(x, sizes):
    return jnp.split(x, np.cumsum(sizes)[:-1].tolist(), axis=-1)


def _rope(x, pos):
    half = x.shape[-1] // 2
    inv = ROPE_THETA ** (-jnp.arange(half, dtype=F32) / half)
    ang = pos.astype(F32)[:, None] * inv[None, :]
    cos = jnp.cos(ang)[None, :, None, :]
    sin = jnp.sin(ang)[None, :, None, :]
    xf = x.astype(F32)
    x1, x2 = xf[..., :half], xf[..., half:]
    return jnp.concatenate([x1 * cos - x2 * sin, x2 * cos + x1 * sin], axis=-1).astype(x.dtype)


def _sink_softmax(s, sink):
    sink = sink.astype(F32)
    m = jnp.maximum(jnp.max(s, axis=-1, keepdims=True), sink)
    e = jnp.exp(s - m)
    return e / (jnp.sum(e, axis=-1, keepdims=True) + jnp.exp(sink - m))


def _mlstm(q, k, v, ig, fg, C0, n0, m0):
    N, T, H, DK = q.shape
    DV = v.shape[-1]
    L = math.gcd(T, CHUNK)
    NC = T // L
    q = q.astype(F32).reshape(N, NC, L, H, DK)
    k = (k.astype(F32) * (DK ** -0.5)).reshape(N, NC, L, H, DK)
    v = v.astype(F32).reshape(N, NC, L, H, DV)
    ig = ig.reshape(N, NC, L, H)
    b = jnp.cumsum(jax.nn.log_sigmoid(fg).reshape(N, NC, L, H), axis=2)
    bL = b[:, :, -1]
    g = bL[:, :, None] - b + ig
    m_loc = jnp.max(g, axis=2)
    w = jnp.exp(g - m_loc[:, :, None])
    C_loc = jnp.einsum('nclh,nclhv,nclhk->nchvk', w, v, k)
    n_loc = jnp.einsum('nclh,nclhk->nchk', w, k)

    def step(carry, inp):
        C, n, m = carry
        Cl, nl, ml, bl = inp
        m_new = jnp.maximum(bl + m, ml)
        sp = jnp.exp(bl + m - m_new)
        sl = jnp.exp(ml - m_new)
        C_new = sp[..., None, None] * C + sl[..., None, None] * Cl
        n_new = sp[..., None] * n + sl[..., None] * nl
        return (C_new, n_new, m_new), (C, n, m)

    xs = (jnp.moveaxis(C_loc, 1, 0), jnp.moveaxis(n_loc, 1, 0), jnp.moveaxis(m_loc, 1, 0), jnp.moveaxis(bL, 1, 0))
    (Cf, nf, mf), (Cs, ns, ms) = lax.scan(step, (C0.astype(F32), n0.astype(F32), m0.astype(F32)), xs)
    Cs = jnp.moveaxis(Cs, 0, 1)
    ns = jnp.moveaxis(ns, 0, 1)
    ms = jnp.moveaxis(ms, 0, 1)
    causal = (jnp.arange(L)[:, None] >= jnp.arange(L)[None, :])[:, :, None]
    dmat = jnp.where(causal, b[:, :, :, None] - b[:, :, None] + ig[:, :, None], -jnp.inf)
    inter = b + ms[:, :, None]
    m_t = jnp.maximum(inter, jnp.max(dmat, axis=3))
    s = jnp.exp(dmat - m_t[:, :, :, None]) * jnp.einsum('ncthk,ncshk->nctsh', q, k)
    si = jnp.exp(inter - m_t)
    num = jnp.einsum('nctsh,ncshv->ncthv', s, v) + si[..., None] * jnp.einsum('nchvk,ncthk->ncthv', Cs, q)
    den = jnp.sum(s, axis=3) + si * jnp.einsum('nchk,ncthk->ncth', ns, q)
    h = num / jnp.maximum(jnp.abs(den), jnp.exp(-m_t))[..., None]
    return h.reshape(N, T, H, DV), Cf, nf, mf


def _causal_conv(u, buf, w, b):
    T = u.shape[1]
    full = jnp.concatenate([buf.astype(u.dtype), u], axis=1)
    acc = b
    for j in range(CONV_W):
        acc = acc + full[:, j:j + T] * w[j]
    return jax.nn.silu(acc), full[:, T:]


def _ssd(x, dt, A, Bm, Cm, h0):
    N, T, H, P = x.shape
    G, S = Bm.shape[2], Bm.shape[3]
    E = H // G
    L = math.gcd(T, CHUNK)
    NC = T // L
    x = x.astype(F32).reshape(N, NC, L, G, E, P)
    dt = dt.reshape(N, NC, L, G, E)
    Bm = Bm.astype(F32).reshape(N, NC, L, G, S)
    Cm = Cm.astype(F32).reshape(N, NC, L, G, S)
    a = jnp.cumsum(dt * A.reshape(G, E), axis=2)
    aL = a[:, :, -1]
    h_loc = jnp.einsum('nclge,nclgep,nclgs->ncgeps', jnp.exp(aL[:, :, None] - a) * dt, x, Bm)

    def step(h, inp):
        hl, al = inp
        return jnp.exp(al)[..., None, None] * h + hl, h

    hf, hs = lax.scan(step, h0.astype(F32).reshape(N, G, E, P, S),
                      (jnp.moveaxis(h_loc, 1, 0), jnp.moveaxis(aL, 1, 0)))
    hs = jnp.moveaxis(hs, 0, 1)
    causal = (jnp.arange(L)[:, None] >= jnp.arange(L)[None, :])[:, :, None, None]
    decay = jnp.exp(jnp.where(causal, a[:, :, :, None] - a[:, :, None], -jnp.inf))
    cb = jnp.einsum('nctgs,ncugs->nctug', Cm, Bm)
    wmat = decay * cb[..., None] * dt[:, :, None]
    y = jnp.einsum('nctuge,ncugep->nctgep', wmat, x) + \
        jnp.einsum('nctgs,ncgeps->nctgep', Cm, hs) * jnp.exp(a)[..., None]
    return y.reshape(N, T, H, P), hf.reshape(N, H, P, S)


def _swa_prompt(q, k, v, sinks):
    N, T, HQ, Dh = q.shape
    HK = k.shape[2]
    G = HQ // HK
    Bk = WINDOW
    NB = T // Bk
    q = q.reshape(N, NB, Bk, HK, G, Dh)
    k = k.reshape(N, NB, Bk, HK, Dh)
    v = v.reshape(N, NB, Bk, HK, Dh)
    pad = ((0, 0), (1, 0), (0, 0), (0, 0), (0, 0))
    kk = jnp.concatenate([jnp.pad(k, pad)[:, :-1], k], axis=2)
    vv = jnp.concatenate([jnp.pad(v, pad)[:, :-1], v], axis=2)
    s = jnp.einsum('nbqhgd,nbkhd->nbhgqk', q, kk).astype(F32) * (Dh ** -0.5)
    qpos = jnp.arange(NB)[:, None, None] * Bk + jnp.arange(Bk)[None, :, None]
    kpos = jnp.arange(NB)[:, None, None] * Bk - Bk + jnp.arange(2 * Bk)[None, None, :]
    delta = qpos - kpos
    mask = (delta >= 0) & (delta < WINDOW) & (kpos >= 0)
    s = jnp.where(mask[None, :, None, None], s, -jnp.inf)
    p = _sink_softmax(s, sinks.reshape(HK, G)[:, :, None, None])
    o = jnp.einsum('nbhgqk,nbkhd->nbqhgd', p.astype(vv.dtype), vv)
    return o.reshape(N, T, HQ, Dh)


def _swa_decode(q, k, v, kc, vc, sinks):
    N, T, HQ, Dh = q.shape
    HK = k.shape[2]
    G = HQ // HK
    Wc = kc.shape[1]
    kk = jnp.concatenate([kc.astype(k.dtype), k], axis=1)
    vv = jnp.concatenate([vc.astype(v.dtype), v], axis=1)
    s = jnp.einsum('nqhgd,nkhd->nhgqk', q.reshape(N, T, HK, G, Dh), kk).astype(F32) * (Dh ** -0.5)
    delta = (Wc + jnp.arange(T))[:, None] - jnp.arange(Wc + T)[None, :]
    mask = (delta >= 0) & (delta < WINDOW)
    s = jnp.where(mask, s, -jnp.inf)
    p = _sink_softmax(s, sinks.reshape(HK, G)[:, :, None, None])
    o = jnp.einsum('nhgqk,nkhd->nqhgd', p.astype(vv.dtype), vv).reshape(N, T, HQ, Dh)
    return o, kk[:, -Wc:], vv[:, -Wc:]


def _layer(x, pos, conv_buf, ssm_h, C0, n0, m0, kc, vc,
           norm_w, w_in, a_ib, a_fb, a_nw, conv_w, conv_b, dt_bias, A_log, D_skip, b_nw,
           qn_w, kn_w, sinks, w_out):
    N, T, _ = x.shape
    u = _rmsnorm(x, norm_w) @ w_in
    (aq, ak, av, ao, az, ai, af, bz, bxbc, bdt, cq, ck, cv, cz) = _split(u, IN_SIZES)
    ha, C1, n1, m1 = _mlstm(aq.reshape(N, T, A_HEADS, A_QK), ak.reshape(N, T, A_HEADS, A_QK),
                            av.reshape(N, T, A_HEADS, A_V),
                            ai.astype(F32) + a_ib.astype(F32), af.astype(F32) + a_fb.astype(F32), C0, n0, m0)
    ha = _rmsnorm(ha, a_nw.reshape(A_HEADS, A_V)).reshape(N, T, A_WIDTH)
    ya = (ha * jax.nn.sigmoid(ao.astype(F32)) * jax.nn.silu(az.astype(F32))).astype(x.dtype)
    xbc, conv1 = _causal_conv(bxbc, conv_buf, conv_w, conv_b)
    bx, bB, bC = _split(xbc, (B_WIDTH, B_GROUPS * B_STATE, B_GROUPS * B_STATE))
    dt = jax.nn.softplus(bdt.astype(F32) + dt_bias.astype(F32))
    A = -jnp.exp(A_log.astype(F32))
    bx = bx.reshape(N, T, B_HEADS, B_P)
    yb, h1 = _ssd(bx, dt, A, bB.reshape(N, T, B_GROUPS, B_STATE), bC.reshape(N, T, B_GROUPS, B_STATE), ssm_h)
    yb = yb + D_skip.astype(F32)[:, None] * bx.astype(F32)
    gb = (yb.reshape(N, T, B_WIDTH) * jax.nn.silu(bz.astype(F32))).reshape(N, T, B_GROUPS, B_WIDTH // B_GROUPS)
    yb = _rmsnorm(gb, b_nw.reshape(B_GROUPS, B_WIDTH // B_GROUPS)).reshape(N, T, B_WIDTH).astype(x.dtype)
    q = _rope(_rmsnorm(cq.reshape(N, T, C_HEADS, C_HD), qn_w), pos)
    k = _rope(_rmsnorm(ck.reshape(N, T, C_KV, C_HD), kn_w), pos)
    v = cv.reshape(N, T, C_KV, C_HD)
    if kc is None:
        o = _swa_prompt(q, k, v, sinks)
        k1, v1 = k[:, -WINDOW:], v[:, -WINDOW:]
    else:
        o, k1, v1 = _swa_decode(q, k, v, kc, vc, sinks)
    yc = (o.reshape(N, T, C_WIDTH).astype(F32) * jax.nn.silu(cz.astype(F32))).astype(x.dtype)
    y = jnp.concatenate([ya, yb, yc], axis=-1) @ w_out
    return x + y.astype(x.dtype), (C1, n1, m1, h1, conv1, k1, v1)


def setup_inputs(seed: int = 0) -> dict:
    key = jax.random.key(seed)
    ks = jax.random.split(key, 26)

    def nrm(k, shape, scale):
        return scale * jax.random.normal(k, shape, F32)

    win = min(WINDOW, PAST_LEN)
    dt0 = jnp.exp(jax.random.uniform(ks[16], (DEPTH, B_HEADS), F32, math.log(1e-3), math.log(1e-1)))
    return {
        'x_prompt': nrm(ks[0], (BATCH, SEQ, D_MODEL), 1.0),
        'x_sample': nrm(ks[1], (DEC_BATCH, DEC_SEQ, D_MODEL), 1.0),
        'state_mlstm_C': nrm(ks[2], (DEPTH, DEC_BATCH, A_HEADS, A_V, A_QK), 0.3),
        'state_mlstm_n': nrm(ks[3], (DEPTH, DEC_BATCH, A_HEADS, A_QK), 0.3),
        'state_mlstm_m': nrm(ks[4], (DEPTH, DEC_BATCH, A_HEADS), 1.0),
        'state_ssm': nrm(ks[5], (DEPTH, DEC_BATCH, B_HEADS, B_P, B_STATE), 0.3),
        'state_conv': nrm(ks[6], (DEPTH, DEC_BATCH, CONV_W - 1, B_CONV_DIM), 1.0),
        'cache_k': nrm(ks[7], (DEPTH, DEC_BATCH, win, C_KV, C_HD), 1.0),
        'cache_v': nrm(ks[8], (DEPTH, DEC_BATCH, win, C_KV, C_HD), 1.0),
        'norm_w': 1.0 + nrm(ks[9], (DEPTH, D_MODEL), 0.02),
        'w_in': nrm(ks[10], (DEPTH, D_MODEL, D_IN), D_MODEL ** -0.5),
        'a_igate_b': nrm(ks[11], (DEPTH, A_HEADS), 0.1) - 1.0,
        'a_fgate_b': jnp.linspace(3.0, 6.0, A_HEADS, dtype=F32)[None, :] + nrm(ks[12], (DEPTH, A_HEADS), 0.1),
        'a_norm_w': 1.0 + nrm(ks[13], (DEPTH, A_WIDTH), 0.02),
        'b_conv_w': nrm(ks[14], (DEPTH, CONV_W, B_CONV_DIM), CONV_W ** -0.5),
        'b_conv_b': nrm(ks[15], (DEPTH, B_CONV_DIM), 0.02),
        'b_dt_bias': dt0 + jnp.log(-jnp.expm1(-dt0)),
        'b_A_log': jnp.log(jax.random.uniform(ks[17], (DEPTH, B_HEADS), F32, 1.0, 16.0)),
        'b_D': 1.0 + nrm(ks[18], (DEPTH, B_HEADS), 0.1),
        'b_norm_w': 1.0 + nrm(ks[19], (DEPTH, B_WIDTH), 0.02),
        'c_qnorm_w': 1.0 + nrm(ks[20], (DEPTH, C_HD), 0.02),
        'c_knorm_w': 1.0 + nrm(ks[21], (DEPTH, C_HD), 0.02),
        'c_sinks': nrm(ks[22], (DEPTH, C_HEADS), 0.5),
        'w_out': nrm(ks[23], (DEPTH, D_MIX, D_MODEL), 0.5 * D_MIX ** -0.5),
    }


def reference(x_prompt, x_sample, state_mlstm_C, state_mlstm_n, state_mlstm_m, state_ssm, state_conv,
              cache_k, cache_v, norm_w, w_in, a_igate_b, a_fgate_b, a_norm_w, b_conv_w, b_conv_b,
              b_dt_bias, b_A_log, b_D, b_norm_w, c_qnorm_w, c_knorm_w, c_sinks, w_out):
    Bp, Tp, _ = x_prompt.shape
    Ts = x_sample.shape[1]
    pos_p = jnp.arange(Tp, dtype=jnp.int32)
    pos_s = PAST_LEN + jnp.arange(Ts, dtype=jnp.int32)
    zC = jnp.zeros((Bp, A_HEADS, A_V, A_QK), F32)
    zn = jnp.zeros((Bp, A_HEADS, A_QK), F32)
    zm = jnp.zeros((Bp, A_HEADS), F32)
    zh = jnp.zeros((Bp, B_HEADS, B_P, B_STATE), F32)
    zconv = jnp.zeros((Bp, CONV_W - 1, B_CONV_DIM), x_prompt.dtype)
    hp, hs = x_prompt, x_sample
    st_prompt, st_sample = [], []
    for l in range(DEPTH):
        lw = (norm_w[l], w_in[l], a_igate_b[l], a_fgate_b[l], a_norm_w[l], b_conv_w[l], b_conv_b[l],
              b_dt_bias[l], b_A_log[l], b_D[l], b_norm_w[l], c_qnorm_w[l], c_knorm_w[l], c_sinks[l], w_out[l])
        hp, sp = _layer(hp, pos_p, zconv, zh, zC, zn, zm, None, None, *lw)
        hs, ss = _layer(hs, pos_s, state_conv[l], state_ssm[l], state_mlstm_C[l], state_mlstm_n[l],
                        state_mlstm_m[l], cache_k[l], cache_v[l], *lw)
        st_prompt.append(sp)
        st_sample.append(ss)
    p_C, p_n, p_m, p_h, p_conv, p_k, p_v = [jnp.stack(t) for t in zip(*st_prompt)]
    s_C, s_n, s_m, s_h, s_conv, s_k, s_v = [jnp.stack(t) for t in zip(*st_sample)]
    return (hp, hs, p_C, p_n, p_m, p_h, p_conv, p_k, p_v, s_C, s_n, s_m, s_h, s_conv, s_k, s_v)
```

```python
import functools
import math

import jax
import jax.numpy as jnp
import numpy as np
from jax import lax
from jax.experimental import pallas as pl
from jax.experimental.pallas import tpu as pltpu

F32 = jnp.float32
MXU_DTYPE = jnp.bfloat16

D_MODEL = 1024
A_HEADS, A_QK, A_V = 4, 64, 128
A_WIDTH = A_HEADS * A_V
B_HEADS, B_P, B_GROUPS, B_STATE = 8, 64, 2, 128
B_WIDTH = B_HEADS * B_P
CONV_W = 4
B_CONV_DIM = B_WIDTH + 2 * B_GROUPS * B_STATE
C_HEADS, C_KV, C_HD = 8, 2, 64
C_WIDTH = C_HEADS * C_HD
WINDOW = 128
ROPE_THETA = 10000.0
D_MIX = A_WIDTH + B_WIDTH + C_WIDTH
NORM_EPS = 1e-6
PAST_LEN = 8192

LANE = 128
SUBLANE = 8
HALF = LANE // 2

CHUNK = 128
N_GATES = 2 * A_HEADS + B_HEADS
TAIL0 = SUBLANE - (CONV_W - 1)

OFF_AQ = 0
OFF_AK = OFF_AQ + A_HEADS * A_QK
OFF_AV = OFF_AK + A_HEADS * A_QK
OFF_AO = OFF_AV + A_WIDTH
OFF_AZ = OFF_AO + A_WIDTH
OFF_BZ = OFF_AZ + A_WIDTH
OFF_BXBC = OFF_BZ + B_WIDTH
OFF_CQ = OFF_BXBC + B_CONV_DIM
OFF_CK = OFF_CQ + C_WIDTH
OFF_CV = OFF_CK + C_KV * C_HD
OFF_CZ = OFF_CV + C_KV * C_HD
OFF_G = OFF_CZ + C_WIDTH
D_INP = OFF_G + LANE
_SRC_SIZES = (A_HEADS * A_QK, A_HEADS * A_QK, A_WIDTH, A_WIDTH, A_WIDTH, A_HEADS, A_HEADS,
              B_WIDTH, B_CONV_DIM, B_HEADS, C_WIDTH, C_KV * C_HD, C_KV * C_HD, C_WIDTH)
_SRC_OFF = np.concatenate([[0], np.cumsum(_SRC_SIZES)]).tolist()

VMEM_LIMIT_BYTES = 56 * 1024 * 1024


def _dot(a, b):
    return jnp.dot(a, b, preferred_element_type=F32)


def _dot_nt(a, b):
    return lax.dot_general(a, b, (((1,), (1,)), ((), ())), preferred_element_type=F32)


def _dot_tn(a, b):
    return lax.dot_general(a, b, (((0,), (0,)), ((), ())), preferred_element_type=F32)


def _split3(z):
    hi = z.astype(MXU_DTYPE)
    r1 = z - hi.astype(F32)
    mid = r1.astype(MXU_DTYPE)
    lo = (r1 - mid.astype(F32)).astype(MXU_DTYPE)
    return hi, mid, lo


def _softplus_terms(x):
    t = jnp.log1p(jnp.exp(-jnp.abs(x)))
    return jnp.maximum(x, 0.0) + t, jnp.minimum(x, 0.0) - t


def _silu(x):
    return x * jax.nn.sigmoid(x)


def _seg64_sum(s, lane):
    for k in (1, 2, 4, 8, 16, 32):
        up = pltpu.roll(s, LANE - k, axis=1)
        dn = pltpu.roll(s, k, axis=1)
        s = s + jnp.where((lane & k) == 0, up, dn)
    return s


def _norm_rope(x, w, cos, sin_signed, lane):
    ms = _seg64_sum(x * x, lane) * (1.0 / C_HD)
    xn = x * lax.rsqrt(ms + NORM_EPS) * w
    partner = jnp.where((lane & (C_HD // 2)) == 0,
                        pltpu.roll(xn, LANE - C_HD // 2, axis=1), pltpu.roll(xn, C_HD // 2, axis=1))
    return xn * cos + partner * sin_signed


def _prompt_layer_kernel(
        sinks_ref,
        x_ref, nw_ref, win_ref, wgt_ref, wout_ref, gbc_ref, gbr_ref, alc_ref, alr_ref,
        anw_ref, cw_ref, cb_ref, dsk_ref, bnw_ref, qnw_ref, knw_ref, cos_ref, sin_ref,
        o_ref, cst_ref, nst_ref, mst_ref, hst_ref, conv_ref, k_ref, v_ref,
        u_scr, gr_scr, y_scr, cv_scr, *, tb):
    L = CHUNK
    nch = tb // L
    t_id = pl.program_id(1)

    @pl.when(t_id == 0)
    def _():
        cst_ref[...] = jnp.zeros_like(cst_ref)
        nst_ref[...] = jnp.zeros_like(nst_ref)
        mst_ref[...] = jnp.zeros_like(mst_ref)
        hst_ref[...] = jnp.zeros_like(hst_ref)
        conv_ref[...] = jnp.zeros_like(conv_ref)
        k_ref[...] = jnp.zeros_like(k_ref)
        v_ref[...] = jnp.zeros_like(v_ref)

    x = x_ref[...]
    xn = (x * lax.rsqrt(jnp.mean(x * x, axis=-1, keepdims=True) + NORM_EPS) * nw_ref[...]).astype(MXU_DTYPE)
    u_scr[...] = _dot(xn, win_ref[...])
    gr = _dot_nt(wgt_ref[...], xn)
    for c in range(nch):
        gr_scr[c] = gr[:, c * L:(c + 1) * L]

    lane = lax.broadcasted_iota(jnp.int32, (L, LANE), 1)
    row = lax.broadcasted_iota(jnp.int32, (L, LANE), 0)
    lo = lane < HALF
    causal = row >= lane
    tri = jnp.where(causal, 1.0, 0.0).astype(MXU_DTYPE)
    grow_id = lax.broadcasted_iota(jnp.int32, (2 * SUBLANE, L), 0)
    lane2 = lax.broadcasted_iota(jnp.int32, (2 * L, LANE), 1)
    lo2 = lane2 < HALF
    srow = lax.broadcasted_iota(jnp.int32, (L, 2 * L), 0)
    scol = lax.broadcasted_iota(jnp.int32, (L, 2 * L), 1)
    neg_inf = -jnp.inf

    def chunk(c, carry):
        r = pl.ds(pl.multiple_of(c * L, L), L)

        pre_c = u_scr[r, OFF_G:OFF_G + LANE] + gbc_ref[...]
        sp_c, ls_c = _softplus_terms(pre_c)
        a_neg_c = -jnp.exp(alc_ref[...])
        z_c = jnp.where((lane >= A_HEADS) & (lane < 2 * A_HEADS), ls_c,
                        jnp.where((lane >= 2 * A_HEADS) & (lane < N_GATES), sp_c * a_neg_c, 0.0))
        zc_parts = _split3(z_c)
        cum_c = _dot(tri, zc_parts[0]) + _dot(tri, zc_parts[1]) + _dot(tri, zc_parts[2])

        pre_r = gr_scr[c] + gbr_ref[...]
        sp_r, ls_r = _softplus_terms(pre_r)
        a_neg_r = -jnp.exp(alr_ref[...])
        z_r = jnp.where((grow_id >= A_HEADS) & (grow_id < 2 * A_HEADS), ls_r,
                        jnp.where(grow_id >= 2 * A_HEADS, sp_r * a_neg_r, 0.0))
        zr_parts = _split3(z_r)
        cum_r = _dot_nt(zr_parts[0], tri) + _dot_nt(zr_parts[1], tri) + _dot_nt(zr_parts[2], tri)

        for h in range(A_HEADS):
            pb, e = h // 2, h % 2
            own = lo if e == 0 else jnp.logical_not(lo)
            q_p = u_scr[r, OFF_AQ + pb * LANE:OFF_AQ + (pb + 1) * LANE].astype(MXU_DTYPE)
            k_f = jnp.where(own, u_scr[r, OFF_AK + pb * LANE:OFF_AK + (pb + 1) * LANE] * (A_QK ** -0.5), 0.0)
            k_m = k_f.astype(MXU_DTYPE)
            v_h = u_scr[r, OFF_AV + h * LANE:OFF_AV + (h + 1) * LANE]
            b_col = cum_c[:, A_HEADS + h:A_HEADS + h + 1]
            i_col = pre_c[:, h:h + 1]
            b_row = cum_r[A_HEADS + h:A_HEADS + h + 1, :]
            i_row = pre_r[h:h + 1, :]
            b_end = b_row[:, L - 1:L]
            m_prev = mst_ref[h:h + 1, 0:1]
            c_prev = cst_ref[h]
            n_prev = nst_ref[h:h + 1, :]

            g_row = b_end - b_row + i_row
            m_loc = jnp.max(g_row, axis=1, keepdims=True)
            dmat = jnp.where(causal, b_col - b_row + i_row, neg_inf)
            inter = b_col + m_prev
            m_t = jnp.maximum(inter, jnp.max(dmat, axis=1, keepdims=True))
            s = jnp.exp(dmat - m_t) * _dot_nt(q_p, k_m)
            si = jnp.exp(inter - m_t)
            num = _dot(s.astype(MXU_DTYPE), v_h.astype(MXU_DTYPE)) + si * _dot_nt(q_p, c_prev.astype(MXU_DTYPE))
            qn = jnp.sum(q_p.astype(F32) * n_prev, axis=1, keepdims=True)
            den = jnp.sum(s, axis=1, keepdims=True) + si * qn
            hh = num / jnp.maximum(jnp.abs(den), jnp.exp(-m_t))
            hn = hh * lax.rsqrt(jnp.mean(hh * hh, axis=1, keepdims=True) + NORM_EPS) * anw_ref[:, h * LANE:(h + 1) * LANE]
            ao = u_scr[r, OFF_AO + h * LANE:OFF_AO + (h + 1) * LANE]
            az = u_scr[r, OFF_AZ + h * LANE:OFF_AZ + (h + 1) * LANE]
            y_scr[r, h * LANE:(h + 1) * LANE] = (hn * jax.nn.sigmoid(ao) * _silu(az)).astype(y_scr.dtype)

            m_new = jnp.maximum(b_end + m_prev, m_loc)
            sp = jnp.exp(b_end + m_prev - m_new)
            sl = jnp.exp(m_loc - m_new)
            w_col = jnp.exp(b_end - b_col + i_col - m_loc)
            c_loc = _dot_tn((w_col * v_h).astype(MXU_DTYPE), k_m)
            cst_ref[h] = sp * c_prev + sl * c_loc
            nst_ref[h:h + 1, :] = sp * n_prev + sl * jnp.sum(w_col * k_f, axis=0, keepdims=True)
            mst_ref[h:h + 1, :] = jnp.broadcast_to(m_new, (1, LANE))

        cv_scr[TAIL0:SUBLANE, :] = conv_ref[TAIL0:SUBLANE, :]
        u_bxbc = u_scr[r, OFF_BXBC:OFF_BXBC + B_CONV_DIM]
        cv_scr[SUBLANE:SUBLANE + L, :] = u_bxbc
        acc = cb_ref[...] + cv_scr[TAIL0:TAIL0 + L, :] * cw_ref[0:1, :]
        for j in range(1, CONV_W):
            acc = acc + cv_scr[TAIL0 + j:TAIL0 + j + L, :] * cw_ref[j:j + 1, :]
        conv_ref[TAIL0:SUBLANE, :] = u_bxbc[L - (CONV_W - 1):L, :]
        xbc = _silu(acc)

        for g in range(B_GROUPS):
            b_g = xbc[:, B_WIDTH + g * B_STATE:B_WIDTH + (g + 1) * B_STATE].astype(MXU_DTYPE)
            c_g = xbc[:, B_WIDTH + (B_GROUPS + g) * B_STATE:B_WIDTH + (B_GROUPS + g + 1) * B_STATE].astype(MXU_DTYPE)
            cb = _dot_nt(c_g, b_g)
            gated = []
            for pj in range(B_HEADS // B_GROUPS // 2):
                pb = g * (B_HEADS // B_GROUPS // 2) + pj
                x_p = xbc[:, pb * LANE:(pb + 1) * LANE]
                y_p = dsk_ref[:, pb * LANE:(pb + 1) * LANE] * x_p
                a_cols, a_ends, dtw = [], [], []
                for e in range(2):
                    hd = 2 * pb + e
                    gi = 2 * A_HEADS + hd
                    a_col = cum_c[:, gi:gi + 1]
                    a_row = cum_r[gi:gi + 1, :]
                    dt_row = sp_r[gi:gi + 1, :]
                    dt_col = sp_c[:, gi:gi + 1]
                    a_end = a_row[:, L - 1:L]
                    wmat = jnp.exp(jnp.where(causal, a_col - a_row, neg_inf)) * cb * dt_row
                    own = lo if e == 0 else jnp.logical_not(lo)
                    y_p = y_p + _dot(wmat.astype(MXU_DTYPE), jnp.where(own, x_p, 0.0).astype(MXU_DTYPE))
                    a_cols.append(a_col)
                    a_ends.append(a_end)
                    dtw.append(jnp.exp(a_end - a_col) * dt_col)
                h_prev = hst_ref[pb * LANE:(pb + 1) * LANE, :]
                y_p = y_p + _dot_nt(c_g, h_prev.astype(MXU_DTYPE)) * jnp.where(lo, jnp.exp(a_cols[0]), jnp.exp(a_cols[1]))
                xw = (x_p * jnp.where(lo, dtw[0], dtw[1])).astype(MXU_DTYPE)
                h_loc = _dot_tn(xw, b_g)
                hst_ref[pb * LANE:(pb + 1) * LANE, :] = \
                    jnp.where(row < B_P, jnp.exp(a_ends[0]), jnp.exp(a_ends[1])) * h_prev + h_loc
                bz = u_scr[r, OFF_BZ + pb * LANE:OFF_BZ + (pb + 1) * LANE]
                gated.append(y_p * _silu(bz))
            ms = sum(jnp.sum(gp * gp, axis=1, keepdims=True) for gp in gated) * (1.0 / (B_WIDTH // B_GROUPS))
            inv = lax.rsqrt(ms + NORM_EPS)
            for pj, gp in enumerate(gated):
                pb = g * (B_HEADS // B_GROUPS // 2) + pj
                y_scr[r, A_WIDTH + pb * LANE:A_WIDTH + (pb + 1) * LANE] = \
                    (gp * inv * bnw_ref[:, pb * LANE:(pb + 1) * LANE]).astype(y_scr.dtype)

        cos = cos_ref[r, :]
        sin = sin_ref[r, :]
        k_new = _norm_rope(u_scr[r, OFF_CK:OFF_CK + LANE], knw_ref[...], cos, sin, lane)
        v_new = u_scr[r, OFF_CV:OFF_CV + LANE]
        kk = jnp.concatenate([k_ref[...], k_new], axis=0)
        vv = jnp.concatenate([v_ref[...], v_new], axis=0)
        kk_sw = pltpu.roll(kk, HALF, axis=1)
        vv_sw = pltpu.roll(vv, HALF, axis=1)

        def variants(a, a_sw):
            return [[jnp.where(lo2, a, 0.0).astype(MXU_DTYPE), jnp.where(lo2, 0.0, a_sw).astype(MXU_DTYPE)],
                    [jnp.where(lo2, a_sw, 0.0).astype(MXU_DTYPE), jnp.where(lo2, 0.0, a).astype(MXU_DTYPE)]]

        k_var = variants(kk, kk_sw)
        v_var = variants(vv, vv_sw)
        first = jnp.logical_and(t_id == 0, c == 0)
        shift = jnp.where(first, 2 * L, 0)
        valid = ((scol < L) & (scol > srow + shift)) | ((scol >= L) & (scol - L <= srow))
        for pb in range(C_HEADS // 2):
            q_p = _norm_rope(u_scr[r, OFF_CQ + pb * LANE:OFF_CQ + (pb + 1) * LANE], qnw_ref[...], cos, sin, lane)
            q_p = q_p.astype(MXU_DTYPE)
            o_p = jnp.zeros((L, LANE), F32)
            for e in range(2):
                hd = 2 * pb + e
                g = hd // (C_HEADS // C_KV)
                sc = jnp.where(valid, _dot_nt(q_p, k_var[g][e]) * (C_HD ** -0.5), neg_inf)
                sink = sinks_ref[hd]
                m = jnp.maximum(jnp.max(sc, axis=1, keepdims=True), sink)
                ex = jnp.exp(sc - m)
                p = ex / (jnp.sum(ex, axis=1, keepdims=True) + jnp.exp(sink - m))
                o_p = o_p + _dot(p.astype(MXU_DTYPE), v_var[g][e])
            cz = u_scr[r, OFF_CZ + pb * LANE:OFF_CZ + (pb + 1) * LANE]
            y_scr[r, A_WIDTH + B_WIDTH + pb * LANE:A_WIDTH + B_WIDTH + (pb + 1) * LANE] = \
                (o_p * _silu(cz)).astype(y_scr.dtype)
        k_ref[...] = k_new
        v_ref[...] = v_new
        return carry

    lax.fori_loop(0, nch, chunk, 0)

    o_ref[...] = x_ref[...] + _dot(y_scr[...], wout_ref[...])


def _const_spec(shape):
    nd = len(shape)
    return pl.BlockSpec(shape, lambda n, t, _nd=nd: (0,) * _nd)


def _prompt_layer(x, lw, cos_t, sin_t, tb):
    n_seq, t_len, _ = x.shape
    assert t_len % tb == 0 and tb % CHUNK == 0
    nch = tb // CHUNK
    grid = (n_seq, t_len // tb)

    def per_seq(shape):
        nd = len(shape)
        return pl.BlockSpec((None,) + shape, lambda n, t, _nd=nd: (n,) + (0,) * _nd)

    in_specs = [
        pl.BlockSpec(memory_space=pltpu.SMEM),
        pl.BlockSpec((None, tb, D_MODEL), lambda n, t: (n, t, 0)),
        _const_spec((1, D_MODEL)),
        _const_spec((D_MODEL, D_INP)),
        _const_spec((2 * SUBLANE, D_MODEL)),
        _const_spec((D_MIX, D_MODEL)),
        _const_spec((1, LANE)), _const_spec((2 * SUBLANE, LANE)),
        _const_spec((1, LANE)), _const_spec((2 * SUBLANE, LANE)),
        _const_spec((1, A_WIDTH)),
        _const_spec((CONV_W, B_CONV_DIM)), _const_spec((1, B_CONV_DIM)),
        _const_spec((1, B_WIDTH)), _const_spec((1, B_WIDTH)),
        _const_spec((1, LANE)), _const_spec((1, LANE)),
        pl.BlockSpec((tb, LANE), lambda n, t: (t, 0)),
        pl.BlockSpec((tb, LANE), lambda n, t: (t, 0)),
    ]
    out_shape = (
        jax.ShapeDtypeStruct((n_seq, t_len, D_MODEL), F32),
        jax.ShapeDtypeStruct((n_seq, A_HEADS, A_V, LANE), F32),
        jax.ShapeDtypeStruct((n_seq, SUBLANE, LANE), F32),
        jax.ShapeDtypeStruct((n_seq, SUBLANE, LANE), F32),
        jax.ShapeDtypeStruct((n_seq, B_HEADS * B_P, B_STATE), F32),
        jax.ShapeDtypeStruct((n_seq, SUBLANE, B_CONV_DIM), F32),
        jax.ShapeDtypeStruct((n_seq, WINDOW, LANE), F32),
        jax.ShapeDtypeStruct((n_seq, WINDOW, LANE), F32),
    )
    out_specs = (
        pl.BlockSpec((None, tb, D_MODEL), lambda n, t: (n, t, 0)),
        per_seq((A_HEADS, A_V, LANE)), per_seq((SUBLANE, LANE)), per_seq((SUBLANE, LANE)),
        per_seq((B_HEADS * B_P, B_STATE)), per_seq((SUBLANE, B_CONV_DIM)),
        per_seq((WINDOW, LANE)), per_seq((WINDOW, LANE)),
    )
    scratch = [
        pltpu.VMEM((tb, D_INP), F32),
        pltpu.VMEM((nch, 2 * SUBLANE, CHUNK), F32),
        pltpu.VMEM((tb, D_MIX), MXU_DTYPE),
        pltpu.VMEM((SUBLANE + CHUNK, B_CONV_DIM), F32),
    ]
    return pl.pallas_call(
        functools.partial(_prompt_layer_kernel, tb=tb),
        grid=grid, in_specs=in_specs, out_specs=out_specs, out_shape=out_shape, scratch_shapes=scratch,
        compiler_params=pltpu.CompilerParams(dimension_semantics=("arbitrary", "arbitrary"),
                                             vmem_limit_bytes=VMEM_LIMIT_BYTES),
        name="prompt_layer",
    )(lw["sinks"], x, lw["norm_w"], lw["w_in"], lw["wg_t"], lw["w_out"], lw["gb_col"], lw["gb_row"],
      lw["al_col"], lw["al_row"], lw["a_nw"], lw["conv_w"], lw["conv_b"], lw["d_lane"], lw["b_nw"],
      lw["qn_w"], lw["kn_w"], cos_t, sin_t)


def _src(w, i):
    return w[..., _SRC_OFF[i]:_SRC_OFF[i + 1]]


def _prep_weights(norm_w, w_in, a_igate_b, a_fgate_b, a_norm_w, b_conv_w, b_conv_b, b_dt_bias, b_A_log, b_D,
                  b_norm_w, c_qnorm_w, c_knorm_w, c_sinks, w_out):
    depth = w_in.shape[0]
    gates = jnp.concatenate([_src(w_in, 5), _src(w_in, 6), _src(w_in, 9)], axis=-1)
    pad = jnp.zeros(w_in.shape[:2] + (LANE - N_GATES,), w_in.dtype)
    w_perm = jnp.concatenate([w_in[..., :_SRC_OFF[5]], _src(w_in, 7), _src(w_in, 8),
                              w_in[..., _SRC_OFF[10]:], gates, pad], axis=-1).astype(MXU_DTYPE)
    gbias = jnp.concatenate([a_igate_b, a_fgate_b, b_dt_bias], axis=-1).astype(F32)
    alog = jnp.concatenate([jnp.zeros((depth, 2 * A_HEADS), F32), b_A_log.astype(F32)], axis=-1)

    def col_form(v):
        return jnp.pad(v, ((0, 0), (0, LANE - N_GATES)))[:, None, :]

    def row_form(v):
        return jnp.broadcast_to(v[:, :, None], (depth, N_GATES, LANE))

    return {
        "sinks": c_sinks.astype(F32),
        "sink_rows": jnp.broadcast_to(c_sinks.astype(F32)[:, :, None], (depth, C_HEADS, LANE)),
        "a_ib": a_igate_b.astype(F32), "a_fb": a_fgate_b.astype(F32),
        "dt_b": b_dt_bias.astype(F32), "a_log": b_A_log.astype(F32), "d_skip": b_D.astype(F32),
        "norm_w": norm_w.astype(F32)[:, None, :],
        "w_in": w_perm,
        "wg_t": jnp.swapaxes(gates, 1, 2).astype(MXU_DTYPE),
        "w_out": w_out.astype(MXU_DTYPE),
        "gb_col": col_form(gbias), "gb_row": row_form(gbias),
        "al_col": col_form(alog), "al_row": row_form(alog),
        "a_nw": a_norm_w.astype(F32)[:, None, :],
        "conv_w": b_conv_w.astype(F32), "conv_b": b_conv_b.astype(F32)[:, None, :],
        "d_lane": jnp.repeat(b_D.astype(F32), B_P, axis=-1)[:, None, :],
        "b_nw": b_norm_w.astype(F32)[:, None, :],
        "qn_w": jnp.tile(c_qnorm_w.astype(F32), (1, LANE // C_HD))[:, None, :],
        "kn_w": jnp.tile(c_knorm_w.astype(F32), (1, LANE // C_HD))[:, None, :],
    }


def _rope_tables(pos):
    half = C_HD // 2
    inv = ROPE_THETA ** (-jnp.arange(half, dtype=F32) / half)
    ang = pos.astype(F32)[:, None] * inv[None, :]
    cos, sin = jnp.cos(ang), jnp.sin(ang)
    reps = LANE // C_HD
    return jnp.tile(jnp.concatenate([cos, cos], axis=-1), (1, reps)), \
        jnp.tile(jnp.concatenate([-sin, sin], axis=-1), (1, reps))


def _unpad_prompt_states(cst, nst, mst, hst, conv, k1, v1):
    n_seq = cst.shape[0]
    c_even, c_odd = cst[:, 0::2, :, :HALF], cst[:, 1::2, :, HALF:]
    c_state = jnp.stack([c_even, c_odd], axis=2).reshape(n_seq, A_HEADS, A_V, A_QK)
    n_even, n_odd = nst[:, 0:A_HEADS:2, :HALF], nst[:, 1:A_HEADS:2, HALF:]
    n_state = jnp.stack([n_even, n_odd], axis=2).reshape(n_seq, A_HEADS, A_QK)
    return (c_state, n_state, mst[:, :A_HEADS, 0], hst.reshape(n_seq, B_HEADS, B_P, B_STATE),
            conv[:, TAIL0:, :], k1.reshape(n_seq, WINDOW, C_KV, C_HD), v1.reshape(n_seq, WINDOW, C_KV, C_HD))


def _layer_slice(lw, l):
    return {k: v[l] for k, v in lw.items()}


def _own_half_tiled(a, pick):
    return jnp.where(pick, a, pltpu.roll(a, HALF, axis=1))


def _lane_pick(a, lane, idx):
    return jnp.sum(jnp.where(lane == idx, a, 0.0), axis=1, keepdims=True)


def _dec_in_kernel(x_ref, nw_ref, win_ref, cbuf_ref, cw_ref, cb_ref, u_ref, xbc_ref, conv_out_ref):
    x = x_ref[...]
    xn = (x * lax.rsqrt(jnp.mean(x * x, axis=-1, keepdims=True) + NORM_EPS) * nw_ref[...]).astype(MXU_DTYPE)
    u_ref[...] = _dot(xn, win_ref[...])
    bx = u_ref[:, OFF_BXBC:OFF_BXBC + B_CONV_DIM]
    acc = cb_ref[...] + bx * cw_ref[CONV_W - 1:CONV_W, :]
    for j in range(CONV_W - 1):
        acc = acc + cbuf_ref[:, j * B_CONV_DIM:(j + 1) * B_CONV_DIM] * cw_ref[j:j + 1, :]
    xbc_ref[...] = _silu(acc)
    conv_out_ref[:, :(CONV_W - 2) * B_CONV_DIM] = cbuf_ref[:, B_CONV_DIM:]
    conv_out_ref[:, (CONV_W - 2) * B_CONV_DIM:] = bx


def _dec_in(x, cbuf, lw):
    nb = x.shape[0]
    return pl.pallas_call(
        _dec_in_kernel,
        out_shape=(jax.ShapeDtypeStruct((nb, D_INP), F32), jax.ShapeDtypeStruct((nb, B_CONV_DIM), F32),
                   jax.ShapeDtypeStruct((nb, (CONV_W - 1) * B_CONV_DIM), F32)),
        compiler_params=pltpu.CompilerParams(vmem_limit_bytes=VMEM_LIMIT_BYTES),
        name="dec_in",
    )(x, lw["norm_w"], lw["w_in"], cbuf, lw["conv_w"], lw["conv_b"])


def _dec_mlstm_kernel(ib_ref, fb_ref, q_ref, k_ref, v_ref, ao_ref, az_ref, g_ref, c_ref, n_ref, m_ref, anw_ref,
                      co_ref, no_ref, mo_ref, ya_ref):
    h = pl.program_id(0)
    e = h % 2
    nb = q_ref.shape[0]
    lane = lax.broadcasted_iota(jnp.int32, (nb, LANE), 1)
    lo = lane < HALF
    pick = (lane >= HALF).astype(jnp.int32) == e
    qt = _own_half_tiled(q_ref[...], pick)
    kt = _own_half_tiled(k_ref[...], pick) * (A_QK ** -0.5)
    nt = _own_half_tiled(n_ref[...], pick)
    gc = g_ref[...]
    ig = _lane_pick(gc, lane, h) + ib_ref[h]
    _, lf = _softplus_terms(_lane_pick(gc, lane, A_HEADS + h) + fb_ref[h])
    m0 = _lane_pick(m_ref[...], lane, h)
    m_new = jnp.maximum(lf + m0, ig)
    sp = jnp.exp(lf + m0 - m_new)
    sl = jnp.exp(ig - m_new)
    n_new = sp * nt + sl * kt
    v = v_ref[...]
    slk = sl * kt
    hout = jnp.zeros((nb, LANE), F32)
    for cb in range(A_V * A_QK // LANE):
        cols = slice(cb * LANE, (cb + 1) * LANE)
        vexp = jnp.where(lo, v[:, 2 * cb:2 * cb + 1], v[:, 2 * cb + 1:2 * cb + 2])
        new = sp * c_ref[:, cols] + vexp * slk
        co_ref[:, cols] = new
        prod = new * qt
        r0 = jnp.sum(jnp.where(lo, prod, 0.0), axis=1, keepdims=True)
        r1 = jnp.sum(jnp.where(lo, 0.0, prod), axis=1, keepdims=True)
        hout = jnp.where(lane == 2 * cb, r0, jnp.where(lane == 2 * cb + 1, r1, hout))
    den = jnp.sum(jnp.where(lo, n_new * qt, 0.0), axis=1, keepdims=True)
    hh = hout / jnp.maximum(jnp.abs(den), jnp.exp(-m_new))
    hn = hh * lax.rsqrt(jnp.mean(hh * hh, axis=1, keepdims=True) + NORM_EPS) * anw_ref[...]
    ya_ref[...] = hn * jax.nn.sigmoid(ao_ref[...]) * _silu(az_ref[...])
    mo_ref[...] = jnp.broadcast_to(m_new, (nb, LANE))

    @pl.when(e == 0)
    def _():
        no_ref[...] = n_new

    @pl.when(e == 1)
    def _():
        no_ref[...] = jnp.where(pick, n_new, no_ref[...])


def _dec_mlstm(u, c_flat, n_flat, m_pad, lw):
    nb = u.shape[0]
    blk = lambda off, f: pl.BlockSpec((nb, LANE), lambda h, _o=off // LANE, _f=f: (0, _o + _f(h)))
    per_head, per_pair, fixed = (lambda h: h), (lambda h: h // 2), (lambda h: 0)
    smem = pl.BlockSpec(memory_space=pltpu.SMEM)
    c_spec = pl.BlockSpec((nb, A_V * A_QK), lambda h: (0, h))
    return pl.pallas_call(
        _dec_mlstm_kernel,
        grid=(A_HEADS,),
        in_specs=[smem, smem, blk(OFF_AQ, per_pair), blk(OFF_AK, per_pair), blk(OFF_AV, per_head),
                  blk(OFF_AO, per_head), blk(OFF_AZ, per_head), blk(OFF_G, fixed), c_spec,
                  blk(0, per_pair), blk(0, fixed), pl.BlockSpec((1, LANE), lambda h: (0, h))],
        out_specs=(c_spec, blk(0, per_pair), pl.BlockSpec((None, nb, LANE), lambda h: (h, 0, 0)), blk(0, per_head)),
        out_shape=(jax.ShapeDtypeStruct(c_flat.shape, F32), jax.ShapeDtypeStruct(n_flat.shape, F32),
                   jax.ShapeDtypeStruct((A_HEADS, nb, LANE), F32), jax.ShapeDtypeStruct((nb, A_WIDTH), F32)),
        compiler_params=pltpu.CompilerParams(dimension_semantics=("arbitrary",), vmem_limit_bytes=VMEM_LIMIT_BYTES),
        name="dec_mlstm",
    )(lw["a_ib"], lw["a_fb"], u, u, u, u, u, u, c_flat, n_flat, m_pad, lw["a_nw"])


def _dec_ssd_kernel(dtb_ref, alog_ref, dsk_ref, x_ref, b_ref, c_ref, g_ref, s_ref, so_ref, y_ref):
    h = pl.program_id(0)
    e = h % 2
    nb = x_ref.shape[0]
    lane = lax.broadcasted_iota(jnp.int32, (nb, LANE), 1)
    pick = (lane >= HALF).astype(jnp.int32) == e
    lane_rel = lane - HALF * e
    xp = x_ref[...]
    xt = _own_half_tiled(xp, pick)
    dt, _ = _softplus_terms(_lane_pick(g_ref[...], lane, 2 * A_HEADS + h) + dtb_ref[h])
    d_a = jnp.exp(dt * (-jnp.exp(alog_ref[h])))
    dtb = dt * b_ref[...]
    cm = c_ref[...]
    y = jnp.zeros((nb, LANE), F32)
    for p in range(B_P):
        cols = slice(p * B_STATE, (p + 1) * B_STATE)
        new = d_a * s_ref[:, cols] + xt[:, p:p + 1] * dtb
        so_ref[:, cols] = new
        y = jnp.where(lane_rel == p, jnp.sum(new * cm, axis=1, keepdims=True), y)
    y = y + dsk_ref[h] * jnp.where(pick, xp, 0.0)

    @pl.when(e == 0)
    def _():
        y_ref[...] = y

    @pl.when(e == 1)
    def _():
        y_ref[...] = y_ref[...] + y


def _dec_ssd(xbc, u, s_flat, lw):
    nb = u.shape[0]
    smem = pl.BlockSpec(memory_space=pltpu.SMEM)
    hpg = B_HEADS // B_GROUPS
    tile = lambda f: pl.BlockSpec((nb, LANE), f)
    s_spec = pl.BlockSpec((nb, B_P * B_STATE), lambda h: (0, h))
    return pl.pallas_call(
        _dec_ssd_kernel,
        grid=(B_HEADS,),
        in_specs=[smem, smem, smem, tile(lambda h: (0, h // 2)),
                  tile(lambda h: (0, B_WIDTH // LANE + h // hpg)),
                  tile(lambda h: (0, B_WIDTH // LANE + B_GROUPS + h // hpg)),
                  tile(lambda h: (0, OFF_G // LANE)), s_spec],
        out_specs=(s_spec, tile(lambda h: (0, h // 2))),
        out_shape=(jax.ShapeDtypeStruct(s_flat.shape, F32), jax.ShapeDtypeStruct((nb, B_WIDTH), F32)),
        compiler_params=pltpu.CompilerParams(dimension_semantics=("arbitrary",), vmem_limit_bytes=VMEM_LIMIT_BYTES),
        name="dec_ssd",
    )(lw["dt_b"], lw["a_log"], lw["d_skip"], xbc, xbc, xbc, u, s_flat)


def _dec_out_kernel(x_ref, u_ref, ya_ref, ys_ref, ck_ref, cv_ref, cos_ref, sin_ref, qnw_ref, knw_ref,
                    bnw_ref, sink_ref, wout_ref, o_ref, ko_ref, vo_ref, y_scr, qm_scr, kn_scr, oh_scr):
    nb = x_ref.shape[0]
    lane = lax.broadcasted_iota(jnp.int32, (nb, LANE), 1)
    lo = lane < HALF
    cos, sin = cos_ref[...], sin_ref[...]
    kn_scr[...] = _norm_rope(u_ref[:, OFF_CK:OFF_CK + LANE], knw_ref[...], cos, sin, lane)
    n_pair = C_HEADS // 2
    ppg = n_pair // C_KV
    for pb in range(n_pair):
        q = _norm_rope(u_ref[:, OFF_CQ + pb * LANE:OFF_CQ + (pb + 1) * LANE], qnw_ref[...], cos, sin, lane)
        q_sw = pltpu.roll(q, HALF, axis=1)
        if pb // ppg == 0:
            even, odd = jnp.where(lo, q, 0.0), jnp.where(lo, q_sw, 0.0)
        else:
            even, odd = jnp.where(lo, 0.0, q_sw), jnp.where(lo, 0.0, q)
        qm_scr[pl.ds(2 * pb, nb, stride=C_HEADS), :] = even
        qm_scr[pl.ds(2 * pb + 1, nb, stride=C_HEADS), :] = odd

    lane_h = lax.broadcasted_iota(jnp.int32, (C_HEADS, WINDOW), 1)
    row_w = lax.broadcasted_iota(jnp.int32, (WINDOW, LANE), 0)
    sink = sink_ref[:, 0:1]

    def sample_group(gi, carry):
        base = pl.multiple_of(gi * SUBLANE, SUBLANE)
        k_rows = kn_scr[pl.ds(base, SUBLANE), :]
        v_rows = u_ref[pl.ds(base, SUBLANE), OFF_CV:OFF_CV + LANE]
        for i in range(SUBLANE):
            n = base + i
            kt = ck_ref[n]
            vt = cv_ref[n]
            qn = qm_scr[pl.ds(pl.multiple_of(n * C_HEADS, C_HEADS), C_HEADS), :]
            k_row = k_rows[i:i + 1, :]
            v_row = v_rows[i:i + 1, :]
            s_new = jnp.sum(qn * k_row, axis=1, keepdims=True)
            s = jnp.where(lane_h == 0, s_new, _dot_nt(qn.astype(MXU_DTYPE), kt.astype(MXU_DTYPE))) * (C_HD ** -0.5)
            m = jnp.maximum(jnp.max(s, axis=1, keepdims=True), sink)
            ex = jnp.exp(s - m)
            p = ex / (jnp.sum(ex, axis=1, keepdims=True) + jnp.exp(sink - m))
            v_eff = jnp.where(row_w == 0, v_row, vt)
            oh_scr[pl.ds(pl.multiple_of(n * C_HEADS, C_HEADS), C_HEADS), :] = \
                _dot(p.astype(MXU_DTYPE), v_eff.astype(MXU_DTYPE))
            ko_ref[n] = jnp.where(row_w == WINDOW - 1, k_row, pltpu.roll(kt, WINDOW - 1, axis=0))
            vo_ref[n] = jnp.where(row_w == WINDOW - 1, v_row, pltpu.roll(vt, WINDOW - 1, axis=0))
        return carry

    lax.fori_loop(0, nb // SUBLANE, sample_group, 0)

    o_p = []
    for pb in range(n_pair):
        a_even = oh_scr[pl.ds(2 * pb, nb, stride=C_HEADS), :]
        a_odd = oh_scr[pl.ds(2 * pb + 1, nb, stride=C_HEADS), :]
        if pb // ppg == 0:
            o_p.append(jnp.where(lo, a_even, pltpu.roll(a_odd, HALF, axis=1)))
        else:
            o_p.append(jnp.where(lo, pltpu.roll(a_even, HALF, axis=1), a_odd))

    y_scr[:, :A_WIDTH] = ya_ref[...].astype(y_scr.dtype)
    gw = B_WIDTH // B_GROUPS
    for g in range(B_GROUPS):
        gb = ys_ref[:, g * gw:(g + 1) * gw] * _silu(u_ref[:, OFF_BZ + g * gw:OFF_BZ + (g + 1) * gw])
        inv = lax.rsqrt(jnp.mean(gb * gb, axis=1, keepdims=True) + NORM_EPS)
        y_scr[:, A_WIDTH + g * gw:A_WIDTH + (g + 1) * gw] = (gb * inv * bnw_ref[:, g * gw:(g + 1) * gw]).astype(y_scr.dtype)
    for pb in range(n_pair):
        cz = u_ref[:, OFF_CZ + pb * LANE:OFF_CZ + (pb + 1) * LANE]
        y_scr[:, A_WIDTH + B_WIDTH + pb * LANE:A_WIDTH + B_WIDTH + (pb + 1) * LANE] = \
            (o_p[pb] * _silu(cz)).astype(y_scr.dtype)
    o_ref[...] = x_ref[...] + _dot(y_scr[...], wout_ref[...])


def _dec_out(x, u, ya, ys, ck3, cv3, cos_s, sin_s, lw):
    nb = x.shape[0]
    vmem = pl.BlockSpec(memory_space=pltpu.VMEM)
    return pl.pallas_call(
        _dec_out_kernel,
        in_specs=[vmem] * 13,
        out_shape=(jax.ShapeDtypeStruct((nb, D_MODEL), F32), jax.ShapeDtypeStruct(ck3.shape, F32),
                   jax.ShapeDtypeStruct(cv3.shape, F32)),
        scratch_shapes=[pltpu.VMEM((nb, D_MIX), MXU_DTYPE), pltpu.VMEM((C_HEADS * nb, LANE), F32),
                        pltpu.VMEM((nb, LANE), F32), pltpu.VMEM((nb * C_HEADS, LANE), F32)],
        compiler_params=pltpu.CompilerParams(vmem_limit_bytes=VMEM_LIMIT_BYTES),
        name="dec_out",
    )(x, u, ya, ys, ck3, cv3, cos_s, sin_s, lw["qn_w"], lw["kn_w"], lw["b_nw"], lw["sink_rows"], lw["w_out"])


def _sample_layer(x, st_c, st_n, st_m, st_h, st_conv, ck, cv, cos_s, sin_s, lw):
    nb = x.shape[0]
    u, xbc, conv_new = _dec_in(x, st_conv.reshape(nb, -1), lw)
    m_pad = jnp.pad(st_m.astype(F32), ((0, 0), (0, LANE - A_HEADS)))
    c_new, n_new, m_new, ya = _dec_mlstm(u, st_c.reshape(nb, -1), st_n.reshape(nb, -1), m_pad, lw)
    h_new, ys = _dec_ssd(xbc, u, st_h.reshape(nb, -1), lw)
    y, k_new, v_new = _dec_out(x, u, ya, ys, ck.reshape(nb, WINDOW, LANE), cv.reshape(nb, WINDOW, LANE),
                               cos_s, sin_s, lw)
    return y, (c_new.reshape(st_c.shape), n_new.reshape(st_n.shape), m_new[:, :, 0].T, h_new.reshape(st_h.shape),
               conv_new.reshape(st_conv.shape), k_new.reshape(ck.shape), v_new.reshape(cv.shape))


def _kernel_impl(x_prompt, x_sample, state_mlstm_C, state_mlstm_n, state_mlstm_m, state_ssm, state_conv,
                 cache_k, cache_v, norm_w, w_in, a_igate_b, a_fgate_b, a_norm_w, b_conv_w, b_conv_b,
                 b_dt_bias, b_A_log, b_D, b_norm_w, c_qnorm_w, c_knorm_w, c_sinks, w_out, *, tb):
    depth = w_in.shape[0]
    t_len = x_prompt.shape[1]
    lw = _prep_weights(norm_w, w_in, a_igate_b, a_fgate_b, a_norm_w, b_conv_w, b_conv_b, b_dt_bias, b_A_log,
                       b_D, b_norm_w, c_qnorm_w, c_knorm_w, c_sinks, w_out)
    cos_p, sin_p = _rope_tables(jnp.arange(t_len, dtype=jnp.int32))
    hp = x_prompt
    st_prompt = []
    for l in range(depth):
        res = _prompt_layer(hp, _layer_slice(lw, l), cos_p, sin_p, tb)
        hp = res[0]
        st_prompt.append(_unpad_prompt_states(*res[1:]))
    p_states = [jnp.stack(t) for t in zip(*st_prompt)]

    assert x_sample.shape[1] == 1
    cos_s, sin_s = _rope_tables(PAST_LEN + jnp.arange(1, dtype=jnp.int32))
    hs = x_sample[:, 0, :]
    st_sample = []
    for l in range(depth):
        hs, st = _sample_layer(hs, state_mlstm_C[l], state_mlstm_n[l], state_mlstm_m[l], state_ssm[l],
                               state_conv[l], cache_k[l], cache_v[l], cos_s, sin_s, _layer_slice(lw, l))
        st_sample.append(st)
    s_states = [jnp.stack(t) for t in zip(*st_sample)]
    return (hp, hs[:, None, :], *p_states, *s_states)


def kernel(x_prompt, x_sample, state_mlstm_C, state_mlstm_n, state_mlstm_m, state_ssm, state_conv, cache_k, cache_v, norm_w, w_in, a_igate_b, a_fgate_b, a_norm_w, b_conv_w, b_conv_b, b_dt_bias, b_A_log, b_D, b_norm_w, c_qnorm_w, c_knorm_w, c_sinks, w_out):
    out = _kernel_impl(x_prompt, x_sample, state_mlstm_C, state_mlstm_n, state_mlstm_m, state_ssm, state_conv,
                       cache_k, cache_v, norm_w, w_in, a_igate_b, a_fgate_b, a_norm_w, b_conv_w, b_conv_b,
                       b_dt_bias, b_A_log, b_D, b_norm_w, c_qnorm_w, c_knorm_w, c_sinks, w_out, tb=512)
    return tuple(o for o in out if o is not None)
```

```python
import functools
import math

import jax
import jax.numpy as jnp
import numpy as np
from jax import lax
from jax.experimental import pallas as pl
from jax.experimental.pallas import tpu as pltpu

F32 = jnp.float32
MXU_DTYPE = jnp.bfloat16

D_MODEL = 1024
A_HEADS, A_QK, A_V = 4, 64, 128
A_WIDTH = A_HEADS * A_V
B_HEADS, B_P, B_GROUPS, B_STATE = 8, 64, 2, 128
B_WIDTH = B_HEADS * B_P
CONV_W = 4
B_CONV_DIM = B_WIDTH + 2 * B_GROUPS * B_STATE
C_HEADS, C_KV, C_HD = 8, 2, 64
C_WIDTH = C_HEADS * C_HD
WINDOW = 128
ROPE_THETA = 10000.0
D_MIX = A_WIDTH + B_WIDTH + C_WIDTH
NORM_EPS = 1e-6
PAST_LEN = 8192

LANE = 128
SUBLANE = 8
HALF = LANE // 2

CHUNK = 128
N_GATES = 2 * A_HEADS + B_HEADS
TAIL0 = SUBLANE - (CONV_W - 1)

OFF_AQ = 0
OFF_AK = OFF_AQ + A_HEADS * A_QK
OFF_AV = OFF_AK + A_HEADS * A_QK
OFF_AO = OFF_AV + A_WIDTH
OFF_AZ = OFF_AO + A_WIDTH
OFF_BZ = OFF_AZ + A_WIDTH
OFF_BXBC = OFF_BZ + B_WIDTH
OFF_CQ = OFF_BXBC + B_CONV_DIM
OFF_CK = OFF_CQ + C_WIDTH
OFF_CV = OFF_CK + C_KV * C_HD
OFF_CZ = OFF_CV + C_KV * C_HD
OFF_G = OFF_CZ + C_WIDTH
D_INP = OFF_G + LANE
_SRC_SIZES = (A_HEADS * A_QK, A_HEADS * A_QK, A_WIDTH, A_WIDTH, A_WIDTH, A_HEADS, A_HEADS,
              B_WIDTH, B_CONV_DIM, B_HEADS, C_WIDTH, C_KV * C_HD, C_KV * C_HD, C_WIDTH)
_SRC_OFF = np.concatenate([[0], np.cumsum(_SRC_SIZES)]).tolist()

VMEM_LIMIT_BYTES = 56 * 1024 * 1024


def _dot(a, b):
    return jnp.dot(a, b, preferred_element_type=F32)


def _dot_nt(a, b):
    return lax.dot_general(a, b, (((1,), (1,)), ((), ())), preferred_element_type=F32)


def _dot_tn(a, b):
    return lax.dot_general(a, b, (((0,), (0,)), ((), ())), preferred_element_type=F32)


def _split3(z):
    hi = z.astype(MXU_DTYPE)
    r1 = z - hi.astype(F32)
    mid = r1.astype(MXU_DTYPE)
    lo = (r1 - mid.astype(F32)).astype(MXU_DTYPE)
    return hi, mid, lo


def _softplus_terms(x):
    t = jnp.log1p(jnp.exp(-jnp.abs(x)))
    return jnp.maximum(x, 0.0) + t, jnp.minimum(x, 0.0) - t


def _silu(x):
    return x * jax.nn.sigmoid(x)


def _seg64_sum(s, lane):
    for k in (1, 2, 4, 8, 16, 32):
        up = pltpu.roll(s, LANE - k, axis=1)
        dn = pltpu.roll(s, k, axis=1)
        s = s + jnp.where((lane & k) == 0, up, dn)
    return s


def _norm_rope(x, w, cos, sin_signed, lane):
    ms = _seg64_sum(x * x, lane) * (1.0 / C_HD)
    xn = x * lax.rsqrt(ms + NORM_EPS) * w
    partner = jnp.where((lane & (C_HD // 2)) == 0,
                        pltpu.roll(xn, LANE - C_HD // 2, axis=1), pltpu.roll(xn, C_HD // 2, axis=1))
    return xn * cos + partner * sin_signed


def _prompt_layer_kernel(
        sinks_ref,
        x_ref, nw_ref, win_ref, wgt_ref, wout_ref, gbc_ref, gbr_ref, alc_ref, alr_ref,
        anw_ref, cw_ref, cb_ref, dsk_ref, bnw_ref, qnw_ref, knw_ref, cos_ref, sin_ref,
        o_ref, cst_ref, nst_ref, mst_ref, hst_ref, conv_ref, k_ref, v_ref,
        u_scr, gr_scr, y_scr, cv_scr, *, tb):
    L = CHUNK
    nch = tb // L
    t_id = pl.program_id(1)

    @pl.when(t_id == 0)
    def _():
        cst_ref[...] = jnp.zeros_like(cst_ref)
        nst_ref[...] = jnp.zeros_like(nst_ref)
        mst_ref[...] = jnp.zeros_like(mst_ref)
        hst_ref[...] = jnp.zeros_like(hst_ref)
        conv_ref[...] = jnp.zeros_like(conv_ref)
        k_ref[...] = jnp.zeros_like(k_ref)
        v_ref[...] = jnp.zeros_like(v_ref)

    x = x_ref[...]
    xn = (x * lax.rsqrt(jnp.mean(x * x, axis=-1, keepdims=True) + NORM_EPS) * nw_ref[...]).astype(MXU_DTYPE)
    u_scr[...] = _dot(xn, win_ref[...])
    gr = _dot_nt(wgt_ref[...], xn)
    for c in range(nch):
        gr_scr[c] = gr[:, c * L:(c + 1) * L]

    lane = lax.broadcasted_iota(jnp.int32, (L, LANE), 1)
    row = lax.broadcasted_iota(jnp.int32, (L, LANE), 0)
    lo = lane < HALF
    causal = row >= lane
    tri = jnp.where(causal, 1.0, 0.0).astype(MXU_DTYPE)
    grow_id = lax.broadcasted_iota(jnp.int32, (2 * SUBLANE, L), 0)
    lane2 = lax.broadcasted_iota(jnp.int32, (2 * L, LANE), 1)
    lo2 = lane2 < HALF
    srow = lax.broadcasted_iota(jnp.int32, (L, 2 * L), 0)
    scol = lax.broadcasted_iota(jnp.int32, (L, 2 * L), 1)
    neg_inf = -jnp.inf

    def chunk(c, carry):
        r = pl.ds(pl.multiple_of(c * L, L), L)

        pre_c = u_scr[r, OFF_G:OFF_G + LANE] + gbc_ref[...]
        sp_c, ls_c = _softplus_terms(pre_c)
        a_neg_c = -jnp.exp(alc_ref[...])
        z_c = jnp.where((lane >= A_HEADS) & (lane < 2 * A_HEADS), ls_c,
                        jnp.where((lane >= 2 * A_HEADS) & (lane < N_GATES), sp_c * a_neg_c, 0.0))
        zc_parts = _split3(z_c)
        cum_c = _dot(tri, zc_parts[0]) + _dot(tri, zc_parts[1]) + _dot(tri, zc_parts[2])

        pre_r = gr_scr[c] + gbr_ref[...]
        sp_r, ls_r = _softplus_terms(pre_r)
        a_neg_r = -jnp.exp(alr_ref[...])
        z_r = jnp.where((grow_id >= A_HEADS) & (grow_id < 2 * A_HEADS), ls_r,
                        jnp.where(grow_id >= 2 * A_HEADS, sp_r * a_neg_r, 0.0))
        zr_parts = _split3(z_r)
        cum_r = _dot_nt(zr_parts[0], tri) + _dot_nt(zr_parts[1], tri) + _dot_nt(zr_parts[2], tri)

        for h in range(A_HEADS):
            pb, e = h // 2, h % 2
            own = lo if e == 0 else jnp.logical_not(lo)
            q_p = u_scr[r, OFF_AQ + pb * LANE:OFF_AQ + (pb + 1) * LANE].astype(MXU_DTYPE)
            k_f = jnp.where(own, u_scr[r, OFF_AK + pb * LANE:OFF_AK + (pb + 1) * LANE] * (A_QK ** -0.5), 0.0)
            k_m = k_f.astype(MXU_DTYPE)
            v_h = u_scr[r, OFF_AV + h * LANE:OFF_AV + (h + 1) * LANE]
            b_col = cum_c[:, A_HEADS + h:A_HEADS + h + 1]
            i_col = pre_c[:, h:h + 1]
            b_row = cum_r[A_HEADS + h:A_HEADS + h + 1, :]
            i_row = pre_r[h:h + 1, :]
            b_end = b_row[:, L - 1:L]
            m_prev = mst_ref[h:h + 1, 0:1]
            c_prev = cst_ref[h]
            n_prev = nst_ref[h:h + 1, :]

            g_row = b_end - b_row + i_row
            m_loc = jnp.max(g_row, axis=1, keepdims=True)
            dmat = jnp.where(causal, b_col - b_row + i_row, neg_inf)
            inter = b_col + m_prev
            m_t = jnp.maximum(inter, jnp.max(dmat, axis=1, keepdims=True))
            s = jnp.exp(dmat - m_t) * _dot_nt(q_p, k_m)
            si = jnp.exp(inter - m_t)
            num = _dot(s.astype(MXU_DTYPE), v_h.astype(MXU_DTYPE)) + si * _dot_nt(q_p, c_prev.astype(MXU_DTYPE))
            qn = jnp.sum(q_p.astype(F32) * n_prev, axis=1, keepdims=True)
            den = jnp.sum(s, axis=1, keepdims=True) + si * qn
            hh = num / jnp.maximum(jnp.abs(den), jnp.exp(-m_t))
            hn = hh * lax.rsqrt(jnp.mean(hh * hh, axis=1, keepdims=True) + NORM_EPS) * anw_ref[:, h * LANE:(h + 1) * LANE]
            ao = u_scr[r, OFF_AO + h * LANE:OFF_AO + (h + 1) * LANE]
            az = u_scr[r, OFF_AZ + h * LANE:OFF_AZ + (h + 1) * LANE]
            y_scr[r, h * LANE:(h + 1) * LANE] = (hn * jax.nn.sigmoid(ao) * _silu(az)).astype(y_scr.dtype)

            m_new = jnp.maximum(b_end + m_prev, m_loc)
            sp = jnp.exp(b_end + m_prev - m_new)
            sl = jnp.exp(m_loc - m_new)
            w_col = jnp.exp(b_end - b_col + i_col - m_loc)
            c_loc = _dot_tn((w_col * v_h).astype(MXU_DTYPE), k_m)
            cst_ref[h] = sp * c_prev + sl * c_loc
            nst_ref[h:h + 1, :] = sp * n_prev + sl * jnp.sum(w_col * k_f, axis=0, keepdims=True)
            mst_ref[h:h + 1, :] = jnp.broadcast_to(m_new, (1, LANE))

        cv_scr[TAIL0:SUBLANE, :] = conv_ref[TAIL0:SUBLANE, :]
        u_bxbc = u_scr[r, OFF_BXBC:OFF_BXBC + B_CONV_DIM]
        cv_scr[SUBLANE:SUBLANE + L, :] = u_bxbc
        acc = cb_ref[...] + cv_scr[TAIL0:TAIL0 + L, :] * cw_ref[0:1, :]
        for j in range(1, CONV_W):
            acc = acc + cv_scr[TAIL0 + j:TAIL0 + j + L, :] * cw_ref[j:j + 1, :]
        conv_ref[TAIL0:SUBLANE, :] = u_bxbc[L - (CONV_W - 1):L, :]
        xbc = _silu(acc)

        for g in range(B_GROUPS):
            b_g = xbc[:, B_WIDTH + g * B_STATE:B_WIDTH + (g + 1) * B_STATE].astype(MXU_DTYPE)
            c_g = xbc[:, B_WIDTH + (B_GROUPS + g) * B_STATE:B_WIDTH + (B_GROUPS + g + 1) * B_STATE].astype(MXU_DTYPE)
            cb = _dot_nt(c_g, b_g)
            gated = []
            for pj in range(B_HEADS // B_GROUPS // 2):
                pb = g * (B_HEADS // B_GROUPS // 2) + pj
                x_p = xbc[:, pb * LANE:(pb + 1) * LANE]
                y_p = dsk_ref[:, pb * LANE:(pb + 1) * LANE] * x_p
                a_cols, a_ends, dtw = [], [], []
                for e in range(2):
                    hd = 2 * pb + e
                    gi = 2 * A_HEADS + hd
                    a_col = cum_c[:, gi:gi + 1]
                    a_row = cum_r[gi:gi + 1, :]
                    dt_row = sp_r[gi:gi + 1, :]
                    dt_col = sp_c[:, gi:gi + 1]
                    a_end = a_row[:, L - 1:L]
                    wmat = jnp.exp(jnp.where(causal, a_col - a_row, neg_inf)) * cb * dt_row
                    own = lo if e == 0 else jnp.logical_not(lo)
                    y_p = y_p + _dot(wmat.astype(MXU_DTYPE), jnp.where(own, x_p, 0.0).astype(MXU_DTYPE))
                    a_cols.append(a_col)
                    a_ends.append(a_end)
                    dtw.append(jnp.exp(a_end - a_col) * dt_col)
                h_prev = hst_ref[pb * LANE:(pb + 1) * LANE, :]
                y_p = y_p + _dot_nt(c_g, h_prev.astype(MXU_DTYPE)) * jnp.where(lo, jnp.exp(a_cols[0]), jnp.exp(a_cols[1]))
                xw = (x_p * jnp.where(lo, dtw[0], dtw[1])).astype(MXU_DTYPE)
                h_loc = _dot_tn(xw, b_g)
                hst_ref[pb * LANE:(pb + 1) * LANE, :] = \
                    jnp.where(row < B_P, jnp.exp(a_ends[0]), jnp.exp(a_ends[1])) * h_prev + h_loc
                bz = u_scr[r, OFF_BZ + pb * LANE:OFF_BZ + (pb + 1) * LANE]
                gated.append(y_p * _silu(bz))
            ms = sum(jnp.sum(gp * gp, axis=1, keepdims=True) for gp in gated) * (1.0 / (B_WIDTH // B_GROUPS))
            inv = lax.rsqrt(ms + NORM_EPS)
            for pj, gp in enumerate(gated):
                pb = g * (B_HEADS // B_GROUPS // 2) + pj
                y_scr[r, A_WIDTH + pb * LANE:A_WIDTH + (pb + 1) * LANE] = \
                    (gp * inv * bnw_ref[:, pb * LANE:(pb + 1) * LANE]).astype(y_scr.dtype)

        cos = cos_ref[r, :]
        sin = sin_ref[r, :]
        k_new = _norm_rope(u_scr[r, OFF_CK:OFF_CK + LANE], knw_ref[...], cos, sin, lane)
        v_new = u_scr[r, OFF_CV:OFF_CV + LANE]
        kk = jnp.concatenate([k_ref[...], k_new], axis=0)
        vv = jnp.concatenate([v_ref[...], v_new], axis=0)
        kk_sw = pltpu.roll(kk, HALF, axis=1)
        vv_sw = pltpu.roll(vv, HALF, axis=1)

        def variants(a, a_sw):
            return [[jnp.where(lo2, a, 0.0).astype(MXU_DTYPE), jnp.where(lo2, 0.0, a_sw).astype(MXU_DTYPE)],
                    [jnp.where(lo2, a_sw, 0.0).astype(MXU_DTYPE), jnp.where(lo2, 0.0, a).astype(MXU_DTYPE)]]

        k_var = variants(kk, kk_sw)
        v_var = variants(vv, vv_sw)
        first = jnp.logical_and(t_id == 0, c == 0)
        shift = jnp.where(first, 2 * L, 0)
        valid = ((scol < L) & (scol > srow + shift)) | ((scol >= L) & (scol - L <= srow))
        for pb in range(C_HEADS // 2):
            q_p = _norm_rope(u_scr[r, OFF_CQ + pb * LANE:OFF_CQ + (pb + 1) * LANE], qnw_ref[...], cos, sin, lane)
            q_p = q_p.astype(MXU_DTYPE)
            o_p = jnp.zeros((L, LANE), F32)
            for e in range(2):
                hd = 2 * pb + e
                g = hd // (C_HEADS // C_KV)
                sc = jnp.where(valid, _dot_nt(q_p, k_var[g][e]) * (C_HD ** -0.5), neg_inf)
                sink = sinks_ref[hd]
                m = jnp.maximum(jnp.max(sc, axis=1, keepdims=True), sink)
                ex = jnp.exp(sc - m)
                p = ex / (jnp.sum(ex, axis=1, keepdims=True) + jnp.exp(sink - m))
                o_p = o_p + _dot(p.astype(MXU_DTYPE), v_var[g][e])
            cz = u_scr[r, OFF_CZ + pb * LANE:OFF_CZ + (pb + 1) * LANE]
            y_scr[r, A_WIDTH + B_WIDTH + pb * LANE:A_WIDTH + B_WIDTH + (pb + 1) * LANE] = \
                (o_p * _silu(cz)).astype(y_scr.dtype)
        k_ref[...] = k_new
        v_ref[...] = v_new
        return carry

    lax.fori_loop(0, nch, chunk, 0)

    o_ref[...] = x_ref[...] + _dot(y_scr[...], wout_ref[...])


C_EXT = A_V + 2 * SUBLANE
D_INT = OFF_G + N_GATES


def _norm_rope_t(x_t, w_rep, cos, sin):
    half = C_HD // 2
    out = []
    for hh in range(LANE // C_HD):
        xh = x_t[hh * C_HD:(hh + 1) * C_HD, :]
        ms = jnp.mean(xh * xh, axis=0, keepdims=True)
        xn = xh * lax.rsqrt(ms + NORM_EPS) * w_rep[hh * C_HD:(hh + 1) * C_HD, :]
        x1, x2 = xn[:half, :], xn[half:, :]
        out += [x1 * cos - x2 * sin, x2 * cos + x1 * sin]
    return jnp.concatenate(out, axis=0)


def _prompt_layer_t_kernel(
        sinks_ref,
        x_ref, nw_ref, wt_ref, wout_ref, gb_ref, al_ref, anw_ref, cw_ref, cb_ref, dsk_ref, bnw_ref,
        qnw_ref, knw_ref, cos_ref, sin_ref,
        o_ref, cst_ref, mst_ref, hst_ref, convt_ref, kt_ref, vt_ref,
        u_scr, y_scr, kprev_scr, roll_scr, *, tb):
    L = CHUNK
    nch = tb // L
    t_id = pl.program_id(1)

    @pl.when(t_id == 0)
    def _():
        cst_ref[...] = jnp.zeros_like(cst_ref)
        mst_ref[...] = jnp.zeros_like(mst_ref)
        hst_ref[...] = jnp.zeros_like(hst_ref)
        convt_ref[...] = jnp.zeros_like(convt_ref)
        kt_ref[...] = jnp.zeros_like(kt_ref)
        vt_ref[...] = jnp.zeros_like(vt_ref)
        kprev_scr[...] = jnp.zeros_like(kprev_scr)
        roll_scr[...] = jnp.zeros_like(roll_scr)

    x = x_ref[...]
    xn = (x * lax.rsqrt(jnp.mean(x * x, axis=-1, keepdims=True) + NORM_EPS) * nw_ref[...]).astype(MXU_DTYPE)
    u_scr[...] = _dot_nt(wt_ref[...], xn)

    row = lax.broadcasted_iota(jnp.int32, (L, L), 0)
    lane = lax.broadcasted_iota(jnp.int32, (L, L), 1)
    lo = lane < HALF
    top = row < B_P
    src_le_t = row <= lane
    tri = jnp.where(lane <= row, 1.0, 0.0).astype(MXU_DTYPE)
    grow = lax.broadcasted_iota(jnp.int32, (N_GATES, L), 0)
    krow = lax.broadcasted_iota(jnp.int32, (2 * L, L), 0)
    klane = lax.broadcasted_iota(jnp.int32, (2 * L, L), 1)
    neg_inf = -jnp.inf
    a_neg = -jnp.exp(al_ref[...])

    def chunk(c, carry):
        cols = pl.ds(pl.multiple_of(c * L, L), L)

        pre_r = u_scr[OFF_G:OFF_G + N_GATES, cols] + gb_ref[...]
        sp_r, ls_r = _softplus_terms(pre_r)
        z_r = jnp.where((grow >= A_HEADS) & (grow < 2 * A_HEADS), ls_r,
                        jnp.where(grow >= 2 * A_HEADS, sp_r * a_neg, 0.0))
        zp = _split3(z_r)
        cum_r = _dot_nt(zp[0], tri) + _dot_nt(zp[1], tri) + _dot_nt(zp[2], tri)
        colf = jnp.concatenate([pre_r, cum_r, jnp.zeros((L - 2 * N_GATES, L), F32)], axis=0).T

        for h in range(A_HEADS):
            pb, e = h // 2, h % 2
            if e == 0:
                q_t = u_scr[OFF_AQ + pb * LANE:OFF_AQ + (pb + 1) * LANE, cols].astype(MXU_DTYPE)
                k_p = (u_scr[OFF_AK + pb * LANE:OFF_AK + (pb + 1) * LANE, cols] * (A_QK ** -0.5)).T
            k_m = jnp.where(lo if e == 0 else jnp.logical_not(lo), k_p, 0.0).astype(MXU_DTYPE)
            v_t = u_scr[OFF_AV + h * LANE:OFF_AV + (h + 1) * LANE, cols]
            b_row = cum_r[A_HEADS + h:A_HEADS + h + 1, :]
            i_row = pre_r[h:h + 1, :]
            c_col = colf[:, N_GATES + A_HEADS + h:N_GATES + A_HEADS + h + 1] - colf[:, h:h + 1]
            b_end = b_row[:, L - 1:L]
            m_prev = mst_ref[h:h + 1, 0:1]
            c_prev = cst_ref[h]

            g_row = b_end - b_row + i_row
            m_loc = jnp.max(g_row, axis=1, keepdims=True)
            d_t = jnp.where(src_le_t, b_row - c_col, neg_inf)
            inter = b_row + m_prev
            m_t = jnp.maximum(inter, jnp.max(d_t, axis=0, keepdims=True))
            res = _dot(jnp.concatenate([k_m, c_prev.astype(MXU_DTYPE)], axis=0), q_t)
            s_t = jnp.exp(d_t - m_t) * res[:L, :]
            si = jnp.exp(inter - m_t)
            num = _dot(v_t.astype(MXU_DTYPE), s_t.astype(MXU_DTYPE)) + si * res[L:L + A_V, :]
            den = jnp.sum(s_t, axis=0, keepdims=True) + si * res[L + A_V:L + A_V + 1, :]
            hh = num * (1.0 / jnp.maximum(jnp.abs(den), jnp.exp(-m_t)))
            hn = hh * lax.rsqrt(jnp.mean(hh * hh, axis=0, keepdims=True) + NORM_EPS) * anw_ref[h * LANE:(h + 1) * LANE, :]
            ao = u_scr[OFF_AO + h * LANE:OFF_AO + (h + 1) * LANE, cols]
            az = u_scr[OFF_AZ + h * LANE:OFF_AZ + (h + 1) * LANE, cols]
            y_scr[h * LANE:(h + 1) * LANE, cols] = (hn * jax.nn.sigmoid(ao) * _silu(az)).astype(y_scr.dtype)

            m_new = jnp.maximum(b_end + m_prev, m_loc)
            sp = jnp.exp(b_end + m_prev - m_new)
            sl = jnp.exp(m_loc - m_new)
            w_row = jnp.exp(g_row - m_loc)
            v_ext = jnp.concatenate([v_t, jnp.ones((1, L), F32), jnp.zeros((C_EXT - A_V - 1, L), F32)], axis=0)
            cst_ref[h] = sp * c_prev + sl * _dot((v_ext * w_row).astype(MXU_DTYPE), k_m)
            mst_ref[h:h + 1, :] = jnp.broadcast_to(m_new, (1, LANE))

        cur = u_scr[OFF_BXBC:OFF_BXBC + B_CONV_DIM, cols]
        acc = cb_ref[...] + cur * cw_ref[CONV_W - 1]
        for j in range(1, CONV_W):
            rolled = pltpu.roll(cur, j, axis=1)
            acc = acc + jnp.where(lane[:1, :] >= j, rolled, roll_scr[j - 1]) * cw_ref[CONV_W - 1 - j]
            roll_scr[j - 1] = rolled
        convt_ref[...] = cur
        xbc = _silu(acc)

        for g in range(B_GROUPS):
            b_t = xbc[B_WIDTH + g * B_STATE:B_WIDTH + (g + 1) * B_STATE, :]
            c_t = xbc[B_WIDTH + (B_GROUPS + g) * B_STATE:B_WIDTH + (B_GROUPS + g + 1) * B_STATE, :]
            b_m = b_t.T.astype(MXU_DTYPE)
            hpg = B_HEADS // B_GROUPS
            h_prev = hst_ref[g * hpg * B_P:(g + 1) * hpg * B_P, :]
            res = _dot(jnp.concatenate([b_m, h_prev.astype(MXU_DTYPE)], axis=0), c_t.astype(MXU_DTYPE))
            cb_tt = res[:L, :]
            gated = []
            for pj in range(hpg // 2):
                pb = g * (hpg // 2) + pj
                x_t = xbc[pb * LANE:(pb + 1) * LANE, :]
                ys, a_ends, dec_rows, ea_rows = [], [], [], []
                for e in range(2):
                    gi = 2 * A_HEADS + 2 * pb + e
                    a_row = cum_r[gi:gi + 1, :]
                    dt_row = sp_r[gi:gi + 1, :]
                    a_col = colf[:, N_GATES + gi:N_GATES + gi + 1]
                    a_end = a_row[:, L - 1:L]
                    w_t = jnp.exp(jnp.where(src_le_t, a_row - a_col, neg_inf)) * cb_tt
                    xdt = (x_t[e * B_P:(e + 1) * B_P, :] * dt_row).astype(MXU_DTYPE)
                    ys.append(_dot(xdt, w_t.astype(MXU_DTYPE)))
                    a_ends.append(jnp.exp(a_end))
                    dec_rows.append(jnp.exp(a_end - a_row) * dt_row)
                    ea_rows.append(jnp.exp(a_row))
                inter = res[L + pj * LANE:L + (pj + 1) * LANE, :] * jnp.where(top, ea_rows[0], ea_rows[1])
                y_p = jnp.concatenate(ys, axis=0) + inter + dsk_ref[pb * LANE:(pb + 1) * LANE, :] * x_t
                h_loc = _dot((x_t * jnp.where(top, dec_rows[0], dec_rows[1])).astype(MXU_DTYPE), b_m)
                hst_ref[pb * LANE:(pb + 1) * LANE, :] = \
                    jnp.where(top, a_ends[0], a_ends[1]) * h_prev[pj * LANE:(pj + 1) * LANE, :] + h_loc
                bz = u_scr[OFF_BZ + pb * LANE:OFF_BZ + (pb + 1) * LANE, cols]
                gated.append(y_p * _silu(bz))
            ms = sum(jnp.sum(gp * gp, axis=0, keepdims=True) for gp in gated) * (1.0 / (B_WIDTH // B_GROUPS))
            inv = lax.rsqrt(ms + NORM_EPS)
            for pj, gp in enumerate(gated):
                pb = g * (hpg // 2) + pj
                y_scr[A_WIDTH + pb * LANE:A_WIDTH + (pb + 1) * LANE, cols] = \
                    (gp * inv * bnw_ref[pb * LANE:(pb + 1) * LANE, :]).astype(y_scr.dtype)

        cos = cos_ref[:, cols]
        sin = sin_ref[:, cols]
        k_t = _norm_rope_t(u_scr[OFF_CK:OFF_CK + LANE, cols], knw_ref[...], cos, sin)
        v_t = u_scr[OFF_CV:OFF_CV + LANE, cols]
        k_p = k_t.T
        k_sw = pltpu.roll(k_p, HALF, axis=1)
        k_cur = [[jnp.where(lo, k_p, 0.0).astype(MXU_DTYPE), jnp.where(lo, 0.0, k_sw).astype(MXU_DTYPE)],
                 [jnp.where(lo, k_sw, 0.0).astype(MXU_DTYPE), jnp.where(lo, 0.0, k_p).astype(MXU_DTYPE)]]
        shift = jnp.where(jnp.logical_and(t_id == 0, c == 0), 2 * L, 0)
        valid = ((krow < L) & (krow > klane + shift)) | ((krow >= L) & (krow - L <= klane))
        ppg = C_HEADS // 2 // C_KV
        for pb in range(C_HEADS // 2):
            g = pb // ppg
            q_t = _norm_rope_t(u_scr[OFF_CQ + pb * LANE:OFF_CQ + (pb + 1) * LANE, cols], qnw_ref[...], cos, sin)
            keys = jnp.concatenate([kprev_scr[2 * g], k_cur[g][0], kprev_scr[2 * g + 1], k_cur[g][1]], axis=0)
            sc = _dot(keys, q_t.astype(MXU_DTYPE)) * (C_HD ** -0.5)
            v_g = jnp.concatenate([vt_ref[g * C_HD:(g + 1) * C_HD, :], v_t[g * C_HD:(g + 1) * C_HD, :]],
                                  axis=1).astype(MXU_DTYPE)
            outs = []
            for e in range(2):
                s = jnp.where(valid, sc[e * 2 * L:(e + 1) * 2 * L, :], neg_inf)
                sink = sinks_ref[2 * pb + e]
                m = jnp.maximum(jnp.max(s, axis=0, keepdims=True), sink)
                ex = jnp.exp(s - m)
                p = ex * (1.0 / (jnp.sum(ex, axis=0, keepdims=True) + jnp.exp(sink - m)))
                outs.append(_dot(v_g, p.astype(MXU_DTYPE)))
            cz = u_scr[OFF_CZ + pb * LANE:OFF_CZ + (pb + 1) * LANE, cols]
            y_scr[A_WIDTH + B_WIDTH + pb * LANE:A_WIDTH + B_WIDTH + (pb + 1) * LANE, cols] = \
                (jnp.concatenate(outs, axis=0) * _silu(cz)).astype(y_scr.dtype)
        for g in range(C_KV):
            for e in range(2):
                kprev_scr[2 * g + e] = k_cur[g][e]
        kt_ref[...] = k_t
        vt_ref[...] = v_t
        return carry

    lax.fori_loop(0, nch, chunk, 0)

    o_ref[...] = x_ref[...] + _dot_tn(y_scr[...], wout_ref[...])


def _prompt_layer_t(x, lw, cos_t, sin_t, tb):
    n_seq, t_len, _ = x.shape
    assert t_len % tb == 0 and tb % CHUNK == 0
    grid = (n_seq, t_len // tb)

    def per_seq(shape):
        nd = len(shape)
        return pl.BlockSpec((None,) + shape, lambda n, t, _nd=nd: (n,) + (0,) * _nd)

    in_specs = [
        pl.BlockSpec(memory_space=pltpu.SMEM),
        pl.BlockSpec((None, tb, D_MODEL), lambda n, t: (n, t, 0)),
        _const_spec((1, D_MODEL)),
        _const_spec((D_INT, D_MODEL)),
        _const_spec((D_MIX, D_MODEL)),
        _const_spec((N_GATES, LANE)), _const_spec((N_GATES, LANE)),
        _const_spec((A_WIDTH, LANE)),
        _const_spec((CONV_W, B_CONV_DIM, LANE)), _const_spec((B_CONV_DIM, LANE)),
        _const_spec((B_WIDTH, LANE)), _const_spec((B_WIDTH, LANE)),
        _const_spec((LANE, LANE)), _const_spec((LANE, LANE)),
        pl.BlockSpec((C_HD // 2, tb), lambda n, t: (0, t)),
        pl.BlockSpec((C_HD // 2, tb), lambda n, t: (0, t)),
    ]
    out_shape = (
        jax.ShapeDtypeStruct((n_seq, t_len, D_MODEL), F32),
        jax.ShapeDtypeStruct((n_seq, A_HEADS, C_EXT, LANE), F32),
        jax.ShapeDtypeStruct((n_seq, SUBLANE, LANE), F32),
        jax.ShapeDtypeStruct((n_seq, B_HEADS * B_P, B_STATE), F32),
        jax.ShapeDtypeStruct((n_seq, B_CONV_DIM, LANE), F32),
        jax.ShapeDtypeStruct((n_seq, LANE, WINDOW), F32),
        jax.ShapeDtypeStruct((n_seq, LANE, WINDOW), F32),
    )
    out_specs = (
        pl.BlockSpec((None, tb, D_MODEL), lambda n, t: (n, t, 0)),
        per_seq((A_HEADS, C_EXT, LANE)), per_seq((SUBLANE, LANE)),
        per_seq((B_HEADS * B_P, B_STATE)), per_seq((B_CONV_DIM, LANE)),
        per_seq((LANE, WINDOW)), per_seq((LANE, WINDOW)),
    )
    scratch = [
        pltpu.VMEM((D_INT, tb), F32),
        pltpu.VMEM((D_MIX, tb), MXU_DTYPE),
        pltpu.VMEM((2 * C_KV, CHUNK, LANE), MXU_DTYPE),
        pltpu.VMEM((CONV_W - 1, B_CONV_DIM, LANE), F32),
    ]
    return pl.pallas_call(
        functools.partial(_prompt_layer_t_kernel, tb=tb),
        grid=grid, in_specs=in_specs, out_specs=out_specs, out_shape=out_shape, scratch_shapes=scratch,
        compiler_params=pltpu.CompilerParams(dimension_semantics=("arbitrary", "arbitrary"),
                                             vmem_limit_bytes=VMEM_LIMIT_BYTES),
        name="prompt_layer",
    )(lw["sinks"], x, lw["norm_w"], lw["w_t"], lw["w_out"], lw["gb_rep"], lw["al_rep"], lw["a_nw_rep"],
      lw["conv_w_rep"], lw["conv_b_rep"], lw["d_rep"], lw["b_nw_rep"], lw["qn_rep"], lw["kn_rep"], cos_t, sin_t)


def _unpad_prompt_states_t(cst, mst, hst, convt, kt, vt):
    n_seq = cst.shape[0]
    c_even, c_odd = cst[:, 0::2, :A_V, :HALF], cst[:, 1::2, :A_V, HALF:]
    c_state = jnp.stack([c_even, c_odd], axis=2).reshape(n_seq, A_HEADS, A_V, A_QK)
    n_even, n_odd = cst[:, 0::2, A_V, :HALF], cst[:, 1::2, A_V, HALF:]
    n_state = jnp.stack([n_even, n_odd], axis=2).reshape(n_seq, A_HEADS, A_QK)
    conv = jnp.swapaxes(convt[:, :, LANE - (CONV_W - 1):], 1, 2)
    k1 = jnp.transpose(kt.reshape(n_seq, C_KV, C_HD, WINDOW), (0, 3, 1, 2))
    v1 = jnp.transpose(vt.reshape(n_seq, C_KV, C_HD, WINDOW), (0, 3, 1, 2))
    return (c_state, n_state, mst[:, :A_HEADS, 0], hst.reshape(n_seq, B_HEADS, B_P, B_STATE), conv, k1, v1)


def _const_spec(shape):
    nd = len(shape)
    return pl.BlockSpec(shape, lambda n, t, _nd=nd: (0,) * _nd)


def _prompt_layer(x, lw, cos_t, sin_t, tb):
    n_seq, t_len, _ = x.shape
    assert t_len % tb == 0 and tb % CHUNK == 0
    nch = tb // CHUNK
    grid = (n_seq, t_len // tb)

    def per_seq(shape):
        nd = len(shape)
        return pl.BlockSpec((None,) + shape, lambda n, t, _nd=nd: (n,) + (0,) * _nd)

    in_specs = [
        pl.BlockSpec(memory_space=pltpu.SMEM),
        pl.BlockSpec((None, tb, D_MODEL), lambda n, t: (n, t, 0)),
        _const_spec((1, D_MODEL)),
        _const_spec((D_MODEL, D_INP)),
        _const_spec((2 * SUBLANE, D_MODEL)),
        _const_spec((D_MIX, D_MODEL)),
        _const_spec((1, LANE)), _const_spec((2 * SUBLANE, LANE)),
        _const_spec((1, LANE)), _const_spec((2 * SUBLANE, LANE)),
        _const_spec((1, A_WIDTH)),
        _const_spec((CONV_W, B_CONV_DIM)), _const_spec((1, B_CONV_DIM)),
        _const_spec((1, B_WIDTH)), _const_spec((1, B_WIDTH)),
        _const_spec((1, LANE)), _const_spec((1, LANE)),
        pl.BlockSpec((tb, LANE), lambda n, t: (t, 0)),
        pl.BlockSpec((tb, LANE), lambda n, t: (t, 0)),
    ]
    out_shape = (
        jax.ShapeDtypeStruct((n_seq, t_len, D_MODEL), F32),
        jax.ShapeDtypeStruct((n_seq, A_HEADS, A_V, LANE), F32),
        jax.ShapeDtypeStruct((n_seq, SUBLANE, LANE), F32),
        jax.ShapeDtypeStruct((n_seq, SUBLANE, LANE), F32),
        jax.ShapeDtypeStruct((n_seq, B_HEADS * B_P, B_STATE), F32),
        jax.ShapeDtypeStruct((n_seq, SUBLANE, B_CONV_DIM), F32),
        jax.ShapeDtypeStruct((n_seq, WINDOW, LANE), F32),
        jax.ShapeDtypeStruct((n_seq, WINDOW, LANE), F32),
    )
    out_specs = (
        pl.BlockSpec((None, tb, D_MODEL), lambda n, t: (n, t, 0)),
        per_seq((A_HEADS, A_V, LANE)), per_seq((SUBLANE, LANE)), per_seq((SUBLANE, LANE)),
        per_seq((B_HEADS * B_P, B_STATE)), per_seq((SUBLANE, B_CONV_DIM)),
        per_seq((WINDOW, LANE)), per_seq((WINDOW, LANE)),
    )
    scratch = [
        pltpu.VMEM((tb, D_INP), F32),
        pltpu.VMEM((nch, 2 * SUBLANE, CHUNK), F32),
        pltpu.VMEM((tb, D_MIX), MXU_DTYPE),
        pltpu.VMEM((SUBLANE + CHUNK, B_CONV_DIM), F32),
    ]
    return pl.pallas_call(
        functools.partial(_prompt_layer_kernel, tb=tb),
        grid=grid, in_specs=in_specs, out_specs=out_specs, out_shape=out_shape, scratch_shapes=scratch,
        compiler_params=pltpu.CompilerParams(dimension_semantics=("arbitrary", "arbitrary"),
                                             vmem_limit_bytes=VMEM_LIMIT_BYTES),
        name="prompt_layer",
    )(lw["sinks"], x, lw["norm_w"], lw["w_in"], lw["wg_t"], lw["w_out"], lw["gb_col"], lw["gb_row"],
      lw["al_col"], lw["al_row"], lw["a_nw"], lw["conv_w"], lw["conv_b"], lw["d_lane"], lw["b_nw"],
      lw["qn_w"], lw["kn_w"], cos_t, sin_t)


def _src(w, i):
    return w[..., _SRC_OFF[i]:_SRC_OFF[i + 1]]


def _prep_weights(norm_w, w_in, a_igate_b, a_fgate_b, a_norm_w, b_conv_w, b_conv_b, b_dt_bias, b_A_log, b_D,
                  b_norm_w, c_qnorm_w, c_knorm_w, c_sinks, w_out):
    depth = w_in.shape[0]
    gates = jnp.concatenate([_src(w_in, 5), _src(w_in, 6), _src(w_in, 9)], axis=-1)
    pad = jnp.zeros(w_in.shape[:2] + (LANE - N_GATES,), w_in.dtype)
    w_perm = jnp.concatenate([w_in[..., :_SRC_OFF[5]], _src(w_in, 7), _src(w_in, 8),
                              w_in[..., _SRC_OFF[10]:], gates, pad], axis=-1).astype(MXU_DTYPE)
    gbias = jnp.concatenate([a_igate_b, a_fgate_b, b_dt_bias], axis=-1).astype(F32)
    alog = jnp.concatenate([jnp.zeros((depth, 2 * A_HEADS), F32), b_A_log.astype(F32)], axis=-1)

    def col_form(v):
        return jnp.pad(v, ((0, 0), (0, LANE - N_GATES)))[:, None, :]

    def row_form(v):
        return jnp.broadcast_to(v[:, :, None], (depth, N_GATES, LANE))

    w_t = jnp.swapaxes(w_in, 1, 2)
    w_t = jnp.concatenate([w_t[:, :_SRC_OFF[5]], w_t[:, _SRC_OFF[7]:_SRC_OFF[9]], w_t[:, _SRC_OFF[10]:],
                           w_t[:, _SRC_OFF[5]:_SRC_OFF[7]], w_t[:, _SRC_OFF[9]:_SRC_OFF[10]]], axis=1)

    def rep(v):
        return jnp.broadcast_to(v.astype(F32)[..., None], v.shape + (LANE,))

    return {
        "w_t": w_t.astype(MXU_DTYPE),
        "gb_rep": rep(gbias), "al_rep": rep(alog),
        "a_nw_rep": rep(a_norm_w), "conv_w_rep": rep(b_conv_w), "conv_b_rep": rep(b_conv_b),
        "d_rep": rep(jnp.repeat(b_D, B_P, axis=-1)), "b_nw_rep": rep(b_norm_w),
        "qn_rep": rep(jnp.tile(c_qnorm_w, (1, LANE // C_HD))), "kn_rep": rep(jnp.tile(c_knorm_w, (1, LANE // C_HD))),
        "sinks": c_sinks.astype(F32),
        "sink_rows": jnp.broadcast_to(c_sinks.astype(F32)[:, :, None], (depth, C_HEADS, LANE)),
        "a_ib": a_igate_b.astype(F32), "a_fb": a_fgate_b.astype(F32),
        "dt_b": b_dt_bias.astype(F32), "a_log": b_A_log.astype(F32), "d_skip": b_D.astype(F32),
        "norm_w": norm_w.astype(F32)[:, None, :],
        "w_in": w_perm,
        "wg_t": jnp.swapaxes(gates, 1, 2).astype(MXU_DTYPE),
        "w_out": w_out.astype(MXU_DTYPE),
        "gb_col": col_form(gbias), "gb_row": row_form(gbias),
        "al_col": col_form(alog), "al_row": row_form(alog),
        "a_nw": a_norm_w.astype(F32)[:, None, :],
        "conv_w": b_conv_w.astype(F32), "conv_b": b_conv_b.astype(F32)[:, None, :],
        "d_lane": jnp.repeat(b_D.astype(F32), B_P, axis=-1)[:, None, :],
        "b_nw": b_norm_w.astype(F32)[:, None, :],
        "qn_w": jnp.tile(c_qnorm_w.astype(F32), (1, LANE // C_HD))[:, None, :],
        "kn_w": jnp.tile(c_knorm_w.astype(F32), (1, LANE // C_HD))[:, None, :],
    }


def _rope_tables(pos):
    half = C_HD // 2
    inv = ROPE_THETA ** (-jnp.arange(half, dtype=F32) / half)
    ang = pos.astype(F32)[:, None] * inv[None, :]
    cos, sin = jnp.cos(ang), jnp.sin(ang)
    reps = LANE // C_HD
    return jnp.tile(jnp.concatenate([cos, cos], axis=-1), (1, reps)), \
        jnp.tile(jnp.concatenate([-sin, sin], axis=-1), (1, reps))


def _rope_tables_t(pos):
    half = C_HD // 2
    inv = ROPE_THETA ** (-jnp.arange(half, dtype=F32) / half)
    ang = inv[:, None] * pos.astype(F32)[None, :]
    return jnp.cos(ang), jnp.sin(ang)


def _unpad_prompt_states(cst, nst, mst, hst, conv, k1, v1):
    n_seq = cst.shape[0]
    c_even, c_odd = cst[:, 0::2, :, :HALF], cst[:, 1::2, :, HALF:]
    c_state = jnp.stack([c_even, c_odd], axis=2).reshape(n_seq, A_HEADS, A_V, A_QK)
    n_even, n_odd = nst[:, 0:A_HEADS:2, :HALF], nst[:, 1:A_HEADS:2, HALF:]
    n_state = jnp.stack([n_even, n_odd], axis=2).reshape(n_seq, A_HEADS, A_QK)
    return (c_state, n_state, mst[:, :A_HEADS, 0], hst.reshape(n_seq, B_HEADS, B_P, B_STATE),
            conv[:, TAIL0:, :], k1.reshape(n_seq, WINDOW, C_KV, C_HD), v1.reshape(n_seq, WINDOW, C_KV, C_HD))


def _layer_slice(lw, l):
    return {k: v[l] for k, v in lw.items()}


def _own_half_tiled(a, pick):
    return jnp.where(pick, a, pltpu.roll(a, HALF, axis=1))


def _lane_pick(a, lane, idx):
    return jnp.sum(jnp.where(lane == idx, a, 0.0), axis=1, keepdims=True)


def _dec_in_kernel(x_ref, nw_ref, win_ref, cbuf_ref, cw_ref, cb_ref, u_ref, xbc_ref, conv_out_ref):
    x = x_ref[...]
    xn = (x * lax.rsqrt(jnp.mean(x * x, axis=-1, keepdims=True) + NORM_EPS) * nw_ref[...]).astype(MXU_DTYPE)
    u_ref[...] = _dot(xn, win_ref[...])
    bx = u_ref[:, OFF_BXBC:OFF_BXBC + B_CONV_DIM]
    acc = cb_ref[...] + bx * cw_ref[CONV_W - 1:CONV_W, :]
    for j in range(CONV_W - 1):
        acc = acc + cbuf_ref[:, j * B_CONV_DIM:(j + 1) * B_CONV_DIM] * cw_ref[j:j + 1, :]
    xbc_ref[...] = _silu(acc)
    conv_out_ref[:, :(CONV_W - 2) * B_CONV_DIM] = cbuf_ref[:, B_CONV_DIM:]
    conv_out_ref[:, (CONV_W - 2) * B_CONV_DIM:] = bx


def _dec_in(x, cbuf, lw):
    nb = x.shape[0]
    return pl.pallas_call(
        _dec_in_kernel,
        out_shape=(jax.ShapeDtypeStruct((nb, D_INP), F32), jax.ShapeDtypeStruct((nb, B_CONV_DIM), F32),
                   jax.ShapeDtypeStruct((nb, (CONV_W - 1) * B_CONV_DIM), F32)),
        compiler_params=pltpu.CompilerParams(vmem_limit_bytes=VMEM_LIMIT_BYTES),
        name="dec_in",
    )(x, lw["norm_w"], lw["w_in"], cbuf, lw["conv_w"], lw["conv_b"])


def _dec_mlstm_kernel(ib_ref, fb_ref, q_ref, k_ref, v_ref, ao_ref, az_ref, g_ref, c_ref, n_ref, m_ref, anw_ref,
                      co_ref, no_ref, mo_ref, ya_ref):
    h = pl.program_id(0)
    e = h % 2
    nb = q_ref.shape[0]
    lane = lax.broadcasted_iota(jnp.int32, (nb, LANE), 1)
    lo = lane < HALF
    pick = (lane >= HALF).astype(jnp.int32) == e
    qt = _own_half_tiled(q_ref[...], pick)
    kt = _own_half_tiled(k_ref[...], pick) * (A_QK ** -0.5)
    nt = _own_half_tiled(n_ref[...], pick)
    gc = g_ref[...]
    ig = _lane_pick(gc, lane, h) + ib_ref[h]
    _, lf = _softplus_terms(_lane_pick(gc, lane, A_HEADS + h) + fb_ref[h])
    m0 = _lane_pick(m_ref[...], lane, h)
    m_new = jnp.maximum(lf + m0, ig)
    sp = jnp.exp(lf + m0 - m_new)
    sl = jnp.exp(ig - m_new)
    n_new = sp * nt + sl * kt
    v = v_ref[...]
    slk = sl * kt
    hout = jnp.zeros((nb, LANE), F32)
    for cb in range(A_V * A_QK // LANE):
        cols = slice(cb * LANE, (cb + 1) * LANE)
        vexp = jnp.where(lo, v[:, 2 * cb:2 * cb + 1], v[:, 2 * cb + 1:2 * cb + 2])
        new = sp * c_ref[:, cols] + vexp * slk
        co_ref[:, cols] = new
        prod = new * qt
        r0 = jnp.sum(jnp.where(lo, prod, 0.0), axis=1, keepdims=True)
        r1 = jnp.sum(jnp.where(lo, 0.0, prod), axis=1, keepdims=True)
        hout = jnp.where(lane == 2 * cb, r0, jnp.where(lane == 2 * cb + 1, r1, hout))
    den = jnp.sum(jnp.where(lo, n_new * qt, 0.0), axis=1, keepdims=True)
    hh = hout / jnp.maximum(jnp.abs(den), jnp.exp(-m_new))
    hn = hh * lax.rsqrt(jnp.mean(hh * hh, axis=1, keepdims=True) + NORM_EPS) * anw_ref[...]
    ya_ref[...] = hn * jax.nn.sigmoid(ao_ref[...]) * _silu(az_ref[...])
    mo_ref[...] = jnp.broadcast_to(m_new, (nb, LANE))

    @pl.when(e == 0)
    def _():
        no_ref[...] = n_new

    @pl.when(e == 1)
    def _():
        no_ref[...] = jnp.where(pick, n_new, no_ref[...])


def _dec_mlstm(u, c_flat, n_flat, m_pad, lw):
    nb = u.shape[0]
    blk = lambda off, f: pl.BlockSpec((nb, LANE), lambda h, _o=off // LANE, _f=f: (0, _o + _f(h)))
    per_head, per_pair, fixed = (lambda h: h), (lambda h: h // 2), (lambda h: 0)
    smem = pl.BlockSpec(memory_space=pltpu.SMEM)
    c_spec = pl.BlockSpec((nb, A_V * A_QK), lambda h: (0, h))
    return pl.pallas_call(
        _dec_mlstm_kernel,
        grid=(A_HEADS,),
        in_specs=[smem, smem, blk(OFF_AQ, per_pair), blk(OFF_AK, per_pair), blk(OFF_AV, per_head),
                  blk(OFF_AO, per_head), blk(OFF_AZ, per_head), blk(OFF_G, fixed), c_spec,
                  blk(0, per_pair), blk(0, fixed), pl.BlockSpec((1, LANE), lambda h: (0, h))],
        out_specs=(c_spec, blk(0, per_pair), pl.BlockSpec((None, nb, LANE), lambda h: (h, 0, 0)), blk(0, per_head)),
        out_shape=(jax.ShapeDtypeStruct(c_flat.shape, F32), jax.ShapeDtypeStruct(n_flat.shape, F32),
                   jax.ShapeDtypeStruct((A_HEADS, nb, LANE), F32), jax.ShapeDtypeStruct((nb, A_WIDTH), F32)),
        compiler_params=pltpu.CompilerParams(dimension_semantics=("arbitrary",), vmem_limit_bytes=VMEM_LIMIT_BYTES),
        name="dec_mlstm",
    )(lw["a_ib"], lw["a_fb"], u, u, u, u, u, u, c_flat, n_flat, m_pad, lw["a_nw"])


def _dec_ssd_kernel(dtb_ref, alog_ref, dsk_ref, x_ref, b_ref, c_ref, g_ref, s_ref, so_ref, y_ref):
    h = pl.program_id(0)
    e = h % 2
    nb = x_ref.shape[0]
    lane = lax.broadcasted_iota(jnp.int32, (nb, LANE), 1)
    pick = (lane >= HALF).astype(jnp.int32) == e
    lane_rel = lane - HALF * e
    xp = x_ref[...]
    xt = _own_half_tiled(xp, pick)
    dt, _ = _softplus_terms(_lane_pick(g_ref[...], lane, 2 * A_HEADS + h) + dtb_ref[h])
    d_a = jnp.exp(dt * (-jnp.exp(alog_ref[h])))
    dtb = dt * b_ref[...]
    cm = c_ref[...]
    y = jnp.zeros((nb, LANE), F32)
    for p in range(B_P):
        cols = slice(p * B_STATE, (p + 1) * B_STATE)
        new = d_a * s_ref[:, cols] + xt[:, p:p + 1] * dtb
        so_ref[:, cols] = new
        y = jnp.where(lane_rel == p, jnp.sum(new * cm, axis=1, keepdims=True), y)
    y = y + dsk_ref[h] * jnp.where(pick, xp, 0.0)

    @pl.when(e == 0)
    def _():
        y_ref[...] = y

    @pl.when(e == 1)
    def _():
        y_ref[...] = y_ref[...] + y


def _dec_ssd(xbc, u, s_flat, lw):
    nb = u.shape[0]
    smem = pl.BlockSpec(memory_space=pltpu.SMEM)
    hpg = B_HEADS // B_GROUPS
    tile = lambda f: pl.BlockSpec((nb, LANE), f)
    s_spec = pl.BlockSpec((nb, B_P * B_STATE), lambda h: (0, h))
    return pl.pallas_call(
        _dec_ssd_kernel,
        grid=(B_HEADS,),
        in_specs=[smem, smem, smem, tile(lambda h: (0, h // 2)),
                  tile(lambda h: (0, B_WIDTH // LANE + h // hpg)),
                  tile(lambda h: (0, B_WIDTH // LANE + B_GROUPS + h // hpg)),
                  tile(lambda h: (0, OFF_G // LANE)), s_spec],
        out_specs=(s_spec, tile(lambda h: (0, h // 2))),
        out_shape=(jax.ShapeDtypeStruct(s_flat.shape, F32), jax.ShapeDtypeStruct((nb, B_WIDTH), F32)),
        compiler_params=pltpu.CompilerParams(dimension_semantics=("arbitrary",), vmem_limit_bytes=VMEM_LIMIT_BYTES),
        name="dec_ssd",
    )(lw["dt_b"], lw["a_log"], lw["d_skip"], xbc, xbc, xbc, u, s_flat)


def _dec_out_kernel(x_ref, u_ref, ya_ref, ys_ref, ck_ref, cv_ref, cos_ref, sin_ref, qnw_ref, knw_ref,
                    bnw_ref, sink_ref, wout_ref, o_ref, ko_ref, vo_ref, y_scr, qm_scr, kn_scr, oh_scr):
    nb = x_ref.shape[0]
    lane = lax.broadcasted_iota(jnp.int32, (nb, LANE), 1)
    lo = lane < HALF
    cos, sin = cos_ref[...], sin_ref[...]
    kn_scr[...] = _norm_rope(u_ref[:, OFF_CK:OFF_CK + LANE], knw_ref[...], cos, sin, lane)
    n_pair = C_HEADS // 2
    ppg = n_pair // C_KV
    for pb in range(n_pair):
        q = _norm_rope(u_ref[:, OFF_CQ + pb * LANE:OFF_CQ + (pb + 1) * LANE], qnw_ref[...], cos, sin, lane)
        q_sw = pltpu.roll(q, HALF, axis=1)
        if pb // ppg == 0:
            even, odd = jnp.where(lo, q, 0.0), jnp.where(lo, q_sw, 0.0)
        else:
            even, odd = jnp.where(lo, 0.0, q_sw), jnp.where(lo, 0.0, q)
        qm_scr[pl.ds(2 * pb, nb, stride=C_HEADS), :] = even
        qm_scr[pl.ds(2 * pb + 1, nb, stride=C_HEADS), :] = odd

    lane_h = lax.broadcasted_iota(jnp.int32, (C_HEADS, WINDOW), 1)
    row_w = lax.broadcasted_iota(jnp.int32, (WINDOW, LANE), 0)
    sink = sink_ref[:, 0:1]

    def sample_group(gi, carry):
        base = pl.multiple_of(gi * SUBLANE, SUBLANE)
        k_rows = kn_scr[pl.ds(base, SUBLANE), :]
        v_rows = u_ref[pl.ds(base, SUBLANE), OFF_CV:OFF_CV + LANE]
        for i in range(SUBLANE):
            n = base + i
            kt = ck_ref[n]
            vt = cv_ref[n]
            qn = qm_scr[pl.ds(pl.multiple_of(n * C_HEADS, C_HEADS), C_HEADS), :]
            k_row = k_rows[i:i + 1, :]
            v_row = v_rows[i:i + 1, :]
            s_new = jnp.sum(qn * k_row, axis=1, keepdims=True)
            s = jnp.where(lane_h == 0, s_new, _dot_nt(qn.astype(MXU_DTYPE), kt.astype(MXU_DTYPE))) * (C_HD ** -0.5)
            m = jnp.maximum(jnp.max(s, axis=1, keepdims=True), sink)
            ex = jnp.exp(s - m)
            p = ex / (jnp.sum(ex, axis=1, keepdims=True) + jnp.exp(sink - m))
            v_eff = jnp.where(row_w == 0, v_row, vt)
            oh_scr[pl.ds(pl.multiple_of(n * C_HEADS, C_HEADS), C_HEADS), :] = \
                _dot(p.astype(MXU_DTYPE), v_eff.astype(MXU_DTYPE))
            ko_ref[n] = jnp.where(row_w == WINDOW - 1, k_row, pltpu.roll(kt, WINDOW - 1, axis=0))
            vo_ref[n] = jnp.where(row_w == WINDOW - 1, v_row, pltpu.roll(vt, WINDOW - 1, axis=0))
        return carry

    lax.fori_loop(0, nb // SUBLANE, sample_group, 0)

    o_p = []
    for pb in range(n_pair):
        a_even = oh_scr[pl.ds(2 * pb, nb, stride=C_HEADS), :]
        a_odd = oh_scr[pl.ds(2 * pb + 1, nb, stride=C_HEADS), :]
        if pb // ppg == 0:
            o_p.append(jnp.where(lo, a_even, pltpu.roll(a_odd, HALF, axis=1)))
        else:
            o_p.append(jnp.where(lo, pltpu.roll(a_even, HALF, axis=1), a_odd))

    y_scr[:, :A_WIDTH] = ya_ref[...].astype(y_scr.dtype)
    gw = B_WIDTH // B_GROUPS
    for g in range(B_GROUPS):
        gb = ys_ref[:, g * gw:(g + 1) * gw] * _silu(u_ref[:, OFF_BZ + g * gw:OFF_BZ + (g + 1) * gw])
        inv = lax.rsqrt(jnp.mean(gb * gb, axis=1, keepdims=True) + NORM_EPS)
        y_scr[:, A_WIDTH + g * gw:A_WIDTH + (g + 1) * gw] = (gb * inv * bnw_ref[:, g * gw:(g + 1) * gw]).astype(y_scr.dtype)
    for pb in range(n_pair):
        cz = u_ref[:, OFF_CZ + pb * LANE:OFF_CZ + (pb + 1) * LANE]
        y_scr[:, A_WIDTH + B_WIDTH + pb * LANE:A_WIDTH + B_WIDTH + (pb + 1) * LANE] = \
            (o_p[pb] * _silu(cz)).astype(y_scr.dtype)
    o_ref[...] = x_ref[...] + _dot(y_scr[...], wout_ref[...])


def _dec_out(x, u, ya, ys, ck3, cv3, cos_s, sin_s, lw):
    nb = x.shape[0]
    vmem = pl.BlockSpec(memory_space=pltpu.VMEM)
    return pl.pallas_call(
        _dec_out_kernel,
        in_specs=[vmem] * 13,
        out_shape=(jax.ShapeDtypeStruct((nb, D_MODEL), F32), jax.ShapeDtypeStruct(ck3.shape, F32),
                   jax.ShapeDtypeStruct(cv3.shape, F32)),
        scratch_shapes=[pltpu.VMEM((nb, D_MIX), MXU_DTYPE), pltpu.VMEM((C_HEADS * nb, LANE), F32),
                        pltpu.VMEM((nb, LANE), F32), pltpu.VMEM((nb * C_HEADS, LANE), F32)],
        compiler_params=pltpu.CompilerParams(vmem_limit_bytes=VMEM_LIMIT_BYTES),
        name="dec_out",
    )(x, u, ya, ys, ck3, cv3, cos_s, sin_s, lw["qn_w"], lw["kn_w"], lw["b_nw"], lw["sink_rows"], lw["w_out"])


def _sample_layer(x, st_c, st_n, st_m, st_h, st_conv, ck, cv, cos_s, sin_s, lw):
    nb = x.shape[0]
    u, xbc, conv_new = _dec_in(x, st_conv.reshape(nb, -1), lw)
    m_pad = jnp.pad(st_m.astype(F32), ((0, 0), (0, LANE - A_HEADS)))
    c_new, n_new, m_new, ya = _dec_mlstm(u, st_c.reshape(nb, -1), st_n.reshape(nb, -1), m_pad, lw)
    h_new, ys = _dec_ssd(xbc, u, st_h.reshape(nb, -1), lw)
    y, k_new, v_new = _dec_out(x, u, ya, ys, ck.reshape(nb, WINDOW, LANE), cv.reshape(nb, WINDOW, LANE),
                               cos_s, sin_s, lw)
    return y, (c_new.reshape(st_c.shape), n_new.reshape(st_n.shape), m_new[:, :, 0].T, h_new.reshape(st_h.shape),
               conv_new.reshape(st_conv.shape), k_new.reshape(ck.shape), v_new.reshape(cv.shape))


def _kernel_impl(x_prompt, x_sample, state_mlstm_C, state_mlstm_n, state_mlstm_m, state_ssm, state_conv,
                 cache_k, cache_v, norm_w, w_in, a_igate_b, a_fgate_b, a_norm_w, b_conv_w, b_conv_b,
                 b_dt_bias, b_A_log, b_D, b_norm_w, c_qnorm_w, c_knorm_w, c_sinks, w_out, *, tb):
    depth = w_in.shape[0]
    t_len = x_prompt.shape[1]
    lw = _prep_weights(norm_w, w_in, a_igate_b, a_fgate_b, a_norm_w, b_conv_w, b_conv_b, b_dt_bias, b_A_log,
                       b_D, b_norm_w, c_qnorm_w, c_knorm_w, c_sinks, w_out)
    cos_p, sin_p = _rope_tables_t(jnp.arange(t_len, dtype=jnp.int32))
    hp = x_prompt
    st_prompt = []
    for l in range(depth):
        res = _prompt_layer_t(hp, _layer_slice(lw, l), cos_p, sin_p, tb)
        hp = res[0]
        st_prompt.append(_unpad_prompt_states_t(*res[1:]))
    p_states = [jnp.stack(t) for t in zip(*st_prompt)]

    assert x_sample.shape[1] == 1
    cos_s, sin_s = _rope_tables(PAST_LEN + jnp.arange(1, dtype=jnp.int32))
    hs = x_sample[:, 0, :]
    st_sample = []
    for l in range(depth):
        hs, st = _sample_layer(hs, state_mlstm_C[l], state_mlstm_n[l], state_mlstm_m[l], state_ssm[l],
                               state_conv[l], cache_k[l], cache_v[l], cos_s, sin_s, _layer_slice(lw, l))
        st_sample.append(st)
    s_states = [jnp.stack(t) for t in zip(*st_sample)]
    return (hp, hs[:, None, :], *p_states, *s_states)


def kernel(x_prompt, x_sample, state_mlstm_C, state_mlstm_n, state_mlstm_m, state_ssm, state_conv, cache_k, cache_v, norm_w, w_in, a_igate_b, a_fgate_b, a_norm_w, b_conv_w, b_conv_b, b_dt_bias, b_A_log, b_D, b_norm_w, c_qnorm_w, c_knorm_w, c_sinks, w_out):
    out = _kernel_impl(x_prompt, x_sample, state_mlstm_C, state_mlstm_n, state_mlstm_m, state_ssm, state_conv,
                       cache_k, cache_v, norm_w, w_in, a_igate_b, a_fgate_b, a_norm_w, b_conv_w, b_conv_b,
                       b_dt_bias, b_A_log, b_D, b_norm_w, c_qnorm_w, c_knorm_w, c_sinks, w_out, tb=512)
    return tuple(o for o in out if o is not None)
```

```python
import functools

import jax
import jax.numpy as jnp
import numpy as np
from jax import lax
from jax.experimental import pallas as pl
from jax.experimental.pallas import tpu as pltpu

F32 = jnp.float32
MXU_DTYPE = jnp.bfloat16

D_MODEL = 1024
A_HEADS, A_QK, A_V = 4, 64, 128
A_WIDTH = A_HEADS * A_V
B_HEADS, B_P, B_GROUPS, B_STATE = 8, 64, 2, 128
B_WIDTH = B_HEADS * B_P
CONV_W = 4
B_CONV_DIM = B_WIDTH + 2 * B_GROUPS * B_STATE
C_HEADS, C_KV, C_HD = 8, 2, 64
C_WIDTH = C_HEADS * C_HD
WINDOW = 128
ROPE_THETA = 10000.0
D_MIX = A_WIDTH + B_WIDTH + C_WIDTH
NORM_EPS = 1e-6
PAST_LEN = 8192

LANE = 128
SUBLANE = 8
HALF = LANE // 2

CHUNK = 128
PROMPT_BLOCK = 512
SAMPLE_BLOCK = SUBLANE
N_GATES = 2 * A_HEADS + B_HEADS

OFF_AQ = 0
OFF_AK = OFF_AQ + A_HEADS * A_QK
OFF_AV = OFF_AK + A_HEADS * A_QK
OFF_AO = OFF_AV + A_WIDTH
OFF_AZ = OFF_AO + A_WIDTH
OFF_BZ = OFF_AZ + A_WIDTH
OFF_BXBC = OFF_BZ + B_WIDTH
OFF_CQ = OFF_BXBC + B_CONV_DIM
OFF_CK = OFF_CQ + C_WIDTH
OFF_CV = OFF_CK + C_KV * C_HD
OFF_CZ = OFF_CV + C_KV * C_HD
OFF_G = OFF_CZ + C_WIDTH
D_INT = OFF_G + N_GATES
_SRC_SIZES = (A_HEADS * A_QK, A_HEADS * A_QK, A_WIDTH, A_WIDTH, A_WIDTH, A_HEADS, A_HEADS,
              B_WIDTH, B_CONV_DIM, B_HEADS, C_WIDTH, C_KV * C_HD, C_KV * C_HD, C_WIDTH)
_SRC_OFF = np.concatenate([[0], np.cumsum(_SRC_SIZES)]).tolist()

C_EXT = A_V + 2 * SUBLANE
VMEM_LIMIT_BYTES = 56 * 1024 * 1024

REP_SP, REP_SL = 0, A_HEADS
REP_DA, REP_DT = 2 * A_HEADS, 2 * A_HEADS + B_HEADS
REP_SNEW = 2 * A_HEADS + 2 * B_HEADS
REP_IDEN = REP_SNEW + C_HEADS
N_REP = REP_IDEN + A_HEADS


def _dot(a, b):
    return jnp.dot(a, b, preferred_element_type=F32)


def _dot_nt(a, b):
    return lax.dot_general(a, b, (((1,), (1,)), ((), ())), preferred_element_type=F32)


def _dot_tn(a, b):
    return lax.dot_general(a, b, (((0,), (0,)), ((), ())), preferred_element_type=F32)


def _split3(z):
    hi = z.astype(MXU_DTYPE)
    r1 = z - hi.astype(F32)
    mid = r1.astype(MXU_DTYPE)
    lo = (r1 - mid.astype(F32)).astype(MXU_DTYPE)
    return hi, mid, lo


def _softplus_terms(x):
    t = jnp.log1p(jnp.exp(-jnp.abs(x)))
    return jnp.maximum(x, 0.0) + t, jnp.minimum(x, 0.0) - t


def _silu(x):
    return x * jax.nn.sigmoid(x)


def _rmsnorm_rows(x, w):
    return x * lax.rsqrt(jnp.mean(x * x, axis=-1, keepdims=True) + NORM_EPS) * w


def _norm_rope_t(x_t, w_rep, cos, sin):
    half = C_HD // 2
    out = []
    for hh in range(LANE // C_HD):
        xh = x_t[hh * C_HD:(hh + 1) * C_HD, :]
        ms = jnp.mean(xh * xh, axis=0, keepdims=True)
        xn = xh * lax.rsqrt(ms + NORM_EPS) * w_rep[hh * C_HD:(hh + 1) * C_HD, :]
        x1, x2 = xn[:half, :], xn[half:, :]
        out += [x1 * cos - x2 * sin, x2 * cos + x1 * sin]
    return jnp.concatenate(out, axis=0)


def _prompt_layer_kernel(
        sinks_ref,
        x_ref, nw_ref, wt_ref, wout_ref, gb_ref, al_ref, anw_ref, cw_ref, cb_ref, dsk_ref, bnw_ref,
        qnw_ref, knw_ref, cos_ref, sin_ref,
        o_ref, cst_ref, mst_ref, hst_ref, convt_ref, kt_ref, vt_ref,
        u_scr, y_scr, kprev_scr, roll_scr, *, tb):
    L = CHUNK
    nch = tb // L
    t_id = pl.program_id(1)

    @pl.when(t_id == 0)
    def _():
        cst_ref[...] = jnp.zeros_like(cst_ref)
        mst_ref[...] = jnp.zeros_like(mst_ref)
        hst_ref[...] = jnp.zeros_like(hst_ref)
        convt_ref[...] = jnp.zeros_like(convt_ref)
        kt_ref[...] = jnp.zeros_like(kt_ref)
        vt_ref[...] = jnp.zeros_like(vt_ref)
        kprev_scr[...] = jnp.zeros_like(kprev_scr)
        roll_scr[...] = jnp.zeros_like(roll_scr)

    xn = _rmsnorm_rows(x_ref[...], nw_ref[...]).astype(MXU_DTYPE)
    u_scr[...] = _dot_nt(wt_ref[...], xn)

    row = lax.broadcasted_iota(jnp.int32, (L, L), 0)
    lane = lax.broadcasted_iota(jnp.int32, (L, L), 1)
    lo = lane < HALF
    top = row < B_P
    src_le_t = row <= lane
    tri = jnp.where(lane <= row, 1.0, 0.0).astype(MXU_DTYPE)
    grow = lax.broadcasted_iota(jnp.int32, (N_GATES, L), 0)
    krow = lax.broadcasted_iota(jnp.int32, (2 * L, L), 0)
    klane = lax.broadcasted_iota(jnp.int32, (2 * L, L), 1)
    neg_inf = -jnp.inf
    a_neg = -jnp.exp(al_ref[...])

    def chunk(c, carry):
        cols = pl.ds(pl.multiple_of(c * L, L), L)

        pre_r = u_scr[OFF_G:OFF_G + N_GATES, cols] + gb_ref[...]
        sp_r, ls_r = _softplus_terms(pre_r)
        z_r = jnp.where((grow >= A_HEADS) & (grow < 2 * A_HEADS), ls_r,
                        jnp.where(grow >= 2 * A_HEADS, sp_r * a_neg, 0.0))
        zp = _split3(z_r)
        cum_r = _dot_nt(zp[0], tri) + _dot_nt(zp[1], tri) + _dot_nt(zp[2], tri)
        colf = jnp.concatenate([pre_r, cum_r, jnp.zeros((L - 2 * N_GATES, L), F32)], axis=0).T

        for h in range(A_HEADS):
            pb, e = h // 2, h % 2
            if e == 0:
                q_t = u_scr[OFF_AQ + pb * LANE:OFF_AQ + (pb + 1) * LANE, cols].astype(MXU_DTYPE)
                k_p = (u_scr[OFF_AK + pb * LANE:OFF_AK + (pb + 1) * LANE, cols] * (A_QK ** -0.5)).T
            k_m = jnp.where(lo if e == 0 else jnp.logical_not(lo), k_p, 0.0).astype(MXU_DTYPE)
            v_t = u_scr[OFF_AV + h * LANE:OFF_AV + (h + 1) * LANE, cols]
            b_row = cum_r[A_HEADS + h:A_HEADS + h + 1, :]
            i_row = pre_r[h:h + 1, :]
            c_col = colf[:, N_GATES + A_HEADS + h:N_GATES + A_HEADS + h + 1] - colf[:, h:h + 1]
            b_end = b_row[:, L - 1:L]
            m_prev = mst_ref[h:h + 1, 0:1]
            c_prev = cst_ref[h]

            g_row = b_end - b_row + i_row
            m_loc = jnp.max(g_row, axis=1, keepdims=True)
            d_t = jnp.where(src_le_t, b_row - c_col, neg_inf)
            inter = b_row + m_prev
            m_t = jnp.maximum(inter, jnp.max(d_t, axis=0, keepdims=True))
            res = _dot(jnp.concatenate([k_m, c_prev.astype(MXU_DTYPE)], axis=0), q_t)
            s_t = jnp.exp(d_t - m_t) * res[:L, :]
            si = jnp.exp(inter - m_t)
            num = _dot(v_t.astype(MXU_DTYPE), s_t.astype(MXU_DTYPE)) + si * res[L:L + A_V, :]
            den = jnp.sum(s_t, axis=0, keepdims=True) + si * res[L + A_V:L + A_V + 1, :]
            hh = num * (1.0 / jnp.maximum(jnp.abs(den), jnp.exp(-m_t)))
            hn = hh * lax.rsqrt(jnp.mean(hh * hh, axis=0, keepdims=True) + NORM_EPS) * anw_ref[h * LANE:(h + 1) * LANE, :]
            ao = u_scr[OFF_AO + h * LANE:OFF_AO + (h + 1) * LANE, cols]
            az = u_scr[OFF_AZ + h * LANE:OFF_AZ + (h + 1) * LANE, cols]
            y_scr[h * LANE:(h + 1) * LANE, cols] = (hn * jax.nn.sigmoid(ao) * _silu(az)).astype(y_scr.dtype)

            m_new = jnp.maximum(b_end + m_prev, m_loc)
            sp = jnp.exp(b_end + m_prev - m_new)
            sl = jnp.exp(m_loc - m_new)
            w_row = jnp.exp(g_row - m_loc)
            v_ext = jnp.concatenate([v_t, jnp.ones((1, L), F32), jnp.zeros((C_EXT - A_V - 1, L), F32)], axis=0)
            cst_ref[h] = sp * c_prev + sl * _dot((v_ext * w_row).astype(MXU_DTYPE), k_m)
            mst_ref[h:h + 1, :] = jnp.broadcast_to(m_new, (1, LANE))

        cur = u_scr[OFF_BXBC:OFF_BXBC + B_CONV_DIM, cols]
        acc = cb_ref[...] + cur * cw_ref[CONV_W - 1]
        for j in range(1, CONV_W):
            rolled = pltpu.roll(cur, j, axis=1)
            acc = acc + jnp.where(lane[:1, :] >= j, rolled, roll_scr[j - 1]) * cw_ref[CONV_W - 1 - j]
            roll_scr[j - 1] = rolled
        convt_ref[...] = cur
        xbc = _silu(acc)

        for g in range(B_GROUPS):
            b_t = xbc[B_WIDTH + g * B_STATE:B_WIDTH + (g + 1) * B_STATE, :]
            c_t = xbc[B_WIDTH + (B_GROUPS + g) * B_STATE:B_WIDTH + (B_GROUPS + g + 1) * B_STATE, :]
            b_m = b_t.T.astype(MXU_DTYPE)
            hpg = B_HEADS // B_GROUPS
            h_prev = hst_ref[g * hpg * B_P:(g + 1) * hpg * B_P, :]
            res = _dot(jnp.concatenate([b_m, h_prev.astype(MXU_DTYPE)], axis=0), c_t.astype(MXU_DTYPE))
            cb_tt = res[:L, :]
            gated = []
            for pj in range(hpg // 2):
                pb = g * (hpg // 2) + pj
                x_t = xbc[pb * LANE:(pb + 1) * LANE, :]
                ys, a_ends, dec_rows, ea_rows = [], [], [], []
                for e in range(2):
                    gi = 2 * A_HEADS + 2 * pb + e
                    a_row = cum_r[gi:gi + 1, :]
                    dt_row = sp_r[gi:gi + 1, :]
                    a_col = colf[:, N_GATES + gi:N_GATES + gi + 1]
                    a_end = a_row[:, L - 1:L]
                    w_t = jnp.exp(jnp.where(src_le_t, a_row - a_col, neg_inf)) * cb_tt
                    xdt = (x_t[e * B_P:(e + 1) * B_P, :] * dt_row).astype(MXU_DTYPE)
                    ys.append(_dot(xdt, w_t.astype(MXU_DTYPE)))
                    a_ends.append(jnp.exp(a_end))
                    dec_rows.append(jnp.exp(a_end - a_row) * dt_row)
                    ea_rows.append(jnp.exp(a_row))
                inter = res[L + pj * LANE:L + (pj + 1) * LANE, :] * jnp.where(top, ea_rows[0], ea_rows[1])
                y_p = jnp.concatenate(ys, axis=0) + inter + dsk_ref[pb * LANE:(pb + 1) * LANE, :] * x_t
                h_loc = _dot((x_t * jnp.where(top, dec_rows[0], dec_rows[1])).astype(MXU_DTYPE), b_m)
                hst_ref[pb * LANE:(pb + 1) * LANE, :] = \
                    jnp.where(top, a_ends[0], a_ends[1]) * h_prev[pj * LANE:(pj + 1) * LANE, :] + h_loc
                bz = u_scr[OFF_BZ + pb * LANE:OFF_BZ + (pb + 1) * LANE, cols]
                gated.append(y_p * _silu(bz))
            ms = sum(jnp.sum(gp * gp, axis=0, keepdims=True) for gp in gated) * (1.0 / (B_WIDTH // B_GROUPS))
            inv = lax.rsqrt(ms + NORM_EPS)
            for pj, gp in enumerate(gated):
                pb = g * (hpg // 2) + pj
                y_scr[A_WIDTH + pb * LANE:A_WIDTH + (pb + 1) * LANE, cols] = \
                    (gp * inv * bnw_ref[pb * LANE:(pb + 1) * LANE, :]).astype(y_scr.dtype)

        cos = cos_ref[:, cols]
        sin = sin_ref[:, cols]
        k_t = _norm_rope_t(u_scr[OFF_CK:OFF_CK + LANE, cols], knw_ref[...], cos, sin)
        v_t = u_scr[OFF_CV:OFF_CV + LANE, cols]
        k_p = k_t.T
        k_sw = pltpu.roll(k_p, HALF, axis=1)
        k_cur = [[jnp.where(lo, k_p, 0.0).astype(MXU_DTYPE), jnp.where(lo, 0.0, k_sw).astype(MXU_DTYPE)],
                 [jnp.where(lo, k_sw, 0.0).astype(MXU_DTYPE), jnp.where(lo, 0.0, k_p).astype(MXU_DTYPE)]]
        shift = jnp.where(jnp.logical_and(t_id == 0, c == 0), 2 * L, 0)
        valid = ((krow < L) & (krow > klane + shift)) | ((krow >= L) & (krow - L <= klane))
        ppg = C_HEADS // 2 // C_KV
        for pb in range(C_HEADS // 2):
            g = pb // ppg
            q_t = _norm_rope_t(u_scr[OFF_CQ + pb * LANE:OFF_CQ + (pb + 1) * LANE, cols], qnw_ref[...], cos, sin)
            keys = jnp.concatenate([kprev_scr[2 * g], k_cur[g][0], kprev_scr[2 * g + 1], k_cur[g][1]], axis=0)
            sc = _dot(keys, q_t.astype(MXU_DTYPE)) * (C_HD ** -0.5)
            v_g = jnp.concatenate([vt_ref[g * C_HD:(g + 1) * C_HD, :], v_t[g * C_HD:(g + 1) * C_HD, :]],
                                  axis=1).astype(MXU_DTYPE)
            outs = []
            for e in range(2):
                s = jnp.where(valid, sc[e * 2 * L:(e + 1) * 2 * L, :], neg_inf)
                sink = sinks_ref[2 * pb + e]
                m = jnp.maximum(jnp.max(s, axis=0, keepdims=True), sink)
                ex = jnp.exp(s - m)
                p = ex * (1.0 / (jnp.sum(ex, axis=0, keepdims=True) + jnp.exp(sink - m)))
                outs.append(_dot(v_g, p.astype(MXU_DTYPE)))
            cz = u_scr[OFF_CZ + pb * LANE:OFF_CZ + (pb + 1) * LANE, cols]
            y_scr[A_WIDTH + B_WIDTH + pb * LANE:A_WIDTH + B_WIDTH + (pb + 1) * LANE, cols] = \
                (jnp.concatenate(outs, axis=0) * _silu(cz)).astype(y_scr.dtype)
        for g in range(C_KV):
            for e in range(2):
                kprev_scr[2 * g + e] = k_cur[g][e]
        kt_ref[...] = k_t
        vt_ref[...] = v_t
        return carry

    lax.fori_loop(0, nch, chunk, 0)

    o_ref[...] = x_ref[...] + _dot_tn(y_scr[...], wout_ref[...])


def _prompt_layer(x, lw, l, cos_t, sin_t, tb):
    n_seq, t_len, _ = x.shape
    assert t_len % tb == 0 and tb % CHUNK == 0
    grid = (n_seq, t_len // tb)

    def per_seq(shape):
        nd = len(shape)
        return pl.BlockSpec((None,) + shape, lambda n, t, _nd=nd: (n,) + (0,) * _nd)

    def per_layer(shape):
        nd = len(shape)
        return pl.BlockSpec((None,) + shape, lambda n, t, _nd=nd: (l,) + (0,) * _nd)

    in_specs = [
        pl.BlockSpec(memory_space=pltpu.SMEM),
        pl.BlockSpec((None, tb, D_MODEL), lambda n, t: (n, t, 0)),
        per_layer((1, D_MODEL)),
        per_layer((D_INT, D_MODEL)),
        per_layer((D_MIX, D_MODEL)),
        per_layer((N_GATES, LANE)), per_layer((N_GATES, LANE)),
        per_layer((A_WIDTH, LANE)),
        per_layer((CONV_W, B_CONV_DIM, LANE)), per_layer((B_CONV_DIM, LANE)),
        per_layer((B_WIDTH, LANE)), per_layer((B_WIDTH, LANE)),
        per_layer((LANE, LANE)), per_layer((LANE, LANE)),
        pl.BlockSpec((C_HD // 2, tb), lambda n, t: (0, t)),
        pl.BlockSpec((C_HD // 2, tb), lambda n, t: (0, t)),
    ]
    out_shape = (
        jax.ShapeDtypeStruct((n_seq, t_len, D_MODEL), F32),
        jax.ShapeDtypeStruct((n_seq, A_HEADS, C_EXT, LANE), F32),
        jax.ShapeDtypeStruct((n_seq, SUBLANE, LANE), F32),
        jax.ShapeDtypeStruct((n_seq, B_HEADS * B_P, B_STATE), F32),
        jax.ShapeDtypeStruct((n_seq, B_CONV_DIM, LANE), F32),
        jax.ShapeDtypeStruct((n_seq, LANE, WINDOW), F32),
        jax.ShapeDtypeStruct((n_seq, LANE, WINDOW), F32),
    )
    out_specs = (
        pl.BlockSpec((None, tb, D_MODEL), lambda n, t: (n, t, 0)),
        per_seq((A_HEADS, C_EXT, LANE)), per_seq((SUBLANE, LANE)),
        per_seq((B_HEADS * B_P, B_STATE)), per_seq((B_CONV_DIM, LANE)),
        per_seq((LANE, WINDOW)), per_seq((LANE, WINDOW)),
    )
    scratch = [
        pltpu.VMEM((D_INT, tb), F32),
        pltpu.VMEM((D_MIX, tb), MXU_DTYPE),
        pltpu.VMEM((2 * C_KV, CHUNK, LANE), MXU_DTYPE),
        pltpu.VMEM((CONV_W - 1, B_CONV_DIM, LANE), F32),
    ]
    return pl.pallas_call(
        functools.partial(_prompt_layer_kernel, tb=tb),
        grid=grid, in_specs=in_specs, out_specs=out_specs, out_shape=out_shape, scratch_shapes=scratch,
        compiler_params=pltpu.CompilerParams(dimension_semantics=("arbitrary", "arbitrary"),
                                             vmem_limit_bytes=VMEM_LIMIT_BYTES),
        name="prompt_layer",
    )(lw["sinks"][l], x, lw["norm_w"], lw["w_t"], lw["w_out"], lw["gb_rep"], lw["al_rep"], lw["a_nw_rep"],
      lw["conv_w_rep"], lw["conv_b_rep"], lw["d_rep"], lw["b_nw_rep"], lw["qn_rep"], lw["kn_rep"], cos_t, sin_t)


def _unpad_prompt_states(cst, mst, hst, convt, kt, vt):
    n_seq = cst.shape[0]
    c_even, c_odd = cst[:, 0::2, :A_V, :HALF], cst[:, 1::2, :A_V, HALF:]
    c_state = jnp.stack([c_even, c_odd], axis=2).reshape(n_seq, A_HEADS, A_V, A_QK)
    n_even, n_odd = cst[:, 0::2, A_V, :HALF], cst[:, 1::2, A_V, HALF:]
    n_state = jnp.stack([n_even, n_odd], axis=2).reshape(n_seq, A_HEADS, A_QK)
    conv = jnp.swapaxes(convt[:, :, LANE - (CONV_W - 1):], 1, 2)
    k1 = jnp.transpose(kt.reshape(n_seq, C_KV, C_HD, WINDOW), (0, 3, 1, 2))
    v1 = jnp.transpose(vt.reshape(n_seq, C_KV, C_HD, WINDOW), (0, 3, 1, 2))
    return (c_state, n_state, mst[:, :A_HEADS, 0], hst.reshape(n_seq, B_HEADS, B_P, B_STATE), conv, k1, v1)


ITEMS_IN_FLIGHT = 8


def _trace_interleaved(item_iter, depth):
    active, exhausted = [], False
    while True:
        while not exhausted and len(active) < depth:
            nxt = next(item_iter, None)
            if nxt is None:
                exhausted = True
            else:
                active.append(nxt)
        if not active:
            return
        still = []
        for gen in active:
            try:
                next(gen)
                still.append(gen)
            except StopIteration:
                pass
        active = still


def _decode_kernel(
        x_ref, nw_ref, wt_ref, wout_ref, gb_ref, al_ref, anw_ref, cw_ref, cb_ref, dsk_ref, bnw_ref,
        qnw_ref, knw_ref, sink_ref, cos_ref, sin_ref,
        c_ref, n_ref, m_ref, s_ref, cv_ref, k_ref, v_ref,
        y_ref, co_ref, no_ref, mo_ref, so_ref, cvo_ref, ko_ref, vo_ref,
        hs_scr, ut_scr, vrow_scr, xbc_scr, xt_scr, qa_scr, knew_scr, rep_scr, h_scr, yt_scr, yall_scr):
    l = pl.program_id(0)
    j = pl.program_id(1)
    nb = x_ref.shape[0]
    hd_a = A_QK

    @pl.when(j == 0)
    def _layer_start():
        @pl.when(l == 0)
        def _():
            hs_scr[...] = x_ref[...]

        xn = _rmsnorm_rows(hs_scr[...], nw_ref[...]).astype(MXU_DTYPE)
        ut_scr[...] = _dot_nt(wt_ref[...], xn)
        yt_scr[...] = jnp.zeros_like(yt_scr)

        pre = ut_scr[OFF_G:OFF_G + N_GATES, :] + gb_ref[...]
        spl, lsg = _softplus_terms(pre)
        ig, lf, dt = pre[:A_HEADS], lsg[A_HEADS:2 * A_HEADS], spl[2 * A_HEADS:]
        d_a = jnp.exp(dt * (-jnp.exp(al_ref[2 * A_HEADS:, :])))
        m0 = m_ref[...]
        m_new = jnp.maximum(lf + m0, ig)
        sp = jnp.exp(lf + m0 - m_new)
        sl = jnp.exp(ig - m_new)
        mo_ref[...] = m_new
        inv_den = []
        for h in range(A_HEADS):
            k_t = ut_scr[OFF_AK + h * hd_a:OFF_AK + (h + 1) * hd_a, :] * (A_QK ** -0.5)
            q_t = ut_scr[OFF_AQ + h * hd_a:OFF_AQ + (h + 1) * hd_a, :]
            n_new = sp[h:h + 1] * n_ref[h] + sl[h:h + 1] * k_t
            no_ref[h] = n_new
            den = jnp.sum(n_new * q_t, axis=0, keepdims=True)
            inv_den.append(1.0 / jnp.maximum(jnp.abs(den), jnp.exp(-m_new[h:h + 1])))

        cos, sin = cos_ref[...], sin_ref[...]
        kn_t = _norm_rope_t(ut_scr[OFF_CK:OFF_CK + LANE, :], knw_ref[...], cos, sin)
        knew_scr[...] = kn_t
        s_new = []
        for pb in range(C_HEADS // 2):
            q_t = _norm_rope_t(ut_scr[OFF_CQ + pb * LANE:OFF_CQ + (pb + 1) * LANE, :], qnw_ref[...], cos, sin)
            qa_scr[pb * LANE:(pb + 1) * LANE, :] = q_t
            for e in range(2):
                g = (2 * pb + e) // (C_HEADS // C_KV)
                s_new.append(jnp.sum(q_t[e * C_HD:(e + 1) * C_HD, :] * kn_t[g * C_HD:(g + 1) * C_HD, :],
                                     axis=0, keepdims=True))

        table = jnp.concatenate([sp, sl, d_a, dt] + s_new + inv_den + [jnp.zeros((LANE - N_REP, nb), F32)], axis=0)
        table_t = table.T
        for r in range(N_REP):
            rep_scr[r] = jnp.broadcast_to(table_t[:, r:r + 1], (nb, LANE))

        ubx = ut_scr[OFF_BXBC:OFF_BXBC + B_CONV_DIM, :].T
        acc = cb_ref[...] + ubx * cw_ref[CONV_W - 1:CONV_W, :]
        for t in range(CONV_W - 1):
            acc = acc + cv_ref[t] * cw_ref[t:t + 1, :]
        xbc = _silu(acc)
        xbc_scr[...] = xbc
        for t in range(CONV_W - 2):
            cvo_ref[t] = cv_ref[t + 1]
        cvo_ref[CONV_W - 2] = ubx
        xt_scr[...] = xbc[:, :B_WIDTH].T
        vrow_scr[...] = ut_scr[OFF_AV:OFF_AV + A_WIDTH, :].T

    base = pl.multiple_of(j * SAMPLE_BLOCK, SAMPLE_BLOCK)
    rows = pl.ds(base, SAMPLE_BLOCK)
    lane_s = lax.broadcasted_iota(jnp.int32, (C_HD, nb), 1)
    lane_k = lax.broadcasted_iota(jnp.int32, (C_HD, WINDOW), 1)
    lane_r = lane_k[:1, :]
    v_rows = vrow_scr[rows, :]
    xbc_rows = xbc_scr[rows, :]
    reps = [rep_scr[r, rows, :] for r in range(N_REP)]
    hpk = C_HEADS // C_KV
    sels = [lane_s == base + i for i in range(SAMPLE_BLOCK)]

    def col(tile, i):
        return jnp.sum(jnp.where(sels[i], tile, 0.0), axis=1, keepdims=True)

    def rep(r, i):
        return reps[r][i:i + 1, :]

    acc = {}

    def mlstm_item(h, i, k_all, q_all):
        kc, qc = col(k_all, i), col(q_all, i)
        yield
        v_row = v_rows[i:i + 1, h * A_V:(h + 1) * A_V]
        c_new = rep(REP_SP + h, i) * c_ref[i, h] + (rep(REP_SL + h, i) * v_row) * kc
        co_ref[i, h] = c_new
        acc[("h", h)].append(jnp.sum(c_new * qc, axis=0, keepdims=True) * rep(REP_IDEN + h, i))
        if i == SAMPLE_BLOCK - 1:
            h_scr[rows, h * A_V:(h + 1) * A_V] = jnp.concatenate(acc.pop(("h", h)), axis=0)

    def ssd_item(hd, i, x_all):
        g = hd // (B_HEADS // B_GROUPS)
        xc = col(x_all, i)
        yield
        b_row = xbc_rows[i:i + 1, B_WIDTH + g * B_STATE:B_WIDTH + (g + 1) * B_STATE]
        c_row = xbc_rows[i:i + 1, B_WIDTH + (B_GROUPS + g) * B_STATE:B_WIDTH + (B_GROUPS + g + 1) * B_STATE]
        h_new = rep(REP_DA + hd, i) * s_ref[i, hd] + (rep(REP_DT + hd, i) * b_row) * xc
        so_ref[i, hd] = h_new
        y_col = jnp.sum(h_new * c_row, axis=1, keepdims=True)
        yield
        acc[("y", hd)] = jnp.where(sels[i], y_col, acc[("y", hd)])
        if i == SAMPLE_BLOCK - 1:
            yt_scr[hd * B_P:(hd + 1) * B_P, :] = yt_scr[hd * B_P:(hd + 1) * B_P, :] + acc.pop(("y", hd))

    def attn_item(g, i, kn_all, vn_all, q_all, sink):
        kc_new, vc_new = col(kn_all, i), col(vn_all, i)
        qcs = [col(q_all[hh], i) for hh in range(hpk)]
        yield
        k_t = k_ref[i, g]
        v_t = v_ref[i, g]
        ko_ref[i, g] = jnp.where(lane_k == WINDOW - 1, kc_new, pltpu.roll(k_t, WINDOW - 1, axis=1))
        vo_ref[i, g] = jnp.where(lane_k == WINDOW - 1, vc_new, pltpu.roll(v_t, WINDOW - 1, axis=1))
        s_rows = [jnp.where(lane_r == 0, rep(REP_SNEW + g * hpk + hh, i), jnp.sum(k_t * qcs[hh], axis=0, keepdims=True))
                  for hh in range(hpk)]
        s = jnp.concatenate(s_rows, axis=0) * (C_HD ** -0.5)
        m = jnp.maximum(jnp.max(s, axis=1, keepdims=True), sink)
        yield
        ex = jnp.exp(s - m)
        den = jnp.sum(ex, axis=1, keepdims=True) + jnp.exp(sink - m)
        yield
        p = ex * (1.0 / den)
        v_eff = jnp.where(lane_k == 0, vc_new, v_t)
        o_cols = [jnp.sum(v_eff * p[hh:hh + 1, :], axis=1, keepdims=True) for hh in range(hpk)]
        yield
        for hh in range(hpk):
            hd = g * hpk + hh
            acc[("o", hd)] = jnp.where(sels[i], o_cols[hh], acc[("o", hd)])
            if i == SAMPLE_BLOCK - 1:
                r0 = B_WIDTH + hd * C_HD
                yt_scr[r0:r0 + C_HD, :] = yt_scr[r0:r0 + C_HD, :] + acc.pop(("o", hd))

    def items():
        for h in range(A_HEADS):
            k_all = ut_scr[OFF_AK + h * hd_a:OFF_AK + (h + 1) * hd_a, :] * (A_QK ** -0.5)
            q_all = ut_scr[OFF_AQ + h * hd_a:OFF_AQ + (h + 1) * hd_a, :]
            acc[("h", h)] = []
            for i in range(SAMPLE_BLOCK):
                yield mlstm_item(h, i, k_all, q_all)
        for hd in range(B_HEADS):
            x_all = xt_scr[hd * B_P:(hd + 1) * B_P, :]
            acc[("y", hd)] = jnp.zeros((B_P, nb), F32)
            for i in range(SAMPLE_BLOCK):
                yield ssd_item(hd, i, x_all)
        for g in range(C_KV):
            kn_all = knew_scr[g * C_HD:(g + 1) * C_HD, :]
            vn_all = ut_scr[OFF_CV + g * C_HD:OFF_CV + (g + 1) * C_HD, :]
            q_all = [qa_scr[(g * hpk + hh) * C_HD:(g * hpk + hh + 1) * C_HD, :] for hh in range(hpk)]
            sink = sink_ref[g * hpk:(g + 1) * hpk, 0:1]
            for hh in range(hpk):
                acc[("o", g * hpk + hh)] = jnp.zeros((C_HD, nb), F32)
            for i in range(SAMPLE_BLOCK):
                yield attn_item(g, i, kn_all, vn_all, q_all, sink)

    _trace_interleaved(items(), ITEMS_IN_FLIGHT)

    @pl.when(j == pl.num_programs(1) - 1)
    def _layer_end():
        for h in range(A_HEADS):
            h_t = h_scr[:, h * A_V:(h + 1) * A_V].T
            hn = h_t * lax.rsqrt(jnp.mean(h_t * h_t, axis=0, keepdims=True) + NORM_EPS) * anw_ref[h * A_V:(h + 1) * A_V, :]
            ao = ut_scr[OFF_AO + h * A_V:OFF_AO + (h + 1) * A_V, :]
            az = ut_scr[OFF_AZ + h * A_V:OFF_AZ + (h + 1) * A_V, :]
            yall_scr[h * A_V:(h + 1) * A_V, :] = (hn * jax.nn.sigmoid(ao) * _silu(az)).astype(yall_scr.dtype)
        gw = B_WIDTH // B_GROUPS
        for g in range(B_GROUPS):
            r = slice(g * gw, (g + 1) * gw)
            y_g = yt_scr[r, :] + dsk_ref[r, :] * xt_scr[r, :]
            gated = y_g * _silu(ut_scr[OFF_BZ + g * gw:OFF_BZ + (g + 1) * gw, :])
            inv = lax.rsqrt(jnp.mean(gated * gated, axis=0, keepdims=True) + NORM_EPS)
            yall_scr[A_WIDTH + g * gw:A_WIDTH + (g + 1) * gw, :] = (gated * inv * bnw_ref[r, :]).astype(yall_scr.dtype)
        yall_scr[A_WIDTH + B_WIDTH:, :] = \
            (yt_scr[B_WIDTH:, :] * _silu(ut_scr[OFF_CZ:OFF_CZ + C_WIDTH, :])).astype(yall_scr.dtype)
        hs_new = hs_scr[...] + _dot_tn(yall_scr[...], wout_ref[...])
        hs_scr[...] = hs_new

        @pl.when(l == pl.num_programs(0) - 1)
        def _():
            y_ref[...] = hs_new


def _decode(x, lw, cos_s, sin_s, c_v, n_v, m_v, s_v, conv_v, k_v, v_v):
    depth, nb = c_v.shape[0], x.shape[0]
    assert nb == LANE, "samples sit on the 128 lanes next to lane-replicated parameters"
    grid = (depth, nb // SAMPLE_BLOCK)

    def per_layer(shape):
        nd = len(shape)
        return pl.BlockSpec((None,) + shape, lambda l, j, _nd=nd: (l,) + (0,) * _nd)

    def per_block(shape):
        nd = len(shape)
        return pl.BlockSpec((None, SAMPLE_BLOCK) + shape, lambda l, j, _nd=nd: (l, j) + (0,) * _nd)

    def weight(shape):
        nd = len(shape)
        return pl.BlockSpec((None,) + shape, lambda l, j, _nd=nd: (l,) + (0,) * _nd, pipeline_mode=pl.Buffered(1))

    fixed = lambda shape: pl.BlockSpec(shape, lambda l, j, _nd=len(shape): (0,) * _nd)
    state_specs = [
        per_block((A_HEADS, A_QK, A_V)), per_layer((A_HEADS, A_QK, nb)), per_layer((A_HEADS, nb)),
        per_block((B_HEADS, B_P, B_STATE)), per_layer((CONV_W - 1, nb, B_CONV_DIM)),
        per_block((C_KV, C_HD, WINDOW)), per_block((C_KV, C_HD, WINDOW)),
    ]
    in_specs = [
        fixed((nb, D_MODEL)),
        per_layer((1, D_MODEL)), weight((D_INT, D_MODEL)), weight((D_MIX, D_MODEL)),
        per_layer((N_GATES, LANE)), per_layer((N_GATES, LANE)), per_layer((A_WIDTH, LANE)),
        per_layer((CONV_W, B_CONV_DIM)), per_layer((1, B_CONV_DIM)),
        per_layer((B_WIDTH, LANE)), per_layer((B_WIDTH, LANE)), per_layer((LANE, LANE)), per_layer((LANE, LANE)),
        per_layer((C_HEADS, LANE)), fixed((C_HD // 2, LANE)), fixed((C_HD // 2, LANE)),
    ] + state_specs
    states = (c_v, n_v, m_v, s_v, conv_v, k_v, v_v)
    out_shape = (jax.ShapeDtypeStruct((nb, D_MODEL), F32),) + tuple(jax.ShapeDtypeStruct(a.shape, F32) for a in states)
    scratch = [
        pltpu.VMEM((nb, D_MODEL), F32),
        pltpu.VMEM((D_INT, nb), F32),
        pltpu.VMEM((nb, A_WIDTH), F32),
        pltpu.VMEM((nb, B_CONV_DIM), F32),
        pltpu.VMEM((B_WIDTH, nb), F32),
        pltpu.VMEM((C_WIDTH, nb), F32),
        pltpu.VMEM((C_KV * C_HD, nb), F32),
        pltpu.VMEM((N_REP, nb, LANE), F32),
        pltpu.VMEM((nb, A_WIDTH), F32),
        pltpu.VMEM((B_WIDTH + C_WIDTH, nb), F32),
        pltpu.VMEM((D_MIX, nb), MXU_DTYPE),
    ]
    return pl.pallas_call(
        _decode_kernel,
        grid=grid, in_specs=in_specs, out_specs=(fixed((nb, D_MODEL)),) + tuple(state_specs),
        out_shape=out_shape, scratch_shapes=scratch,
        compiler_params=pltpu.CompilerParams(dimension_semantics=("arbitrary", "arbitrary"),
                                             vmem_limit_bytes=VMEM_LIMIT_BYTES),
        name="decode",
    )(x, lw["norm_w"], lw["w_t"], lw["w_out"], lw["gb_rep"], lw["al_rep"], lw["a_nw_rep"], lw["conv_w"],
      lw["conv_b"], lw["d_rep"], lw["b_nw_rep"], lw["qn_rep"], lw["kn_rep"], lw["sink_rep"], cos_s, sin_s, *states)


def _prep_weights(norm_w, w_in, a_igate_b, a_fgate_b, a_norm_w, b_conv_w, b_conv_b, b_dt_bias, b_A_log, b_D,
                  b_norm_w, c_qnorm_w, c_knorm_w, c_sinks, w_out):
    depth = w_in.shape[0]
    w_t = jnp.swapaxes(w_in, 1, 2)
    w_t = jnp.concatenate([w_t[:, :_SRC_OFF[5]], w_t[:, _SRC_OFF[7]:_SRC_OFF[9]], w_t[:, _SRC_OFF[10]:],
                           w_t[:, _SRC_OFF[5]:_SRC_OFF[7]], w_t[:, _SRC_OFF[9]:_SRC_OFF[10]]], axis=1)
    gbias = jnp.concatenate([a_igate_b, a_fgate_b, b_dt_bias], axis=-1)
    alog = jnp.concatenate([jnp.zeros((depth, 2 * A_HEADS), b_A_log.dtype), b_A_log], axis=-1)

    def rep(v):
        return jnp.broadcast_to(v.astype(F32)[..., None], v.shape + (LANE,))

    return {
        "w_t": w_t.astype(MXU_DTYPE), "w_out": w_out.astype(MXU_DTYPE),
        "norm_w": norm_w.astype(F32)[:, None, :],
        "gb_rep": rep(gbias), "al_rep": rep(alog),
        "a_nw_rep": rep(a_norm_w), "conv_w_rep": rep(b_conv_w), "conv_b_rep": rep(b_conv_b),
        "conv_w": b_conv_w.astype(F32), "conv_b": b_conv_b.astype(F32)[:, None, :],
        "d_rep": rep(jnp.repeat(b_D, B_P, axis=-1)), "b_nw_rep": rep(b_norm_w),
        "qn_rep": rep(jnp.tile(c_qnorm_w, (1, LANE // C_HD))), "kn_rep": rep(jnp.tile(c_knorm_w, (1, LANE // C_HD))),
        "sinks": c_sinks.astype(F32), "sink_rep": rep(c_sinks),
    }


def _rope_tables_t(pos, width=None):
    half = C_HD // 2
    inv = ROPE_THETA ** (-jnp.arange(half, dtype=F32) / half)
    ang = inv[:, None] * pos.astype(F32)[None, :]
    cos, sin = jnp.cos(ang), jnp.sin(ang)
    if width is not None:
        cos, sin = jnp.broadcast_to(cos, (half, width)), jnp.broadcast_to(sin, (half, width))
    return cos, sin


def _kernel_impl(x_prompt, x_sample, state_mlstm_C, state_mlstm_n, state_mlstm_m, state_ssm, state_conv,
                 cache_k, cache_v, norm_w, w_in, a_igate_b, a_fgate_b, a_norm_w, b_conv_w, b_conv_b,
                 b_dt_bias, b_A_log, b_D, b_norm_w, c_qnorm_w, c_knorm_w, c_sinks, w_out, *, tb):
    depth = w_in.shape[0]
    t_len = x_prompt.shape[1]
    lw = _prep_weights(norm_w, w_in, a_igate_b, a_fgate_b, a_norm_w, b_conv_w, b_conv_b, b_dt_bias, b_A_log,
                       b_D, b_norm_w, c_qnorm_w, c_knorm_w, c_sinks, w_out)
    cos_p, sin_p = _rope_tables_t(jnp.arange(t_len, dtype=jnp.int32))
    hp = x_prompt
    st_prompt = []
    for l in range(depth):
        res = _prompt_layer(hp, lw, l, cos_p, sin_p, tb)
        hp = res[0]
        st_prompt.append(_unpad_prompt_states(*res[1:]))
    p_states = [jnp.stack(t) for t in zip(*st_prompt)]

    assert x_sample.shape[1] == 1
    cos_s, sin_s = _rope_tables_t(PAST_LEN + jnp.arange(1, dtype=jnp.int32), LANE)
    outs = _decode(
        x_sample[:, 0, :], lw, cos_s, sin_s,
        jnp.transpose(state_mlstm_C, (0, 1, 2, 4, 3)), jnp.transpose(state_mlstm_n, (0, 2, 3, 1)),
        jnp.transpose(state_mlstm_m, (0, 2, 1)), state_ssm, jnp.transpose(state_conv, (0, 2, 1, 3)),
        jnp.transpose(cache_k, (0, 1, 3, 4, 2)), jnp.transpose(cache_v, (0, 1, 3, 4, 2)))
    hs, c_o, n_o, m_o, s_o, conv_o, k_o, v_o = outs
    s_states = (jnp.transpose(c_o, (0, 1, 2, 4, 3)), jnp.transpose(n_o, (0, 3, 1, 2)), jnp.transpose(m_o, (0, 2, 1)),
                s_o, jnp.transpose(conv_o, (0, 2, 1, 3)),
                jnp.transpose(k_o, (0, 1, 4, 2, 3)), jnp.transpose(v_o, (0, 1, 4, 2, 3)))
    return (hp, hs[:, None, :], *p_states, *s_states)


def kernel(x_prompt, x_sample, state_mlstm_C, state_mlstm_n, state_mlstm_m, state_ssm, state_conv, cache_k, cache_v, norm_w, w_in, a_igate_b, a_fgate_b, a_norm_w, b_conv_w, b_conv_b, b_dt_bias, b_A_log, b_D, b_norm_w, c_qnorm_w, c_knorm_w, c_sinks, w_out):
    return _kernel_impl(x_prompt, x_sample, state_mlstm_C, state_mlstm_n, state_mlstm_m, state_ssm, state_conv,
                        cache_k, cache_v, norm_w, w_in, a_igate_b, a_fgate_b, a_norm_w, b_conv_w, b_conv_b,
                        b_dt_bias, b_A_log, b_D, b_norm_w, c_qnorm_w, c_knorm_w, c_sinks, w_out, tb=PROMPT_BLOCK)
```

```python
import functools

import jax
import jax.numpy as jnp
import numpy as np
from jax import lax
from jax.experimental import pallas as pl
from jax.experimental.pallas import tpu as pltpu

F32 = jnp.float32
MXU_DTYPE = jnp.bfloat16

D_MODEL = 1024
A_HEADS, A_QK, A_V = 4, 64, 128
A_WIDTH = A_HEADS * A_V
B_HEADS, B_P, B_GROUPS, B_STATE = 8, 64, 2, 128
B_WIDTH = B_HEADS * B_P
CONV_W = 4
B_CONV_DIM = B_WIDTH + 2 * B_GROUPS * B_STATE
C_HEADS, C_KV, C_HD = 8, 2, 64
C_WIDTH = C_HEADS * C_HD
WINDOW = 128
ROPE_THETA = 10000.0
D_MIX = A_WIDTH + B_WIDTH + C_WIDTH
NORM_EPS = 1e-6
PAST_LEN = 8192

LANE = 128
SUBLANE = 8
HALF = LANE // 2

CHUNK = 128
PROMPT_BLOCK = 256
SAMPLE_BLOCK = SUBLANE
N_GATES = 2 * A_HEADS + B_HEADS

OFF_AQ = 0
OFF_AK = OFF_AQ + A_HEADS * A_QK
OFF_AV = OFF_AK + A_HEADS * A_QK
OFF_AO = OFF_AV + A_WIDTH
OFF_AZ = OFF_AO + A_WIDTH
OFF_BZ = OFF_AZ + A_WIDTH
OFF_BXBC = OFF_BZ + B_WIDTH
OFF_CQ = OFF_BXBC + B_CONV_DIM
OFF_CK = OFF_CQ + C_WIDTH
OFF_CV = OFF_CK + C_KV * C_HD
OFF_CZ = OFF_CV + C_KV * C_HD
OFF_G = OFF_CZ + C_WIDTH
D_INT = OFF_G + N_GATES
INPROJ_SITES = (2, 6)
INPROJ_PIECES = len(INPROJ_SITES)
_PIECES_PER_BLOCK = INPROJ_PIECES * (PROMPT_BLOCK // CHUNK)
PIECE_ROWS = -(-D_INT // (_PIECES_PER_BLOCK * 2 * SUBLANE)) * 2 * SUBLANE
D_PAD = PIECE_ROWS * _PIECES_PER_BLOCK
_SRC_SIZES = (A_HEADS * A_QK, A_HEADS * A_QK, A_WIDTH, A_WIDTH, A_WIDTH, A_HEADS, A_HEADS,
              B_WIDTH, B_CONV_DIM, B_HEADS, C_WIDTH, C_KV * C_HD, C_KV * C_HD, C_WIDTH)
_SRC_OFF = np.concatenate([[0], np.cumsum(_SRC_SIZES)]).tolist()

C_EXT = A_V + 2 * SUBLANE
VMEM_LIMIT_BYTES = 56 * 1024 * 1024

REP_SP, REP_SL = 0, A_HEADS
REP_DA, REP_DT = 2 * A_HEADS, 2 * A_HEADS + B_HEADS
REP_SNEW = 2 * A_HEADS + 2 * B_HEADS
REP_IDEN = REP_SNEW + C_HEADS
N_REP = REP_IDEN + A_HEADS


def _dot(a, b):
    return jnp.dot(a, b, preferred_element_type=F32)


def _dot_nt(a, b):
    return lax.dot_general(a, b, (((1,), (1,)), ((), ())), preferred_element_type=F32)


def _dot_tn(a, b):
    return lax.dot_general(a, b, (((0,), (0,)), ((), ())), preferred_element_type=F32)


def _split3(z):
    hi = z.astype(MXU_DTYPE)
    r1 = z - hi.astype(F32)
    mid = r1.astype(MXU_DTYPE)
    lo = (r1 - mid.astype(F32)).astype(MXU_DTYPE)
    return hi, mid, lo


def _softplus_terms(x):
    t = jnp.log1p(jnp.exp(-jnp.abs(x)))
    return jnp.maximum(x, 0.0) + t, jnp.minimum(x, 0.0) - t


def _silu(x):
    return x * jax.nn.sigmoid(x)


def _rmsnorm_rows(x, w):
    return x * lax.rsqrt(jnp.mean(x * x, axis=-1, keepdims=True) + NORM_EPS) * w


def _norm_rope_t(x_t, w_rep, cos, sin):
    half = C_HD // 2
    out = []
    for hh in range(LANE // C_HD):
        xh = x_t[hh * C_HD:(hh + 1) * C_HD, :]
        ms = jnp.mean(xh * xh, axis=0, keepdims=True)
        xn = xh * lax.rsqrt(ms + NORM_EPS) * w_rep[hh * C_HD:(hh + 1) * C_HD, :]
        x1, x2 = xn[:half, :], xn[half:, :]
        out += [x1 * cos - x2 * sin, x2 * cos + x1 * sin]
    return jnp.concatenate(out, axis=0)


def _prompt_layer_kernel(
        sinks_ref,
        x_ref, xnext_ref, nw_ref, wt_ref, wout_ref, gb_ref, al_ref, anw_ref, cw_ref, cb_ref, dsk_ref, bnw_ref,
        qnw_ref, knw_ref, cos_ref, sin_ref,
        o_ref, cst_ref, mst_ref, hst_ref, convt_ref, kt_ref, vt_ref,
        u0_scr, u1_scr, xn_scr, y_scr, kprev_scr, roll_scr, *, tb):
    L = CHUNK
    nch = tb // L
    rows_per_chunk = u0_scr.shape[0] // nch
    t_id = pl.program_id(1)
    slot = t_id % 2

    @pl.when(t_id == 0)
    def _():
        cst_ref[...] = jnp.zeros_like(cst_ref)
        mst_ref[...] = jnp.zeros_like(mst_ref)
        hst_ref[...] = jnp.zeros_like(hst_ref)
        convt_ref[...] = jnp.zeros_like(convt_ref)
        kt_ref[...] = jnp.zeros_like(kt_ref)
        vt_ref[...] = jnp.zeros_like(vt_ref)
        kprev_scr[...] = jnp.zeros_like(kprev_scr)
        roll_scr[...] = jnp.zeros_like(roll_scr)
        u0_scr[...] = _dot_nt(wt_ref[...], _rmsnorm_rows(x_ref[...], nw_ref[...]).astype(MXU_DTYPE))

    xn_scr[...] = _rmsnorm_rows(xnext_ref[...], nw_ref[...]).astype(MXU_DTYPE)
    u_cur, u_next = u0_scr, u1_scr

    row = lax.broadcasted_iota(jnp.int32, (L, L), 0)
    lane = lax.broadcasted_iota(jnp.int32, (L, L), 1)
    lo = lane < HALF
    top = row < B_P
    src_le_t = row <= lane
    tri = jnp.where(lane <= row, 1.0, 0.0).astype(MXU_DTYPE)
    grow = lax.broadcasted_iota(jnp.int32, (N_GATES, L), 0)
    krow = lax.broadcasted_iota(jnp.int32, (2 * L, L), 0)
    klane = lax.broadcasted_iota(jnp.int32, (2 * L, L), 1)
    neg_inf = -jnp.inf
    a_neg = -jnp.exp(al_ref[...])

    def chunk(c, carry):
        cols = pl.ds(pl.multiple_of(c * L, L), L)

        def next_block_piece(site):
            if site not in INPROJ_SITES:
                return
            k = INPROJ_SITES.index(site)
            pr = rows_per_chunk // INPROJ_PIECES
            wrows = pl.ds(pl.multiple_of(c * rows_per_chunk + k * pr, pr), pr)
            u_next[wrows, :] = _dot_nt(wt_ref[wrows, :], xn_scr[...])

        pre_r = u_cur[OFF_G:OFF_G + N_GATES, cols] + gb_ref[...]
        sp_r, ls_r = _softplus_terms(pre_r)
        z_r = jnp.where((grow >= A_HEADS) & (grow < 2 * A_HEADS), ls_r,
                        jnp.where(grow >= 2 * A_HEADS, sp_r * a_neg, 0.0))
        zp = _split3(z_r)
        cum_r = _dot_nt(zp[0], tri) + _dot_nt(zp[1], tri) + _dot_nt(zp[2], tri)
        colf = jnp.concatenate([pre_r, cum_r, jnp.zeros((L - 2 * N_GATES, L), F32)], axis=0).T

        next_block_piece(0)

        for h in range(A_HEADS):
            pb, e = h // 2, h % 2
            if e == 0:
                q_t = u_cur[OFF_AQ + pb * LANE:OFF_AQ + (pb + 1) * LANE, cols].astype(MXU_DTYPE)
                k_p = (u_cur[OFF_AK + pb * LANE:OFF_AK + (pb + 1) * LANE, cols] * (A_QK ** -0.5)).T
            k_m = jnp.where(lo if e == 0 else jnp.logical_not(lo), k_p, 0.0).astype(MXU_DTYPE)
            v_t = u_cur[OFF_AV + h * LANE:OFF_AV + (h + 1) * LANE, cols]
            b_row = cum_r[A_HEADS + h:A_HEADS + h + 1, :]
            i_row = pre_r[h:h + 1, :]
            c_col = colf[:, N_GATES + A_HEADS + h:N_GATES + A_HEADS + h + 1] - colf[:, h:h + 1]
            b_end = b_row[:, L - 1:L]
            m_prev = mst_ref[h:h + 1, 0:1]
            c_prev = cst_ref[h]

            g_row = b_end - b_row + i_row
            m_loc = jnp.max(g_row, axis=1, keepdims=True)
            d_t = jnp.where(src_le_t, b_row - c_col, neg_inf)
            inter = b_row + m_prev
            m_t = jnp.maximum(inter, jnp.max(d_t, axis=0, keepdims=True))
            res = _dot(jnp.concatenate([k_m, c_prev.astype(MXU_DTYPE)], axis=0), q_t)
            s_t = jnp.exp(d_t - m_t) * res[:L, :]
            si = jnp.exp(inter - m_t)
            num = _dot(v_t.astype(MXU_DTYPE), s_t.astype(MXU_DTYPE)) + si * res[L:L + A_V, :]
            den = jnp.sum(s_t, axis=0, keepdims=True) + si * res[L + A_V:L + A_V + 1, :]
            hh = num * (1.0 / jnp.maximum(jnp.abs(den), jnp.exp(-m_t)))
            hn = hh * lax.rsqrt(jnp.mean(hh * hh, axis=0, keepdims=True) + NORM_EPS) * anw_ref[h * LANE:(h + 1) * LANE, :]
            ao = u_cur[OFF_AO + h * LANE:OFF_AO + (h + 1) * LANE, cols]
            az = u_cur[OFF_AZ + h * LANE:OFF_AZ + (h + 1) * LANE, cols]
            y_scr[h * LANE:(h + 1) * LANE, cols] = (hn * jax.nn.sigmoid(ao) * _silu(az)).astype(y_scr.dtype)

            m_new = jnp.maximum(b_end + m_prev, m_loc)
            sp = jnp.exp(b_end + m_prev - m_new)
            sl = jnp.exp(m_loc - m_new)
            w_row = jnp.exp(g_row - m_loc)
            v_ext = jnp.concatenate([v_t, jnp.ones((1, L), F32), jnp.zeros((C_EXT - A_V - 1, L), F32)], axis=0)
            cst_ref[h] = sp * c_prev + sl * _dot((v_ext * w_row).astype(MXU_DTYPE), k_m)
            mst_ref[h:h + 1, :] = jnp.broadcast_to(m_new, (1, LANE))
            next_block_piece(1 + h)

        cur = u_cur[OFF_BXBC:OFF_BXBC + B_CONV_DIM, cols]
        acc = cb_ref[...] + cur * cw_ref[CONV_W - 1]
        for j in range(1, CONV_W):
            rolled = pltpu.roll(cur, j, axis=1)
            acc = acc + jnp.where(lane[:1, :] >= j, rolled, roll_scr[j - 1]) * cw_ref[CONV_W - 1 - j]
            roll_scr[j - 1] = rolled
        convt_ref[...] = cur
        xbc = _silu(acc)

        for g in range(B_GROUPS):
            b_t = xbc[B_WIDTH + g * B_STATE:B_WIDTH + (g + 1) * B_STATE, :]
            c_t = xbc[B_WIDTH + (B_GROUPS + g) * B_STATE:B_WIDTH + (B_GROUPS + g + 1) * B_STATE, :]
            b_m = b_t.T.astype(MXU_DTYPE)
            hpg = B_HEADS // B_GROUPS
            h_prev = hst_ref[g * hpg * B_P:(g + 1) * hpg * B_P, :]
            res = _dot(jnp.concatenate([b_m, h_prev.astype(MXU_DTYPE)], axis=0), c_t.astype(MXU_DTYPE))
            cb_tt = res[:L, :]
            gated = []
            for pj in range(hpg // 2):
                pb = g * (hpg // 2) + pj
                x_t = xbc[pb * LANE:(pb + 1) * LANE, :]
                ys, a_ends, dec_rows, ea_rows = [], [], [], []
                for e in range(2):
                    gi = 2 * A_HEADS + 2 * pb + e
                    a_row = cum_r[gi:gi + 1, :]
                    dt_row = sp_r[gi:gi + 1, :]
                    a_col = colf[:, N_GATES + gi:N_GATES + gi + 1]
                    a_end = a_row[:, L - 1:L]
                    w_t = jnp.exp(jnp.where(src_le_t, a_row - a_col, neg_inf)) * cb_tt
                    xdt = (x_t[e * B_P:(e + 1) * B_P, :] * dt_row).astype(MXU_DTYPE)
                    ys.append(_dot(xdt, w_t.astype(MXU_DTYPE)))
                    a_ends.append(jnp.exp(a_end))
                    dec_rows.append(jnp.exp(a_end - a_row) * dt_row)
                    ea_rows.append(jnp.exp(a_row))
                inter = res[L + pj * LANE:L + (pj + 1) * LANE, :] * jnp.where(top, ea_rows[0], ea_rows[1])
                y_p = jnp.concatenate(ys, axis=0) + inter + dsk_ref[pb * LANE:(pb + 1) * LANE, :] * x_t
                h_loc = _dot((x_t * jnp.where(top, dec_rows[0], dec_rows[1])).astype(MXU_DTYPE), b_m)
                hst_ref[pb * LANE:(pb + 1) * LANE, :] = \
                    jnp.where(top, a_ends[0], a_ends[1]) * h_prev[pj * LANE:(pj + 1) * LANE, :] + h_loc
                bz = u_cur[OFF_BZ + pb * LANE:OFF_BZ + (pb + 1) * LANE, cols]
                gated.append(y_p * _silu(bz))
            ms = sum(jnp.sum(gp * gp, axis=0, keepdims=True) for gp in gated) * (1.0 / (B_WIDTH // B_GROUPS))
            inv = lax.rsqrt(ms + NORM_EPS)
            for pj, gp in enumerate(gated):
                pb = g * (hpg // 2) + pj
                y_scr[A_WIDTH + pb * LANE:A_WIDTH + (pb + 1) * LANE, cols] = \
                    (gp * inv * bnw_ref[pb * LANE:(pb + 1) * LANE, :]).astype(y_scr.dtype)
            next_block_piece(1 + A_HEADS + g)

        cos = cos_ref[:, cols]
        sin = sin_ref[:, cols]
        k_t = _norm_rope_t(u_cur[OFF_CK:OFF_CK + LANE, cols], knw_ref[...], cos, sin)
        v_t = u_cur[OFF_CV:OFF_CV + LANE, cols]
        k_p = k_t.T
        k_sw = pltpu.roll(k_p, HALF, axis=1)
        k_cur = [[jnp.where(lo, k_p, 0.0).astype(MXU_DTYPE), jnp.where(lo, 0.0, k_sw).astype(MXU_DTYPE)],
                 [jnp.where(lo, k_sw, 0.0).astype(MXU_DTYPE), jnp.where(lo, 0.0, k_p).astype(MXU_DTYPE)]]
        shift = jnp.where(jnp.logical_and(t_id == 0, c == 0), 2 * L, 0)
        valid = ((krow < L) & (krow > klane + shift)) | ((krow >= L) & (krow - L <= klane))
        ppg = C_HEADS // 2 // C_KV
        for pb in range(C_HEADS // 2):
            g = pb // ppg
            q_t = _norm_rope_t(u_cur[OFF_CQ + pb * LANE:OFF_CQ + (pb + 1) * LANE, cols], qnw_ref[...], cos, sin)
            keys = jnp.concatenate([kprev_scr[2 * g], k_cur[g][0], kprev_scr[2 * g + 1], k_cur[g][1]], axis=0)
            sc = _dot(keys, q_t.astype(MXU_DTYPE)) * (C_HD ** -0.5)
            v_g = jnp.concatenate([vt_ref[g * C_HD:(g + 1) * C_HD, :], v_t[g * C_HD:(g + 1) * C_HD, :]],
                                  axis=1).astype(MXU_DTYPE)
            outs = []
            for e in range(2):
                s = jnp.where(valid, sc[e * 2 * L:(e + 1) * 2 * L, :], neg_inf)
                sink = sinks_ref[2 * pb + e]
                m = jnp.maximum(jnp.max(s, axis=0, keepdims=True), sink)
                ex = jnp.exp(s - m)
                p = ex * (1.0 / (jnp.sum(ex, axis=0, keepdims=True) + jnp.exp(sink - m)))
                outs.append(_dot(v_g, p.astype(MXU_DTYPE)))
            cz = u_cur[OFF_CZ + pb * LANE:OFF_CZ + (pb + 1) * LANE, cols]
            y_scr[A_WIDTH + B_WIDTH + pb * LANE:A_WIDTH + B_WIDTH + (pb + 1) * LANE, cols] = \
                (jnp.concatenate(outs, axis=0) * _silu(cz)).astype(y_scr.dtype)
            if pb == 1:
                next_block_piece(1 + A_HEADS + B_GROUPS)
        for g in range(C_KV):
            for e in range(2):
                kprev_scr[2 * g + e] = k_cur[g][e]
        kt_ref[...] = k_t
        vt_ref[...] = v_t
        return carry

    def run_chunks(cur, nxt):
        nonlocal u_cur, u_next
        u_cur, u_next = cur, nxt
        lax.fori_loop(0, nch, lambda c, carry: chunk(c, carry), 0)

    @pl.when(slot == 0)
    def _():
        run_chunks(u0_scr, u1_scr)

    @pl.when(slot == 1)
    def _():
        run_chunks(u1_scr, u0_scr)

    o_ref[...] = x_ref[...] + _dot_tn(y_scr[...], wout_ref[...])


def _prompt_layer(x, lw, l, cos_t, sin_t, tb):
    n_seq, t_len, _ = x.shape
    assert t_len % tb == 0 and tb % CHUNK == 0 and (D_PAD // (tb // CHUNK)) % (2 * SUBLANE) == 0
    n_t = t_len // tb
    grid = (n_seq, n_t)

    def per_seq(shape):
        nd = len(shape)
        return pl.BlockSpec((None,) + shape, lambda n, t, _nd=nd: (n,) + (0,) * _nd)

    def per_layer(shape, **kw):
        nd = len(shape)
        return pl.BlockSpec((None,) + shape, lambda n, t, _nd=nd: (l,) + (0,) * _nd, **kw)

    in_specs = [
        pl.BlockSpec(memory_space=pltpu.SMEM),
        pl.BlockSpec((None, tb, D_MODEL), lambda n, t: (n, t, 0)),
        pl.BlockSpec((None, tb, D_MODEL), lambda n, t: (n, jnp.minimum(t + 1, n_t - 1), 0)),
        per_layer((1, D_MODEL)),
        per_layer((D_PAD, D_MODEL), pipeline_mode=pl.Buffered(1)),
        per_layer((D_MIX, D_MODEL), pipeline_mode=pl.Buffered(1)),
        per_layer((N_GATES, LANE)), per_layer((N_GATES, LANE)),
        per_layer((A_WIDTH, LANE)),
        per_layer((CONV_W, B_CONV_DIM, LANE)), per_layer((B_CONV_DIM, LANE)),
        per_layer((B_WIDTH, LANE)), per_layer((B_WIDTH, LANE)),
        per_layer((LANE, LANE)), per_layer((LANE, LANE)),
        pl.BlockSpec((C_HD // 2, tb), lambda n, t: (0, t)),
        pl.BlockSpec((C_HD // 2, tb), lambda n, t: (0, t)),
    ]
    out_shape = (
        jax.ShapeDtypeStruct((n_seq, t_len, D_MODEL), F32),
        jax.ShapeDtypeStruct((n_seq, A_HEADS, C_EXT, LANE), F32),
        jax.ShapeDtypeStruct((n_seq, SUBLANE, LANE), F32),
        jax.ShapeDtypeStruct((n_seq, B_HEADS * B_P, B_STATE), F32),
        jax.ShapeDtypeStruct((n_seq, B_CONV_DIM, LANE), F32),
        jax.ShapeDtypeStruct((n_seq, LANE, WINDOW), F32),
        jax.ShapeDtypeStruct((n_seq, LANE, WINDOW), F32),
    )
    out_specs = (
        pl.BlockSpec((None, tb, D_MODEL), lambda n, t: (n, t, 0)),
        per_seq((A_HEADS, C_EXT, LANE)), per_seq((SUBLANE, LANE)),
        per_seq((B_HEADS * B_P, B_STATE)), per_seq((B_CONV_DIM, LANE)),
        per_seq((LANE, WINDOW)), per_seq((LANE, WINDOW)),
    )
    scratch = [
        pltpu.VMEM((D_PAD, tb), F32),
        pltpu.VMEM((D_PAD, tb), F32),
        pltpu.VMEM((tb, D_MODEL), MXU_DTYPE),
        pltpu.VMEM((D_MIX, tb), MXU_DTYPE),
        pltpu.VMEM((2 * C_KV, CHUNK, LANE), MXU_DTYPE),
        pltpu.VMEM((CONV_W - 1, B_CONV_DIM, LANE), F32),
    ]
    return pl.pallas_call(
        functools.partial(_prompt_layer_kernel, tb=tb),
        grid=grid, in_specs=in_specs, out_specs=out_specs, out_shape=out_shape, scratch_shapes=scratch,
        compiler_params=pltpu.CompilerParams(dimension_semantics=("arbitrary", "arbitrary"),
                                             vmem_limit_bytes=VMEM_LIMIT_BYTES),
        name="prompt_layer",
    )(lw["sinks"][l], x, x, lw["norm_w"], lw["w_t"], lw["w_out"], lw["gb_rep"], lw["al_rep"], lw["a_nw_rep"],
      lw["conv_w_rep"], lw["conv_b_rep"], lw["d_rep"], lw["b_nw_rep"], lw["qn_rep"], lw["kn_rep"], cos_t, sin_t)


def _unpad_prompt_states(cst, mst, hst, convt, kt, vt):
    n_seq = cst.shape[0]
    c_even, c_odd = cst[:, 0::2, :A_V, :HALF], cst[:, 1::2, :A_V, HALF:]
    c_state = jnp.stack([c_even, c_odd], axis=2).reshape(n_seq, A_HEADS, A_V, A_QK)
    n_even, n_odd = cst[:, 0::2, A_V, :HALF], cst[:, 1::2, A_V, HALF:]
    n_state = jnp.stack([n_even, n_odd], axis=2).reshape(n_seq, A_HEADS, A_QK)
    conv = jnp.swapaxes(convt[:, :, LANE - (CONV_W - 1):], 1, 2)
    k1 = jnp.transpose(kt.reshape(n_seq, C_KV, C_HD, WINDOW), (0, 3, 1, 2))
    v1 = jnp.transpose(vt.reshape(n_seq, C_KV, C_HD, WINDOW), (0, 3, 1, 2))
    return (c_state, n_state, mst[:, :A_HEADS, 0], hst.reshape(n_seq, B_HEADS, B_P, B_STATE), conv, k1, v1)


ITEMS_IN_FLIGHT = 8


def _trace_interleaved(item_iter, depth):
    active, exhausted = [], False
    while True:
        while not exhausted and len(active) < depth:
            nxt = next(item_iter, None)
            if nxt is None:
                exhausted = True
            else:
                active.append(nxt)
        if not active:
            return
        still = []
        for gen in active:
            try:
                next(gen)
                still.append(gen)
            except StopIteration:
                pass
        active = still


def _decode_kernel(
        x_ref, nw_ref, wt_ref, wout_ref, gb_ref, al_ref, anw_ref, cw_ref, cb_ref, dsk_ref, bnw_ref,
        qnw_ref, knw_ref, sink_ref, cos_ref, sin_ref,
        c_ref, n_ref, m_ref, s_ref, cv_ref, k_ref, v_ref,
        y_ref, co_ref, no_ref, mo_ref, so_ref, cvo_ref, ko_ref, vo_ref,
        hs_scr, ut_scr, vrow_scr, xbc_scr, xt_scr, qa_scr, knew_scr, rep_scr, h_scr, yt_scr, yall_scr):
    l = pl.program_id(0)
    j = pl.program_id(1)
    nb = x_ref.shape[0]
    hd_a = A_QK

    @pl.when(j == 0)
    def _layer_start():
        @pl.when(l == 0)
        def _():
            hs_scr[...] = x_ref[...]

        xn = _rmsnorm_rows(hs_scr[...], nw_ref[...]).astype(MXU_DTYPE)
        ut_scr[...] = _dot_nt(wt_ref[...], xn)
        yt_scr[...] = jnp.zeros_like(yt_scr)

        pre = ut_scr[OFF_G:OFF_G + N_GATES, :] + gb_ref[...]
        spl, lsg = _softplus_terms(pre)
        ig, lf, dt = pre[:A_HEADS], lsg[A_HEADS:2 * A_HEADS], spl[2 * A_HEADS:]
        d_a = jnp.exp(dt * (-jnp.exp(al_ref[2 * A_HEADS:, :])))
        m0 = m_ref[...]
        m_new = jnp.maximum(lf + m0, ig)
        sp = jnp.exp(lf + m0 - m_new)
        sl = jnp.exp(ig - m_new)
        mo_ref[...] = m_new
        inv_den = []
        for h in range(A_HEADS):
            k_t = ut_scr[OFF_AK + h * hd_a:OFF_AK + (h + 1) * hd_a, :] * (A_QK ** -0.5)
            q_t = ut_scr[OFF_AQ + h * hd_a:OFF_AQ + (h + 1) * hd_a, :]
            n_new = sp[h:h + 1] * n_ref[h] + sl[h:h + 1] * k_t
            no_ref[h] = n_new
            den = jnp.sum(n_new * q_t, axis=0, keepdims=True)
            inv_den.append(1.0 / jnp.maximum(jnp.abs(den), jnp.exp(-m_new[h:h + 1])))

        cos, sin = cos_ref[...], sin_ref[...]
        kn_t = _norm_rope_t(ut_scr[OFF_CK:OFF_CK + LANE, :], knw_ref[...], cos, sin)
        knew_scr[...] = kn_t
        s_new = []
        for pb in range(C_HEADS // 2):
            q_t = _norm_rope_t(ut_scr[OFF_CQ + pb * LANE:OFF_CQ + (pb + 1) * LANE, :], qnw_ref[...], cos, sin)
            qa_scr[pb * LANE:(pb + 1) * LANE, :] = q_t
            for e in range(2):
                g = (2 * pb + e) // (C_HEADS // C_KV)
                s_new.append(jnp.sum(q_t[e * C_HD:(e + 1) * C_HD, :] * kn_t[g * C_HD:(g + 1) * C_HD, :],
                                     axis=0, keepdims=True))

        table = jnp.concatenate([sp, sl, d_a, dt] + s_new + inv_den + [jnp.zeros((LANE - N_REP, nb), F32)], axis=0)
        table_t = table.T
        for r in range(N_REP):
            rep_scr[r] = jnp.broadcast_to(table_t[:, r:r + 1], (nb, LANE))

        ubx = ut_scr[OFF_BXBC:OFF_BXBC + B_CONV_DIM, :].T
        acc = cb_ref[...] + ubx * cw_ref[CONV_W - 1:CONV_W, :]
        for t in range(CONV_W - 1):
            acc = acc + cv_ref[t] * cw_ref[t:t + 1, :]
        xbc = _silu(acc)
        xbc_scr[...] = xbc
        for t in range(CONV_W - 2):
            cvo_ref[t] = cv_ref[t + 1]
        cvo_ref[CONV_W - 2] = ubx
        xt_scr[...] = xbc[:, :B_WIDTH].T
        vrow_scr[...] = ut_scr[OFF_AV:OFF_AV + A_WIDTH, :].T

    base = pl.multiple_of(j * SAMPLE_BLOCK, SAMPLE_BLOCK)
    rows = pl.ds(base, SAMPLE_BLOCK)
    lane_s = lax.broadcasted_iota(jnp.int32, (C_HD, nb), 1)
    lane_k = lax.broadcasted_iota(jnp.int32, (C_HD, WINDOW), 1)
    lane_r = lane_k[:1, :]
    v_rows = vrow_scr[rows, :]
    xbc_rows = xbc_scr[rows, :]
    reps = [rep_scr[r, rows, :] for r in range(N_REP)]
    hpk = C_HEADS // C_KV
    sels = [lane_s == base + i for i in range(SAMPLE_BLOCK)]

    def col(tile, i):
        return jnp.sum(jnp.where(sels[i], tile, 0.0), axis=1, keepdims=True)

    def rep(r, i):
        return reps[r][i:i + 1, :]

    acc = {}

    def mlstm_item(h, i, k_all, q_all):
        kc, qc = col(k_all, i), col(q_all, i)
        yield
        v_row = v_rows[i:i + 1, h * A_V:(h + 1) * A_V]
        c_new = rep(REP_SP + h, i) * c_ref[i, h] + (rep(REP_SL + h, i) * v_row) * kc
        co_ref[i, h] = c_new
        acc[("h", h)].append(jnp.sum(c_new * qc, axis=0, keepdims=True) * rep(REP_IDEN + h, i))
        if i == SAMPLE_BLOCK - 1:
            h_scr[rows, h * A_V:(h + 1) * A_V] = jnp.concatenate(acc.pop(("h", h)), axis=0)

    def ssd_item(hd, i, x_all):
        g = hd // (B_HEADS // B_GROUPS)
        xc = col(x_all, i)
        yield
        b_row = xbc_rows[i:i + 1, B_WIDTH + g * B_STATE:B_WIDTH + (g + 1) * B_STATE]
        c_row = xbc_rows[i:i + 1, B_WIDTH + (B_GROUPS + g) * B_STATE:B_WIDTH + (B_GROUPS + g + 1) * B_STATE]
        h_new = rep(REP_DA + hd, i) * s_ref[i, hd] + (rep(REP_DT + hd, i) * b_row) * xc
        so_ref[i, hd] = h_new
        y_col = jnp.sum(h_new * c_row, axis=1, keepdims=True)
        yield
        acc[("y", hd)] = jnp.where(sels[i], y_col, acc[("y", hd)])
        if i == SAMPLE_BLOCK - 1:
            yt_scr[hd * B_P:(hd + 1) * B_P, :] = yt_scr[hd * B_P:(hd + 1) * B_P, :] + acc.pop(("y", hd))

    def attn_item(g, i, kn_all, vn_all, q_all, sink):
        kc_new, vc_new = col(kn_all, i), col(vn_all, i)
        qcs = [col(q_all[hh], i) for hh in range(hpk)]
        yield
        k_t = k_ref[i, g]
        v_t = v_ref[i, g]
        ko_ref[i, g] = jnp.where(lane_k == WINDOW - 1, kc_new, pltpu.roll(k_t, WINDOW - 1, axis=1))
        vo_ref[i, g] = jnp.where(lane_k == WINDOW - 1, vc_new, pltpu.roll(v_t, WINDOW - 1, axis=1))
        s_rows = [jnp.where(lane_r == 0, rep(REP_SNEW + g * hpk + hh, i), jnp.sum(k_t * qcs[hh], axis=0, keepdims=True))
                  for hh in range(hpk)]
        s = jnp.concatenate(s_rows, axis=0) * (C_HD ** -0.5)
        m = jnp.maximum(jnp.max(s, axis=1, keepdims=True), sink)
        yield
        ex = jnp.exp(s - m)
        den = jnp.sum(ex, axis=1, keepdims=True) + jnp.exp(sink - m)
        yield
        p = ex * (1.0 / den)
        v_eff = jnp.where(lane_k == 0, vc_new, v_t)
        o_cols = [jnp.sum(v_eff * p[hh:hh + 1, :], axis=1, keepdims=True) for hh in range(hpk)]
        yield
        for hh in range(hpk):
            hd = g * hpk + hh
            acc[("o", hd)] = jnp.where(sels[i], o_cols[hh], acc[("o", hd)])
            if i == SAMPLE_BLOCK - 1:
                r0 = B_WIDTH + hd * C_HD
                yt_scr[r0:r0 + C_HD, :] = yt_scr[r0:r0 + C_HD, :] + acc.pop(("o", hd))

    def items():
        for h in range(A_HEADS):
            k_all = ut_scr[OFF_AK + h * hd_a:OFF_AK + (h + 1) * hd_a, :] * (A_QK ** -0.5)
            q_all = ut_scr[OFF_AQ + h * hd_a:OFF_AQ + (h + 1) * hd_a, :]
            acc[("h", h)] = []
            for i in range(SAMPLE_BLOCK):
                yield mlstm_item(h, i, k_all, q_all)
        for hd in range(B_HEADS):
            x_all = xt_scr[hd * B_P:(hd + 1) * B_P, :]
            acc[("y", hd)] = jnp.zeros((B_P, nb), F32)
            for i in range(SAMPLE_BLOCK):
                yield ssd_item(hd, i, x_all)
        for g in range(C_KV):
            kn_all = knew_scr[g * C_HD:(g + 1) * C_HD, :]
            vn_all = ut_scr[OFF_CV + g * C_HD:OFF_CV + (g + 1) * C_HD, :]
            q_all = [qa_scr[(g * hpk + hh) * C_HD:(g * hpk + hh + 1) * C_HD, :] for hh in range(hpk)]
            sink = sink_ref[g * hpk:(g + 1) * hpk, 0:1]
            for hh in range(hpk):
                acc[("o", g * hpk + hh)] = jnp.zeros((C_HD, nb), F32)
            for i in range(SAMPLE_BLOCK):
                yield attn_item(g, i, kn_all, vn_all, q_all, sink)

    _trace_interleaved(items(), ITEMS_IN_FLIGHT)

    @pl.when(j == pl.num_programs(1) - 1)
    def _layer_end():
        for h in range(A_HEADS):
            h_t = h_scr[:, h * A_V:(h + 1) * A_V].T
            hn = h_t * lax.rsqrt(jnp.mean(h_t * h_t, axis=0, keepdims=True) + NORM_EPS) * anw_ref[h * A_V:(h + 1) * A_V, :]
            ao = ut_scr[OFF_AO + h * A_V:OFF_AO + (h + 1) * A_V, :]
            az = ut_scr[OFF_AZ + h * A_V:OFF_AZ + (h + 1) * A_V, :]
            yall_scr[h * A_V:(h + 1) * A_V, :] = (hn * jax.nn.sigmoid(ao) * _silu(az)).astype(yall_scr.dtype)
        gw = B_WIDTH // B_GROUPS
        for g in range(B_GROUPS):
            r = slice(g * gw, (g + 1) * gw)
            y_g = yt_scr[r, :] + dsk_ref[r, :] * xt_scr[r, :]
            gated = y_g * _silu(ut_scr[OFF_BZ + g * gw:OFF_BZ + (g + 1) * gw, :])
            inv = lax.rsqrt(jnp.mean(gated * gated, axis=0, keepdims=True) + NORM_EPS)
            yall_scr[A_WIDTH + g * gw:A_WIDTH + (g + 1) * gw, :] = (gated * inv * bnw_ref[r, :]).astype(yall_scr.dtype)
        yall_scr[A_WIDTH + B_WIDTH:, :] = \
            (yt_scr[B_WIDTH:, :] * _silu(ut_scr[OFF_CZ:OFF_CZ + C_WIDTH, :])).astype(yall_scr.dtype)
        hs_new = hs_scr[...] + _dot_tn(yall_scr[...], wout_ref[...])
        hs_scr[...] = hs_new

        @pl.when(l == pl.num_programs(0) - 1)
        def _():
            y_ref[...] = hs_new


def _decode(x, lw, cos_s, sin_s, c_v, n_v, m_v, s_v, conv_v, k_v, v_v):
    depth, nb = c_v.shape[0], x.shape[0]
    assert nb == LANE, "samples sit on the 128 lanes next to lane-replicated parameters"
    grid = (depth, nb // SAMPLE_BLOCK)

    def per_layer(shape):
        nd = len(shape)
        return pl.BlockSpec((None,) + shape, lambda l, j, _nd=nd: (l,) + (0,) * _nd)

    def per_block(shape):
        nd = len(shape)
        return pl.BlockSpec((None, SAMPLE_BLOCK) + shape, lambda l, j, _nd=nd: (l, j) + (0,) * _nd)

    def weight(shape):
        nd = len(shape)
        return pl.BlockSpec((None,) + shape, lambda l, j, _nd=nd: (l,) + (0,) * _nd, pipeline_mode=pl.Buffered(1))

    fixed = lambda shape: pl.BlockSpec(shape, lambda l, j, _nd=len(shape): (0,) * _nd)
    state_specs = [
        per_block((A_HEADS, A_QK, A_V)), per_layer((A_HEADS, A_QK, nb)), per_layer((A_HEADS, nb)),
        per_block((B_HEADS, B_P, B_STATE)), per_layer((CONV_W - 1, nb, B_CONV_DIM)),
        per_block((C_KV, C_HD, WINDOW)), per_block((C_KV, C_HD, WINDOW)),
    ]
    in_specs = [
        fixed((nb, D_MODEL)),
        per_layer((1, D_MODEL)), weight((D_INT, D_MODEL)), weight((D_MIX, D_MODEL)),
        per_layer((N_GATES, LANE)), per_layer((N_GATES, LANE)), per_layer((A_WIDTH, LANE)),
        per_layer((CONV_W, B_CONV_DIM)), per_layer((1, B_CONV_DIM)),
        per_layer((B_WIDTH, LANE)), per_layer((B_WIDTH, LANE)), per_layer((LANE, LANE)), per_layer((LANE, LANE)),
        per_layer((C_HEADS, LANE)), fixed((C_HD // 2, LANE)), fixed((C_HD // 2, LANE)),
    ] + state_specs
    states = (c_v, n_v, m_v, s_v, conv_v, k_v, v_v)
    out_shape = (jax.ShapeDtypeStruct((nb, D_MODEL), F32),) + tuple(jax.ShapeDtypeStruct(a.shape, F32) for a in states)
    scratch = [
        pltpu.VMEM((nb, D_MODEL), F32),
        pltpu.VMEM((D_INT, nb), F32),
        pltpu.VMEM((nb, A_WIDTH), F32),
        pltpu.VMEM((nb, B_CONV_DIM), F32),
        pltpu.VMEM((B_WIDTH, nb), F32),
        pltpu.VMEM((C_WIDTH, nb), F32),
        pltpu.VMEM((C_KV * C_HD, nb), F32),
        pltpu.VMEM((N_REP, nb, LANE), F32),
        pltpu.VMEM((nb, A_WIDTH), F32),
        pltpu.VMEM((B_WIDTH + C_WIDTH, nb), F32),
        pltpu.VMEM((D_MIX, nb), MXU_DTYPE),
    ]
    return pl.pallas_call(
        _decode_kernel,
        grid=grid, in_specs=in_specs, out_specs=(fixed((nb, D_MODEL)),) + tuple(state_specs),
        out_shape=out_shape, scratch_shapes=scratch,
        compiler_params=pltpu.CompilerParams(dimension_semantics=("arbitrary", "arbitrary"),
                                             vmem_limit_bytes=VMEM_LIMIT_BYTES),
        name="decode",
    )(x, lw["norm_w"], lw["w_t"], lw["w_out"], lw["gb_rep"], lw["al_rep"], lw["a_nw_rep"], lw["conv_w"],
      lw["conv_b"], lw["d_rep"], lw["b_nw_rep"], lw["qn_rep"], lw["kn_rep"], lw["sink_rep"], cos_s, sin_s, *states)


def _prep_weights(norm_w, w_in, a_igate_b, a_fgate_b, a_norm_w, b_conv_w, b_conv_b, b_dt_bias, b_A_log, b_D,
                  b_norm_w, c_qnorm_w, c_knorm_w, c_sinks, w_out):
    depth = w_in.shape[0]
    w_t = jnp.swapaxes(w_in, 1, 2)
    w_t = jnp.concatenate([w_t[:, :_SRC_OFF[5]], w_t[:, _SRC_OFF[7]:_SRC_OFF[9]], w_t[:, _SRC_OFF[10]:],
                           w_t[:, _SRC_OFF[5]:_SRC_OFF[7]], w_t[:, _SRC_OFF[9]:_SRC_OFF[10]],
                           jnp.zeros((depth, D_PAD - D_INT, D_MODEL), w_in.dtype)], axis=1)
    gbias = jnp.concatenate([a_igate_b, a_fgate_b, b_dt_bias], axis=-1)
    alog = jnp.concatenate([jnp.zeros((depth, 2 * A_HEADS), b_A_log.dtype), b_A_log], axis=-1)

    def rep(v):
        return jnp.broadcast_to(v.astype(F32)[..., None], v.shape + (LANE,))

    return {
        "w_t": w_t.astype(MXU_DTYPE), "w_out": w_out.astype(MXU_DTYPE),
        "norm_w": norm_w.astype(F32)[:, None, :],
        "gb_rep": rep(gbias), "al_rep": rep(alog),
        "a_nw_rep": rep(a_norm_w), "conv_w_rep": rep(b_conv_w), "conv_b_rep": rep(b_conv_b),
        "conv_w": b_conv_w.astype(F32), "conv_b": b_conv_b.astype(F32)[:, None, :],
        "d_rep": rep(jnp.repeat(b_D, B_P, axis=-1)), "b_nw_rep": rep(b_norm_w),
        "qn_rep": rep(jnp.tile(c_qnorm_w, (1, LANE // C_HD))), "kn_rep": rep(jnp.tile(c_knorm_w, (1, LANE // C_HD))),
        "sinks": c_sinks.astype(F32), "sink_rep": rep(c_sinks),
    }


def _rope_tables_t(pos, width=None):
    half = C_HD // 2
    inv = ROPE_THETA ** (-jnp.arange(half, dtype=F32) / half)
    ang = inv[:, None] * pos.astype(F32)[None, :]
    cos, sin = jnp.cos(ang), jnp.sin(ang)
    if width is not None:
        cos, sin = jnp.broadcast_to(cos, (half, width)), jnp.broadcast_to(sin, (half, width))
    return cos, sin


def _kernel_impl(x_prompt, x_sample, state_mlstm_C, state_mlstm_n, state_mlstm_m, state_ssm, state_conv,
                 cache_k, cache_v, norm_w, w_in, a_igate_b, a_fgate_b, a_norm_w, b_conv_w, b_conv_b,
                 b_dt_bias, b_A_log, b_D, b_norm_w, c_qnorm_w, c_knorm_w, c_sinks, w_out, *, tb):
    depth = w_in.shape[0]
    t_len = x_prompt.shape[1]
    lw = _prep_weights(norm_w, w_in, a_igate_b, a_fgate_b, a_norm_w, b_conv_w, b_conv_b, b_dt_bias, b_A_log,
                       b_D, b_norm_w, c_qnorm_w, c_knorm_w, c_sinks, w_out)
    cos_p, sin_p = _rope_tables_t(jnp.arange(t_len, dtype=jnp.int32))
    hp = x_prompt
    st_prompt = []
    for l in range(depth):
        res = _prompt_layer(hp, lw, l, cos_p, sin_p, tb)
        hp = res[0]
        st_prompt.append(_unpad_prompt_states(*res[1:]))
    p_states = [jnp.stack(t) for t in zip(*st_prompt)]

    assert x_sample.shape[1] == 1
    cos_s, sin_s = _rope_tables_t(PAST_LEN + jnp.arange(1, dtype=jnp.int32), LANE)
    outs = _decode(
        x_sample[:, 0, :], lw, cos_s, sin_s,
        jnp.transpose(state_mlstm_C, (0, 1, 2, 4, 3)), jnp.transpose(state_mlstm_n, (0, 2, 3, 1)),
        jnp.transpose(state_mlstm_m, (0, 2, 1)), state_ssm, jnp.transpose(state_conv, (0, 2, 1, 3)),
        jnp.transpose(cache_k, (0, 1, 3, 4, 2)), jnp.transpose(cache_v, (0, 1, 3, 4, 2)))
    hs, c_o, n_o, m_o, s_o, conv_o, k_o, v_o = outs
    s_states = (jnp.transpose(c_o, (0, 1, 2, 4, 3)), jnp.transpose(n_o, (0, 3, 1, 2)), jnp.transpose(m_o, (0, 2, 1)),
                s_o, jnp.transpose(conv_o, (0, 2, 1, 3)),
                jnp.transpose(k_o, (0, 1, 4, 2, 3)), jnp.transpose(v_o, (0, 1, 4, 2, 3)))
    return (hp, hs[:, None, :], *p_states, *s_states)


def kernel(x_prompt, x_sample, state_mlstm_C, state_mlstm_n, state_mlstm_m, state_ssm, state_conv, cache_k, cache_v, norm_w, w_in, a_igate_b, a_fgate_b, a_norm_w, b_conv_w, b_conv_b, b_dt_bias, b_A_log, b_D, b_norm_w, c_qnorm_w, c_knorm_w, c_sinks, w_out):
    return _kernel_impl(x_prompt, x_sample, state_mlstm_C, state_mlstm_n, state_mlstm_m, state_ssm, state_conv,
                        cache_k, cache_v, norm_w, w_in, a_igate_b, a_fgate_b, a_norm_w, b_conv_w, b_conv_b,
                        b_dt_bias, b_A_log, b_D, b_norm_w, c_qnorm_w, c_knorm_w, c_sinks, w_out, tb=PROMPT_BLOCK)
```

```python
import functools

import jax
import jax.numpy as jnp
import numpy as np
from jax import lax
from jax.experimental import pallas as pl
from jax.experimental.pallas import tpu as pltpu

F32 = jnp.float32
MXU_DTYPE = jnp.bfloat16

D_MODEL = 1024
A_HEADS, A_QK, A_V = 4, 64, 128
A_WIDTH = A_HEADS * A_V
B_HEADS, B_P, B_GROUPS, B_STATE = 8, 64, 2, 128
B_WIDTH = B_HEADS * B_P
CONV_W = 4
B_CONV_DIM = B_WIDTH + 2 * B_GROUPS * B_STATE
C_HEADS, C_KV, C_HD = 8, 2, 64
C_WIDTH = C_HEADS * C_HD
WINDOW = 128
ROPE_THETA = 10000.0
D_MIX = A_WIDTH + B_WIDTH + C_WIDTH
NORM_EPS = 1e-6
PAST_LEN = 8192

LANE = 128
SUBLANE = 8
HALF = LANE // 2

CHUNK = 128
PROMPT_BLOCK = 256
SAMPLE_BLOCK = SUBLANE
N_GATES = 2 * A_HEADS + B_HEADS

OFF_AQ = 0
OFF_AK = OFF_AQ + A_HEADS * A_QK
OFF_AV = OFF_AK + A_HEADS * A_QK
OFF_AO = OFF_AV + A_WIDTH
OFF_AZ = OFF_AO + A_WIDTH
OFF_BZ = OFF_AZ + A_WIDTH
OFF_BXBC = OFF_BZ + B_WIDTH
OFF_CQ = OFF_BXBC + B_CONV_DIM
OFF_CK = OFF_CQ + C_WIDTH
OFF_CV = OFF_CK + C_KV * C_HD
OFF_CZ = OFF_CV + C_KV * C_HD
OFF_G = OFF_CZ + C_WIDTH
D_INT = OFF_G + N_GATES
INPROJ_SITES = (2, 6)
INPROJ_PIECES = len(INPROJ_SITES)
PROMPT_ITEMS_IN_FLIGHT = 2
_PIECES_PER_BLOCK = INPROJ_PIECES * (PROMPT_BLOCK // CHUNK)
PIECE_ROWS = -(-D_INT // (_PIECES_PER_BLOCK * 2 * SUBLANE)) * 2 * SUBLANE
D_PAD = PIECE_ROWS * _PIECES_PER_BLOCK
_SRC_SIZES = (A_HEADS * A_QK, A_HEADS * A_QK, A_WIDTH, A_WIDTH, A_WIDTH, A_HEADS, A_HEADS,
              B_WIDTH, B_CONV_DIM, B_HEADS, C_WIDTH, C_KV * C_HD, C_KV * C_HD, C_WIDTH)
_SRC_OFF = np.concatenate([[0], np.cumsum(_SRC_SIZES)]).tolist()

C_EXT = A_V + 2 * SUBLANE
VMEM_LIMIT_BYTES = 56 * 1024 * 1024

REP_SP, REP_SL = 0, A_HEADS
REP_DA, REP_DT = 2 * A_HEADS, 2 * A_HEADS + B_HEADS
REP_SNEW = 2 * A_HEADS + 2 * B_HEADS
REP_IDEN = REP_SNEW + C_HEADS
N_REP = REP_IDEN + A_HEADS


def _dot(a, b):
    return jnp.dot(a, b, preferred_element_type=F32)


def _dot_nt(a, b):
    return lax.dot_general(a, b, (((1,), (1,)), ((), ())), preferred_element_type=F32)


def _dot_tn(a, b):
    return lax.dot_general(a, b, (((0,), (0,)), ((), ())), preferred_element_type=F32)


def _split3(z):
    hi = z.astype(MXU_DTYPE)
    r1 = z - hi.astype(F32)
    mid = r1.astype(MXU_DTYPE)
    lo = (r1 - mid.astype(F32)).astype(MXU_DTYPE)
    return hi, mid, lo


def _softplus_terms(x):
    t = jnp.log1p(jnp.exp(-jnp.abs(x)))
    return jnp.maximum(x, 0.0) + t, jnp.minimum(x, 0.0) - t


def _silu(x):
    return x * jax.nn.sigmoid(x)


def _rmsnorm_rows(x, w):
    return x * lax.rsqrt(jnp.mean(x * x, axis=-1, keepdims=True) + NORM_EPS) * w


def _norm_rope_t(x_t, w_rep, cos, sin):
    half = C_HD // 2
    out = []
    for hh in range(LANE // C_HD):
        xh = x_t[hh * C_HD:(hh + 1) * C_HD, :]
        ms = jnp.mean(xh * xh, axis=0, keepdims=True)
        xn = xh * lax.rsqrt(ms + NORM_EPS) * w_rep[hh * C_HD:(hh + 1) * C_HD, :]
        x1, x2 = xn[:half, :], xn[half:, :]
        out += [x1 * cos - x2 * sin, x2 * cos + x1 * sin]
    return jnp.concatenate(out, axis=0)


def _prompt_layer_kernel(
        sinks_ref,
        x_ref, xnext_ref, nw_ref, wt_ref, wout_ref, gb_ref, al_ref, anw_ref, cw_ref, cb_ref, dsk_ref, bnw_ref,
        qnw_ref, knw_ref, cos_ref, sin_ref,
        o_ref, cst_ref, mst_ref, hst_ref, convt_ref, kt_ref, vt_ref,
        u0_scr, u1_scr, xn_scr, y_scr, kprev_scr, roll_scr, *, tb):
    L = CHUNK
    nch = tb // L
    rows_per_chunk = u0_scr.shape[0] // nch
    t_id = pl.program_id(1)
    slot = t_id % 2

    @pl.when(t_id == 0)
    def _():
        cst_ref[...] = jnp.zeros_like(cst_ref)
        mst_ref[...] = jnp.zeros_like(mst_ref)
        hst_ref[...] = jnp.zeros_like(hst_ref)
        convt_ref[...] = jnp.zeros_like(convt_ref)
        kt_ref[...] = jnp.zeros_like(kt_ref)
        vt_ref[...] = jnp.zeros_like(vt_ref)
        kprev_scr[...] = jnp.zeros_like(kprev_scr)
        roll_scr[...] = jnp.zeros_like(roll_scr)
        u0_scr[...] = _dot_nt(wt_ref[...], _rmsnorm_rows(x_ref[...], nw_ref[...]).astype(MXU_DTYPE))

    xn_scr[...] = _rmsnorm_rows(xnext_ref[...], nw_ref[...]).astype(MXU_DTYPE)
    u_cur, u_next = u0_scr, u1_scr

    row = lax.broadcasted_iota(jnp.int32, (L, L), 0)
    lane = lax.broadcasted_iota(jnp.int32, (L, L), 1)
    lo = lane < HALF
    top = row < B_P
    src_le_t = row <= lane
    tri = jnp.where(lane <= row, 1.0, 0.0).astype(MXU_DTYPE)
    grow = lax.broadcasted_iota(jnp.int32, (N_GATES, L), 0)
    krow = lax.broadcasted_iota(jnp.int32, (2 * L, L), 0)
    klane = lax.broadcasted_iota(jnp.int32, (2 * L, L), 1)
    neg_inf = -jnp.inf
    a_neg = -jnp.exp(al_ref[...])

    def chunk(c, carry):
        cols = pl.ds(pl.multiple_of(c * L, L), L)

        def next_block_piece(site):
            if site not in INPROJ_SITES:
                return
            k = INPROJ_SITES.index(site)
            pr = rows_per_chunk // INPROJ_PIECES
            wrows = pl.ds(pl.multiple_of(c * rows_per_chunk + k * pr, pr), pr)
            u_next[wrows, :] = _dot_nt(wt_ref[wrows, :], xn_scr[...])

        pre_r = u_cur[OFF_G:OFF_G + N_GATES, cols] + gb_ref[...]
        sp_r, ls_r = _softplus_terms(pre_r)
        z_r = jnp.where((grow >= A_HEADS) & (grow < 2 * A_HEADS), ls_r,
                        jnp.where(grow >= 2 * A_HEADS, sp_r * a_neg, 0.0))
        zp = _split3(z_r)
        cum_r = _dot_nt(zp[0], tri) + _dot_nt(zp[1], tri) + _dot_nt(zp[2], tri)
        colf = jnp.concatenate([pre_r, cum_r, jnp.zeros((L - 2 * N_GATES, L), F32)], axis=0).T

        def piece_item(site):
            next_block_piece(site)
            return
            yield

        def mlstm_head(h, q_t, k_p):
            e = h % 2
            k_m = jnp.where(lo if e == 0 else jnp.logical_not(lo), k_p, 0.0).astype(MXU_DTYPE)
            v_t = u_cur[OFF_AV + h * LANE:OFF_AV + (h + 1) * LANE, cols]
            b_row = cum_r[A_HEADS + h:A_HEADS + h + 1, :]
            i_row = pre_r[h:h + 1, :]
            c_col = colf[:, N_GATES + A_HEADS + h:N_GATES + A_HEADS + h + 1] - colf[:, h:h + 1]
            b_end = b_row[:, L - 1:L]
            m_prev = mst_ref[h:h + 1, 0:1]
            c_prev = cst_ref[h]

            g_row = b_end - b_row + i_row
            m_loc = jnp.max(g_row, axis=1, keepdims=True)
            d_t = jnp.where(src_le_t, b_row - c_col, neg_inf)
            inter = b_row + m_prev
            m_t = jnp.maximum(inter, jnp.max(d_t, axis=0, keepdims=True))
            res = _dot(jnp.concatenate([k_m, c_prev.astype(MXU_DTYPE)], axis=0), q_t)
            w_row = jnp.exp(g_row - m_loc)
            v_ext = jnp.concatenate([v_t, jnp.ones((1, L), F32), jnp.zeros((C_EXT - A_V - 1, L), F32)], axis=0)
            c_loc = _dot((v_ext * w_row).astype(MXU_DTYPE), k_m)
            e_t = jnp.exp(d_t - m_t)
            si = jnp.exp(inter - m_t)
            yield
            s_t = e_t * res[:L, :]
            sv = _dot(v_t.astype(MXU_DTYPE), s_t.astype(MXU_DTYPE))
            yield
            num = sv + si * res[L:L + A_V, :]
            den = jnp.sum(s_t, axis=0, keepdims=True) + si * res[L + A_V:L + A_V + 1, :]
            hh = num * (1.0 / jnp.maximum(jnp.abs(den), jnp.exp(-m_t)))
            hn = hh * lax.rsqrt(jnp.mean(hh * hh, axis=0, keepdims=True) + NORM_EPS) * anw_ref[h * LANE:(h + 1) * LANE, :]
            ao = u_cur[OFF_AO + h * LANE:OFF_AO + (h + 1) * LANE, cols]
            az = u_cur[OFF_AZ + h * LANE:OFF_AZ + (h + 1) * LANE, cols]
            y_scr[h * LANE:(h + 1) * LANE, cols] = (hn * jax.nn.sigmoid(ao) * _silu(az)).astype(y_scr.dtype)

            m_new = jnp.maximum(b_end + m_prev, m_loc)
            sp = jnp.exp(b_end + m_prev - m_new)
            sl = jnp.exp(m_loc - m_new)
            cst_ref[h] = sp * c_prev + sl * c_loc
            mst_ref[h:h + 1, :] = jnp.broadcast_to(m_new, (1, LANE))

        def conv_silu():
            cur = u_cur[OFF_BXBC:OFF_BXBC + B_CONV_DIM, cols]
            acc = cb_ref[...] + cur * cw_ref[CONV_W - 1]
            for j in range(1, CONV_W):
                rolled = pltpu.roll(cur, j, axis=1)
                acc = acc + jnp.where(lane[:1, :] >= j, rolled, roll_scr[j - 1]) * cw_ref[CONV_W - 1 - j]
                roll_scr[j - 1] = rolled
            convt_ref[...] = cur
            return _silu(acc)

        def ssd_group(g, xbc):
            b_t = xbc[B_WIDTH + g * B_STATE:B_WIDTH + (g + 1) * B_STATE, :]
            c_t = xbc[B_WIDTH + (B_GROUPS + g) * B_STATE:B_WIDTH + (B_GROUPS + g + 1) * B_STATE, :]
            b_m = b_t.T.astype(MXU_DTYPE)
            hpg = B_HEADS // B_GROUPS
            h_prev = hst_ref[g * hpg * B_P:(g + 1) * hpg * B_P, :]
            yield
            res = _dot(jnp.concatenate([b_m, h_prev.astype(MXU_DTYPE)], axis=0), c_t.astype(MXU_DTYPE))
            x_ts, h_locs, rows_of = [], [], []
            for pj in range(hpg // 2):
                pb = g * (hpg // 2) + pj
                x_t = xbc[pb * LANE:(pb + 1) * LANE, :]
                a_ends, dec_rows, ea_rows, a_rows, a_cols, dt_rows = [], [], [], [], [], []
                for e in range(2):
                    gi = 2 * A_HEADS + 2 * pb + e
                    a_row = cum_r[gi:gi + 1, :]
                    dt_row = sp_r[gi:gi + 1, :]
                    a_end = a_row[:, L - 1:L]
                    a_rows.append(a_row)
                    a_cols.append(colf[:, N_GATES + gi:N_GATES + gi + 1])
                    dt_rows.append(dt_row)
                    a_ends.append(jnp.exp(a_end))
                    dec_rows.append(jnp.exp(a_end - a_row) * dt_row)
                    ea_rows.append(jnp.exp(a_row))
                h_locs.append(_dot((x_t * jnp.where(top, dec_rows[0], dec_rows[1])).astype(MXU_DTYPE), b_m))
                x_ts.append(x_t)
                rows_of.append((a_ends, ea_rows, a_rows, a_cols, dt_rows))
            yield
            cb_tt = res[:L, :]
            ys_all = []
            for pj in range(hpg // 2):
                a_ends, ea_rows, a_rows, a_cols, dt_rows = rows_of[pj]
                ys = []
                for e in range(2):
                    w_t = jnp.exp(jnp.where(src_le_t, a_rows[e] - a_cols[e], neg_inf)) * cb_tt
                    xdt = (x_ts[pj][e * B_P:(e + 1) * B_P, :] * dt_rows[e]).astype(MXU_DTYPE)
                    ys.append(_dot(xdt, w_t.astype(MXU_DTYPE)))
                ys_all.append(ys)
            yield
            gated = []
            for pj in range(hpg // 2):
                pb = g * (hpg // 2) + pj
                a_ends, ea_rows, a_rows, a_cols, dt_rows = rows_of[pj]
                inter = res[L + pj * LANE:L + (pj + 1) * LANE, :] * jnp.where(top, ea_rows[0], ea_rows[1])
                y_p = jnp.concatenate(ys_all[pj], axis=0) + inter + dsk_ref[pb * LANE:(pb + 1) * LANE, :] * x_ts[pj]
                hst_ref[pb * LANE:(pb + 1) * LANE, :] = \
                    jnp.where(top, a_ends[0], a_ends[1]) * h_prev[pj * LANE:(pj + 1) * LANE, :] + h_locs[pj]
                bz = u_cur[OFF_BZ + pb * LANE:OFF_BZ + (pb + 1) * LANE, cols]
                gated.append(y_p * _silu(bz))
            ms = sum(jnp.sum(gp * gp, axis=0, keepdims=True) for gp in gated) * (1.0 / (B_WIDTH // B_GROUPS))
            inv = lax.rsqrt(ms + NORM_EPS)
            for pj, gp in enumerate(gated):
                pb = g * (hpg // 2) + pj
                y_scr[A_WIDTH + pb * LANE:A_WIDTH + (pb + 1) * LANE, cols] = \
                    (gp * inv * bnw_ref[pb * LANE:(pb + 1) * LANE, :]).astype(y_scr.dtype)

        cos = cos_ref[:, cols]
        sin = sin_ref[:, cols]
        ppg = C_HEADS // 2 // C_KV
        shared = {}

        def swa_keys():
            k_t = _norm_rope_t(u_cur[OFF_CK:OFF_CK + LANE, cols], knw_ref[...], cos, sin)
            v_t = u_cur[OFF_CV:OFF_CV + LANE, cols]
            k_p = k_t.T
            k_sw = pltpu.roll(k_p, HALF, axis=1)
            k_cur = [[jnp.where(lo, k_p, 0.0).astype(MXU_DTYPE), jnp.where(lo, 0.0, k_sw).astype(MXU_DTYPE)],
                     [jnp.where(lo, k_sw, 0.0).astype(MXU_DTYPE), jnp.where(lo, 0.0, k_p).astype(MXU_DTYPE)]]
            shift = jnp.where(jnp.logical_and(t_id == 0, c == 0), 2 * L, 0)
            valid = ((krow < L) & (krow > klane + shift)) | ((krow >= L) & (krow - L <= klane))
            shared.update(k_t=k_t, v_t=v_t, k_cur=k_cur, valid=valid)

        def swa_pair(pb):
            g = pb // ppg
            k_cur, v_t, valid = shared["k_cur"], shared["v_t"], shared["valid"]
            q_t = _norm_rope_t(u_cur[OFF_CQ + pb * LANE:OFF_CQ + (pb + 1) * LANE, cols], qnw_ref[...], cos, sin)
            keys = jnp.concatenate([kprev_scr[2 * g], k_cur[g][0], kprev_scr[2 * g + 1], k_cur[g][1]], axis=0)
            sc = _dot(keys, q_t.astype(MXU_DTYPE)) * (C_HD ** -0.5)
            v_g = jnp.concatenate([vt_ref[g * C_HD:(g + 1) * C_HD, :], v_t[g * C_HD:(g + 1) * C_HD, :]],
                                  axis=1).astype(MXU_DTYPE)
            yield
            outs = []
            for e in range(2):
                s = jnp.where(valid, sc[e * 2 * L:(e + 1) * 2 * L, :], neg_inf)
                sink = sinks_ref[2 * pb + e]
                m = jnp.maximum(jnp.max(s, axis=0, keepdims=True), sink)
                ex = jnp.exp(s - m)
                p = ex * (1.0 / (jnp.sum(ex, axis=0, keepdims=True) + jnp.exp(sink - m)))
                outs.append(_dot(v_g, p.astype(MXU_DTYPE)))
            yield
            cz = u_cur[OFF_CZ + pb * LANE:OFF_CZ + (pb + 1) * LANE, cols]
            y_scr[A_WIDTH + B_WIDTH + pb * LANE:A_WIDTH + B_WIDTH + (pb + 1) * LANE, cols] = \
                (jnp.concatenate(outs, axis=0) * _silu(cz)).astype(y_scr.dtype)

        def items():
            for h in range(A_HEADS):
                if h % 2 == 0:
                    pb = h // 2
                    q_t = u_cur[OFF_AQ + pb * LANE:OFF_AQ + (pb + 1) * LANE, cols].astype(MXU_DTYPE)
                    k_p = (u_cur[OFF_AK + pb * LANE:OFF_AK + (pb + 1) * LANE, cols] * (A_QK ** -0.5)).T
                yield mlstm_head(h, q_t, k_p)
                yield piece_item(1 + h)
            xbc = conv_silu()
            for g in range(B_GROUPS):
                yield ssd_group(g, xbc)
                yield piece_item(1 + A_HEADS + g)
            swa_keys()
            for pb in range(C_HEADS // 2):
                yield swa_pair(pb)
                if pb == 1:
                    yield piece_item(1 + A_HEADS + B_GROUPS)

        next_block_piece(0)
        _trace_interleaved(items(), PROMPT_ITEMS_IN_FLIGHT)
        for g in range(C_KV):
            for e in range(2):
                kprev_scr[2 * g + e] = shared["k_cur"][g][e]
        kt_ref[...] = shared["k_t"]
        vt_ref[...] = shared["v_t"]
        return carry

    def run_chunks(cur, nxt):
        nonlocal u_cur, u_next
        u_cur, u_next = cur, nxt
        lax.fori_loop(0, nch, lambda c, carry: chunk(c, carry), 0)

    @pl.when(slot == 0)
    def _():
        run_chunks(u0_scr, u1_scr)

    @pl.when(slot == 1)
    def _():
        run_chunks(u1_scr, u0_scr)

    o_ref[...] = x_ref[...] + _dot_tn(y_scr[...], wout_ref[...])


def _prompt_layer(x, lw, l, cos_t, sin_t, tb):
    n_seq, t_len, _ = x.shape
    assert t_len % tb == 0 and tb % CHUNK == 0 and (D_PAD // (tb // CHUNK)) % (2 * SUBLANE) == 0
    n_t = t_len // tb
    grid = (n_seq, n_t)

    def per_seq(shape):
        nd = len(shape)
        return pl.BlockSpec((None,) + shape, lambda n, t, _nd=nd: (n,) + (0,) * _nd)

    def per_layer(shape, **kw):
        nd = len(shape)
        return pl.BlockSpec((None,) + shape, lambda n, t, _nd=nd: (l,) + (0,) * _nd, **kw)

    in_specs = [
        pl.BlockSpec(memory_space=pltpu.SMEM),
        pl.BlockSpec((None, tb, D_MODEL), lambda n, t: (n, t, 0)),
        pl.BlockSpec((None, tb, D_MODEL), lambda n, t: (n, jnp.minimum(t + 1, n_t - 1), 0)),
        per_layer((1, D_MODEL)),
        per_layer((D_PAD, D_MODEL), pipeline_mode=pl.Buffered(1)),
        per_layer((D_MIX, D_MODEL), pipeline_mode=pl.Buffered(1)),
        per_layer((N_GATES, LANE)), per_layer((N_GATES, LANE)),
        per_layer((A_WIDTH, LANE)),
        per_layer((CONV_W, B_CONV_DIM, LANE)), per_layer((B_CONV_DIM, LANE)),
        per_layer((B_WIDTH, LANE)), per_layer((B_WIDTH, LANE)),
        per_layer((LANE, LANE)), per_layer((LANE, LANE)),
        pl.BlockSpec((C_HD // 2, tb), lambda n, t: (0, t)),
        pl.BlockSpec((C_HD // 2, tb), lambda n, t: (0, t)),
    ]
    out_shape = (
        jax.ShapeDtypeStruct((n_seq, t_len, D_MODEL), F32),
        jax.ShapeDtypeStruct((n_seq, A_HEADS, C_EXT, LANE), F32),
        jax.ShapeDtypeStruct((n_seq, SUBLANE, LANE), F32),
        jax.ShapeDtypeStruct((n_seq, B_HEADS * B_P, B_STATE), F32),
        jax.ShapeDtypeStruct((n_seq, B_CONV_DIM, LANE), F32),
        jax.ShapeDtypeStruct((n_seq, LANE, WINDOW), F32),
        jax.ShapeDtypeStruct((n_seq, LANE, WINDOW), F32),
    )
    out_specs = (
        pl.BlockSpec((None, tb, D_MODEL), lambda n, t: (n, t, 0)),
        per_seq((A_HEADS, C_EXT, LANE)), per_seq((SUBLANE, LANE)),
        per_seq((B_HEADS * B_P, B_STATE)), per_seq((B_CONV_DIM, LANE)),
        per_seq((LANE, WINDOW)), per_seq((LANE, WINDOW)),
    )
    scratch = [
        pltpu.VMEM((D_PAD, tb), F32),
        pltpu.VMEM((D_PAD, tb), F32),
        pltpu.VMEM((tb, D_MODEL), MXU_DTYPE),
        pltpu.VMEM((D_MIX, tb), MXU_DTYPE),
        pltpu.VMEM((2 * C_KV, CHUNK, LANE), MXU_DTYPE),
        pltpu.VMEM((CONV_W - 1, B_CONV_DIM, LANE), F32),
    ]
    return pl.pallas_call(
        functools.partial(_prompt_layer_kernel, tb=tb),
        grid=grid, in_specs=in_specs, out_specs=out_specs, out_shape=out_shape, scratch_shapes=scratch,
        compiler_params=pltpu.CompilerParams(dimension_semantics=("arbitrary", "arbitrary"),
                                             vmem_limit_bytes=VMEM_LIMIT_BYTES),
        name="prompt_layer",
    )(lw["sinks"][l], x, x, lw["norm_w"], lw["w_t"], lw["w_out"], lw["gb_rep"], lw["al_rep"], lw["a_nw_rep"],
      lw["conv_w_rep"], lw["conv_b_rep"], lw["d_rep"], lw["b_nw_rep"], lw["qn_rep"], lw["kn_rep"], cos_t, sin_t)


def _unpad_prompt_states(cst, mst, hst, convt, kt, vt):
    n_seq = cst.shape[0]
    c_even, c_odd = cst[:, 0::2, :A_V, :HALF], cst[:, 1::2, :A_V, HALF:]
    c_state = jnp.stack([c_even, c_odd], axis=2).reshape(n_seq, A_HEADS, A_V, A_QK)
    n_even, n_odd = cst[:, 0::2, A_V, :HALF], cst[:, 1::2, A_V, HALF:]
    n_state = jnp.stack([n_even, n_odd], axis=2).reshape(n_seq, A_HEADS, A_QK)
    conv = jnp.swapaxes(convt[:, :, LANE - (CONV_W - 1):], 1, 2)
    k1 = jnp.transpose(kt.reshape(n_seq, C_KV, C_HD, WINDOW), (0, 3, 1, 2))
    v1 = jnp.transpose(vt.reshape(n_seq, C_KV, C_HD, WINDOW), (0, 3, 1, 2))
    return (c_state, n_state, mst[:, :A_HEADS, 0], hst.reshape(n_seq, B_HEADS, B_P, B_STATE), conv, k1, v1)


ITEMS_IN_FLIGHT = 16


def _trace_interleaved(item_iter, depth):
    active, exhausted = [], False
    while True:
        while not exhausted and len(active) < depth:
            nxt = next(item_iter, None)
            if nxt is None:
                exhausted = True
            else:
                active.append(nxt)
        if not active:
            return
        still = []
        for gen in active:
            try:
                next(gen)
                still.append(gen)
            except StopIteration:
                pass
        active = still


def _decode_kernel(
        x_ref, nw_ref, wt_ref, wout_ref, gb_ref, al_ref, anw_ref, cw_ref, cb_ref, dsk_ref, bnw_ref,
        qnw_ref, knw_ref, sink_ref, cos_ref, sin_ref,
        c_ref, n_ref, m_ref, s_ref, cv_ref, k_ref, v_ref,
        y_ref, co_ref, no_ref, mo_ref, so_ref, cvo_ref, ko_ref, vo_ref,
        hs_scr, ut_scr, vrow_scr, xbc_scr, xt_scr, qa_scr, knew_scr, rep_scr, h_scr, yt_scr, yall_scr):
    l = pl.program_id(0)
    j = pl.program_id(1)
    nb = x_ref.shape[0]
    hd_a = A_QK

    @pl.when(j == 0)
    def _layer_start():
        @pl.when(l == 0)
        def _():
            hs_scr[...] = x_ref[...]

        xn = _rmsnorm_rows(hs_scr[...], nw_ref[...]).astype(MXU_DTYPE)
        ut_scr[...] = _dot_nt(wt_ref[...], xn)
        yt_scr[...] = jnp.zeros_like(yt_scr)

        pre = ut_scr[OFF_G:OFF_G + N_GATES, :] + gb_ref[...]
        spl, lsg = _softplus_terms(pre)
        ig, lf, dt = pre[:A_HEADS], lsg[A_HEADS:2 * A_HEADS], spl[2 * A_HEADS:]
        d_a = jnp.exp(dt * (-jnp.exp(al_ref[2 * A_HEADS:, :])))
        m0 = m_ref[...]
        m_new = jnp.maximum(lf + m0, ig)
        sp = jnp.exp(lf + m0 - m_new)
        sl = jnp.exp(ig - m_new)
        mo_ref[...] = m_new
        inv_den = []
        for h in range(A_HEADS):
            k_t = ut_scr[OFF_AK + h * hd_a:OFF_AK + (h + 1) * hd_a, :] * (A_QK ** -0.5)
            q_t = ut_scr[OFF_AQ + h * hd_a:OFF_AQ + (h + 1) * hd_a, :]
            n_new = sp[h:h + 1] * n_ref[h] + sl[h:h + 1] * k_t
            no_ref[h] = n_new
            den = jnp.sum(n_new * q_t, axis=0, keepdims=True)
            inv_den.append(1.0 / jnp.maximum(jnp.abs(den), jnp.exp(-m_new[h:h + 1])))

        cos, sin = cos_ref[...], sin_ref[...]
        kn_t = _norm_rope_t(ut_scr[OFF_CK:OFF_CK + LANE, :], knw_ref[...], cos, sin)
        knew_scr[...] = kn_t
        s_new = []
        for pb in range(C_HEADS // 2):
            q_t = _norm_rope_t(ut_scr[OFF_CQ + pb * LANE:OFF_CQ + (pb + 1) * LANE, :], qnw_ref[...], cos, sin)
            qa_scr[pb * LANE:(pb + 1) * LANE, :] = q_t
            for e in range(2):
                g = (2 * pb + e) // (C_HEADS // C_KV)
                s_new.append(jnp.sum(q_t[e * C_HD:(e + 1) * C_HD, :] * kn_t[g * C_HD:(g + 1) * C_HD, :],
                                     axis=0, keepdims=True))

        table = jnp.concatenate([sp, sl, d_a, dt] + s_new + inv_den + [jnp.zeros((LANE - N_REP, nb), F32)], axis=0)
        table_t = table.T
        for r in range(N_REP):
            rep_scr[r] = jnp.broadcast_to(table_t[:, r:r + 1], (nb, LANE))

        ubx = ut_scr[OFF_BXBC:OFF_BXBC + B_CONV_DIM, :].T
        acc = cb_ref[...] + ubx * cw_ref[CONV_W - 1:CONV_W, :]
        for t in range(CONV_W - 1):
            acc = acc + cv_ref[t] * cw_ref[t:t + 1, :]
        xbc = _silu(acc)
        xbc_scr[...] = xbc
        for t in range(CONV_W - 2):
            cvo_ref[t] = cv_ref[t + 1]
        cvo_ref[CONV_W - 2] = ubx
        xt_scr[...] = xbc[:, :B_WIDTH].T
        vrow_scr[...] = ut_scr[OFF_AV:OFF_AV + A_WIDTH, :].T

    base = pl.multiple_of(j * SAMPLE_BLOCK, SAMPLE_BLOCK)
    rows = pl.ds(base, SAMPLE_BLOCK)
    lane_s = lax.broadcasted_iota(jnp.int32, (C_HD, nb), 1)
    lane_k = lax.broadcasted_iota(jnp.int32, (C_HD, WINDOW), 1)
    lane_r = lane_k[:1, :]
    v_rows = vrow_scr[rows, :]
    xbc_rows = xbc_scr[rows, :]
    reps = [rep_scr[r, rows, :] for r in range(N_REP)]
    hpk = C_HEADS // C_KV
    sels = [lane_s == base + i for i in range(SAMPLE_BLOCK)]

    def col(tile, i):
        return jnp.sum(jnp.where(sels[i], tile, 0.0), axis=1, keepdims=True)

    def rep(r, i):
        return reps[r][i:i + 1, :]

    acc = {}

    ones_m = jnp.ones((nb, LANE), MXU_DTYPE)

    def cols_mxu(tile):
        stacked = jnp.concatenate([jnp.where(sels[i], tile, 0.0) for i in range(SAMPLE_BLOCK)], axis=0)
        return _dot(stacked.astype(MXU_DTYPE), ones_m)

    def mlstm_item(h, i, kc, qc):
        yield
        v_row = v_rows[i:i + 1, h * A_V:(h + 1) * A_V]
        c_new = rep(REP_SP + h, i) * c_ref[i, h] + (rep(REP_SL + h, i) * v_row) * kc
        co_ref[i, h] = c_new
        acc[("h", h)].append(jnp.sum(c_new * qc, axis=0, keepdims=True) * rep(REP_IDEN + h, i))
        if i == SAMPLE_BLOCK - 1:
            h_scr[rows, h * A_V:(h + 1) * A_V] = jnp.concatenate(acc.pop(("h", h)), axis=0)

    def ssd_item(hd, i, xc):
        g = hd // (B_HEADS // B_GROUPS)
        yield
        b_row = xbc_rows[i:i + 1, B_WIDTH + g * B_STATE:B_WIDTH + (g + 1) * B_STATE]
        c_row = xbc_rows[i:i + 1, B_WIDTH + (B_GROUPS + g) * B_STATE:B_WIDTH + (B_GROUPS + g + 1) * B_STATE]
        h_new = rep(REP_DA + hd, i) * s_ref[i, hd] + (rep(REP_DT + hd, i) * b_row) * xc
        so_ref[i, hd] = h_new
        y_col = jnp.sum(h_new * c_row, axis=1, keepdims=True)
        yield
        acc[("y", hd)] = jnp.where(sels[i], y_col, acc[("y", hd)])
        if i == SAMPLE_BLOCK - 1:
            yt_scr[hd * B_P:(hd + 1) * B_P, :] = yt_scr[hd * B_P:(hd + 1) * B_P, :] + acc.pop(("y", hd))

    def attn_item(g, i, kn_all, vn_all, qcs, sink):
        kc_new, vc_new = col(kn_all, i), col(vn_all, i)
        yield
        k_t = k_ref[i, g]
        v_t = v_ref[i, g]
        ko_ref[i, g] = jnp.where(lane_k == WINDOW - 1, kc_new, pltpu.roll(k_t, WINDOW - 1, axis=1))
        vo_ref[i, g] = jnp.where(lane_k == WINDOW - 1, vc_new, pltpu.roll(v_t, WINDOW - 1, axis=1))
        s_rows = [jnp.where(lane_r == 0, rep(REP_SNEW + g * hpk + hh, i), jnp.sum(k_t * qcs[hh], axis=0, keepdims=True))
                  for hh in range(hpk)]
        s = jnp.concatenate(s_rows, axis=0) * (C_HD ** -0.5)
        m = jnp.maximum(jnp.max(s, axis=1, keepdims=True), sink)
        yield
        m_b = jnp.broadcast_to(m, s.shape)
        yield
        ex = jnp.exp(s - m_b)
        den = jnp.sum(ex, axis=1, keepdims=True) + jnp.exp(sink - m)
        yield
        inv_b = jnp.broadcast_to(1.0 / den, s.shape)
        yield
        p = ex * inv_b
        v_eff = jnp.where(lane_k == 0, vc_new, v_t)
        o_cols = [jnp.sum(v_eff * p[hh:hh + 1, :], axis=1, keepdims=True) for hh in range(hpk)]
        yield
        for hh in range(hpk):
            hd = g * hpk + hh
            acc[("o", hd)] = jnp.where(sels[i], o_cols[hh], acc[("o", hd)])
            if i == SAMPLE_BLOCK - 1:
                r0 = B_WIDTH + hd * C_HD
                yt_scr[r0:r0 + C_HD, :] = yt_scr[r0:r0 + C_HD, :] + acc.pop(("o", hd))

    def items():
        for h in range(A_HEADS):
            k_all = ut_scr[OFF_AK + h * hd_a:OFF_AK + (h + 1) * hd_a, :] * (A_QK ** -0.5)
            q_all = ut_scr[OFF_AQ + h * hd_a:OFF_AQ + (h + 1) * hd_a, :]
            acc[("h", h)] = []
            k_cols, q_cols = cols_mxu(k_all), cols_mxu(q_all)
            for i in range(SAMPLE_BLOCK):
                yield mlstm_item(h, i, k_cols[i * hd_a:(i + 1) * hd_a, :], q_cols[i * hd_a:(i + 1) * hd_a, :])
        for hd in range(B_HEADS):
            x_cols = cols_mxu(xt_scr[hd * B_P:(hd + 1) * B_P, :])
            acc[("y", hd)] = jnp.zeros((B_P, nb), F32)
            for i in range(SAMPLE_BLOCK):
                yield ssd_item(hd, i, x_cols[i * B_P:(i + 1) * B_P, :])
        for g in range(C_KV):
            kn_all = knew_scr[g * C_HD:(g + 1) * C_HD, :]
            vn_all = ut_scr[OFF_CV + g * C_HD:OFF_CV + (g + 1) * C_HD, :]
            q_cols = [cols_mxu(qa_scr[(g * hpk + hh) * C_HD:(g * hpk + hh + 1) * C_HD, :]) for hh in range(hpk)]
            sink = sink_ref[g * hpk:(g + 1) * hpk, 0:1]
            for hh in range(hpk):
                acc[("o", g * hpk + hh)] = jnp.zeros((C_HD, nb), F32)
            for i in range(SAMPLE_BLOCK):
                yield attn_item(g, i, kn_all, vn_all, [qc[i * C_HD:(i + 1) * C_HD, :] for qc in q_cols], sink)

    _trace_interleaved(items(), ITEMS_IN_FLIGHT)

    @pl.when(j == pl.num_programs(1) - 1)
    def _layer_end():
        for h in range(A_HEADS):
            h_t = h_scr[:, h * A_V:(h + 1) * A_V].T
            hn = h_t * lax.rsqrt(jnp.mean(h_t * h_t, axis=0, keepdims=True) + NORM_EPS) * anw_ref[h * A_V:(h + 1) * A_V, :]
            ao = ut_scr[OFF_AO + h * A_V:OFF_AO + (h + 1) * A_V, :]
            az = ut_scr[OFF_AZ + h * A_V:OFF_AZ + (h + 1) * A_V, :]
            yall_scr[h * A_V:(h + 1) * A_V, :] = (hn * jax.nn.sigmoid(ao) * _silu(az)).astype(yall_scr.dtype)
        gw = B_WIDTH // B_GROUPS
        for g in range(B_GROUPS):
            r = slice(g * gw, (g + 1) * gw)
            y_g = yt_scr[r, :] + dsk_ref[r, :] * xt_scr[r, :]
            gated = y_g * _silu(ut_scr[OFF_BZ + g * gw:OFF_BZ + (g + 1) * gw, :])
            inv = lax.rsqrt(jnp.mean(gated * gated, axis=0, keepdims=True) + NORM_EPS)
            yall_scr[A_WIDTH + g * gw:A_WIDTH + (g + 1) * gw, :] = (gated * inv * bnw_ref[r, :]).astype(yall_scr.dtype)
        yall_scr[A_WIDTH + B_WIDTH:, :] = \
            (yt_scr[B_WIDTH:, :] * _silu(ut_scr[OFF_CZ:OFF_CZ + C_WIDTH, :])).astype(yall_scr.dtype)
        hs_new = hs_scr[...] + _dot_tn(yall_scr[...], wout_ref[...])
        hs_scr[...] = hs_new

        @pl.when(l == pl.num_programs(0) - 1)
        def _():
            y_ref[...] = hs_new


def _decode(x, lw, cos_s, sin_s, c_v, n_v, m_v, s_v, conv_v, k_v, v_v):
    depth, nb = c_v.shape[0], x.shape[0]
    assert nb == LANE, "samples sit on the 128 lanes next to lane-replicated parameters"
    grid = (depth, nb // SAMPLE_BLOCK)

    def per_layer(shape):
        nd = len(shape)
        return pl.BlockSpec((None,) + shape, lambda l, j, _nd=nd: (l,) + (0,) * _nd)

    def per_block(shape):
        nd = len(shape)
        return pl.BlockSpec((None, SAMPLE_BLOCK) + shape, lambda l, j, _nd=nd: (l, j) + (0,) * _nd)

    def weight(shape):
        nd = len(shape)
        return pl.BlockSpec((None,) + shape, lambda l, j, _nd=nd: (l,) + (0,) * _nd, pipeline_mode=pl.Buffered(1))

    fixed = lambda shape: pl.BlockSpec(shape, lambda l, j, _nd=len(shape): (0,) * _nd)
    state_specs = [
        per_block((A_HEADS, A_QK, A_V)), per_layer((A_HEADS, A_QK, nb)), per_layer((A_HEADS, nb)),
        per_block((B_HEADS, B_P, B_STATE)), per_layer((CONV_W - 1, nb, B_CONV_DIM)),
        per_block((C_KV, C_HD, WINDOW)), per_block((C_KV, C_HD, WINDOW)),
    ]
    in_specs = [
        fixed((nb, D_MODEL)),
        per_layer((1, D_MODEL)), weight((D_INT, D_MODEL)), weight((D_MIX, D_MODEL)),
        per_layer((N_GATES, LANE)), per_layer((N_GATES, LANE)), per_layer((A_WIDTH, LANE)),
        per_layer((CONV_W, B_CONV_DIM)), per_layer((1, B_CONV_DIM)),
        per_layer((B_WIDTH, LANE)), per_layer((B_WIDTH, LANE)), per_layer((LANE, LANE)), per_layer((LANE, LANE)),
        per_layer((C_HEADS, LANE)), fixed((C_HD // 2, LANE)), fixed((C_HD // 2, LANE)),
    ] + state_specs
    states = (c_v, n_v, m_v, s_v, conv_v, k_v, v_v)
    out_shape = (jax.ShapeDtypeStruct((nb, D_MODEL), F32),) + tuple(jax.ShapeDtypeStruct(a.shape, F32) for a in states)
    scratch = [
        pltpu.VMEM((nb, D_MODEL), F32),
        pltpu.VMEM((D_INT, nb), F32),
        pltpu.VMEM((nb, A_WIDTH), F32),
        pltpu.VMEM((nb, B_CONV_DIM), F32),
        pltpu.VMEM((B_WIDTH, nb), F32),
        pltpu.VMEM((C_WIDTH, nb), F32),
        pltpu.VMEM((C_KV * C_HD, nb), F32),
        pltpu.VMEM((N_REP, nb, LANE), F32),
        pltpu.VMEM((nb, A_WIDTH), F32),
        pltpu.VMEM((B_WIDTH + C_WIDTH, nb), F32),
        pltpu.VMEM((D_MIX, nb), MXU_DTYPE),
    ]
    return pl.pallas_call(
        _decode_kernel,
        grid=grid, in_specs=in_specs, out_specs=(fixed((nb, D_MODEL)),) + tuple(state_specs),
        out_shape=out_shape, scratch_shapes=scratch,
        compiler_params=pltpu.CompilerParams(dimension_semantics=("arbitrary", "arbitrary"),
                                             vmem_limit_bytes=VMEM_LIMIT_BYTES),
        name="decode",
    )(x, lw["norm_w"], lw["w_t"], lw["w_out"], lw["gb_rep"], lw["al_rep"], lw["a_nw_rep"], lw["conv_w"],
      lw["conv_b"], lw["d_rep"], lw["b_nw_rep"], lw["qn_rep"], lw["kn_rep"], lw["sink_rep"], cos_s, sin_s, *states)


def _prep_weights(norm_w, w_in, a_igate_b, a_fgate_b, a_norm_w, b_conv_w, b_conv_b, b_dt_bias, b_A_log, b_D,
                  b_norm_w, c_qnorm_w, c_knorm_w, c_sinks, w_out):
    depth = w_in.shape[0]
    w_t = jnp.swapaxes(w_in, 1, 2)
    w_t = jnp.concatenate([w_t[:, :_SRC_OFF[5]], w_t[:, _SRC_OFF[7]:_SRC_OFF[9]], w_t[:, _SRC_OFF[10]:],
                           w_t[:, _SRC_OFF[5]:_SRC_OFF[7]], w_t[:, _SRC_OFF[9]:_SRC_OFF[10]],
                           jnp.zeros((depth, D_PAD - D_INT, D_MODEL), w_in.dtype)], axis=1)
    gbias = jnp.concatenate([a_igate_b, a_fgate_b, b_dt_bias], axis=-1)
    alog = jnp.concatenate([jnp.zeros((depth, 2 * A_HEADS), b_A_log.dtype), b_A_log], axis=-1)

    def rep(v):
        return jnp.broadcast_to(v.astype(F32)[..., None], v.shape + (LANE,))

    return {
        "w_t": w_t.astype(MXU_DTYPE), "w_out": w_out.astype(MXU_DTYPE),
        "norm_w": norm_w.astype(F32)[:, None, :],
        "gb_rep": rep(gbias), "al_rep": rep(alog),
        "a_nw_rep": rep(a_norm_w), "conv_w_rep": rep(b_conv_w), "conv_b_rep": rep(b_conv_b),
        "conv_w": b_conv_w.astype(F32), "conv_b": b_conv_b.astype(F32)[:, None, :],
        "d_rep": rep(jnp.repeat(b_D, B_P, axis=-1)), "b_nw_rep": rep(b_norm_w),
        "qn_rep": rep(jnp.tile(c_qnorm_w, (1, LANE // C_HD))), "kn_rep": rep(jnp.tile(c_knorm_w, (1, LANE // C_HD))),
        "sinks": c_sinks.astype(F32), "sink_rep": rep(c_sinks),
    }


def _rope_tables_t(pos, width=None):
    half = C_HD // 2
    inv = ROPE_THETA ** (-jnp.arange(half, dtype=F32) / half)
    ang = inv[:, None] * pos.astype(F32)[None, :]
    cos, sin = jnp.cos(ang), jnp.sin(ang)
    if width is not None:
        cos, sin = jnp.broadcast_to(cos, (half, width)), jnp.broadcast_to(sin, (half, width))
    return cos, sin


def _kernel_impl(x_prompt, x_sample, state_mlstm_C, state_mlstm_n, state_mlstm_m, state_ssm, state_conv,
                 cache_k, cache_v, norm_w, w_in, a_igate_b, a_fgate_b, a_norm_w, b_conv_w, b_conv_b,
                 b_dt_bias, b_A_log, b_D, b_norm_w, c_qnorm_w, c_knorm_w, c_sinks, w_out, *, tb):
    depth = w_in.shape[0]
    t_len = x_prompt.shape[1]
    lw = _prep_weights(norm_w, w_in, a_igate_b, a_fgate_b, a_norm_w, b_conv_w, b_conv_b, b_dt_bias, b_A_log,
                       b_D, b_norm_w, c_qnorm_w, c_knorm_w, c_sinks, w_out)
    cos_p, sin_p = _rope_tables_t(jnp.arange(t_len, dtype=jnp.int32))
    hp = x_prompt
    st_prompt = []
    for l in range(depth):
        res = _prompt_layer(hp, lw, l, cos_p, sin_p, tb)
        hp = res[0]
        st_prompt.append(_unpad_prompt_states(*res[1:]))
    p_states = [jnp.stack(t) for t in zip(*st_prompt)]

    assert x_sample.shape[1] == 1
    cos_s, sin_s = _rope_tables_t(PAST_LEN + jnp.arange(1, dtype=jnp.int32), LANE)
    outs = _decode(
        x_sample[:, 0, :], lw, cos_s, sin_s,
        jnp.transpose(state_mlstm_C, (0, 1, 2, 4, 3)), jnp.transpose(state_mlstm_n, (0, 2, 3, 1)),
        jnp.transpose(state_mlstm_m, (0, 2, 1)), state_ssm, jnp.transpose(state_conv, (0, 2, 1, 3)),
        jnp.transpose(cache_k, (0, 1, 3, 4, 2)), jnp.transpose(cache_v, (0, 1, 3, 4, 2)))
    hs, c_o, n_o, m_o, s_o, conv_o, k_o, v_o = outs
    s_states = (jnp.transpose(c_o, (0, 1, 2, 4, 3)), jnp.transpose(n_o, (0, 3, 1, 2)), jnp.transpose(m_o, (0, 2, 1)),
                s_o, jnp.transpose(conv_o, (0, 2, 1, 3)),
                jnp.transpose(k_o, (0, 1, 4, 2, 3)), jnp.transpose(v_o, (0, 1, 4, 2, 3)))
    return (hp, hs[:, None, :], *p_states, *s_states)


def kernel(x_prompt, x_sample, state_mlstm_C, state_mlstm_n, state_mlstm_m, state_ssm, state_conv, cache_k, cache_v, norm_w, w_in, a_igate_b, a_fgate_b, a_norm_w, b_conv_w, b_conv_b, b_dt_bias, b_A_log, b_D, b_norm_w, c_qnorm_w, c_knorm_w, c_sinks, w_out):
    return _kernel_impl(x_prompt, x_sample, state_mlstm_C, state_mlstm_n, state_mlstm_m, state_ssm, state_conv,
                        cache_k, cache_v, norm_w, w_in, a_igate_b, a_fgate_b, a_norm_w, b_conv_w, b_conv_b,
                        b_dt_bias, b_A_log, b_D, b_norm_w, c_qnorm_w, c_knorm_w, c_sinks, w_out, tb=PROMPT_BLOCK)
```

```python
import functools

import jax
import jax.numpy as jnp
import numpy as np
from jax import lax
from jax.experimental import pallas as pl
from jax.experimental.pallas import tpu as pltpu

F32 = jnp.float32
MXU_DTYPE = jnp.bfloat16

D_MODEL = 1024
A_HEADS, A_QK, A_V = 4, 64, 128
A_WIDTH = A_HEADS * A_V
B_HEADS, B_P, B_GROUPS, B_STATE = 8, 64, 2, 128
B_WIDTH = B_HEADS * B_P
CONV_W = 4
B_CONV_DIM = B_WIDTH + 2 * B_GROUPS * B_STATE
C_HEADS, C_KV, C_HD = 8, 2, 64
C_WIDTH = C_HEADS * C_HD
WINDOW = 128
ROPE_THETA = 10000.0
D_MIX = A_WIDTH + B_WIDTH + C_WIDTH
NORM_EPS = 1e-6
PAST_LEN = 8192

LANE = 128
SUBLANE = 8
HALF = LANE // 2

CHUNK = 128
PROMPT_BLOCK = 256
SAMPLE_BLOCK = SUBLANE
N_GATES = 2 * A_HEADS + B_HEADS

_SRC_SIZES = (A_HEADS * A_QK, A_HEADS * A_QK, A_WIDTH, A_WIDTH, A_WIDTH, A_HEADS, A_HEADS,
              B_WIDTH, B_CONV_DIM, B_HEADS, C_WIDTH, C_KV * C_HD, C_KV * C_HD, C_WIDTH)
(OFF_AQ, OFF_AK, OFF_AV, OFF_AO, OFF_AZ, OFF_GA, _, OFF_BZ, OFF_BXBC, OFF_GB, OFF_CQ, OFF_CK, OFF_CV, OFF_CZ,
 D_INT) = np.concatenate([[0], np.cumsum(_SRC_SIZES)]).tolist()
assert all(o % SUBLANE == 0 for o in (OFF_GA, OFF_BZ, OFF_BXBC, OFF_GB, OFF_CQ, OFF_CK, OFF_CV, OFF_CZ))
INPROJ_SITES = (2, 6)
INPROJ_PIECES = len(INPROJ_SITES)
PROMPT_ITEMS_IN_FLIGHT = 2
_PIECES_PER_BLOCK = INPROJ_PIECES * (PROMPT_BLOCK // CHUNK)
PIECE_ROWS = -(-D_INT // (_PIECES_PER_BLOCK * 2 * SUBLANE)) * 2 * SUBLANE
D_PAD = PIECE_ROWS * _PIECES_PER_BLOCK

C_EXT = A_V + 2 * SUBLANE
VMEM_LIMIT_BYTES = 56 * 1024 * 1024

REP_SP, REP_SL = 0, A_HEADS
REP_DA, REP_DT = 2 * A_HEADS, 2 * A_HEADS + B_HEADS
REP_SNEW = 2 * A_HEADS + 2 * B_HEADS
REP_IDEN = REP_SNEW + C_HEADS
N_REP = REP_IDEN + A_HEADS


def _dot(a, b):
    return jnp.dot(a, b, preferred_element_type=F32)


def _dot_nt(a, b):
    return lax.dot_general(a, b, (((1,), (1,)), ((), ())), preferred_element_type=F32)


def _dot_tn(a, b):
    return lax.dot_general(a, b, (((0,), (0,)), ((), ())), preferred_element_type=F32)


def _split3(z):
    hi = z.astype(MXU_DTYPE)
    r1 = z - hi.astype(F32)
    mid = r1.astype(MXU_DTYPE)
    lo = (r1 - mid.astype(F32)).astype(MXU_DTYPE)
    return hi, mid, lo


def _softplus_terms(x):
    t = jnp.log1p(jnp.exp(-jnp.abs(x)))
    return jnp.maximum(x, 0.0) + t, jnp.minimum(x, 0.0) - t


def _silu(x):
    return x * jax.nn.sigmoid(x)


def _rmsnorm_rows(x, w):
    return x * lax.rsqrt(jnp.mean(x * x, axis=-1, keepdims=True) + NORM_EPS) * w


def _norm_rope_t(x_t, w_rep, cos, sin):
    half = C_HD // 2
    out = []
    for hh in range(LANE // C_HD):
        xh = x_t[hh * C_HD:(hh + 1) * C_HD, :]
        ms = jnp.mean(xh * xh, axis=0, keepdims=True)
        xn = xh * lax.rsqrt(ms + NORM_EPS) * w_rep[hh * C_HD:(hh + 1) * C_HD, :]
        x1, x2 = xn[:half, :], xn[half:, :]
        out += [x1 * cos - x2 * sin, x2 * cos + x1 * sin]
    return jnp.concatenate(out, axis=0)


def _prompt_layer_kernel(
        sinks_ref,
        x_ref, xnext_ref, nw_ref, wt_ref, wout_ref, gb_ref, al_ref, anw_ref, cw_ref, cb_ref, dsk_ref, bnw_ref,
        qnw_ref, knw_ref, cos_ref, sin_ref,
        o_ref, cst_ref, mst_ref, hst_ref, convt_ref, kt_ref, vt_ref,
        u0_scr, u1_scr, xn_scr, y_scr, kprev_scr, roll_scr, *, tb):
    L = CHUNK
    nch = tb // L
    rows_per_chunk = u0_scr.shape[0] // nch
    t_id = pl.program_id(1)
    slot = t_id % 2

    @pl.when(t_id == 0)
    def _():
        cst_ref[...] = jnp.zeros_like(cst_ref)
        mst_ref[...] = jnp.zeros_like(mst_ref)
        hst_ref[...] = jnp.zeros_like(hst_ref)
        convt_ref[...] = jnp.zeros_like(convt_ref)
        kt_ref[...] = jnp.zeros_like(kt_ref)
        vt_ref[...] = jnp.zeros_like(vt_ref)
        kprev_scr[...] = jnp.zeros_like(kprev_scr)
        roll_scr[...] = jnp.zeros_like(roll_scr)
        u0_scr[...] = _dot_nt(wt_ref[...], _rmsnorm_rows(x_ref[...], nw_ref[...]).astype(MXU_DTYPE))

    xn_scr[...] = _rmsnorm_rows(xnext_ref[...], nw_ref[...]).astype(MXU_DTYPE)
    u_cur, u_next = u0_scr, u1_scr

    row = lax.broadcasted_iota(jnp.int32, (L, L), 0)
    lane = lax.broadcasted_iota(jnp.int32, (L, L), 1)
    lo = lane < HALF
    top = row < B_P
    src_le_t = row <= lane
    tri = jnp.where(lane <= row, 1.0, 0.0).astype(MXU_DTYPE)
    grow = lax.broadcasted_iota(jnp.int32, (N_GATES, L), 0)
    krow = lax.broadcasted_iota(jnp.int32, (2 * L, L), 0)
    klane = lax.broadcasted_iota(jnp.int32, (2 * L, L), 1)
    neg_inf = -jnp.inf
    a_neg = -jnp.exp(al_ref[...])

    def chunk(c, carry):
        cols = pl.ds(pl.multiple_of(c * L, L), L)

        def next_block_piece(site):
            if site not in INPROJ_SITES:
                return
            k = INPROJ_SITES.index(site)
            pr = rows_per_chunk // INPROJ_PIECES
            wrows = pl.ds(pl.multiple_of(c * rows_per_chunk + k * pr, pr), pr)
            u_next[wrows, :] = _dot_nt(wt_ref[wrows, :], xn_scr[...])

        pre_r = jnp.concatenate([u_cur[OFF_GA:OFF_GA + 2 * A_HEADS, cols], u_cur[OFF_GB:OFF_GB + B_HEADS, cols]],
                                axis=0) + gb_ref[...]
        sp_r, ls_r = _softplus_terms(pre_r)
        z_r = jnp.where((grow >= A_HEADS) & (grow < 2 * A_HEADS), ls_r,
                        jnp.where(grow >= 2 * A_HEADS, sp_r * a_neg, 0.0))
        zp = _split3(z_r)
        cum_r = _dot_nt(zp[0], tri) + _dot_nt(zp[1], tri) + _dot_nt(zp[2], tri)
        colf = jnp.concatenate([pre_r, cum_r, jnp.zeros((L - 2 * N_GATES, L), F32)], axis=0).T

        def piece_item(site):
            next_block_piece(site)
            return
            yield

        def mlstm_head(h, q_t, k_p):
            e = h % 2
            k_m = jnp.where(lo if e == 0 else jnp.logical_not(lo), k_p, 0.0).astype(MXU_DTYPE)
            v_t = u_cur[OFF_AV + h * LANE:OFF_AV + (h + 1) * LANE, cols]
            b_row = cum_r[A_HEADS + h:A_HEADS + h + 1, :]
            i_row = pre_r[h:h + 1, :]
            c_col = colf[:, N_GATES + A_HEADS + h:N_GATES + A_HEADS + h + 1] - colf[:, h:h + 1]
            b_end = b_row[:, L - 1:L]
            m_prev = mst_ref[h:h + 1, 0:1]
            c_prev = cst_ref[h]

            g_row = b_end - b_row + i_row
            m_loc = jnp.max(g_row, axis=1, keepdims=True)
            d_t = jnp.where(src_le_t, b_row - c_col, neg_inf)
            inter = b_row + m_prev
            m_t = jnp.maximum(inter, jnp.max(d_t, axis=0, keepdims=True))
            res = _dot(jnp.concatenate([k_m, c_prev.astype(MXU_DTYPE)], axis=0), q_t)
            w_row = jnp.exp(g_row - m_loc)
            v_ext = jnp.concatenate([v_t, jnp.ones((1, L), F32), jnp.zeros((C_EXT - A_V - 1, L), F32)], axis=0)
            c_loc = _dot((v_ext * w_row).astype(MXU_DTYPE), k_m)
            e_t = jnp.exp(d_t - m_t)
            si = jnp.exp(inter - m_t)
            yield
            s_t = e_t * res[:L, :]
            sv = _dot(v_t.astype(MXU_DTYPE), s_t.astype(MXU_DTYPE))
            yield
            num = sv + si * res[L:L + A_V, :]
            den = jnp.sum(s_t, axis=0, keepdims=True) + si * res[L + A_V:L + A_V + 1, :]
            hh = num * (1.0 / jnp.maximum(jnp.abs(den), jnp.exp(-m_t)))
            hn = hh * lax.rsqrt(jnp.mean(hh * hh, axis=0, keepdims=True) + NORM_EPS) * anw_ref[h * LANE:(h + 1) * LANE, :]
            ao = u_cur[OFF_AO + h * LANE:OFF_AO + (h + 1) * LANE, cols]
            az = u_cur[OFF_AZ + h * LANE:OFF_AZ + (h + 1) * LANE, cols]
            y_scr[h * LANE:(h + 1) * LANE, cols] = (hn * jax.nn.sigmoid(ao) * _silu(az)).astype(y_scr.dtype)

            m_new = jnp.maximum(b_end + m_prev, m_loc)
            sp = jnp.exp(b_end + m_prev - m_new)
            sl = jnp.exp(m_loc - m_new)
            cst_ref[h] = sp * c_prev + sl * c_loc
            mst_ref[h:h + 1, :] = jnp.broadcast_to(m_new, (1, LANE))

        def conv_silu():
            cur = u_cur[OFF_BXBC:OFF_BXBC + B_CONV_DIM, cols]
            acc = cb_ref[...] + cur * cw_ref[CONV_W - 1]
            for j in range(1, CONV_W):
                rolled = pltpu.roll(cur, j, axis=1)
                acc = acc + jnp.where(lane[:1, :] >= j, rolled, roll_scr[j - 1]) * cw_ref[CONV_W - 1 - j]
                roll_scr[j - 1] = rolled
            convt_ref[...] = cur
            return _silu(acc)

        def ssd_group(g, xbc):
            b_t = xbc[B_WIDTH + g * B_STATE:B_WIDTH + (g + 1) * B_STATE, :]
            c_t = xbc[B_WIDTH + (B_GROUPS + g) * B_STATE:B_WIDTH + (B_GROUPS + g + 1) * B_STATE, :]
            b_m = b_t.T.astype(MXU_DTYPE)
            hpg = B_HEADS // B_GROUPS
            h_prev = hst_ref[g * hpg * B_P:(g + 1) * hpg * B_P, :]
            yield
            res = _dot(jnp.concatenate([b_m, h_prev.astype(MXU_DTYPE)], axis=0), c_t.astype(MXU_DTYPE))
            x_ts, h_locs, rows_of = [], [], []
            for pj in range(hpg // 2):
                pb = g * (hpg // 2) + pj
                x_t = xbc[pb * LANE:(pb + 1) * LANE, :]
                a_ends, dec_rows, ea_rows, a_rows, a_cols, dt_rows = [], [], [], [], [], []
                for e in range(2):
                    gi = 2 * A_HEADS + 2 * pb + e
                    a_row = cum_r[gi:gi + 1, :]
                    dt_row = sp_r[gi:gi + 1, :]
                    a_end = a_row[:, L - 1:L]
                    a_rows.append(a_row)
                    a_cols.append(colf[:, N_GATES + gi:N_GATES + gi + 1])
                    dt_rows.append(dt_row)
                    a_ends.append(jnp.exp(a_end))
                    dec_rows.append(jnp.exp(a_end - a_row) * dt_row)
                    ea_rows.append(jnp.exp(a_row))
                h_locs.append(_dot((x_t * jnp.where(top, dec_rows[0], dec_rows[1])).astype(MXU_DTYPE), b_m))
                x_ts.append(x_t)
                rows_of.append((a_ends, ea_rows, a_rows, a_cols, dt_rows))
            yield
            cb_tt = res[:L, :]
            ys_all = []
            for pj in range(hpg // 2):
                a_ends, ea_rows, a_rows, a_cols, dt_rows = rows_of[pj]
                ys = []
                for e in range(2):
                    w_t = jnp.exp(jnp.where(src_le_t, a_rows[e] - a_cols[e], neg_inf)) * cb_tt
                    xdt = (x_ts[pj][e * B_P:(e + 1) * B_P, :] * dt_rows[e]).astype(MXU_DTYPE)
                    ys.append(_dot(xdt, w_t.astype(MXU_DTYPE)))
                ys_all.append(ys)
            yield
            gated = []
            for pj in range(hpg // 2):
                pb = g * (hpg // 2) + pj
                a_ends, ea_rows, a_rows, a_cols, dt_rows = rows_of[pj]
                inter = res[L + pj * LANE:L + (pj + 1) * LANE, :] * jnp.where(top, ea_rows[0], ea_rows[1])
                y_p = jnp.concatenate(ys_all[pj], axis=0) + inter + dsk_ref[pb * LANE:(pb + 1) * LANE, :] * x_ts[pj]
                hst_ref[pb * LANE:(pb + 1) * LANE, :] = \
                    jnp.where(top, a_ends[0], a_ends[1]) * h_prev[pj * LANE:(pj + 1) * LANE, :] + h_locs[pj]
                bz = u_cur[OFF_BZ + pb * LANE:OFF_BZ + (pb + 1) * LANE, cols]
                gated.append(y_p * _silu(bz))
            ms = sum(jnp.sum(gp * gp, axis=0, keepdims=True) for gp in gated) * (1.0 / (B_WIDTH // B_GROUPS))
            inv = lax.rsqrt(ms + NORM_EPS)
            for pj, gp in enumerate(gated):
                pb = g * (hpg // 2) + pj
                y_scr[A_WIDTH + pb * LANE:A_WIDTH + (pb + 1) * LANE, cols] = \
                    (gp * inv * bnw_ref[pb * LANE:(pb + 1) * LANE, :]).astype(y_scr.dtype)

        cos = cos_ref[:, cols]
        sin = sin_ref[:, cols]
        ppg = C_HEADS // 2 // C_KV
        shared = {}

        def swa_keys():
            k_t = _norm_rope_t(u_cur[OFF_CK:OFF_CK + LANE, cols], knw_ref[...], cos, sin)
            v_t = u_cur[OFF_CV:OFF_CV + LANE, cols]
            k_p = k_t.T
            k_sw = pltpu.roll(k_p, HALF, axis=1)
            k_cur = [[jnp.where(lo, k_p, 0.0).astype(MXU_DTYPE), jnp.where(lo, 0.0, k_sw).astype(MXU_DTYPE)],
                     [jnp.where(lo, k_sw, 0.0).astype(MXU_DTYPE), jnp.where(lo, 0.0, k_p).astype(MXU_DTYPE)]]
            shift = jnp.where(jnp.logical_and(t_id == 0, c == 0), 2 * L, 0)
            valid = ((krow < L) & (krow > klane + shift)) | ((krow >= L) & (krow - L <= klane))
            shared.update(k_t=k_t, v_t=v_t, k_cur=k_cur, valid=valid)

        def swa_pair(pb):
            g = pb // ppg
            k_cur, v_t, valid = shared["k_cur"], shared["v_t"], shared["valid"]
            q_t = _norm_rope_t(u_cur[OFF_CQ + pb * LANE:OFF_CQ + (pb + 1) * LANE, cols], qnw_ref[...], cos, sin)
            keys = jnp.concatenate([kprev_scr[2 * g], k_cur[g][0], kprev_scr[2 * g + 1], k_cur[g][1]], axis=0)
            sc = _dot(keys, q_t.astype(MXU_DTYPE)) * (C_HD ** -0.5)
            v_g = jnp.concatenate([vt_ref[g * C_HD:(g + 1) * C_HD, :], v_t[g * C_HD:(g + 1) * C_HD, :]],
                                  axis=1).astype(MXU_DTYPE)
            yield
            outs = []
            for e in range(2):
                s = jnp.where(valid, sc[e * 2 * L:(e + 1) * 2 * L, :], neg_inf)
                sink = sinks_ref[2 * pb + e]
                m = jnp.maximum(jnp.max(s, axis=0, keepdims=True), sink)
                ex = jnp.exp(s - m)
                p = ex * (1.0 / (jnp.sum(ex, axis=0, keepdims=True) + jnp.exp(sink - m)))
                outs.append(_dot(v_g, p.astype(MXU_DTYPE)))
            yield
            cz = u_cur[OFF_CZ + pb * LANE:OFF_CZ + (pb + 1) * LANE, cols]
            y_scr[A_WIDTH + B_WIDTH + pb * LANE:A_WIDTH + B_WIDTH + (pb + 1) * LANE, cols] = \
                (jnp.concatenate(outs, axis=0) * _silu(cz)).astype(y_scr.dtype)

        def items():
            for h in range(A_HEADS):
                if h % 2 == 0:
                    pb = h // 2
                    q_t = u_cur[OFF_AQ + pb * LANE:OFF_AQ + (pb + 1) * LANE, cols].astype(MXU_DTYPE)
                    k_p = (u_cur[OFF_AK + pb * LANE:OFF_AK + (pb + 1) * LANE, cols] * (A_QK ** -0.5)).T
                yield mlstm_head(h, q_t, k_p)
                yield piece_item(1 + h)
            xbc = conv_silu()
            for g in range(B_GROUPS):
                yield ssd_group(g, xbc)
                yield piece_item(1 + A_HEADS + g)
            swa_keys()
            for pb in range(C_HEADS // 2):
                yield swa_pair(pb)
                if pb == 1:
                    yield piece_item(1 + A_HEADS + B_GROUPS)

        next_block_piece(0)
        _trace_interleaved(items(), PROMPT_ITEMS_IN_FLIGHT)
        for g in range(C_KV):
            for e in range(2):
                kprev_scr[2 * g + e] = shared["k_cur"][g][e]
        kt_ref[...] = shared["k_t"]
        vt_ref[...] = shared["v_t"]
        return carry

    def run_chunks(cur, nxt):
        nonlocal u_cur, u_next
        u_cur, u_next = cur, nxt
        lax.fori_loop(0, nch, lambda c, carry: chunk(c, carry), 0)

    @pl.when(slot == 0)
    def _():
        run_chunks(u0_scr, u1_scr)

    @pl.when(slot == 1)
    def _():
        run_chunks(u1_scr, u0_scr)

    o_ref[...] = x_ref[...] + _dot_tn(y_scr[...], wout_ref[...])


def _prompt_layer(x, lw, l, cos_t, sin_t, tb):
    n_seq, t_len, _ = x.shape
    assert t_len % tb == 0 and tb % CHUNK == 0 and (D_PAD // (tb // CHUNK)) % (2 * SUBLANE) == 0
    n_t = t_len // tb
    grid = (n_seq, n_t)

    def per_seq(shape):
        nd = len(shape)
        return pl.BlockSpec((None,) + shape, lambda n, t, _nd=nd: (n,) + (0,) * _nd)

    def per_layer(shape, **kw):
        nd = len(shape)
        return pl.BlockSpec((None,) + shape, lambda n, t, _nd=nd: (l,) + (0,) * _nd, **kw)

    in_specs = [
        pl.BlockSpec(memory_space=pltpu.SMEM),
        pl.BlockSpec((None, tb, D_MODEL), lambda n, t: (n, t, 0)),
        pl.BlockSpec((None, tb, D_MODEL), lambda n, t: (n, jnp.minimum(t + 1, n_t - 1), 0)),
        per_layer((1, D_MODEL)),
        per_layer((D_PAD, D_MODEL), pipeline_mode=pl.Buffered(1)),
        per_layer((D_MIX, D_MODEL), pipeline_mode=pl.Buffered(1)),
        per_layer((N_GATES, LANE)), per_layer((N_GATES, LANE)),
        per_layer((A_WIDTH, LANE)),
        per_layer((CONV_W, B_CONV_DIM, LANE)), per_layer((B_CONV_DIM, LANE)),
        per_layer((B_WIDTH, LANE)), per_layer((B_WIDTH, LANE)),
        per_layer((LANE, LANE)), per_layer((LANE, LANE)),
        pl.BlockSpec((C_HD // 2, tb), lambda n, t: (0, t)),
        pl.BlockSpec((C_HD // 2, tb), lambda n, t: (0, t)),
    ]
    out_shape = (
        jax.ShapeDtypeStruct((n_seq, t_len, D_MODEL), F32),
        jax.ShapeDtypeStruct((n_seq, A_HEADS, C_EXT, LANE), F32),
        jax.ShapeDtypeStruct((n_seq, SUBLANE, LANE), F32),
        jax.ShapeDtypeStruct((n_seq, B_HEADS * B_P, B_STATE), F32),
        jax.ShapeDtypeStruct((n_seq, B_CONV_DIM, LANE), F32),
        jax.ShapeDtypeStruct((n_seq, LANE, WINDOW), F32),
        jax.ShapeDtypeStruct((n_seq, LANE, WINDOW), F32),
    )
    out_specs = (
        pl.BlockSpec((None, tb, D_MODEL), lambda n, t: (n, t, 0)),
        per_seq((A_HEADS, C_EXT, LANE)), per_seq((SUBLANE, LANE)),
        per_seq((B_HEADS * B_P, B_STATE)), per_seq((B_CONV_DIM, LANE)),
        per_seq((LANE, WINDOW)), per_seq((LANE, WINDOW)),
    )
    scratch = [
        pltpu.VMEM((D_PAD, tb), F32),
        pltpu.VMEM((D_PAD, tb), F32),
        pltpu.VMEM((tb, D_MODEL), MXU_DTYPE),
        pltpu.VMEM((D_MIX, tb), MXU_DTYPE),
        pltpu.VMEM((2 * C_KV, CHUNK, LANE), MXU_DTYPE),
        pltpu.VMEM((CONV_W - 1, B_CONV_DIM, LANE), F32),
    ]
    return pl.pallas_call(
        functools.partial(_prompt_layer_kernel, tb=tb),
        grid=grid, in_specs=in_specs, out_specs=out_specs, out_shape=out_shape, scratch_shapes=scratch,
        compiler_params=pltpu.CompilerParams(dimension_semantics=("arbitrary", "arbitrary"),
                                             vmem_limit_bytes=VMEM_LIMIT_BYTES),
        name="prompt_layer",
    )(lw["sinks"][l], x, x, lw["norm_w"], lw["w_t"], lw["w_out"], lw["gb_rep"], lw["al_rep"], lw["a_nw_rep"],
      lw["conv_w_rep"], lw["conv_b_rep"], lw["d_rep"], lw["b_nw_rep"], lw["qn_rep"], lw["kn_rep"], cos_t, sin_t)


def _unpad_prompt_states(cst, mst, hst, convt, kt, vt):
    lead = cst.shape[:2]
    c_even, c_odd = cst[:, :, 0::2, :A_V, :HALF], cst[:, :, 1::2, :A_V, HALF:]
    c_state = jnp.stack([c_even, c_odd], axis=3).reshape(lead + (A_HEADS, A_V, A_QK))
    n_even, n_odd = cst[:, :, 0::2, A_V, :HALF], cst[:, :, 1::2, A_V, HALF:]
    n_state = jnp.stack([n_even, n_odd], axis=3).reshape(lead + (A_HEADS, A_QK))
    conv = jnp.swapaxes(convt[:, :, :, LANE - (CONV_W - 1):], 2, 3)
    k1 = jnp.transpose(kt.reshape(lead + (C_KV, C_HD, WINDOW)), (0, 1, 4, 2, 3))
    v1 = jnp.transpose(vt.reshape(lead + (C_KV, C_HD, WINDOW)), (0, 1, 4, 2, 3))
    return (c_state, n_state, mst[:, :, :A_HEADS, 0], hst.reshape(lead + (B_HEADS, B_P, B_STATE)), conv, k1, v1)


ITEMS_IN_FLIGHT = 16


def _trace_interleaved(item_iter, depth):
    active, exhausted = [], False
    while True:
        while not exhausted and len(active) < depth:
            nxt = next(item_iter, None)
            if nxt is None:
                exhausted = True
            else:
                active.append(nxt)
        if not active:
            return
        still = []
        for gen in active:
            try:
                next(gen)
                still.append(gen)
            except StopIteration:
                pass
        active = still


def _decode_kernel(
        x_ref, nw_ref, wt_ref, wout_ref, gb_ref, al_ref, anw_ref, cw_ref, cb_ref, dsk_ref, bnw_ref,
        qnw_ref, knw_ref, sink_ref, cos_ref, sin_ref,
        c_ref, n_ref, m_ref, s_ref, cv_ref, k_ref, v_ref,
        y_ref, co_ref, no_ref, mo_ref, so_ref, cvo_ref, ko_ref, vo_ref,
        hs_scr, ut_scr, vrow_scr, xbc_scr, xt_scr, ct_scr, qa_scr, knew_scr, rep_scr, h_scr, yt_scr, yall_scr):
    l = pl.program_id(0)
    j = pl.program_id(1)
    nb = x_ref.shape[0]
    hd_a = A_QK

    @pl.when(j == 0)
    def _layer_start():
        @pl.when(l == 0)
        def _():
            hs_scr[...] = x_ref[...]

        xn = _rmsnorm_rows(hs_scr[...], nw_ref[...]).astype(MXU_DTYPE)
        ut_scr[...] = _dot_nt(wt_ref[...], xn)
        yt_scr[...] = jnp.zeros_like(yt_scr)

        pre = jnp.concatenate([ut_scr[OFF_GA:OFF_GA + 2 * A_HEADS, :], ut_scr[OFF_GB:OFF_GB + B_HEADS, :]],
                              axis=0) + gb_ref[...]
        spl, lsg = _softplus_terms(pre)
        ig, lf, dt = pre[:A_HEADS], lsg[A_HEADS:2 * A_HEADS], spl[2 * A_HEADS:]
        d_a = jnp.exp(dt * (-jnp.exp(al_ref[2 * A_HEADS:, :])))
        m0 = m_ref[...]
        m_new = jnp.maximum(lf + m0, ig)
        sp = jnp.exp(lf + m0 - m_new)
        sl = jnp.exp(ig - m_new)
        mo_ref[...] = m_new
        inv_den = []
        for h in range(A_HEADS):
            k_t = ut_scr[OFF_AK + h * hd_a:OFF_AK + (h + 1) * hd_a, :] * (A_QK ** -0.5)
            q_t = ut_scr[OFF_AQ + h * hd_a:OFF_AQ + (h + 1) * hd_a, :]
            n_new = sp[h:h + 1] * n_ref[h] + sl[h:h + 1] * k_t
            no_ref[h] = n_new
            den = jnp.sum(n_new * q_t, axis=0, keepdims=True)
            inv_den.append(1.0 / jnp.maximum(jnp.abs(den), jnp.exp(-m_new[h:h + 1])))

        cos, sin = cos_ref[...], sin_ref[...]
        kn_t = _norm_rope_t(ut_scr[OFF_CK:OFF_CK + LANE, :], knw_ref[...], cos, sin)
        knew_scr[...] = kn_t
        s_new = []
        for pb in range(C_HEADS // 2):
            q_t = _norm_rope_t(ut_scr[OFF_CQ + pb * LANE:OFF_CQ + (pb + 1) * LANE, :], qnw_ref[...], cos, sin)
            qa_scr[pb * LANE:(pb + 1) * LANE, :] = q_t
            for e in range(2):
                g = (2 * pb + e) // (C_HEADS // C_KV)
                s_new.append(jnp.sum(q_t[e * C_HD:(e + 1) * C_HD, :] * kn_t[g * C_HD:(g + 1) * C_HD, :],
                                     axis=0, keepdims=True))

        table = jnp.concatenate([sp, sl, d_a, dt] + s_new + inv_den + [jnp.zeros((LANE - N_REP, nb), F32)], axis=0)
        table_t = table.T
        for r in range(N_REP):
            rep_scr[r] = jnp.broadcast_to(table_t[:, r:r + 1], (nb, LANE))

        ubx = ut_scr[OFF_BXBC:OFF_BXBC + B_CONV_DIM, :].T
        acc = cb_ref[...] + ubx * cw_ref[CONV_W - 1:CONV_W, :]
        for t in range(CONV_W - 1):
            acc = acc + cv_ref[t] * cw_ref[t:t + 1, :]
        xbc = _silu(acc)
        xbc_scr[...] = xbc
        for t in range(CONV_W - 2):
            cvo_ref[t] = cv_ref[t + 1]
        cvo_ref[CONV_W - 2] = ubx
        xt_scr[...] = xbc[:, :B_WIDTH].T
        for g in range(B_GROUPS):
            ct_scr[g] = xbc[:, B_WIDTH + (B_GROUPS + g) * B_STATE:B_WIDTH + (B_GROUPS + g + 1) * B_STATE].T.astype(ct_scr.dtype)
        vrow_scr[...] = ut_scr[OFF_AV:OFF_AV + A_WIDTH, :].T

    base = pl.multiple_of(j * SAMPLE_BLOCK, SAMPLE_BLOCK)
    rows = pl.ds(base, SAMPLE_BLOCK)
    lane_s = lax.broadcasted_iota(jnp.int32, (C_HD, nb), 1)
    lane_k = lax.broadcasted_iota(jnp.int32, (C_HD, WINDOW), 1)
    lane_r = lane_k[:1, :]
    v_rows = vrow_scr[rows, :]
    xbc_rows = xbc_scr[rows, :]
    reps = [rep_scr[r, rows, :] for r in range(N_REP)]
    hpk = C_HEADS // C_KV
    sels = [lane_s == base + i for i in range(SAMPLE_BLOCK)]

    def col(tile, i):
        return jnp.sum(jnp.where(sels[i], tile, 0.0), axis=1, keepdims=True)

    def rep(r, i):
        return reps[r][i:i + 1, :]

    acc = {}

    ones_m = jnp.ones((nb, LANE), MXU_DTYPE)

    def cols_mxu(tile):
        stacked = jnp.concatenate([jnp.where(sels[i], tile, 0.0) for i in range(SAMPLE_BLOCK)], axis=0)
        return _dot(stacked.astype(MXU_DTYPE), ones_m)

    def mlstm_item(h, i, kc, qc):
        yield
        v_row = v_rows[i:i + 1, h * A_V:(h + 1) * A_V]
        c_new = rep(REP_SP + h, i) * c_ref[i, h] + (rep(REP_SL + h, i) * v_row) * kc
        co_ref[i, h] = c_new
        acc[("h", h)].append(jnp.sum(c_new * qc, axis=0, keepdims=True) * rep(REP_IDEN + h, i))
        if i == SAMPLE_BLOCK - 1:
            h_scr[rows, h * A_V:(h + 1) * A_V] = jnp.concatenate(acc.pop(("h", h)), axis=0)

    def ssd_item(hd, i, xc):
        g = hd // (B_HEADS // B_GROUPS)
        yield
        b_row = xbc_rows[i:i + 1, B_WIDTH + g * B_STATE:B_WIDTH + (g + 1) * B_STATE]
        h_new = rep(REP_DA + hd, i) * s_ref[i, hd] + (rep(REP_DT + hd, i) * b_row) * xc
        so_ref[i, hd] = h_new
        acc[("y", hd)].append(h_new.astype(MXU_DTYPE))
        if i < SAMPLE_BLOCK - 1:
            return
        y_all = _dot(jnp.concatenate(acc.pop(("y", hd)), axis=0), ct_scr[g])
        yield
        y_blk = jnp.zeros((B_P, nb), F32)
        for ii in range(SAMPLE_BLOCK):
            y_blk = jnp.where(sels[ii], y_all[ii * B_P:(ii + 1) * B_P, :], y_blk)
        yt_scr[hd * B_P:(hd + 1) * B_P, :] = yt_scr[hd * B_P:(hd + 1) * B_P, :] + y_blk

    def attn_item(g, i, kn_all, vn_all, qcs, sink):
        kc_new, vc_new = col(kn_all, i), col(vn_all, i)
        yield
        k_t = k_ref[i, g]
        v_t = v_ref[i, g]
        ko_ref[i, g] = jnp.where(lane_k == WINDOW - 1, kc_new, pltpu.roll(k_t, WINDOW - 1, axis=1))
        vo_ref[i, g] = jnp.where(lane_k == WINDOW - 1, vc_new, pltpu.roll(v_t, WINDOW - 1, axis=1))
        s_rows = [jnp.where(lane_r == 0, rep(REP_SNEW + g * hpk + hh, i), jnp.sum(k_t * qcs[hh], axis=0, keepdims=True))
                  for hh in range(hpk)]
        s = jnp.concatenate(s_rows, axis=0) * (C_HD ** -0.5)
        m = jnp.maximum(jnp.max(s, axis=1, keepdims=True), sink)
        yield
        m_b = jnp.broadcast_to(m, s.shape)
        yield
        ex = jnp.exp(s - m_b)
        den = jnp.sum(ex, axis=1, keepdims=True) + jnp.exp(sink - m)
        yield
        inv_b = jnp.broadcast_to(1.0 / den, s.shape)
        yield
        p = ex * inv_b
        v_eff = jnp.where(lane_k == 0, vc_new, v_t)
        o_cols = [jnp.sum(v_eff * p[hh:hh + 1, :], axis=1, keepdims=True) for hh in range(hpk)]
        yield
        for hh in range(hpk):
            hd = g * hpk + hh
            acc[("o", hd)] = jnp.where(sels[i], o_cols[hh], acc[("o", hd)])
            if i == SAMPLE_BLOCK - 1:
                r0 = B_WIDTH + hd * C_HD
                yt_scr[r0:r0 + C_HD, :] = yt_scr[r0:r0 + C_HD, :] + acc.pop(("o", hd))

    def items():
        for h in range(A_HEADS):
            k_all = ut_scr[OFF_AK + h * hd_a:OFF_AK + (h + 1) * hd_a, :] * (A_QK ** -0.5)
            q_all = ut_scr[OFF_AQ + h * hd_a:OFF_AQ + (h + 1) * hd_a, :]
            acc[("h", h)] = []
            k_cols, q_cols = cols_mxu(k_all), cols_mxu(q_all)
            for i in range(SAMPLE_BLOCK):
                yield mlstm_item(h, i, k_cols[i * hd_a:(i + 1) * hd_a, :], q_cols[i * hd_a:(i + 1) * hd_a, :])
        for hd in range(B_HEADS):
            x_cols = cols_mxu(xt_scr[hd * B_P:(hd + 1) * B_P, :])
            acc[("y", hd)] = []
            for i in range(SAMPLE_BLOCK):
                yield ssd_item(hd, i, x_cols[i * B_P:(i + 1) * B_P, :])
        for g in range(C_KV):
            kn_all = knew_scr[g * C_HD:(g + 1) * C_HD, :]
            vn_all = ut_scr[OFF_CV + g * C_HD:OFF_CV + (g + 1) * C_HD, :]
            q_cols = [cols_mxu(qa_scr[(g * hpk + hh) * C_HD:(g * hpk + hh + 1) * C_HD, :]) for hh in range(hpk)]
            sink = sink_ref[g * hpk:(g + 1) * hpk, 0:1]
            for hh in range(hpk):
                acc[("o", g * hpk + hh)] = jnp.zeros((C_HD, nb), F32)
            for i in range(SAMPLE_BLOCK):
                yield attn_item(g, i, kn_all, vn_all, [qc[i * C_HD:(i + 1) * C_HD, :] for qc in q_cols], sink)

    _trace_interleaved(items(), ITEMS_IN_FLIGHT)

    @pl.when(j == pl.num_programs(1) - 1)
    def _layer_end():
        for h in range(A_HEADS):
            h_t = h_scr[:, h * A_V:(h + 1) * A_V].T
            hn = h_t * lax.rsqrt(jnp.mean(h_t * h_t, axis=0, keepdims=True) + NORM_EPS) * anw_ref[h * A_V:(h + 1) * A_V, :]
            ao = ut_scr[OFF_AO + h * A_V:OFF_AO + (h + 1) * A_V, :]
            az = ut_scr[OFF_AZ + h * A_V:OFF_AZ + (h + 1) * A_V, :]
            yall_scr[h * A_V:(h + 1) * A_V, :] = (hn * jax.nn.sigmoid(ao) * _silu(az)).astype(yall_scr.dtype)
        gw = B_WIDTH // B_GROUPS
        for g in range(B_GROUPS):
            r = slice(g * gw, (g + 1) * gw)
            y_g = yt_scr[r, :] + dsk_ref[r, :] * xt_scr[r, :]
            gated = y_g * _silu(ut_scr[OFF_BZ + g * gw:OFF_BZ + (g + 1) * gw, :])
            inv = lax.rsqrt(jnp.mean(gated * gated, axis=0, keepdims=True) + NORM_EPS)
            yall_scr[A_WIDTH + g * gw:A_WIDTH + (g + 1) * gw, :] = (gated * inv * bnw_ref[r, :]).astype(yall_scr.dtype)
        yall_scr[A_WIDTH + B_WIDTH:, :] = \
            (yt_scr[B_WIDTH:, :] * _silu(ut_scr[OFF_CZ:OFF_CZ + C_WIDTH, :])).astype(yall_scr.dtype)
        hs_new = hs_scr[...] + _dot_tn(yall_scr[...], wout_ref[...])
        hs_scr[...] = hs_new

        @pl.when(l == pl.num_programs(0) - 1)
        def _():
            y_ref[...] = hs_new


def _decode(x, lw, cos_s, sin_s, c_v, n_v, m_v, s_v, conv_v, k_v, v_v):
    depth, nb = c_v.shape[0], x.shape[0]
    assert nb == LANE, "samples sit on the 128 lanes next to lane-replicated parameters"
    grid = (depth, nb // SAMPLE_BLOCK)

    def per_layer(shape):
        nd = len(shape)
        return pl.BlockSpec((None,) + shape, lambda l, j, _nd=nd: (l,) + (0,) * _nd)

    def per_block(shape):
        nd = len(shape)
        return pl.BlockSpec((None, SAMPLE_BLOCK) + shape, lambda l, j, _nd=nd: (l, j) + (0,) * _nd)

    def weight(shape):
        nd = len(shape)
        return pl.BlockSpec((None,) + shape, lambda l, j, _nd=nd: (l,) + (0,) * _nd, pipeline_mode=pl.Buffered(1))

    fixed = lambda shape: pl.BlockSpec(shape, lambda l, j, _nd=len(shape): (0,) * _nd)
    state_specs = [
        per_block((A_HEADS, A_QK, A_V)), per_layer((A_HEADS, A_QK, nb)), per_layer((A_HEADS, nb)),
        per_block((B_HEADS, B_P, B_STATE)), per_layer((CONV_W - 1, nb, B_CONV_DIM)),
        per_block((C_KV, C_HD, WINDOW)), per_block((C_KV, C_HD, WINDOW)),
    ]
    in_specs = [
        fixed((nb, D_MODEL)),
        per_layer((1, D_MODEL)), weight((D_INT, D_MODEL)), weight((D_MIX, D_MODEL)),
        per_layer((N_GATES, LANE)), per_layer((N_GATES, LANE)), per_layer((A_WIDTH, LANE)),
        per_layer((CONV_W, B_CONV_DIM)), per_layer((1, B_CONV_DIM)),
        per_layer((B_WIDTH, LANE)), per_layer((B_WIDTH, LANE)), per_layer((LANE, LANE)), per_layer((LANE, LANE)),
        per_layer((C_HEADS, LANE)), fixed((C_HD // 2, LANE)), fixed((C_HD // 2, LANE)),
    ] + state_specs
    states = (c_v, n_v, m_v, s_v, conv_v, k_v, v_v)
    out_shape = (jax.ShapeDtypeStruct((nb, D_MODEL), F32),) + tuple(jax.ShapeDtypeStruct(a.shape, F32) for a in states)
    scratch = [
        pltpu.VMEM((nb, D_MODEL), F32),
        pltpu.VMEM((D_INT, nb), F32),
        pltpu.VMEM((nb, A_WIDTH), F32),
        pltpu.VMEM((nb, B_CONV_DIM), F32),
        pltpu.VMEM((B_WIDTH, nb), F32),
        pltpu.VMEM((B_GROUPS, B_STATE, nb), MXU_DTYPE),
        pltpu.VMEM((C_WIDTH, nb), F32),
        pltpu.VMEM((C_KV * C_HD, nb), F32),
        pltpu.VMEM((N_REP, nb, LANE), F32),
        pltpu.VMEM((nb, A_WIDTH), F32),
        pltpu.VMEM((B_WIDTH + C_WIDTH, nb), F32),
        pltpu.VMEM((D_MIX, nb), MXU_DTYPE),
    ]
    return pl.pallas_call(
        _decode_kernel,
        grid=grid, in_specs=in_specs, out_specs=(fixed((nb, D_MODEL)),) + tuple(state_specs),
        out_shape=out_shape, scratch_shapes=scratch,
        compiler_params=pltpu.CompilerParams(dimension_semantics=("arbitrary", "arbitrary"),
                                             vmem_limit_bytes=VMEM_LIMIT_BYTES),
        name="decode",
    )(x, lw["norm_w"], lw["w_t"], lw["w_out"], lw["gb_rep"], lw["al_rep"], lw["a_nw_rep"], lw["conv_w"],
      lw["conv_b"], lw["d_rep"], lw["b_nw_rep"], lw["qn_rep"], lw["kn_rep"], lw["sink_rep"], cos_s, sin_s, *states)


def _prep_weights(norm_w, w_in, a_igate_b, a_fgate_b, a_norm_w, b_conv_w, b_conv_b, b_dt_bias, b_A_log, b_D,
                  b_norm_w, c_qnorm_w, c_knorm_w, c_sinks, w_out):
    depth = w_in.shape[0]
    w_t = jnp.pad(jnp.swapaxes(w_in, 1, 2).astype(MXU_DTYPE), ((0, 0), (0, D_PAD - D_INT), (0, 0)))
    gbias = jnp.concatenate([a_igate_b, a_fgate_b, b_dt_bias], axis=-1)
    alog = jnp.concatenate([jnp.zeros((depth, 2 * A_HEADS), b_A_log.dtype), b_A_log], axis=-1)

    def rep(v):
        return jnp.broadcast_to(v.astype(F32)[..., None], v.shape + (LANE,))

    return {
        "w_t": w_t, "w_out": w_out.astype(MXU_DTYPE),
        "norm_w": norm_w.astype(F32)[:, None, :],
        "gb_rep": rep(gbias), "al_rep": rep(alog),
        "a_nw_rep": rep(a_norm_w), "conv_w_rep": rep(b_conv_w), "conv_b_rep": rep(b_conv_b),
        "conv_w": b_conv_w.astype(F32), "conv_b": b_conv_b.astype(F32)[:, None, :],
        "d_rep": rep(jnp.repeat(b_D, B_P, axis=-1)), "b_nw_rep": rep(b_norm_w),
        "qn_rep": rep(jnp.tile(c_qnorm_w, (1, LANE // C_HD))), "kn_rep": rep(jnp.tile(c_knorm_w, (1, LANE // C_HD))),
        "sinks": c_sinks.astype(F32), "sink_rep": rep(c_sinks),
    }


def _rope_tables_t(pos, width=None):
    half = C_HD // 2
    inv = ROPE_THETA ** (-jnp.arange(half, dtype=F32) / half)
    ang = inv[:, None] * pos.astype(F32)[None, :]
    cos, sin = jnp.cos(ang), jnp.sin(ang)
    if width is not None:
        cos, sin = jnp.broadcast_to(cos, (half, width)), jnp.broadcast_to(sin, (half, width))
    return cos, sin


def _kernel_impl(x_prompt, x_sample, state_mlstm_C, state_mlstm_n, state_mlstm_m, state_ssm, state_conv,
                 cache_k, cache_v, norm_w, w_in, a_igate_b, a_fgate_b, a_norm_w, b_conv_w, b_conv_b,
                 b_dt_bias, b_A_log, b_D, b_norm_w, c_qnorm_w, c_knorm_w, c_sinks, w_out, *, tb):
    depth = w_in.shape[0]
    t_len = x_prompt.shape[1]
    lw = _prep_weights(norm_w, w_in, a_igate_b, a_fgate_b, a_norm_w, b_conv_w, b_conv_b, b_dt_bias, b_A_log,
                       b_D, b_norm_w, c_qnorm_w, c_knorm_w, c_sinks, w_out)
    cos_p, sin_p = _rope_tables_t(jnp.arange(t_len, dtype=jnp.int32))
    hp = x_prompt
    st_prompt = []
    for l in range(depth):
        res = _prompt_layer(hp, lw, l, cos_p, sin_p, tb)
        hp = res[0]
        st_prompt.append(res[1:])
    p_states = _unpad_prompt_states(*[jnp.stack(t) for t in zip(*st_prompt)])

    assert x_sample.shape[1] == 1
    cos_s, sin_s = _rope_tables_t(PAST_LEN + jnp.arange(1, dtype=jnp.int32), LANE)
    outs = _decode(
        x_sample[:, 0, :], lw, cos_s, sin_s,
        jnp.transpose(state_mlstm_C, (0, 1, 2, 4, 3)), jnp.transpose(state_mlstm_n, (0, 2, 3, 1)),
        jnp.transpose(state_mlstm_m, (0, 2, 1)), state_ssm, jnp.transpose(state_conv, (0, 2, 1, 3)),
        jnp.transpose(cache_k, (0, 1, 3, 4, 2)), jnp.transpose(cache_v, (0, 1, 3, 4, 2)))
    hs, c_o, n_o, m_o, s_o, conv_o, k_o, v_o = outs
    s_states = (jnp.transpose(c_o, (0, 1, 2, 4, 3)), jnp.transpose(n_o, (0, 3, 1, 2)), jnp.transpose(m_o, (0, 2, 1)),
                s_o, jnp.transpose(conv_o, (0, 2, 1, 3)),
                jnp.transpose(k_o, (0, 1, 4, 2, 3)), jnp.transpose(v_o, (0, 1, 4, 2, 3)))
    return (hp, hs[:, None, :], *p_states, *s_states)


def kernel(x_prompt, x_sample, state_mlstm_C, state_mlstm_n, state_mlstm_m, state_ssm, state_conv, cache_k, cache_v, norm_w, w_in, a_igate_b, a_fgate_b, a_norm_w, b_conv_w, b_conv_b, b_dt_bias, b_A_log, b_D, b_norm_w, c_qnorm_w, c_knorm_w, c_sinks, w_out):
    return _kernel_impl(x_prompt, x_sample, state_mlstm_C, state_mlstm_n, state_mlstm_m, state_ssm, state_conv,
                        cache_k, cache_v, norm_w, w_in, a_igate_b, a_fgate_b, a_norm_w, b_conv_w, b_conv_b,
                        b_dt_bias, b_A_log, b_D, b_norm_w, c_qnorm_w, c_knorm_w, c_sinks, w_out, tb=PROMPT_BLOCK)
```

```python
import functools

import jax
import jax.numpy as jnp
import numpy as np
from jax import lax
from jax.experimental import pallas as pl
from jax.experimental.pallas import tpu as pltpu

F32 = jnp.float32
MXU_DTYPE = jnp.bfloat16

D_MODEL = 1024
A_HEADS, A_QK, A_V = 4, 64, 128
A_WIDTH = A_HEADS * A_V
B_HEADS, B_P, B_GROUPS, B_STATE = 8, 64, 2, 128
B_WIDTH = B_HEADS * B_P
CONV_W = 4
B_CONV_DIM = B_WIDTH + 2 * B_GROUPS * B_STATE
C_HEADS, C_KV, C_HD = 8, 2, 64
C_WIDTH = C_HEADS * C_HD
WINDOW = 128
ROPE_THETA = 10000.0
D_MIX = A_WIDTH + B_WIDTH + C_WIDTH
NORM_EPS = 1e-6
PAST_LEN = 8192

LANE = 128
SUBLANE = 8
HALF = LANE // 2

CHUNK = 128
PROMPT_BLOCK = 256
SAMPLE_BLOCK = SUBLANE
N_GATES = 2 * A_HEADS + B_HEADS

_SRC_SIZES = (A_HEADS * A_QK, A_HEADS * A_QK, A_WIDTH, A_WIDTH, A_WIDTH, A_HEADS, A_HEADS,
              B_WIDTH, B_CONV_DIM, B_HEADS, C_WIDTH, C_KV * C_HD, C_KV * C_HD, C_WIDTH)
(OFF_AQ, OFF_AK, OFF_AV, OFF_AO, OFF_AZ, OFF_GA, _, OFF_BZ, OFF_BXBC, OFF_GB, OFF_CQ, OFF_CK, OFF_CV, OFF_CZ,
 D_INT) = np.concatenate([[0], np.cumsum(_SRC_SIZES)]).tolist()
assert all(o % SUBLANE == 0 for o in (OFF_GA, OFF_BZ, OFF_BXBC, OFF_GB, OFF_CQ, OFF_CK, OFF_CV, OFF_CZ))
INPROJ_SITES = (2, 6)
INPROJ_PIECES = len(INPROJ_SITES)
PROMPT_ITEMS_IN_FLIGHT = 2
_PIECES_PER_BLOCK = INPROJ_PIECES * (PROMPT_BLOCK // CHUNK)
PIECE_ROWS = -(-D_INT // (_PIECES_PER_BLOCK * 2 * SUBLANE)) * 2 * SUBLANE
LAST_PIECE_START = D_INT - PIECE_ROWS
assert LAST_PIECE_START % (2 * SUBLANE) == 0
OUT_PAIR = 2

C_EXT = A_V + 2 * SUBLANE
VMEM_LIMIT_BYTES = 56 * 1024 * 1024

REP_SP, REP_SL = 0, A_HEADS
REP_DA, REP_DT = 2 * A_HEADS, 2 * A_HEADS + B_HEADS
REP_SNEW = 2 * A_HEADS + 2 * B_HEADS
REP_IDEN = REP_SNEW + C_HEADS
N_REP = REP_IDEN + A_HEADS


def _dot(a, b):
    return jnp.dot(a, b, preferred_element_type=F32)


def _dot_nt(a, b):
    return lax.dot_general(a, b, (((1,), (1,)), ((), ())), preferred_element_type=F32)


def _dot_tn(a, b):
    return lax.dot_general(a, b, (((0,), (0,)), ((), ())), preferred_element_type=F32)


def _split3(z):
    hi = z.astype(MXU_DTYPE)
    r1 = z - hi.astype(F32)
    mid = r1.astype(MXU_DTYPE)
    lo = (r1 - mid.astype(F32)).astype(MXU_DTYPE)
    return hi, mid, lo


def _softplus_terms(x):
    t = jnp.log1p(jnp.exp(-jnp.abs(x)))
    return jnp.maximum(x, 0.0) + t, jnp.minimum(x, 0.0) - t


def _silu(x):
    return x * jax.nn.sigmoid(x)


def _rmsnorm_rows(x, w):
    return x * lax.rsqrt(jnp.mean(x * x, axis=-1, keepdims=True) + NORM_EPS) * w


def _norm_rope_t(x_t, w_rep, cos, sin):
    half = C_HD // 2
    out = []
    for hh in range(LANE // C_HD):
        xh = x_t[hh * C_HD:(hh + 1) * C_HD, :]
        ms = jnp.mean(xh * xh, axis=0, keepdims=True)
        xn = xh * lax.rsqrt(ms + NORM_EPS) * w_rep[hh * C_HD:(hh + 1) * C_HD, :]
        x1, x2 = xn[:half, :], xn[half:, :]
        out += [x1 * cos - x2 * sin, x2 * cos + x1 * sin]
    return jnp.concatenate(out, axis=0)


def _prompt_layer_kernel(
        sinks_ref,
        x_ref, xnext_ref, nw_ref, wt_ref, wout_ref, gb_ref, al_ref, anw_ref, cw_ref, cb_ref, dsk_ref, bnw_ref,
        qnw_ref, knw_ref, cos_ref, sin_ref,
        o_ref, cst_ref, mst_ref, hst_ref, convt_ref, kt_ref, vt_ref,
        u0_scr, u1_scr, xn_scr, y_scr, kprev_scr, roll_scr, *, tb):
    L = CHUNK
    nch = tb // L
    t_id = pl.program_id(1)
    slot = t_id % 2

    @pl.when(t_id == 0)
    def _():
        cst_ref[...] = jnp.zeros_like(cst_ref)
        mst_ref[...] = jnp.zeros_like(mst_ref)
        hst_ref[...] = jnp.zeros_like(hst_ref)
        convt_ref[...] = jnp.zeros_like(convt_ref)
        kt_ref[...] = jnp.zeros_like(kt_ref)
        vt_ref[...] = jnp.zeros_like(vt_ref)
        kprev_scr[...] = jnp.zeros_like(kprev_scr)
        roll_scr[...] = jnp.zeros_like(roll_scr)
        u0_scr[...] = _dot_nt(wt_ref[...], _rmsnorm_rows(x_ref[:tb, :], nw_ref[...]).astype(MXU_DTYPE))

    xn_scr[...] = _rmsnorm_rows(xnext_ref[...], nw_ref[...]).astype(MXU_DTYPE)
    u_cur, u_next = u0_scr, u1_scr

    row = lax.broadcasted_iota(jnp.int32, (L, L), 0)
    lane = lax.broadcasted_iota(jnp.int32, (L, L), 1)
    lo = lane < HALF
    top = row < B_P
    src_le_t = row <= lane
    tri = jnp.where(lane <= row, 1.0, 0.0).astype(MXU_DTYPE)
    grow = lax.broadcasted_iota(jnp.int32, (N_GATES, L), 0)
    krow = lax.broadcasted_iota(jnp.int32, (2 * L, L), 0)
    klane = lax.broadcasted_iota(jnp.int32, (2 * L, L), 1)
    neg_inf = -jnp.inf
    a_neg = -jnp.exp(al_ref[...])

    def chunk(c, carry):
        cols = pl.ds(pl.multiple_of(c * L, L), L)
        ycols = pl.ds(pl.multiple_of((t_id % OUT_PAIR) * tb + c * L, L), L)

        def next_block_piece(site):
            if site not in INPROJ_SITES:
                return
            k = INPROJ_SITES.index(site)
            start = jnp.minimum((c * INPROJ_PIECES + k) * PIECE_ROWS, LAST_PIECE_START)
            wrows = pl.ds(pl.multiple_of(start, 2 * SUBLANE), PIECE_ROWS)
            u_next[wrows, :] = _dot_nt(wt_ref[wrows, :], xn_scr[...])

        pre_r = jnp.concatenate([u_cur[OFF_GA:OFF_GA + 2 * A_HEADS, cols], u_cur[OFF_GB:OFF_GB + B_HEADS, cols]],
                                axis=0) + gb_ref[...]
        sp_r, ls_r = _softplus_terms(pre_r)
        z_r = jnp.where((grow >= A_HEADS) & (grow < 2 * A_HEADS), ls_r,
                        jnp.where(grow >= 2 * A_HEADS, sp_r * a_neg, 0.0))
        zp = _split3(z_r)
        cum_r = _dot_nt(zp[0], tri) + _dot_nt(zp[1], tri) + _dot_nt(zp[2], tri)
        colf = jnp.concatenate([pre_r, cum_r, jnp.zeros((L - 2 * N_GATES, L), F32)], axis=0).T

        def piece_item(site):
            next_block_piece(site)
            return
            yield

        def mlstm_head(h, q_t, k_p):
            e = h % 2
            k_m = jnp.where(lo if e == 0 else jnp.logical_not(lo), k_p, 0.0).astype(MXU_DTYPE)
            v_t = u_cur[OFF_AV + h * LANE:OFF_AV + (h + 1) * LANE, cols]
            b_row = cum_r[A_HEADS + h:A_HEADS + h + 1, :]
            i_row = pre_r[h:h + 1, :]
            c_col = colf[:, N_GATES + A_HEADS + h:N_GATES + A_HEADS + h + 1] - colf[:, h:h + 1]
            b_end = b_row[:, L - 1:L]
            m_prev = mst_ref[h:h + 1, 0:1]
            c_prev = cst_ref[h]

            g_row = b_end - b_row + i_row
            m_loc = jnp.max(g_row, axis=1, keepdims=True)
            d_t = jnp.where(src_le_t, b_row - c_col, neg_inf)
            inter = b_row + m_prev
            m_t = jnp.maximum(inter, jnp.max(d_t, axis=0, keepdims=True))
            res = _dot(jnp.concatenate([k_m, c_prev.astype(MXU_DTYPE)], axis=0), q_t)
            w_row = jnp.exp(g_row - m_loc)
            v_ext = jnp.concatenate([v_t, jnp.ones((1, L), F32), jnp.zeros((C_EXT - A_V - 1, L), F32)], axis=0)
            c_loc = _dot((v_ext * w_row).astype(MXU_DTYPE), k_m)
            e_t = jnp.exp(d_t - m_t)
            si = jnp.exp(inter - m_t)
            yield
            s_t = e_t * res[:L, :]
            sv = _dot(v_t.astype(MXU_DTYPE), s_t.astype(MXU_DTYPE))
            yield
            num = sv + si * res[L:L + A_V, :]
            den = jnp.sum(s_t, axis=0, keepdims=True) + si * res[L + A_V:L + A_V + 1, :]
            hh = num * (1.0 / jnp.maximum(jnp.abs(den), jnp.exp(-m_t)))
            hn = hh * lax.rsqrt(jnp.mean(hh * hh, axis=0, keepdims=True) + NORM_EPS) * anw_ref[h * LANE:(h + 1) * LANE, :]
            ao = u_cur[OFF_AO + h * LANE:OFF_AO + (h + 1) * LANE, cols]
            az = u_cur[OFF_AZ + h * LANE:OFF_AZ + (h + 1) * LANE, cols]
            y_scr[h * LANE:(h + 1) * LANE, ycols] = (hn * jax.nn.sigmoid(ao) * _silu(az)).astype(y_scr.dtype)

            m_new = jnp.maximum(b_end + m_prev, m_loc)
            sp = jnp.exp(b_end + m_prev - m_new)
            sl = jnp.exp(m_loc - m_new)
            cst_ref[h] = sp * c_prev + sl * c_loc
            mst_ref[h:h + 1, :] = jnp.broadcast_to(m_new, (1, LANE))

        def conv_silu():
            cur = u_cur[OFF_BXBC:OFF_BXBC + B_CONV_DIM, cols]
            acc = cb_ref[...] + cur * cw_ref[CONV_W - 1]
            for j in range(1, CONV_W):
                rolled = pltpu.roll(cur, j, axis=1)
                acc = acc + jnp.where(lane[:1, :] >= j, rolled, roll_scr[j - 1]) * cw_ref[CONV_W - 1 - j]
                roll_scr[j - 1] = rolled
            convt_ref[...] = cur
            return _silu(acc)

        def ssd_group(g, xbc):
            b_t = xbc[B_WIDTH + g * B_STATE:B_WIDTH + (g + 1) * B_STATE, :]
            c_t = xbc[B_WIDTH + (B_GROUPS + g) * B_STATE:B_WIDTH + (B_GROUPS + g + 1) * B_STATE, :]
            b_m = b_t.T.astype(MXU_DTYPE)
            hpg = B_HEADS // B_GROUPS
            h_prev = hst_ref[g * hpg * B_P:(g + 1) * hpg * B_P, :]
            yield
            res = _dot(jnp.concatenate([b_m, h_prev.astype(MXU_DTYPE)], axis=0), c_t.astype(MXU_DTYPE))
            x_ts, h_locs, rows_of = [], [], []
            for pj in range(hpg // 2):
                pb = g * (hpg // 2) + pj
                x_t = xbc[pb * LANE:(pb + 1) * LANE, :]
                a_ends, dec_rows, ea_rows, a_rows, a_cols, dt_rows = [], [], [], [], [], []
                for e in range(2):
                    gi = 2 * A_HEADS + 2 * pb + e
                    a_row = cum_r[gi:gi + 1, :]
                    dt_row = sp_r[gi:gi + 1, :]
                    a_end = a_row[:, L - 1:L]
                    a_rows.append(a_row)
                    a_cols.append(colf[:, N_GATES + gi:N_GATES + gi + 1])
                    dt_rows.append(dt_row)
                    a_ends.append(jnp.exp(a_end))
                    dec_rows.append(jnp.exp(a_end - a_row) * dt_row)
                    ea_rows.append(jnp.exp(a_row))
                h_locs.append(_dot((x_t * jnp.where(top, dec_rows[0], dec_rows[1])).astype(MXU_DTYPE), b_m))
                x_ts.append(x_t)
                rows_of.append((a_ends, ea_rows, a_rows, a_cols, dt_rows))
            yield
            cb_tt = res[:L, :]
            ys_all = []
            for pj in range(hpg // 2):
                a_ends, ea_rows, a_rows, a_cols, dt_rows = rows_of[pj]
                ys = []
                for e in range(2):
                    w_t = jnp.exp(jnp.where(src_le_t, a_rows[e] - a_cols[e], neg_inf)) * cb_tt
                    xdt = (x_ts[pj][e * B_P:(e + 1) * B_P, :] * dt_rows[e]).astype(MXU_DTYPE)
                    ys.append(_dot(xdt, w_t.astype(MXU_DTYPE)))
                ys_all.append(ys)
            yield
            gated = []
            for pj in range(hpg // 2):
                pb = g * (hpg // 2) + pj
                a_ends, ea_rows, a_rows, a_cols, dt_rows = rows_of[pj]
                inter = res[L + pj * LANE:L + (pj + 1) * LANE, :] * jnp.where(top, ea_rows[0], ea_rows[1])
                y_p = jnp.concatenate(ys_all[pj], axis=0) + inter + dsk_ref[pb * LANE:(pb + 1) * LANE, :] * x_ts[pj]
                hst_ref[pb * LANE:(pb + 1) * LANE, :] = \
                    jnp.where(top, a_ends[0], a_ends[1]) * h_prev[pj * LANE:(pj + 1) * LANE, :] + h_locs[pj]
                bz = u_cur[OFF_BZ + pb * LANE:OFF_BZ + (pb + 1) * LANE, cols]
                gated.append(y_p * _silu(bz))
            ms = sum(jnp.sum(gp * gp, axis=0, keepdims=True) for gp in gated) * (1.0 / (B_WIDTH // B_GROUPS))
            inv = lax.rsqrt(ms + NORM_EPS)
            for pj, gp in enumerate(gated):
                pb = g * (hpg // 2) + pj
                y_scr[A_WIDTH + pb * LANE:A_WIDTH + (pb + 1) * LANE, ycols] = \
                    (gp * inv * bnw_ref[pb * LANE:(pb + 1) * LANE, :]).astype(y_scr.dtype)

        cos = cos_ref[:, cols]
        sin = sin_ref[:, cols]
        ppg = C_HEADS // 2 // C_KV
        shared = {}

        def swa_keys():
            k_t = _norm_rope_t(u_cur[OFF_CK:OFF_CK + LANE, cols], knw_ref[...], cos, sin)
            v_t = u_cur[OFF_CV:OFF_CV + LANE, cols]
            k_p = k_t.T
            k_sw = pltpu.roll(k_p, HALF, axis=1)
            k_cur = [[jnp.where(lo, k_p, 0.0).astype(MXU_DTYPE), jnp.where(lo, 0.0, k_sw).astype(MXU_DTYPE)],
                     [jnp.where(lo, k_sw, 0.0).astype(MXU_DTYPE), jnp.where(lo, 0.0, k_p).astype(MXU_DTYPE)]]
            shift = jnp.where(jnp.logical_and(t_id == 0, c == 0), 2 * L, 0)
            valid = ((krow < L) & (krow > klane + shift)) | ((krow >= L) & (krow - L <= klane))
            shared.update(k_t=k_t, v_t=v_t, k_cur=k_cur, valid=valid)

        def swa_pair(pb):
            g = pb // ppg
            k_cur, v_t, valid = shared["k_cur"], shared["v_t"], shared["valid"]
            q_t = _norm_rope_t(u_cur[OFF_CQ + pb * LANE:OFF_CQ + (pb + 1) * LANE, cols], qnw_ref[...], cos, sin)
            keys = jnp.concatenate([kprev_scr[2 * g], k_cur[g][0], kprev_scr[2 * g + 1], k_cur[g][1]], axis=0)
            sc = _dot(keys, q_t.astype(MXU_DTYPE)) * (C_HD ** -0.5)
            v_g = jnp.concatenate([vt_ref[g * C_HD:(g + 1) * C_HD, :], v_t[g * C_HD:(g + 1) * C_HD, :]],
                                  axis=1).astype(MXU_DTYPE)
            yield
            outs = []
            for e in range(2):
                s = jnp.where(valid, sc[e * 2 * L:(e + 1) * 2 * L, :], neg_inf)
                sink = sinks_ref[2 * pb + e]
                m = jnp.maximum(jnp.max(s, axis=0, keepdims=True), sink)
                ex = jnp.exp(s - m)
                p = ex * (1.0 / (jnp.sum(ex, axis=0, keepdims=True) + jnp.exp(sink - m)))
                outs.append(_dot(v_g, p.astype(MXU_DTYPE)))
            yield
            cz = u_cur[OFF_CZ + pb * LANE:OFF_CZ + (pb + 1) * LANE, cols]
            y_scr[A_WIDTH + B_WIDTH + pb * LANE:A_WIDTH + B_WIDTH + (pb + 1) * LANE, ycols] = \
                (jnp.concatenate(outs, axis=0) * _silu(cz)).astype(y_scr.dtype)

        def items():
            for h in range(A_HEADS):
                if h % 2 == 0:
                    pb = h // 2
                    q_t = u_cur[OFF_AQ + pb * LANE:OFF_AQ + (pb + 1) * LANE, cols].astype(MXU_DTYPE)
                    k_p = (u_cur[OFF_AK + pb * LANE:OFF_AK + (pb + 1) * LANE, cols] * (A_QK ** -0.5)).T
                yield mlstm_head(h, q_t, k_p)
                yield piece_item(1 + h)
            xbc = conv_silu()
            for g in range(B_GROUPS):
                yield ssd_group(g, xbc)
                yield piece_item(1 + A_HEADS + g)
            swa_keys()
            for pb in range(C_HEADS // 2):
                yield swa_pair(pb)
                if pb == 1:
                    yield piece_item(1 + A_HEADS + B_GROUPS)

        next_block_piece(0)
        _trace_interleaved(items(), PROMPT_ITEMS_IN_FLIGHT)
        for g in range(C_KV):
            for e in range(2):
                kprev_scr[2 * g + e] = shared["k_cur"][g][e]
        kt_ref[...] = shared["k_t"]
        vt_ref[...] = shared["v_t"]
        return carry

    def run_chunks(cur, nxt):
        nonlocal u_cur, u_next
        u_cur, u_next = cur, nxt
        lax.fori_loop(0, nch, lambda c, carry: chunk(c, carry), 0)

    @pl.when(slot == 0)
    def _():
        run_chunks(u0_scr, u1_scr)

    @pl.when(slot == 1)
    def _():
        run_chunks(u1_scr, u0_scr)

    @pl.when(t_id % OUT_PAIR == OUT_PAIR - 1)
    def _():
        o_ref[...] = x_ref[...] + _dot_tn(y_scr[...], wout_ref[...])


def _prompt_layer(x, lw, l, cos_t, sin_t, tb):
    n_seq, t_len, _ = x.shape
    assert t_len % (OUT_PAIR * tb) == 0 and tb % CHUNK == 0 and tb == PROMPT_BLOCK
    n_t = t_len // tb
    grid = (n_seq, n_t)

    def per_seq(shape):
        nd = len(shape)
        return pl.BlockSpec((None,) + shape, lambda n, t, _nd=nd: (n,) + (0,) * _nd)

    def per_layer(shape, **kw):
        nd = len(shape)
        return pl.BlockSpec((None,) + shape, lambda n, t, _nd=nd: (l,) + (0,) * _nd, **kw)

    in_specs = [
        pl.BlockSpec(memory_space=pltpu.SMEM),
        pl.BlockSpec((None, OUT_PAIR * tb, D_MODEL), lambda n, t: (n, t // OUT_PAIR, 0)),
        pl.BlockSpec((None, tb, D_MODEL), lambda n, t: (n, jnp.minimum(t + 1, n_t - 1), 0)),
        per_layer((1, D_MODEL)),
        per_layer((D_INT, D_MODEL), pipeline_mode=pl.Buffered(1)),
        per_layer((D_MIX, D_MODEL), pipeline_mode=pl.Buffered(1)),
        per_layer((N_GATES, LANE)), per_layer((N_GATES, LANE)),
        per_layer((A_WIDTH, LANE)),
        per_layer((CONV_W, B_CONV_DIM, LANE)), per_layer((B_CONV_DIM, LANE)),
        per_layer((B_WIDTH, LANE)), per_layer((B_WIDTH, LANE)),
        per_layer((LANE, LANE)), per_layer((LANE, LANE)),
        pl.BlockSpec((C_HD // 2, tb), lambda n, t: (0, t)),
        pl.BlockSpec((C_HD // 2, tb), lambda n, t: (0, t)),
    ]
    out_shape = (
        jax.ShapeDtypeStruct((n_seq, t_len, D_MODEL), F32),
        jax.ShapeDtypeStruct((n_seq, A_HEADS, C_EXT, LANE), F32),
        jax.ShapeDtypeStruct((n_seq, SUBLANE, LANE), F32),
        jax.ShapeDtypeStruct((n_seq, B_HEADS * B_P, B_STATE), F32),
        jax.ShapeDtypeStruct((n_seq, B_CONV_DIM, LANE), F32),
        jax.ShapeDtypeStruct((n_seq, LANE, WINDOW), F32),
        jax.ShapeDtypeStruct((n_seq, LANE, WINDOW), F32),
    )
    out_specs = (
        pl.BlockSpec((None, OUT_PAIR * tb, D_MODEL), lambda n, t: (n, t // OUT_PAIR, 0)),
        per_seq((A_HEADS, C_EXT, LANE)), per_seq((SUBLANE, LANE)),
        per_seq((B_HEADS * B_P, B_STATE)), per_seq((B_CONV_DIM, LANE)),
        per_seq((LANE, WINDOW)), per_seq((LANE, WINDOW)),
    )
    scratch = [
        pltpu.VMEM((D_INT, tb), F32),
        pltpu.VMEM((D_INT, tb), F32),
        pltpu.VMEM((tb, D_MODEL), MXU_DTYPE),
        pltpu.VMEM((D_MIX, OUT_PAIR * tb), MXU_DTYPE),
        pltpu.VMEM((2 * C_KV, CHUNK, LANE), MXU_DTYPE),
        pltpu.VMEM((CONV_W - 1, B_CONV_DIM, LANE), F32),
    ]
    return pl.pallas_call(
        functools.partial(_prompt_layer_kernel, tb=tb),
        grid=grid, in_specs=in_specs, out_specs=out_specs, out_shape=out_shape, scratch_shapes=scratch,
        compiler_params=pltpu.CompilerParams(dimension_semantics=("arbitrary", "arbitrary"),
                                             vmem_limit_bytes=VMEM_LIMIT_BYTES),
        name="prompt_layer",
    )(lw["sinks"][l], x, x, lw["norm_w"], lw["w_t"], lw["w_out"], lw["gb_rep"], lw["al_rep"], lw["a_nw_rep"],
      lw["conv_w_rep"], lw["conv_b_rep"], lw["d_rep"], lw["b_nw_rep"], lw["qn_rep"], lw["kn_rep"], cos_t, sin_t)


def _unpad_prompt_states(cst, mst, hst, convt, kt, vt):
    lead = cst.shape[:2]
    c_even, c_odd = cst[:, :, 0::2, :A_V, :HALF], cst[:, :, 1::2, :A_V, HALF:]
    c_state = jnp.stack([c_even, c_odd], axis=3).reshape(lead + (A_HEADS, A_V, A_QK))
    n_even, n_odd = cst[:, :, 0::2, A_V, :HALF], cst[:, :, 1::2, A_V, HALF:]
    n_state = jnp.stack([n_even, n_odd], axis=3).reshape(lead + (A_HEADS, A_QK))
    conv = jnp.swapaxes(convt[:, :, :, LANE - (CONV_W - 1):], 2, 3)
    k1 = jnp.transpose(kt.reshape(lead + (C_KV, C_HD, WINDOW)), (0, 1, 4, 2, 3))
    v1 = jnp.transpose(vt.reshape(lead + (C_KV, C_HD, WINDOW)), (0, 1, 4, 2, 3))
    return (c_state, n_state, mst[:, :, :A_HEADS, 0], hst.reshape(lead + (B_HEADS, B_P, B_STATE)), conv, k1, v1)


ITEMS_IN_FLIGHT = 16


def _trace_interleaved(item_iter, depth):
    active, exhausted = [], False
    while True:
        while not exhausted and len(active) < depth:
            nxt = next(item_iter, None)
            if nxt is None:
                exhausted = True
            else:
                active.append(nxt)
        if not active:
            return
        still = []
        for gen in active:
            try:
                next(gen)
                still.append(gen)
            except StopIteration:
                pass
        active = still


def _decode_kernel(
        x_ref, nw_ref, wt_ref, wout_ref, gb_ref, al_ref, anw_ref, cw_ref, cb_ref, dsk_ref, bnw_ref,
        qnw_ref, knw_ref, sink_ref, cos_ref, sin_ref,
        c_ref, n_ref, m_ref, s_ref, cv_ref, k_ref, v_ref,
        y_ref, co_ref, no_ref, mo_ref, so_ref, cvo_ref, ko_ref, vo_ref,
        hs_scr, ut_scr, vrow_scr, xbc_scr, xt_scr, ct_scr, qa_scr, knew_scr, rep_scr, h_scr, yt_scr, yall_scr):
    l = pl.program_id(0)
    j = pl.program_id(1)
    nb = x_ref.shape[0]
    hd_a = A_QK

    @pl.when(j == 0)
    def _layer_start():
        @pl.when(l == 0)
        def _():
            hs_scr[...] = x_ref[...]

        xn = _rmsnorm_rows(hs_scr[...], nw_ref[...]).astype(MXU_DTYPE)
        ut_scr[...] = _dot_nt(wt_ref[...], xn)
        yt_scr[...] = jnp.zeros_like(yt_scr)

        pre = jnp.concatenate([ut_scr[OFF_GA:OFF_GA + 2 * A_HEADS, :], ut_scr[OFF_GB:OFF_GB + B_HEADS, :]],
                              axis=0) + gb_ref[...]
        spl, lsg = _softplus_terms(pre)
        ig, lf, dt = pre[:A_HEADS], lsg[A_HEADS:2 * A_HEADS], spl[2 * A_HEADS:]
        d_a = jnp.exp(dt * (-jnp.exp(al_ref[2 * A_HEADS:, :])))
        m0 = m_ref[...]
        m_new = jnp.maximum(lf + m0, ig)
        sp = jnp.exp(lf + m0 - m_new)
        sl = jnp.exp(ig - m_new)
        mo_ref[...] = m_new
        inv_den = []
        for h in range(A_HEADS):
            k_t = ut_scr[OFF_AK + h * hd_a:OFF_AK + (h + 1) * hd_a, :] * (A_QK ** -0.5)
            q_t = ut_scr[OFF_AQ + h * hd_a:OFF_AQ + (h + 1) * hd_a, :]
            n_new = sp[h:h + 1] * n_ref[h] + sl[h:h + 1] * k_t
            no_ref[h] = n_new
            den = jnp.sum(n_new * q_t, axis=0, keepdims=True)
            inv_den.append(1.0 / jnp.maximum(jnp.abs(den), jnp.exp(-m_new[h:h + 1])))

        cos, sin = cos_ref[...], sin_ref[...]
        kn_t = _norm_rope_t(ut_scr[OFF_CK:OFF_CK + LANE, :], knw_ref[...], cos, sin)
        knew_scr[...] = kn_t
        s_new = []
        for pb in range(C_HEADS // 2):
            q_t = _norm_rope_t(ut_scr[OFF_CQ + pb * LANE:OFF_CQ + (pb + 1) * LANE, :], qnw_ref[...], cos, sin)
            qa_scr[pb * LANE:(pb + 1) * LANE, :] = q_t
            for e in range(2):
                g = (2 * pb + e) // (C_HEADS // C_KV)
                s_new.append(jnp.sum(q_t[e * C_HD:(e + 1) * C_HD, :] * kn_t[g * C_HD:(g + 1) * C_HD, :],
                                     axis=0, keepdims=True))

        table = jnp.concatenate([sp, sl, d_a, dt] + s_new + inv_den + [jnp.zeros((LANE - N_REP, nb), F32)], axis=0)
        table_t = table.T
        for r in range(N_REP):
            rep_scr[r] = jnp.broadcast_to(table_t[:, r:r + 1], (nb, LANE))

        ubx = ut_scr[OFF_BXBC:OFF_BXBC + B_CONV_DIM, :].T
        acc = cb_ref[...] + ubx * cw_ref[CONV_W - 1:CONV_W, :]
        for t in range(CONV_W - 1):
            acc = acc + cv_ref[t] * cw_ref[t:t + 1, :]
        xbc = _silu(acc)
        xbc_scr[...] = xbc
        for t in range(CONV_W - 2):
            cvo_ref[t] = cv_ref[t + 1]
        cvo_ref[CONV_W - 2] = ubx
        xt_scr[...] = xbc[:, :B_WIDTH].T
        for g in range(B_GROUPS):
            ct_scr[g] = xbc[:, B_WIDTH + (B_GROUPS + g) * B_STATE:B_WIDTH + (B_GROUPS + g + 1) * B_STATE].T.astype(ct_scr.dtype)
        vrow_scr[...] = ut_scr[OFF_AV:OFF_AV + A_WIDTH, :].T

    base = pl.multiple_of(j * SAMPLE_BLOCK, SAMPLE_BLOCK)
    rows = pl.ds(base, SAMPLE_BLOCK)
    lane_s = lax.broadcasted_iota(jnp.int32, (C_HD, nb), 1)
    lane_k = lax.broadcasted_iota(jnp.int32, (C_HD, WINDOW), 1)
    lane_r = lane_k[:1, :]
    v_rows = vrow_scr[rows, :]
    xbc_rows = xbc_scr[rows, :]
    reps = [rep_scr[r, rows, :] for r in range(N_REP)]
    hpk = C_HEADS // C_KV
    sels = [lane_s == base + i for i in range(SAMPLE_BLOCK)]

    def col(tile, i):
        return jnp.sum(jnp.where(sels[i], tile, 0.0), axis=1, keepdims=True)

    def rep(r, i):
        return reps[r][i:i + 1, :]

    acc = {}

    ones_m = jnp.ones((nb, LANE), MXU_DTYPE)

    def cols_mxu(tile):
        stacked = jnp.concatenate([jnp.where(sels[i], tile, 0.0) for i in range(SAMPLE_BLOCK)], axis=0)
        return _dot(stacked.astype(MXU_DTYPE), ones_m)

    def mlstm_item(h, i, kc, qc):
        yield
        v_row = v_rows[i:i + 1, h * A_V:(h + 1) * A_V]
        c_new = rep(REP_SP + h, i) * c_ref[i, h] + (rep(REP_SL + h, i) * v_row) * kc
        co_ref[i, h] = c_new
        acc[("h", h)].append(jnp.sum(c_new * qc, axis=0, keepdims=True) * rep(REP_IDEN + h, i))
        if i == SAMPLE_BLOCK - 1:
            h_scr[rows, h * A_V:(h + 1) * A_V] = jnp.concatenate(acc.pop(("h", h)), axis=0)

    def ssd_item(hd, i, xc):
        g = hd // (B_HEADS // B_GROUPS)
        yield
        b_row = xbc_rows[i:i + 1, B_WIDTH + g * B_STATE:B_WIDTH + (g + 1) * B_STATE]
        h_new = rep(REP_DA + hd, i) * s_ref[i, hd] + (rep(REP_DT + hd, i) * b_row) * xc
        so_ref[i, hd] = h_new
        acc[("y", hd)].append(h_new.astype(MXU_DTYPE))
        if i < SAMPLE_BLOCK - 1:
            return
        y_all = _dot(jnp.concatenate(acc.pop(("y", hd)), axis=0), ct_scr[g])
        yield
        y_blk = jnp.zeros((B_P, nb), F32)
        for ii in range(SAMPLE_BLOCK):
            y_blk = jnp.where(sels[ii], y_all[ii * B_P:(ii + 1) * B_P, :], y_blk)
        yt_scr[hd * B_P:(hd + 1) * B_P, :] = yt_scr[hd * B_P:(hd + 1) * B_P, :] + y_blk

    def attn_item(g, i, kn_all, vn_all, qcs, sink):
        kc_new, vc_new = col(kn_all, i), col(vn_all, i)
        yield
        k_t = k_ref[i, g]
        v_t = v_ref[i, g]
        ko_ref[i, g] = jnp.where(lane_k == WINDOW - 1, kc_new, pltpu.roll(k_t, WINDOW - 1, axis=1))
        vo_ref[i, g] = jnp.where(lane_k == WINDOW - 1, vc_new, pltpu.roll(v_t, WINDOW - 1, axis=1))
        s_rows = [jnp.where(lane_r == 0, rep(REP_SNEW + g * hpk + hh, i), jnp.sum(k_t * qcs[hh], axis=0, keepdims=True))
                  for hh in range(hpk)]
        s = jnp.concatenate(s_rows, axis=0) * (C_HD ** -0.5)
        m = jnp.maximum(jnp.max(s, axis=1, keepdims=True), sink)
        yield
        m_b = jnp.broadcast_to(m, s.shape)
        yield
        ex = jnp.exp(s - m_b)
        den = jnp.sum(ex, axis=1, keepdims=True) + jnp.exp(sink - m)
        yield
        inv_b = jnp.broadcast_to(1.0 / den, s.shape)
        yield
        p = ex * inv_b
        v_eff = jnp.where(lane_k == 0, vc_new, v_t)
        o_cols = [jnp.sum(v_eff * p[hh:hh + 1, :], axis=1, keepdims=True) for hh in range(hpk)]
        yield
        for hh in range(hpk):
            hd = g * hpk + hh
            acc[("o", hd)] = jnp.where(sels[i], o_cols[hh], acc[("o", hd)])
            if i == SAMPLE_BLOCK - 1:
                r0 = B_WIDTH + hd * C_HD
                yt_scr[r0:r0 + C_HD, :] = yt_scr[r0:r0 + C_HD, :] + acc.pop(("o", hd))

    def items():
        for h in range(A_HEADS):
            k_all = ut_scr[OFF_AK + h * hd_a:OFF_AK + (h + 1) * hd_a, :] * (A_QK ** -0.5)
            q_all = ut_scr[OFF_AQ + h * hd_a:OFF_AQ + (h + 1) * hd_a, :]
            acc[("h", h)] = []
            k_cols, q_cols = cols_mxu(k_all), cols_mxu(q_all)
            for i in range(SAMPLE_BLOCK):
                yield mlstm_item(h, i, k_cols[i * hd_a:(i + 1) * hd_a, :], q_cols[i * hd_a:(i + 1) * hd_a, :])
        for hd in range(B_HEADS):
            x_cols = cols_mxu(xt_scr[hd * B_P:(hd + 1) * B_P, :])
            acc[("y", hd)] = []
            for i in range(SAMPLE_BLOCK):
                yield ssd_item(hd, i, x_cols[i * B_P:(i + 1) * B_P, :])
        for g in range(C_KV):
            kn_all = knew_scr[g * C_HD:(g + 1) * C_HD, :]
            vn_all = ut_scr[OFF_CV + g * C_HD:OFF_CV + (g + 1) * C_HD, :]
            q_cols = [cols_mxu(qa_scr[(g * hpk + hh) * C_HD:(g * hpk + hh + 1) * C_HD, :]) for hh in range(hpk)]
            sink = sink_ref[g * hpk:(g + 1) * hpk, 0:1]
            for hh in range(hpk):
                acc[("o", g * hpk + hh)] = jnp.zeros((C_HD, nb), F32)
            for i in range(SAMPLE_BLOCK):
                yield attn_item(g, i, kn_all, vn_all, [qc[i * C_HD:(i + 1) * C_HD, :] for qc in q_cols], sink)

    _trace_interleaved(items(), ITEMS_IN_FLIGHT)

    @pl.when(j == pl.num_programs(1) - 1)
    def _layer_end():
        for h in range(A_HEADS):
            h_t = h_scr[:, h * A_V:(h + 1) * A_V].T
            hn = h_t * lax.rsqrt(jnp.mean(h_t * h_t, axis=0, keepdims=True) + NORM_EPS) * anw_ref[h * A_V:(h + 1) * A_V, :]
            ao = ut_scr[OFF_AO + h * A_V:OFF_AO + (h + 1) * A_V, :]
            az = ut_scr[OFF_AZ + h * A_V:OFF_AZ + (h + 1) * A_V, :]
            yall_scr[h * A_V:(h + 1) * A_V, :] = (hn * jax.nn.sigmoid(ao) * _silu(az)).astype(yall_scr.dtype)
        gw = B_WIDTH // B_GROUPS
        for g in range(B_GROUPS):
            r = slice(g * gw, (g + 1) * gw)
            y_g = yt_scr[r, :] + dsk_ref[r, :] * xt_scr[r, :]
            gated = y_g * _silu(ut_scr[OFF_BZ + g * gw:OFF_BZ + (g + 1) * gw, :])
            inv = lax.rsqrt(jnp.mean(gated * gated, axis=0, keepdims=True) + NORM_EPS)
            yall_scr[A_WIDTH + g * gw:A_WIDTH + (g + 1) * gw, :] = (gated * inv * bnw_ref[r, :]).astype(yall_scr.dtype)
        yall_scr[A_WIDTH + B_WIDTH:, :] = \
            (yt_scr[B_WIDTH:, :] * _silu(ut_scr[OFF_CZ:OFF_CZ + C_WIDTH, :])).astype(yall_scr.dtype)
        hs_new = hs_scr[...] + _dot_tn(yall_scr[...], wout_ref[...])
        hs_scr[...] = hs_new

        @pl.when(l == pl.num_programs(0) - 1)
        def _():
            y_ref[...] = hs_new


def _decode(x, lw, cos_s, sin_s, c_v, n_v, m_v, s_v, conv_v, k_v, v_v):
    depth, nb = c_v.shape[0], x.shape[0]
    assert nb == LANE, "samples sit on the 128 lanes next to lane-replicated parameters"
    grid = (depth, nb // SAMPLE_BLOCK)

    def per_layer(shape):
        nd = len(shape)
        return pl.BlockSpec((None,) + shape, lambda l, j, _nd=nd: (l,) + (0,) * _nd)

    def per_block(shape):
        nd = len(shape)
        return pl.BlockSpec((None, SAMPLE_BLOCK) + shape, lambda l, j, _nd=nd: (l, j) + (0,) * _nd)

    def weight(shape):
        nd = len(shape)
        return pl.BlockSpec((None,) + shape, lambda l, j, _nd=nd: (l,) + (0,) * _nd, pipeline_mode=pl.Buffered(1))

    fixed = lambda shape: pl.BlockSpec(shape, lambda l, j, _nd=len(shape): (0,) * _nd)
    state_specs = [
        per_block((A_HEADS, A_QK, A_V)), per_layer((A_HEADS, A_QK, nb)), per_layer((A_HEADS, nb)),
        per_block((B_HEADS, B_P, B_STATE)), per_layer((CONV_W - 1, nb, B_CONV_DIM)),
        per_block((C_KV, C_HD, WINDOW)), per_block((C_KV, C_HD, WINDOW)),
    ]
    in_specs = [
        fixed((nb, D_MODEL)),
        per_layer((1, D_MODEL)), weight((D_INT, D_MODEL)), weight((D_MIX, D_MODEL)),
        per_layer((N_GATES, LANE)), per_layer((N_GATES, LANE)), per_layer((A_WIDTH, LANE)),
        per_layer((CONV_W, B_CONV_DIM)), per_layer((1, B_CONV_DIM)),
        per_layer((B_WIDTH, LANE)), per_layer((B_WIDTH, LANE)), per_layer((LANE, LANE)), per_layer((LANE, LANE)),
        per_layer((C_HEADS, LANE)), fixed((C_HD // 2, LANE)), fixed((C_HD // 2, LANE)),
    ] + state_specs
    states = (c_v, n_v, m_v, s_v, conv_v, k_v, v_v)
    out_shape = (jax.ShapeDtypeStruct((nb, D_MODEL), F32),) + tuple(jax.ShapeDtypeStruct(a.shape, F32) for a in states)
    scratch = [
        pltpu.VMEM((nb, D_MODEL), F32),
        pltpu.VMEM((D_INT, nb), F32),
        pltpu.VMEM((nb, A_WIDTH), F32),
        pltpu.VMEM((nb, B_CONV_DIM), F32),
        pltpu.VMEM((B_WIDTH, nb), F32),
        pltpu.VMEM((B_GROUPS, B_STATE, nb), MXU_DTYPE),
        pltpu.VMEM((C_WIDTH, nb), F32),
        pltpu.VMEM((C_KV * C_HD, nb), F32),
        pltpu.VMEM((N_REP, nb, LANE), F32),
        pltpu.VMEM((nb, A_WIDTH), F32),
        pltpu.VMEM((B_WIDTH + C_WIDTH, nb), F32),
        pltpu.VMEM((D_MIX, nb), MXU_DTYPE),
    ]
    return pl.pallas_call(
        _decode_kernel,
        grid=grid, in_specs=in_specs, out_specs=(fixed((nb, D_MODEL)),) + tuple(state_specs),
        out_shape=out_shape, scratch_shapes=scratch,
        compiler_params=pltpu.CompilerParams(dimension_semantics=("arbitrary", "arbitrary"),
                                             vmem_limit_bytes=VMEM_LIMIT_BYTES),
        name="decode",
    )(x, lw["norm_w"], lw["w_t"], lw["w_out"], lw["gb_rep"], lw["al_rep"], lw["a_nw_rep"], lw["conv_w"],
      lw["conv_b"], lw["d_rep"], lw["b_nw_rep"], lw["qn_rep"], lw["kn_rep"], lw["sink_rep"], cos_s, sin_s, *states)


def _prep_weights(norm_w, w_in, a_igate_b, a_fgate_b, a_norm_w, b_conv_w, b_conv_b, b_dt_bias, b_A_log, b_D,
                  b_norm_w, c_qnorm_w, c_knorm_w, c_sinks, w_out):
    depth = w_in.shape[0]
    w_t = jnp.swapaxes(w_in, 1, 2).astype(MXU_DTYPE)
    gbias = jnp.concatenate([a_igate_b, a_fgate_b, b_dt_bias], axis=-1)
    alog = jnp.concatenate([jnp.zeros((depth, 2 * A_HEADS), b_A_log.dtype), b_A_log], axis=-1)

    def rep(v):
        return jnp.broadcast_to(v.astype(F32)[..., None], v.shape + (LANE,))

    return {
        "w_t": w_t, "w_out": w_out.astype(MXU_DTYPE),
        "norm_w": norm_w.astype(F32)[:, None, :],
        "gb_rep": rep(gbias), "al_rep": rep(alog),
        "a_nw_rep": rep(a_norm_w), "conv_w_rep": rep(b_conv_w), "conv_b_rep": rep(b_conv_b),
        "conv_w": b_conv_w.astype(F32), "conv_b": b_conv_b.astype(F32)[:, None, :],
        "d_rep": rep(jnp.repeat(b_D, B_P, axis=-1)), "b_nw_rep": rep(b_norm_w),
        "qn_rep": rep(jnp.tile(c_qnorm_w, (1, LANE // C_HD))), "kn_rep": rep(jnp.tile(c_knorm_w, (1, LANE // C_HD))),
        "sinks": c_sinks.astype(F32), "sink_rep": rep(c_sinks),
    }


def _rope_tables_t(pos, width=None):
    half = C_HD // 2
    inv = ROPE_THETA ** (-jnp.arange(half, dtype=F32) / half)
    ang = inv[:, None] * pos.astype(F32)[None, :]
    cos, sin = jnp.cos(ang), jnp.sin(ang)
    if width is not None:
        cos, sin = jnp.broadcast_to(cos, (half, width)), jnp.broadcast_to(sin, (half, width))
    return cos, sin


def _kernel_impl(x_prompt, x_sample, state_mlstm_C, state_mlstm_n, state_mlstm_m, state_ssm, state_conv,
                 cache_k, cache_v, norm_w, w_in, a_igate_b, a_fgate_b, a_norm_w, b_conv_w, b_conv_b,
                 b_dt_bias, b_A_log, b_D, b_norm_w, c_qnorm_w, c_knorm_w, c_sinks, w_out, *, tb):
    depth = w_in.shape[0]
    t_len = x_prompt.shape[1]
    lw = _prep_weights(norm_w, w_in, a_igate_b, a_fgate_b, a_norm_w, b_conv_w, b_conv_b, b_dt_bias, b_A_log,
                       b_D, b_norm_w, c_qnorm_w, c_knorm_w, c_sinks, w_out)
    cos_p, sin_p = _rope_tables_t(jnp.arange(t_len, dtype=jnp.int32))
    hp = x_prompt
    st_prompt = []
    for l in range(depth):
        res = _prompt_layer(hp, lw, l, cos_p, sin_p, tb)
        hp = res[0]
        st_prompt.append(res[1:])
    p_states = _unpad_prompt_states(*[jnp.stack(t) for t in zip(*st_prompt)])

    assert x_sample.shape[1] == 1
    cos_s, sin_s = _rope_tables_t(PAST_LEN + jnp.arange(1, dtype=jnp.int32), LANE)
    outs = _decode(
        x_sample[:, 0, :], lw, cos_s, sin_s,
        jnp.transpose(state_mlstm_C, (0, 1, 2, 4, 3)), jnp.transpose(state_mlstm_n, (0, 2, 3, 1)),
        jnp.transpose(state_mlstm_m, (0, 2, 1)), state_ssm, jnp.transpose(state_conv, (0, 2, 1, 3)),
        jnp.transpose(cache_k, (0, 1, 3, 4, 2)), jnp.transpose(cache_v, (0, 1, 3, 4, 2)))
    hs, c_o, n_o, m_o, s_o, conv_o, k_o, v_o = outs
    s_states = (jnp.transpose(c_o, (0, 1, 2, 4, 3)), jnp.transpose(n_o, (0, 3, 1, 2)), jnp.transpose(m_o, (0, 2, 1)),
                s_o, jnp.transpose(conv_o, (0, 2, 1, 3)),
                jnp.transpose(k_o, (0, 1, 4, 2, 3)), jnp.transpose(v_o, (0, 1, 4, 2, 3)))
    return (hp, hs[:, None, :], *p_states, *s_states)


def kernel(x_prompt, x_sample, state_mlstm_C, state_mlstm_n, state_mlstm_m, state_ssm, state_conv, cache_k, cache_v, norm_w, w_in, a_igate_b, a_fgate_b, a_norm_w, b_conv_w, b_conv_b, b_dt_bias, b_A_log, b_D, b_norm_w, c_qnorm_w, c_knorm_w, c_sinks, w_out):
    return _kernel_impl(x_prompt, x_sample, state_mlstm_C, state_mlstm_n, state_mlstm_m, state_ssm, state_conv,
                        cache_k, cache_v, norm_w, w_in, a_igate_b, a_fgate_b, a_norm_w, b_conv_w, b_conv_b,
                        b_dt_bias, b_A_log, b_D, b_norm_w, c_qnorm_w, c_knorm_w, c_sinks, w_out, tb=PROMPT_BLOCK)
```

```python
import functools

import jax
import jax.numpy as jnp
import numpy as np
from jax import lax
from jax.experimental import pallas as pl
from jax.experimental.pallas import tpu as pltpu

F32 = jnp.float32
MXU_DTYPE = jnp.bfloat16

D_MODEL = 1024
A_HEADS, A_QK, A_V = 4, 64, 128
A_WIDTH = A_HEADS * A_V
B_HEADS, B_P, B_GROUPS, B_STATE = 8, 64, 2, 128
B_WIDTH = B_HEADS * B_P
CONV_W = 4
B_CONV_DIM = B_WIDTH + 2 * B_GROUPS * B_STATE
C_HEADS, C_KV, C_HD = 8, 2, 64
C_WIDTH = C_HEADS * C_HD
WINDOW = 128
ROPE_THETA = 10000.0
D_MIX = A_WIDTH + B_WIDTH + C_WIDTH
NORM_EPS = 1e-6
PAST_LEN = 8192

LANE = 128
SUBLANE = 8
HALF = LANE // 2

CHUNK = 128
PROMPT_BLOCK = 256
SAMPLE_BLOCK = SUBLANE
N_GATES = 2 * A_HEADS + B_HEADS

_SRC_SIZES = (A_HEADS * A_QK, A_HEADS * A_QK, A_WIDTH, A_WIDTH, A_WIDTH, A_HEADS, A_HEADS,
              B_WIDTH, B_CONV_DIM, B_HEADS, C_WIDTH, C_KV * C_HD, C_KV * C_HD, C_WIDTH)
(OFF_AQ, OFF_AK, OFF_AV, OFF_AO, OFF_AZ, OFF_GA, _, OFF_BZ, OFF_BXBC, OFF_GB, OFF_CQ, OFF_CK, OFF_CV, OFF_CZ,
 D_INT) = np.concatenate([[0], np.cumsum(_SRC_SIZES)]).tolist()
assert all(o % SUBLANE == 0 for o in (OFF_GA, OFF_BZ, OFF_BXBC, OFF_GB, OFF_CQ, OFF_CK, OFF_CV, OFF_CZ))
INPROJ_SITES = (2, 6)
INPROJ_PIECES = len(INPROJ_SITES)
PROMPT_ITEMS_IN_FLIGHT = 2
_PIECES_PER_BLOCK = INPROJ_PIECES * (PROMPT_BLOCK // CHUNK)
PIECE_ROWS = -(-D_INT // (_PIECES_PER_BLOCK * 2 * SUBLANE)) * 2 * SUBLANE
LAST_PIECE_START = D_INT - PIECE_ROWS
assert LAST_PIECE_START % (2 * SUBLANE) == 0
OUT_PAIR = 2

PK_GB = 0
PK_AL = PK_GB + N_GATES
PK_ANW = PK_AL + N_GATES
PK_DSK = PK_ANW + A_WIDTH
PK_BNW = PK_DSK + B_WIDTH
PK_QNW = PK_BNW + B_WIDTH
PK_KNW = PK_QNW + LANE
PK_SINK = PK_KNW + LANE
PK_CB = PK_SINK + C_HEADS
PK_CW = PK_CB + B_CONV_DIM
PK_ROWS = PK_CW + CONV_W * B_CONV_DIM


def _unpack_params(pk_ref):
    view = lambda start, n: pk_ref.at[pl.ds(start, n)]
    return (view(PK_GB, N_GATES), view(PK_AL, N_GATES), view(PK_ANW, A_WIDTH), view(PK_DSK, B_WIDTH),
            view(PK_BNW, B_WIDTH), view(PK_QNW, LANE), view(PK_KNW, LANE), view(PK_SINK, C_HEADS),
            view(PK_CB, B_CONV_DIM), view(PK_CW, CONV_W * B_CONV_DIM))


C_EXT = A_V + 2 * SUBLANE
ST_SIZES = (A_HEADS * C_EXT, SUBLANE, B_HEADS * B_P, B_CONV_DIM, C_KV * C_HD, C_KV * C_HD)
ST_OFFSETS = tuple(int(o) for o in np.cumsum((0,) + ST_SIZES[:-1]))
ST_ROWS = sum(ST_SIZES)
VMEM_LIMIT_BYTES =56 * 1024 * 1024

REP_SP, REP_SL = 0, A_HEADS
REP_DA, REP_DT = 2 * A_HEADS, 2 * A_HEADS + B_HEADS
REP_SNEW = 2 * A_HEADS + 2 * B_HEADS
REP_IDEN = REP_SNEW + C_HEADS
N_REP = REP_IDEN + A_HEADS


def _dot(a, b):
    return jnp.dot(a, b, preferred_element_type=F32)


def _dot_nt(a, b):
    return lax.dot_general(a, b, (((1,), (1,)), ((), ())), preferred_element_type=F32)


def _dot_tn(a, b):
    return lax.dot_general(a, b, (((0,), (0,)), ((), ())), preferred_element_type=F32)


def _split3(z):
    hi = z.astype(MXU_DTYPE)
    r1 = z - hi.astype(F32)
    mid = r1.astype(MXU_DTYPE)
    lo = (r1 - mid.astype(F32)).astype(MXU_DTYPE)
    return hi, mid, lo


def _softplus_terms(x):
    t = jnp.log1p(jnp.exp(-jnp.abs(x)))
    return jnp.maximum(x, 0.0) + t, jnp.minimum(x, 0.0) - t


def _silu(x):
    return x * jax.nn.sigmoid(x)


def _rmsnorm_rows(x, w):
    return x * lax.rsqrt(jnp.mean(x * x, axis=-1, keepdims=True) + NORM_EPS) * w


def _norm_rope_t(x_t, w_rep, cos, sin):
    half = C_HD // 2
    out = []
    for hh in range(LANE // C_HD):
        xh = x_t[hh * C_HD:(hh + 1) * C_HD, :]
        ms = jnp.mean(xh * xh, axis=0, keepdims=True)
        xn = xh * lax.rsqrt(ms + NORM_EPS) * w_rep[hh * C_HD:(hh + 1) * C_HD, :]
        x1, x2 = xn[:half, :], xn[half:, :]
        out += [x1 * cos - x2 * sin, x2 * cos + x1 * sin]
    return jnp.concatenate(out, axis=0)


def _prompt_layer_kernel(
        sinks_ref,
        x_ref, xnext_ref, nw_ref, wt_ref, wout_ref, pk_ref, cs_ref,
        o_ref, st_ref,
        u0_scr, u1_scr, xn_scr, y_scr, kprev_scr, roll_scr, *, tb):
    L = CHUNK
    nch = tb // L
    t_id = pl.program_id(1)
    slot = t_id % 2
    gb_ref, al_ref, anw_ref, dsk_ref, bnw_ref, qnw_ref, knw_ref, _, cb_ref, cw_ref = _unpack_params(pk_ref)
    conv_tap = lambda j: cw_ref[j * B_CONV_DIM:(j + 1) * B_CONV_DIM, :]
    cst_ref, mst_ref, hst_ref, convt_ref, kt_ref, vt_ref = (
        st_ref.at[pl.ds(start, n)] for start, n in zip(ST_OFFSETS, ST_SIZES))

    @pl.when(t_id == 0)
    def _():
        st_ref[...] = jnp.zeros_like(st_ref)
        kprev_scr[...] = jnp.zeros_like(kprev_scr)
        roll_scr[...] = jnp.zeros_like(roll_scr)
        u0_scr[...] = _dot_nt(wt_ref[...], _rmsnorm_rows(x_ref[:tb, :], nw_ref[...]).astype(MXU_DTYPE))

    xn_scr[...] = _rmsnorm_rows(xnext_ref[...], nw_ref[...]).astype(MXU_DTYPE)
    u_cur, u_next = u0_scr, u1_scr

    row = lax.broadcasted_iota(jnp.int32, (L, L), 0)
    lane = lax.broadcasted_iota(jnp.int32, (L, L), 1)
    lo = lane < HALF
    top = row < B_P
    src_le_t = row <= lane
    tri = jnp.where(lane <= row, 1.0, 0.0).astype(MXU_DTYPE)
    grow = lax.broadcasted_iota(jnp.int32, (N_GATES, L), 0)
    krow = lax.broadcasted_iota(jnp.int32, (2 * L, L), 0)
    klane = lax.broadcasted_iota(jnp.int32, (2 * L, L), 1)
    neg_inf = -jnp.inf
    a_neg = -jnp.exp(al_ref[...])

    def chunk(c, carry):
        cols = pl.ds(pl.multiple_of(c * L, L), L)
        ycols = pl.ds(pl.multiple_of((t_id % OUT_PAIR) * tb + c * L, L), L)

        def next_block_piece(site):
            if site not in INPROJ_SITES:
                return
            k = INPROJ_SITES.index(site)
            start = jnp.minimum((c * INPROJ_PIECES + k) * PIECE_ROWS, LAST_PIECE_START)
            wrows = pl.ds(pl.multiple_of(start, 2 * SUBLANE), PIECE_ROWS)
            u_next[wrows, :] = _dot_nt(wt_ref[wrows, :], xn_scr[...])

        pre_r = jnp.concatenate([u_cur[OFF_GA:OFF_GA + 2 * A_HEADS, cols], u_cur[OFF_GB:OFF_GB + B_HEADS, cols]],
                                axis=0) + gb_ref[...]
        sp_r, ls_r = _softplus_terms(pre_r)
        z_r = jnp.where((grow >= A_HEADS) & (grow < 2 * A_HEADS), ls_r,
                        jnp.where(grow >= 2 * A_HEADS, sp_r * a_neg, 0.0))
        zp = _split3(z_r)
        cum_r = _dot_nt(zp[0], tri) + _dot_nt(zp[1], tri) + _dot_nt(zp[2], tri)
        colf = jnp.concatenate([pre_r, cum_r, jnp.zeros((L - 2 * N_GATES, L), F32)], axis=0).T

        def piece_item(site):
            next_block_piece(site)
            return
            yield

        def mlstm_head(h, q_t, k_p):
            e = h % 2
            k_m = jnp.where(lo if e == 0 else jnp.logical_not(lo), k_p, 0.0).astype(MXU_DTYPE)
            v_t = u_cur[OFF_AV + h * LANE:OFF_AV + (h + 1) * LANE, cols]
            b_row = cum_r[A_HEADS + h:A_HEADS + h + 1, :]
            i_row = pre_r[h:h + 1, :]
            c_col = colf[:, N_GATES + A_HEADS + h:N_GATES + A_HEADS + h + 1] - colf[:, h:h + 1]
            b_end = b_row[:, L - 1:L]
            m_prev = mst_ref[h:h + 1, 0:1]
            c_prev = cst_ref[h * C_EXT:(h + 1) * C_EXT, :]

            g_row = b_end - b_row + i_row
            m_loc = jnp.max(g_row, axis=1, keepdims=True)
            d_t = jnp.where(src_le_t, b_row - c_col, neg_inf)
            inter = b_row + m_prev
            m_t = jnp.maximum(inter, jnp.max(d_t, axis=0, keepdims=True))
            res = _dot(jnp.concatenate([k_m, c_prev.astype(MXU_DTYPE)], axis=0), q_t)
            w_row = jnp.exp(g_row - m_loc)
            v_ext = jnp.concatenate([v_t, jnp.ones((1, L), F32), jnp.zeros((C_EXT - A_V - 1, L), F32)], axis=0)
            c_loc = _dot((v_ext * w_row).astype(MXU_DTYPE), k_m)
            e_t = jnp.exp(d_t - m_t)
            si = jnp.exp(inter - m_t)
            yield
            s_t = e_t * res[:L, :]
            sv = _dot(v_t.astype(MXU_DTYPE), s_t.astype(MXU_DTYPE))
            yield
            num = sv + si * res[L:L + A_V, :]
            den = jnp.sum(s_t, axis=0, keepdims=True) + si * res[L + A_V:L + A_V + 1, :]
            hh = num * (1.0 / jnp.maximum(jnp.abs(den), jnp.exp(-m_t)))
            hn = hh * lax.rsqrt(jnp.mean(hh * hh, axis=0, keepdims=True) + NORM_EPS) * anw_ref[h * LANE:(h + 1) * LANE, :]
            ao = u_cur[OFF_AO + h * LANE:OFF_AO + (h + 1) * LANE, cols]
            az = u_cur[OFF_AZ + h * LANE:OFF_AZ + (h + 1) * LANE, cols]
            y_scr[h * LANE:(h + 1) * LANE, ycols] = (hn * jax.nn.sigmoid(ao) * _silu(az)).astype(y_scr.dtype)

            m_new = jnp.maximum(b_end + m_prev, m_loc)
            sp = jnp.exp(b_end + m_prev - m_new)
            sl = jnp.exp(m_loc - m_new)
            cst_ref[h * C_EXT:(h + 1) * C_EXT, :] = sp * c_prev + sl * c_loc
            mst_ref[h:h + 1, :] = jnp.broadcast_to(m_new, (1, LANE))

        def conv_silu():
            cur = u_cur[OFF_BXBC:OFF_BXBC + B_CONV_DIM, cols]
            acc = cb_ref[...] + cur * conv_tap(CONV_W - 1)
            for j in range(1, CONV_W):
                rolled = pltpu.roll(cur, j, axis=1)
                acc = acc + jnp.where(lane[:1, :] >= j, rolled, roll_scr[j - 1]) * conv_tap(CONV_W - 1 - j)
                roll_scr[j - 1] = rolled
            convt_ref[...] = cur
            return _silu(acc)

        def ssd_group(g, xbc):
            b_t = xbc[B_WIDTH + g * B_STATE:B_WIDTH + (g + 1) * B_STATE, :]
            c_t = xbc[B_WIDTH + (B_GROUPS + g) * B_STATE:B_WIDTH + (B_GROUPS + g + 1) * B_STATE, :]
            b_m = b_t.T.astype(MXU_DTYPE)
            hpg = B_HEADS // B_GROUPS
            h_prev = hst_ref[g * hpg * B_P:(g + 1) * hpg * B_P, :]
            yield
            res = _dot(jnp.concatenate([b_m, h_prev.astype(MXU_DTYPE)], axis=0), c_t.astype(MXU_DTYPE))
            x_ts, h_locs, rows_of = [], [], []
            for pj in range(hpg // 2):
                pb = g * (hpg // 2) + pj
                x_t = xbc[pb * LANE:(pb + 1) * LANE, :]
                a_ends, dec_rows, ea_rows, a_rows, a_cols, dt_rows = [], [], [], [], [], []
                for e in range(2):
                    gi = 2 * A_HEADS + 2 * pb + e
                    a_row = cum_r[gi:gi + 1, :]
                    dt_row = sp_r[gi:gi + 1, :]
                    a_end = a_row[:, L - 1:L]
                    a_rows.append(a_row)
                    a_cols.append(colf[:, N_GATES + gi:N_GATES + gi + 1])
                    dt_rows.append(dt_row)
                    a_ends.append(jnp.exp(a_end))
                    dec_rows.append(jnp.exp(a_end - a_row) * dt_row)
                    ea_rows.append(jnp.exp(a_row))
                h_locs.append(_dot((x_t * jnp.where(top, dec_rows[0], dec_rows[1])).astype(MXU_DTYPE), b_m))
                x_ts.append(x_t)
                rows_of.append((a_ends, ea_rows, a_rows, a_cols, dt_rows))
            yield
            cb_tt = res[:L, :]
            ys_all = []
            for pj in range(hpg // 2):
                a_ends, ea_rows, a_rows, a_cols, dt_rows = rows_of[pj]
                ys = []
                for e in range(2):
                    w_t = jnp.exp(jnp.where(src_le_t, a_rows[e] - a_cols[e], neg_inf)) * cb_tt
                    xdt = (x_ts[pj][e * B_P:(e + 1) * B_P, :] * dt_rows[e]).astype(MXU_DTYPE)
                    ys.append(_dot(xdt, w_t.astype(MXU_DTYPE)))
                ys_all.append(ys)
            yield
            gated = []
            for pj in range(hpg // 2):
                pb = g * (hpg // 2) + pj
                a_ends, ea_rows, a_rows, a_cols, dt_rows = rows_of[pj]
                inter = res[L + pj * LANE:L + (pj + 1) * LANE, :] * jnp.where(top, ea_rows[0], ea_rows[1])
                y_p = jnp.concatenate(ys_all[pj], axis=0) + inter + dsk_ref[pb * LANE:(pb + 1) * LANE, :] * x_ts[pj]
                hst_ref[pb * LANE:(pb + 1) * LANE, :] = \
                    jnp.where(top, a_ends[0], a_ends[1]) * h_prev[pj * LANE:(pj + 1) * LANE, :] + h_locs[pj]
                bz = u_cur[OFF_BZ + pb * LANE:OFF_BZ + (pb + 1) * LANE, cols]
                gated.append(y_p * _silu(bz))
            ms = sum(jnp.sum(gp * gp, axis=0, keepdims=True) for gp in gated) * (1.0 / (B_WIDTH // B_GROUPS))
            inv = lax.rsqrt(ms + NORM_EPS)
            for pj, gp in enumerate(gated):
                pb = g * (hpg // 2) + pj
                y_scr[A_WIDTH + pb * LANE:A_WIDTH + (pb + 1) * LANE, ycols] = \
                    (gp * inv * bnw_ref[pb * LANE:(pb + 1) * LANE, :]).astype(y_scr.dtype)

        cos = cs_ref[:C_HD // 2, cols]
        sin = cs_ref[C_HD // 2:, cols]
        ppg = C_HEADS // 2 // C_KV
        shared = {}

        def swa_keys():
            k_t = _norm_rope_t(u_cur[OFF_CK:OFF_CK + LANE, cols], knw_ref[...], cos, sin)
            v_t = u_cur[OFF_CV:OFF_CV + LANE, cols]
            k_p = k_t.T
            k_sw = pltpu.roll(k_p, HALF, axis=1)
            k_cur = [[jnp.where(lo, k_p, 0.0).astype(MXU_DTYPE), jnp.where(lo, 0.0, k_sw).astype(MXU_DTYPE)],
                     [jnp.where(lo, k_sw, 0.0).astype(MXU_DTYPE), jnp.where(lo, 0.0, k_p).astype(MXU_DTYPE)]]
            shift = jnp.where(jnp.logical_and(t_id == 0, c == 0), 2 * L, 0)
            valid = ((krow < L) & (krow > klane + shift)) | ((krow >= L) & (krow - L <= klane))
            shared.update(k_t=k_t, v_t=v_t, k_cur=k_cur, valid=valid)

        def swa_pair(pb):
            g = pb // ppg
            k_cur, v_t, valid = shared["k_cur"], shared["v_t"], shared["valid"]
            q_t = _norm_rope_t(u_cur[OFF_CQ + pb * LANE:OFF_CQ + (pb + 1) * LANE, cols], qnw_ref[...], cos, sin)
            keys = jnp.concatenate([kprev_scr[2 * g], k_cur[g][0], kprev_scr[2 * g + 1], k_cur[g][1]], axis=0)
            sc = _dot(keys, q_t.astype(MXU_DTYPE)) * (C_HD ** -0.5)
            v_g = jnp.concatenate([vt_ref[g * C_HD:(g + 1) * C_HD, :], v_t[g * C_HD:(g + 1) * C_HD, :]],
                                  axis=1).astype(MXU_DTYPE)
            yield
            outs = []
            for e in range(2):
                s = jnp.where(valid, sc[e * 2 * L:(e + 1) * 2 * L, :], neg_inf)
                sink = sinks_ref[2 * pb + e]
                m = jnp.maximum(jnp.max(s, axis=0, keepdims=True), sink)
                ex = jnp.exp(s - m)
                p = ex * (1.0 / (jnp.sum(ex, axis=0, keepdims=True) + jnp.exp(sink - m)))
                outs.append(_dot(v_g, p.astype(MXU_DTYPE)))
            yield
            cz = u_cur[OFF_CZ + pb * LANE:OFF_CZ + (pb + 1) * LANE, cols]
            y_scr[A_WIDTH + B_WIDTH + pb * LANE:A_WIDTH + B_WIDTH + (pb + 1) * LANE, ycols] = \
                (jnp.concatenate(outs, axis=0) * _silu(cz)).astype(y_scr.dtype)

        def items():
            for h in range(A_HEADS):
                if h % 2 == 0:
                    pb = h // 2
                    q_t = u_cur[OFF_AQ + pb * LANE:OFF_AQ + (pb + 1) * LANE, cols].astype(MXU_DTYPE)
                    k_p = (u_cur[OFF_AK + pb * LANE:OFF_AK + (pb + 1) * LANE, cols] * (A_QK ** -0.5)).T
                yield mlstm_head(h, q_t, k_p)
                yield piece_item(1 + h)
            xbc = conv_silu()
            for g in range(B_GROUPS):
                yield ssd_group(g, xbc)
                yield piece_item(1 + A_HEADS + g)
            swa_keys()
            for pb in range(C_HEADS // 2):
                yield swa_pair(pb)
                if pb == 1:
                    yield piece_item(1 + A_HEADS + B_GROUPS)

        next_block_piece(0)
        _trace_interleaved(items(), PROMPT_ITEMS_IN_FLIGHT)
        for g in range(C_KV):
            for e in range(2):
                kprev_scr[2 * g + e] = shared["k_cur"][g][e]
        kt_ref[...] = shared["k_t"]
        vt_ref[...] = shared["v_t"]
        return carry

    def run_chunks(cur, nxt):
        nonlocal u_cur, u_next
        u_cur, u_next = cur, nxt
        lax.fori_loop(0, nch, lambda c, carry: chunk(c, carry), 0)

    @pl.when(slot == 0)
    def _():
        run_chunks(u0_scr, u1_scr)

    @pl.when(slot == 1)
    def _():
        run_chunks(u1_scr, u0_scr)

    @pl.when(t_id % OUT_PAIR == OUT_PAIR - 1)
    def _():
        o_ref[...] = x_ref[...] + _dot_tn(y_scr[...], wout_ref[...])


def _prompt_layer(x, lw, l, cs_t, tb):
    n_seq, t_len, _ = x.shape
    assert t_len % (OUT_PAIR * tb) == 0 and tb % CHUNK == 0 and tb == PROMPT_BLOCK
    n_t = t_len // tb
    grid = (n_seq, n_t)

    def per_seq(shape):
        nd = len(shape)
        return pl.BlockSpec((None,) + shape, lambda n, t, _nd=nd: (n,) + (0,) * _nd)

    def per_layer(shape, **kw):
        nd = len(shape)
        return pl.BlockSpec((None,) + shape, lambda n, t, _nd=nd: (l,) + (0,) * _nd, **kw)

    in_specs = [
        pl.BlockSpec(memory_space=pltpu.SMEM),
        pl.BlockSpec((None, OUT_PAIR * tb, D_MODEL), lambda n, t: (n, t // OUT_PAIR, 0)),
        pl.BlockSpec((None, tb, D_MODEL), lambda n, t: (n, jnp.minimum(t + 1, n_t - 1), 0)),
        per_layer((1, D_MODEL)),
        per_layer((D_INT, D_MODEL), pipeline_mode=pl.Buffered(1)),
        per_layer((D_MIX, D_MODEL), pipeline_mode=pl.Buffered(1)),
        per_layer((PK_ROWS, LANE), pipeline_mode=pl.Buffered(1)),
        pl.BlockSpec((C_HD, tb), lambda n, t: (0, t)),
    ]
    out_shape = (
        jax.ShapeDtypeStruct((n_seq, t_len, D_MODEL), F32),
        jax.ShapeDtypeStruct((n_seq, ST_ROWS, LANE), F32),
    )
    out_specs = (
        pl.BlockSpec((None, OUT_PAIR * tb, D_MODEL), lambda n, t: (n, t // OUT_PAIR, 0)),
        per_seq((ST_ROWS, LANE)),
    )
    scratch = [
        pltpu.VMEM((D_INT, tb), F32),
        pltpu.VMEM((D_INT, tb), F32),
        pltpu.VMEM((tb, D_MODEL), MXU_DTYPE),
        pltpu.VMEM((D_MIX, OUT_PAIR * tb), MXU_DTYPE),
        pltpu.VMEM((2 * C_KV, CHUNK, LANE), MXU_DTYPE),
        pltpu.VMEM((CONV_W - 1, B_CONV_DIM, LANE), F32),
    ]
    return pl.pallas_call(
        functools.partial(_prompt_layer_kernel, tb=tb),
        grid=grid, in_specs=in_specs, out_specs=out_specs, out_shape=out_shape, scratch_shapes=scratch,
        compiler_params=pltpu.CompilerParams(dimension_semantics=("arbitrary", "arbitrary"),
                                             vmem_limit_bytes=VMEM_LIMIT_BYTES),
        name="prompt_layer",
    )(lw["sinks"][l], x, x, lw["norm_w"], lw["w_t"], lw["w_out"], lw["packed"], cs_t)


def _unpad_prompt_states(st):
    lead = st.shape[:2]
    cst, mst, hst, convt, kt, vt = (st[:, :, o:o + n, :] for o, n in zip(ST_OFFSETS, ST_SIZES))
    cst = cst.reshape(lead + (A_HEADS, C_EXT, LANE))
    c_even, c_odd = cst[:, :, 0::2, :A_V, :HALF], cst[:, :, 1::2, :A_V, HALF:]
    c_state = jnp.stack([c_even, c_odd], axis=3).reshape(lead + (A_HEADS, A_V, A_QK))
    n_even, n_odd = cst[:, :, 0::2, A_V, :HALF], cst[:, :, 1::2, A_V, HALF:]
    n_state = jnp.stack([n_even, n_odd], axis=3).reshape(lead + (A_HEADS, A_QK))
    conv = jnp.swapaxes(convt[:, :, :, LANE - (CONV_W - 1):], 2, 3)
    k1 = jnp.transpose(kt.reshape(lead + (C_KV, C_HD, WINDOW)), (0, 1, 4, 2, 3))
    v1 = jnp.transpose(vt.reshape(lead + (C_KV, C_HD, WINDOW)), (0, 1, 4, 2, 3))
    return (c_state, n_state, mst[:, :, :A_HEADS, 0], hst.reshape(lead + (B_HEADS, B_P, B_STATE)), conv, k1, v1)


ITEMS_IN_FLIGHT = 16


def _trace_interleaved(item_iter, depth):
    active, exhausted = [], False
    while True:
        while not exhausted and len(active) < depth:
            nxt = next(item_iter, None)
            if nxt is None:
                exhausted = True
            else:
                active.append(nxt)
        if not active:
            return
        still = []
        for gen in active:
            try:
                next(gen)
                still.append(gen)
            except StopIteration:
                pass
        active = still


def _decode_kernel(
        x_ref, nw_ref, wt_ref, wout_ref, pk_ref, cw_ref, cb_ref, cs_ref,
        c_ref, n_ref, m_ref, s_ref, cv_ref, k_ref, v_ref,
        y_ref, co_ref, no_ref, mo_ref, so_ref, cvo_ref, ko_ref, vo_ref,
        hs_scr, ut_scr, vrow_scr, xbc_scr, xt_scr, ct_scr, qa_scr, knew_scr, rep_scr, h_scr, yt_scr, yall_scr):
    l = pl.program_id(0)
    j = pl.program_id(1)
    nb = x_ref.shape[0]
    hd_a = A_QK
    gb_ref, al_ref, anw_ref, dsk_ref, bnw_ref, qnw_ref, knw_ref, sink_ref, _, _ = _unpack_params(pk_ref)

    @pl.when(j == 0)
    def _layer_start():
        @pl.when(l == 0)
        def _():
            hs_scr[...] = x_ref[...]

        xn = _rmsnorm_rows(hs_scr[...], nw_ref[...]).astype(MXU_DTYPE)
        ut_scr[...] = _dot_nt(wt_ref[...], xn)
        yt_scr[...] = jnp.zeros_like(yt_scr)

        pre = jnp.concatenate([ut_scr[OFF_GA:OFF_GA + 2 * A_HEADS, :], ut_scr[OFF_GB:OFF_GB + B_HEADS, :]],
                              axis=0) + gb_ref[...]
        spl, lsg = _softplus_terms(pre)
        ig, lf, dt = pre[:A_HEADS], lsg[A_HEADS:2 * A_HEADS], spl[2 * A_HEADS:]
        d_a = jnp.exp(dt * (-jnp.exp(al_ref[2 * A_HEADS:, :])))
        m0 = m_ref[...]
        m_new = jnp.maximum(lf + m0, ig)
        sp = jnp.exp(lf + m0 - m_new)
        sl = jnp.exp(ig - m_new)
        mo_ref[...] = m_new
        inv_den = []
        for h in range(A_HEADS):
            k_t = ut_scr[OFF_AK + h * hd_a:OFF_AK + (h + 1) * hd_a, :] * (A_QK ** -0.5)
            q_t = ut_scr[OFF_AQ + h * hd_a:OFF_AQ + (h + 1) * hd_a, :]
            n_new = sp[h:h + 1] * n_ref[h] + sl[h:h + 1] * k_t
            no_ref[h] = n_new
            den = jnp.sum(n_new * q_t, axis=0, keepdims=True)
            inv_den.append(1.0 / jnp.maximum(jnp.abs(den), jnp.exp(-m_new[h:h + 1])))

        cos, sin = cs_ref[:C_HD // 2, :], cs_ref[C_HD // 2:, :]
        kn_t = _norm_rope_t(ut_scr[OFF_CK:OFF_CK + LANE, :], knw_ref[...], cos, sin)
        knew_scr[...] = kn_t
        s_new = []
        for pb in range(C_HEADS // 2):
            q_t = _norm_rope_t(ut_scr[OFF_CQ + pb * LANE:OFF_CQ + (pb + 1) * LANE, :], qnw_ref[...], cos, sin)
            qa_scr[pb * LANE:(pb + 1) * LANE, :] = q_t
            for e in range(2):
                g = (2 * pb + e) // (C_HEADS // C_KV)
                s_new.append(jnp.sum(q_t[e * C_HD:(e + 1) * C_HD, :] * kn_t[g * C_HD:(g + 1) * C_HD, :],
                                     axis=0, keepdims=True))

        table = jnp.concatenate([sp, sl, d_a, dt] + s_new + inv_den + [jnp.zeros((LANE - N_REP, nb), F32)], axis=0)
        table_t = table.T
        for r in range(N_REP):
            rep_scr[r] = jnp.broadcast_to(table_t[:, r:r + 1], (nb, LANE))

        ubx = ut_scr[OFF_BXBC:OFF_BXBC + B_CONV_DIM, :].T
        acc = cb_ref[...] + ubx * cw_ref[CONV_W - 1:CONV_W, :]
        for t in range(CONV_W - 1):
            acc = acc + cv_ref[t] * cw_ref[t:t + 1, :]
        xbc = _silu(acc)
        xbc_scr[...] = xbc
        for t in range(CONV_W - 2):
            cvo_ref[t] = cv_ref[t + 1]
        cvo_ref[CONV_W - 2] = ubx
        xt_scr[...] = xbc[:, :B_WIDTH].T
        for g in range(B_GROUPS):
            ct_scr[g] = xbc[:, B_WIDTH + (B_GROUPS + g) * B_STATE:B_WIDTH + (B_GROUPS + g + 1) * B_STATE].T.astype(ct_scr.dtype)
        vrow_scr[...] = ut_scr[OFF_AV:OFF_AV + A_WIDTH, :].T

    base = pl.multiple_of(j * SAMPLE_BLOCK, SAMPLE_BLOCK)
    rows = pl.ds(base, SAMPLE_BLOCK)
    lane_s = lax.broadcasted_iota(jnp.int32, (C_HD, nb), 1)
    lane_k = lax.broadcasted_iota(jnp.int32, (C_HD, WINDOW), 1)
    lane_r = lane_k[:1, :]
    v_rows = vrow_scr[rows, :]
    xbc_rows = xbc_scr[rows, :]
    reps = [rep_scr[r, rows, :] for r in range(N_REP)]
    hpk = C_HEADS // C_KV
    sels = [lane_s == base + i for i in range(SAMPLE_BLOCK)]

    def col(tile, i):
        return jnp.sum(jnp.where(sels[i], tile, 0.0), axis=1, keepdims=True)

    def rep(r, i):
        return reps[r][i:i + 1, :]

    acc = {}

    ones_m = jnp.ones((nb, LANE), MXU_DTYPE)

    def cols_mxu(tile):
        stacked = jnp.concatenate([jnp.where(sels[i], tile, 0.0) for i in range(SAMPLE_BLOCK)], axis=0)
        return _dot(stacked.astype(MXU_DTYPE), ones_m)

    def mlstm_item(h, i, kc, qc):
        yield
        v_row = v_rows[i:i + 1, h * A_V:(h + 1) * A_V]
        c_new = rep(REP_SP + h, i) * c_ref[i, h] + (rep(REP_SL + h, i) * v_row) * kc
        co_ref[i, h] = c_new
        acc[("h", h)].append(jnp.sum(c_new * qc, axis=0, keepdims=True) * rep(REP_IDEN + h, i))
        if i == SAMPLE_BLOCK - 1:
            h_scr[rows, h * A_V:(h + 1) * A_V] = jnp.concatenate(acc.pop(("h", h)), axis=0)

    def ssd_item(hd, i, xc):
        g = hd // (B_HEADS // B_GROUPS)
        yield
        b_row = xbc_rows[i:i + 1, B_WIDTH + g * B_STATE:B_WIDTH + (g + 1) * B_STATE]
        h_new = rep(REP_DA + hd, i) * s_ref[i, hd] + (rep(REP_DT + hd, i) * b_row) * xc
        so_ref[i, hd] = h_new
        acc[("y", hd)].append(h_new.astype(MXU_DTYPE))
        if i < SAMPLE_BLOCK - 1:
            return
        y_all = _dot(jnp.concatenate(acc.pop(("y", hd)), axis=0), ct_scr[g])
        yield
        y_blk = jnp.zeros((B_P, nb), F32)
        for ii in range(SAMPLE_BLOCK):
            y_blk = jnp.where(sels[ii], y_all[ii * B_P:(ii + 1) * B_P, :], y_blk)
        yt_scr[hd * B_P:(hd + 1) * B_P, :] = yt_scr[hd * B_P:(hd + 1) * B_P, :] + y_blk

    def attn_item(g, i, kn_all, vn_all, qcs, sink):
        kc_new, vc_new = col(kn_all, i), col(vn_all, i)
        yield
        k_t = k_ref[i, g]
        v_t = v_ref[i, g]
        ko_ref[i, g] = jnp.where(lane_k == WINDOW - 1, kc_new, pltpu.roll(k_t, WINDOW - 1, axis=1))
        vo_ref[i, g] = jnp.where(lane_k == WINDOW - 1, vc_new, pltpu.roll(v_t, WINDOW - 1, axis=1))
        s_rows = [jnp.where(lane_r == 0, rep(REP_SNEW + g * hpk + hh, i), jnp.sum(k_t * qcs[hh], axis=0, keepdims=True))
                  for hh in range(hpk)]
        s = jnp.concatenate(s_rows, axis=0) * (C_HD ** -0.5)
        m = jnp.maximum(jnp.max(s, axis=1, keepdims=True), sink)
        yield
        m_b = jnp.broadcast_to(m, s.shape)
        yield
        ex = jnp.exp(s - m_b)
        den = jnp.sum(ex, axis=1, keepdims=True) + jnp.exp(sink - m)
        yield
        inv_b = jnp.broadcast_to(1.0 / den, s.shape)
        yield
        p = ex * inv_b
        v_eff = jnp.where(lane_k == 0, vc_new, v_t)
        o_cols = [jnp.sum(v_eff * p[hh:hh + 1, :], axis=1, keepdims=True) for hh in range(hpk)]
        yield
        for hh in range(hpk):
            hd = g * hpk + hh
            acc[("o", hd)] = jnp.where(sels[i], o_cols[hh], acc[("o", hd)])
            if i == SAMPLE_BLOCK - 1:
                r0 = B_WIDTH + hd * C_HD
                yt_scr[r0:r0 + C_HD, :] = yt_scr[r0:r0 + C_HD, :] + acc.pop(("o", hd))

    def items():
        for h in range(A_HEADS):
            k_all = ut_scr[OFF_AK + h * hd_a:OFF_AK + (h + 1) * hd_a, :] * (A_QK ** -0.5)
            q_all = ut_scr[OFF_AQ + h * hd_a:OFF_AQ + (h + 1) * hd_a, :]
            acc[("h", h)] = []
            k_cols, q_cols = cols_mxu(k_all), cols_mxu(q_all)
            for i in range(SAMPLE_BLOCK):
                yield mlstm_item(h, i, k_cols[i * hd_a:(i + 1) * hd_a, :], q_cols[i * hd_a:(i + 1) * hd_a, :])
        for hd in range(B_HEADS):
            x_cols = cols_mxu(xt_scr[hd * B_P:(hd + 1) * B_P, :])
            acc[("y", hd)] = []
            for i in range(SAMPLE_BLOCK):
                yield ssd_item(hd, i, x_cols[i * B_P:(i + 1) * B_P, :])
        for g in range(C_KV):
            kn_all = knew_scr[g * C_HD:(g + 1) * C_HD, :]
            vn_all = ut_scr[OFF_CV + g * C_HD:OFF_CV + (g + 1) * C_HD, :]
            q_cols = [cols_mxu(qa_scr[(g * hpk + hh) * C_HD:(g * hpk + hh + 1) * C_HD, :]) for hh in range(hpk)]
            sink = sink_ref[g * hpk:(g + 1) * hpk, 0:1]
            for hh in range(hpk):
                acc[("o", g * hpk + hh)] = jnp.zeros((C_HD, nb), F32)
            for i in range(SAMPLE_BLOCK):
                yield attn_item(g, i, kn_all, vn_all, [qc[i * C_HD:(i + 1) * C_HD, :] for qc in q_cols], sink)

    _trace_interleaved(items(), ITEMS_IN_FLIGHT)

    @pl.when(j == pl.num_programs(1) - 1)
    def _layer_end():
        for h in range(A_HEADS):
            h_t = h_scr[:, h * A_V:(h + 1) * A_V].T
            hn = h_t * lax.rsqrt(jnp.mean(h_t * h_t, axis=0, keepdims=True) + NORM_EPS) * anw_ref[h * A_V:(h + 1) * A_V, :]
            ao = ut_scr[OFF_AO + h * A_V:OFF_AO + (h + 1) * A_V, :]
            az = ut_scr[OFF_AZ + h * A_V:OFF_AZ + (h + 1) * A_V, :]
            yall_scr[h * A_V:(h + 1) * A_V, :] = (hn * jax.nn.sigmoid(ao) * _silu(az)).astype(yall_scr.dtype)
        gw = B_WIDTH // B_GROUPS
        for g in range(B_GROUPS):
            r = slice(g * gw, (g + 1) * gw)
            y_g = yt_scr[r, :] + dsk_ref[r, :] * xt_scr[r, :]
            gated = y_g * _silu(ut_scr[OFF_BZ + g * gw:OFF_BZ + (g + 1) * gw, :])
            inv = lax.rsqrt(jnp.mean(gated * gated, axis=0, keepdims=True) + NORM_EPS)
            yall_scr[A_WIDTH + g * gw:A_WIDTH + (g + 1) * gw, :] = (gated * inv * bnw_ref[r, :]).astype(yall_scr.dtype)
        yall_scr[A_WIDTH + B_WIDTH:, :] = \
            (yt_scr[B_WIDTH:, :] * _silu(ut_scr[OFF_CZ:OFF_CZ + C_WIDTH, :])).astype(yall_scr.dtype)
        hs_new = hs_scr[...] + _dot_tn(yall_scr[...], wout_ref[...])
        hs_scr[...] = hs_new

        @pl.when(l == pl.num_programs(0) - 1)
        def _():
            y_ref[...] = hs_new


def _decode(x, lw, cs_s, c_v, n_v, m_v, s_v, conv_v, k_v, v_v):
    depth, nb = c_v.shape[0], x.shape[0]
    assert nb == LANE, "samples sit on the 128 lanes next to lane-replicated parameters"
    grid = (depth, nb // SAMPLE_BLOCK)

    def per_layer(shape):
        nd = len(shape)
        return pl.BlockSpec((None,) + shape, lambda l, j, _nd=nd: (l,) + (0,) * _nd)

    def per_block(shape):
        nd = len(shape)
        return pl.BlockSpec((None, SAMPLE_BLOCK) + shape, lambda l, j, _nd=nd: (l, j) + (0,) * _nd)

    def weight(shape):
        nd = len(shape)
        return pl.BlockSpec((None,) + shape, lambda l, j, _nd=nd: (l,) + (0,) * _nd, pipeline_mode=pl.Buffered(1))

    fixed = lambda shape: pl.BlockSpec(shape, lambda l, j, _nd=len(shape): (0,) * _nd)
    state_specs = [
        per_block((A_HEADS, A_QK, A_V)), per_layer((A_HEADS, A_QK, nb)), per_layer((A_HEADS, nb)),
        per_block((B_HEADS, B_P, B_STATE)), per_layer((CONV_W - 1, nb, B_CONV_DIM)),
        per_block((C_KV, C_HD, WINDOW)), per_block((C_KV, C_HD, WINDOW)),
    ]
    in_specs = [
        fixed((nb, D_MODEL)),
        per_layer((1, D_MODEL)), weight((D_INT, D_MODEL)), weight((D_MIX, D_MODEL)),
        weight((PK_ROWS, LANE)), per_layer((CONV_W, B_CONV_DIM)), per_layer((1, B_CONV_DIM)),
        fixed((C_HD, LANE)),
    ] + state_specs
    states = (c_v, n_v, m_v, s_v, conv_v, k_v, v_v)
    out_shape = (jax.ShapeDtypeStruct((nb, D_MODEL), F32),) + tuple(jax.ShapeDtypeStruct(a.shape, F32) for a in states)
    scratch = [
        pltpu.VMEM((nb, D_MODEL), F32),
        pltpu.VMEM((D_INT, nb), F32),
        pltpu.VMEM((nb, A_WIDTH), F32),
        pltpu.VMEM((nb, B_CONV_DIM), F32),
        pltpu.VMEM((B_WIDTH, nb), F32),
        pltpu.VMEM((B_GROUPS, B_STATE, nb), MXU_DTYPE),
        pltpu.VMEM((C_WIDTH, nb), F32),
        pltpu.VMEM((C_KV * C_HD, nb), F32),
        pltpu.VMEM((N_REP, nb, LANE), F32),
        pltpu.VMEM((nb, A_WIDTH), F32),
        pltpu.VMEM((B_WIDTH + C_WIDTH, nb), F32),
        pltpu.VMEM((D_MIX, nb), MXU_DTYPE),
    ]
    return pl.pallas_call(
        _decode_kernel,
        grid=grid, in_specs=in_specs, out_specs=(fixed((nb, D_MODEL)),) + tuple(state_specs),
        out_shape=out_shape, scratch_shapes=scratch,
        compiler_params=pltpu.CompilerParams(dimension_semantics=("arbitrary", "arbitrary"),
                                             vmem_limit_bytes=VMEM_LIMIT_BYTES),
        name="decode",
    )(x, lw["norm_w"], lw["w_t"], lw["w_out"], lw["packed"], lw["conv_w"], lw["conv_b"], cs_s, *states)


def _prep_weights(norm_w, w_in, a_igate_b, a_fgate_b, a_norm_w, b_conv_w, b_conv_b, b_dt_bias, b_A_log, b_D,
                  b_norm_w, c_qnorm_w, c_knorm_w, c_sinks, w_out):
    depth = w_in.shape[0]
    w_t = jnp.swapaxes(w_in, 1, 2).astype(MXU_DTYPE)
    gbias = jnp.concatenate([a_igate_b, a_fgate_b, b_dt_bias], axis=-1)
    alog = jnp.concatenate([jnp.zeros((depth, 2 * A_HEADS), b_A_log.dtype), b_A_log], axis=-1)

    rows = jnp.concatenate([
        gbias, alog, a_norm_w, jnp.repeat(b_D, B_P, axis=-1), b_norm_w,
        jnp.tile(c_qnorm_w, (1, LANE // C_HD)), jnp.tile(c_knorm_w, (1, LANE // C_HD)), c_sinks,
        b_conv_b, b_conv_w.reshape(depth, CONV_W * B_CONV_DIM)], axis=-1).astype(F32)
    assert rows.shape == (depth, PK_ROWS)
    return {
        "w_t": w_t, "w_out": w_out.astype(MXU_DTYPE),
        "norm_w": norm_w.astype(F32)[:, None, :],
        "packed": jnp.broadcast_to(rows[..., None], (depth, PK_ROWS, LANE)),
        "conv_w": b_conv_w.astype(F32), "conv_b": b_conv_b.astype(F32)[:, None, :],
        "sinks": c_sinks.astype(F32),
    }


def _rope_tables_t(pos, width=None):
    half = C_HD // 2
    inv = ROPE_THETA ** (-jnp.arange(half, dtype=F32) / half)
    ang = inv[:, None] * pos.astype(F32)[None, :]
    table = jnp.concatenate([jnp.cos(ang), jnp.sin(ang)], axis=0)
    if width is not None:
        table = jnp.broadcast_to(table, (C_HD, width))
    return table


def _kernel_impl(x_prompt, x_sample, state_mlstm_C, state_mlstm_n, state_mlstm_m, state_ssm, state_conv,
                 cache_k, cache_v, norm_w, w_in, a_igate_b, a_fgate_b, a_norm_w, b_conv_w, b_conv_b,
                 b_dt_bias, b_A_log, b_D, b_norm_w, c_qnorm_w, c_knorm_w, c_sinks, w_out, *, tb):
    depth = w_in.shape[0]
    t_len = x_prompt.shape[1]
    lw = _prep_weights(norm_w, w_in, a_igate_b, a_fgate_b, a_norm_w, b_conv_w, b_conv_b, b_dt_bias, b_A_log,
                       b_D, b_norm_w, c_qnorm_w, c_knorm_w, c_sinks, w_out)
    cs_p = _rope_tables_t(jnp.arange(t_len, dtype=jnp.int32))
    hp = x_prompt
    st_prompt = []
    for l in range(depth):
        res = _prompt_layer(hp, lw, l, cs_p, tb)
        hp = res[0]
        st_prompt.append(res[1])
    p_states = _unpad_prompt_states(jnp.stack(st_prompt))

    assert x_sample.shape[1] == 1
    cs_s = _rope_tables_t(PAST_LEN + jnp.arange(1, dtype=jnp.int32), LANE)
    outs = _decode(
        x_sample[:, 0, :], lw, cs_s,
        jnp.transpose(state_mlstm_C, (0, 1, 2, 4, 3)), jnp.transpose(state_mlstm_n, (0, 2, 3, 1)),
        jnp.transpose(state_mlstm_m, (0, 2, 1)), state_ssm, jnp.transpose(state_conv, (0, 2, 1, 3)),
        jnp.transpose(cache_k, (0, 1, 3, 4, 2)), jnp.transpose(cache_v, (0, 1, 3, 4, 2)))
    hs, c_o, n_o, m_o, s_o, conv_o, k_o, v_o = outs
    s_states = (jnp.transpose(c_o, (0, 1, 2, 4, 3)), jnp.transpose(n_o, (0, 3, 1, 2)), jnp.transpose(m_o, (0, 2, 1)),
                s_o, jnp.transpose(conv_o, (0, 2, 1, 3)),
                jnp.transpose(k_o, (0, 1, 4, 2, 3)), jnp.transpose(v_o, (0, 1, 4, 2, 3)))
    return (hp, hs[:, None, :], *p_states, *s_states)


def kernel(x_prompt, x_sample, state_mlstm_C, state_mlstm_n, state_mlstm_m, state_ssm, state_conv, cache_k, cache_v, norm_w, w_in, a_igate_b, a_fgate_b, a_norm_w, b_conv_w, b_conv_b, b_dt_bias, b_A_log, b_D, b_norm_w, c_qnorm_w, c_knorm_w, c_sinks, w_out):
    return _kernel_impl(x_prompt, x_sample, state_mlstm_C, state_mlstm_n, state_mlstm_m, state_ssm, state_conv,
                        cache_k, cache_v, norm_w, w_in, a_igate_b, a_fgate_b, a_norm_w, b_conv_w, b_conv_b,
                        b_dt_bias, b_A_log, b_D, b_norm_w, c_qnorm_w, c_knorm_w, c_sinks, w_out, tb=PROMPT_BLOCK)
```

```python
import functools

import jax
import jax.numpy as jnp
import numpy as np
from jax import lax
from jax.experimental import pallas as pl
from jax.experimental.pallas import tpu as pltpu

F32 = jnp.float32
MXU_DTYPE = jnp.bfloat16

D_MODEL = 1024
A_HEADS, A_QK, A_V = 4, 64, 128
A_WIDTH = A_HEADS * A_V
B_HEADS, B_P, B_GROUPS, B_STATE = 8, 64, 2, 128
B_WIDTH = B_HEADS * B_P
CONV_W = 4
B_CONV_DIM = B_WIDTH + 2 * B_GROUPS * B_STATE
C_HEADS, C_KV, C_HD = 8, 2, 64
C_WIDTH = C_HEADS * C_HD
WINDOW = 128
ROPE_THETA = 10000.0
D_MIX = A_WIDTH + B_WIDTH + C_WIDTH
NORM_EPS = 1e-6
PAST_LEN = 8192

LANE = 128
SUBLANE = 8
HALF = LANE // 2

CHUNK = 128
PROMPT_BLOCK = 256
SAMPLE_BLOCK = SUBLANE
N_GATES = 2 * A_HEADS + B_HEADS

_SRC_SIZES = (A_HEADS * A_QK, A_HEADS * A_QK, A_WIDTH, A_WIDTH, A_WIDTH, A_HEADS, A_HEADS,
              B_WIDTH, B_CONV_DIM, B_HEADS, C_WIDTH, C_KV * C_HD, C_KV * C_HD, C_WIDTH)
(OFF_AQ, OFF_AK, OFF_AV, OFF_AO, OFF_AZ, OFF_GA, _, OFF_BZ, OFF_BXBC, OFF_GB, OFF_CQ, OFF_CK, OFF_CV, OFF_CZ,
 D_INT) = np.concatenate([[0], np.cumsum(_SRC_SIZES)]).tolist()
assert all(o % SUBLANE == 0 for o in (OFF_GA, OFF_BZ, OFF_BXBC, OFF_GB, OFF_CQ, OFF_CK, OFF_CV, OFF_CZ))
INPROJ_SITES = (2, 6)
INPROJ_PIECES = len(INPROJ_SITES)
PROMPT_ITEMS_IN_FLIGHT = 2
_PIECES_PER_BLOCK = INPROJ_PIECES * (PROMPT_BLOCK // CHUNK)
PIECE_ROWS = -(-D_INT // (_PIECES_PER_BLOCK * 2 * SUBLANE)) * 2 * SUBLANE
LAST_PIECE_START = D_INT - PIECE_ROWS
assert LAST_PIECE_START % (2 * SUBLANE) == 0
OUT_PAIR = 2

PK_GB = 0
PK_AL = PK_GB + N_GATES
PK_ANW = PK_AL + N_GATES
PK_DSK = PK_ANW + A_WIDTH
PK_BNW = PK_DSK + B_WIDTH
PK_QNW = PK_BNW + B_WIDTH
PK_KNW = PK_QNW + LANE
PK_SINK = PK_KNW + LANE
PK_CB = PK_SINK + C_HEADS
PK_CW = PK_CB + B_CONV_DIM
PK_ROWS = PK_CW + CONV_W * B_CONV_DIM


def _unpack_params(pk_ref):
    view = lambda start, n: pk_ref.at[pl.ds(start, n)]
    return (view(PK_GB, N_GATES), view(PK_AL, N_GATES), view(PK_ANW, A_WIDTH), view(PK_DSK, B_WIDTH),
            view(PK_BNW, B_WIDTH), view(PK_QNW, LANE), view(PK_KNW, LANE), view(PK_SINK, C_HEADS),
            view(PK_CB, B_CONV_DIM), view(PK_CW, CONV_W * B_CONV_DIM))


C_EXT = A_V + 2 * SUBLANE
ST_SIZES = (A_HEADS * C_EXT, SUBLANE, B_HEADS * B_P, B_CONV_DIM, C_KV * C_HD, C_KV * C_HD)
ST_OFFSETS = tuple(int(o) for o in np.cumsum((0,) + ST_SIZES[:-1]))
ST_ROWS = sum(ST_SIZES)
VMEM_LIMIT_BYTES =56 * 1024 * 1024

REP_SP, REP_SL = 0, A_HEADS
REP_DA, REP_DT = 2 * A_HEADS, 2 * A_HEADS + B_HEADS
REP_SNEW = 2 * A_HEADS + 2 * B_HEADS
REP_IDEN = REP_SNEW + C_HEADS
N_REP = REP_IDEN + A_HEADS


def _dot(a, b):
    return jnp.dot(a, b, preferred_element_type=F32)


def _dot_nt(a, b):
    return lax.dot_general(a, b, (((1,), (1,)), ((), ())), preferred_element_type=F32)


def _dot_tn(a, b):
    return lax.dot_general(a, b, (((0,), (0,)), ((), ())), preferred_element_type=F32)


def _split3(z):
    hi = z.astype(MXU_DTYPE)
    r1 = z - hi.astype(F32)
    mid = r1.astype(MXU_DTYPE)
    lo = (r1 - mid.astype(F32)).astype(MXU_DTYPE)
    return hi, mid, lo


def _softplus_terms(x):
    t = jnp.log1p(jnp.exp(-jnp.abs(x)))
    return jnp.maximum(x, 0.0) + t, jnp.minimum(x, 0.0) - t


def _silu(x):
    return x * jax.nn.sigmoid(x)


def _rmsnorm_rows(x, w):
    return x * lax.rsqrt(jnp.mean(x * x, axis=-1, keepdims=True) + NORM_EPS) * w


def _norm_rope_t(x_t, w_rep, cos, sin):
    half = C_HD // 2
    out = []
    for hh in range(LANE // C_HD):
        xh = x_t[hh * C_HD:(hh + 1) * C_HD, :]
        ms = jnp.mean(xh * xh, axis=0, keepdims=True)
        xn = xh * lax.rsqrt(ms + NORM_EPS) * w_rep[hh * C_HD:(hh + 1) * C_HD, :]
        x1, x2 = xn[:half, :], xn[half:, :]
        out += [x1 * cos - x2 * sin, x2 * cos + x1 * sin]
    return jnp.concatenate(out, axis=0)


def _prompt_layer_kernel(
        sinks_ref,
        x_ref, xnext_ref, nw_ref, wt_ref, wout_ref, pk_ref, cs_ref,
        o_ref, st_ref,
        u0_scr, u1_scr, xn_scr, y_scr, kprev_scr, roll_scr, *, tb):
    L = CHUNK
    nch = tb // L
    t_id = pl.program_id(1)
    sub = 0
    gb_ref, al_ref, anw_ref, dsk_ref, bnw_ref, qnw_ref, knw_ref, _, cb_ref, cw_ref = _unpack_params(pk_ref)
    conv_tap = lambda j: cw_ref[j * B_CONV_DIM:(j + 1) * B_CONV_DIM, :]
    cst_ref, mst_ref, hst_ref, convt_ref, kt_ref, vt_ref = (
        st_ref.at[pl.ds(start, n)] for start, n in zip(ST_OFFSETS, ST_SIZES))

    @pl.when(t_id == 0)
    def _():
        st_ref[...] = jnp.zeros_like(st_ref)
        kprev_scr[...] = jnp.zeros_like(kprev_scr)
        roll_scr[...] = jnp.zeros_like(roll_scr)
        u0_scr[...] = _dot_nt(wt_ref[...], _rmsnorm_rows(x_ref[:tb, :], nw_ref[...]).astype(MXU_DTYPE))

    u_cur, u_next = u0_scr, u1_scr

    row = lax.broadcasted_iota(jnp.int32, (L, L), 0)
    lane = lax.broadcasted_iota(jnp.int32, (L, L), 1)
    lo = lane < HALF
    top = row < B_P
    src_le_t = row <= lane
    tri = jnp.where(lane <= row, 1.0, 0.0).astype(MXU_DTYPE)
    grow = lax.broadcasted_iota(jnp.int32, (N_GATES, L), 0)
    krow = lax.broadcasted_iota(jnp.int32, (2 * L, L), 0)
    klane = lax.broadcasted_iota(jnp.int32, (2 * L, L), 1)
    neg_inf = -jnp.inf
    a_neg = -jnp.exp(al_ref[...])

    def chunk(c, carry):
        cols = pl.ds(pl.multiple_of(c * L, L), L)
        ycols = pl.ds(pl.multiple_of(sub * tb + c * L, L), L)

        def next_block_piece(site):
            if site not in INPROJ_SITES:
                return
            k = INPROJ_SITES.index(site)
            start = jnp.minimum((c * INPROJ_PIECES + k) * PIECE_ROWS, LAST_PIECE_START)
            wrows = pl.ds(pl.multiple_of(start, 2 * SUBLANE), PIECE_ROWS)
            u_next[wrows, :] = _dot_nt(wt_ref[wrows, :], xn_scr[...])

        pre_r = jnp.concatenate([u_cur[OFF_GA:OFF_GA + 2 * A_HEADS, cols], u_cur[OFF_GB:OFF_GB + B_HEADS, cols]],
                                axis=0) + gb_ref[...]
        sp_r, ls_r = _softplus_terms(pre_r)
        z_r = jnp.where((grow >= A_HEADS) & (grow < 2 * A_HEADS), ls_r,
                        jnp.where(grow >= 2 * A_HEADS, sp_r * a_neg, 0.0))
        zp = _split3(z_r)
        cum_r = _dot_nt(zp[0], tri) + _dot_nt(zp[1], tri) + _dot_nt(zp[2], tri)
        colf = jnp.concatenate([pre_r, cum_r, jnp.zeros((L - 2 * N_GATES, L), F32)], axis=0).T

        def piece_item(site):
            next_block_piece(site)
            return
            yield

        def mlstm_head(h, q_t, k_p):
            e = h % 2
            k_m = jnp.where(lo if e == 0 else jnp.logical_not(lo), k_p, 0.0).astype(MXU_DTYPE)
            v_t = u_cur[OFF_AV + h * LANE:OFF_AV + (h + 1) * LANE, cols]
            b_row = cum_r[A_HEADS + h:A_HEADS + h + 1, :]
            i_row = pre_r[h:h + 1, :]
            c_col = colf[:, N_GATES + A_HEADS + h:N_GATES + A_HEADS + h + 1] - colf[:, h:h + 1]
            b_end = b_row[:, L - 1:L]
            m_prev = mst_ref[h:h + 1, 0:1]
            c_prev = cst_ref[h * C_EXT:(h + 1) * C_EXT, :]

            g_row = b_end - b_row + i_row
            m_loc = jnp.max(g_row, axis=1, keepdims=True)
            d_t = jnp.where(src_le_t, b_row - c_col, neg_inf)
            inter = b_row + m_prev
            m_t = jnp.maximum(inter, jnp.max(d_t, axis=0, keepdims=True))
            res = _dot(jnp.concatenate([k_m, c_prev.astype(MXU_DTYPE)], axis=0), q_t)
            w_row = jnp.exp(g_row - m_loc)
            v_ext = jnp.concatenate([v_t, jnp.ones((1, L), F32), jnp.zeros((C_EXT - A_V - 1, L), F32)], axis=0)
            c_loc = _dot((v_ext * w_row).astype(MXU_DTYPE), k_m)
            e_t = jnp.exp(d_t - m_t)
            si = jnp.exp(inter - m_t)
            yield
            s_t = e_t * res[:L, :]
            sv = _dot(v_t.astype(MXU_DTYPE), s_t.astype(MXU_DTYPE))
            yield
            num = sv + si * res[L:L + A_V, :]
            den = jnp.sum(s_t, axis=0, keepdims=True) + si * res[L + A_V:L + A_V + 1, :]
            hh = num * (1.0 / jnp.maximum(jnp.abs(den), jnp.exp(-m_t)))
            hn = hh * lax.rsqrt(jnp.mean(hh * hh, axis=0, keepdims=True) + NORM_EPS) * anw_ref[h * LANE:(h + 1) * LANE, :]
            ao = u_cur[OFF_AO + h * LANE:OFF_AO + (h + 1) * LANE, cols]
            az = u_cur[OFF_AZ + h * LANE:OFF_AZ + (h + 1) * LANE, cols]
            y_scr[h * LANE:(h + 1) * LANE, ycols] = (hn * jax.nn.sigmoid(ao) * _silu(az)).astype(y_scr.dtype)

            m_new = jnp.maximum(b_end + m_prev, m_loc)
            sp = jnp.exp(b_end + m_prev - m_new)
            sl = jnp.exp(m_loc - m_new)
            cst_ref[h * C_EXT:(h + 1) * C_EXT, :] = sp * c_prev + sl * c_loc
            mst_ref[h:h + 1, :] = jnp.broadcast_to(m_new, (1, LANE))

        def conv_silu():
            cur = u_cur[OFF_BXBC:OFF_BXBC + B_CONV_DIM, cols]
            acc = cb_ref[...] + cur * conv_tap(CONV_W - 1)
            for j in range(1, CONV_W):
                rolled = pltpu.roll(cur, j, axis=1)
                acc = acc + jnp.where(lane[:1, :] >= j, rolled, roll_scr[j - 1]) * conv_tap(CONV_W - 1 - j)
                roll_scr[j - 1] = rolled
            convt_ref[...] = cur
            return _silu(acc)

        def ssd_group(g, xbc):
            b_t = xbc[B_WIDTH + g * B_STATE:B_WIDTH + (g + 1) * B_STATE, :]
            c_t = xbc[B_WIDTH + (B_GROUPS + g) * B_STATE:B_WIDTH + (B_GROUPS + g + 1) * B_STATE, :]
            b_m = b_t.T.astype(MXU_DTYPE)
            hpg = B_HEADS // B_GROUPS
            h_prev = hst_ref[g * hpg * B_P:(g + 1) * hpg * B_P, :]
            yield
            res = _dot(jnp.concatenate([b_m, h_prev.astype(MXU_DTYPE)], axis=0), c_t.astype(MXU_DTYPE))
            x_ts, h_locs, rows_of = [], [], []
            for pj in range(hpg // 2):
                pb = g * (hpg // 2) + pj
                x_t = xbc[pb * LANE:(pb + 1) * LANE, :]
                a_ends, dec_rows, ea_rows, a_rows, a_cols, dt_rows = [], [], [], [], [], []
                for e in range(2):
                    gi = 2 * A_HEADS + 2 * pb + e
                    a_row = cum_r[gi:gi + 1, :]
                    dt_row = sp_r[gi:gi + 1, :]
                    a_end = a_row[:, L - 1:L]
                    a_rows.append(a_row)
                    a_cols.append(colf[:, N_GATES + gi:N_GATES + gi + 1])
                    dt_rows.append(dt_row)
                    a_ends.append(jnp.exp(a_end))
                    dec_rows.append(jnp.exp(a_end - a_row) * dt_row)
                    ea_rows.append(jnp.exp(a_row))
                h_locs.append(_dot((x_t * jnp.where(top, dec_rows[0], dec_rows[1])).astype(MXU_DTYPE), b_m))
                x_ts.append(x_t)
                rows_of.append((a_ends, ea_rows, a_rows, a_cols, dt_rows))
            yield
            cb_tt = res[:L, :]
            ys_all = []
            for pj in range(hpg // 2):
                a_ends, ea_rows, a_rows, a_cols, dt_rows = rows_of[pj]
                ys = []
                for e in range(2):
                    w_t = jnp.exp(jnp.where(src_le_t, a_rows[e] - a_cols[e], neg_inf)) * cb_tt
                    xdt = (x_ts[pj][e * B_P:(e + 1) * B_P, :] * dt_rows[e]).astype(MXU_DTYPE)
                    ys.append(_dot(xdt, w_t.astype(MXU_DTYPE)))
                ys_all.append(ys)
            yield
            gated = []
            for pj in range(hpg // 2):
                pb = g * (hpg // 2) + pj
                a_ends, ea_rows, a_rows, a_cols, dt_rows = rows_of[pj]
                inter = res[L + pj * LANE:L + (pj + 1) * LANE, :] * jnp.where(top, ea_rows[0], ea_rows[1])
                y_p = jnp.concatenate(ys_all[pj], axis=0) + inter + dsk_ref[pb * LANE:(pb + 1) * LANE, :] * x_ts[pj]
                hst_ref[pb * LANE:(pb + 1) * LANE, :] = \
                    jnp.where(top, a_ends[0], a_ends[1]) * h_prev[pj * LANE:(pj + 1) * LANE, :] + h_locs[pj]
                bz = u_cur[OFF_BZ + pb * LANE:OFF_BZ + (pb + 1) * LANE, cols]
                gated.append(y_p * _silu(bz))
            ms = sum(jnp.sum(gp * gp, axis=0, keepdims=True) for gp in gated) * (1.0 / (B_WIDTH // B_GROUPS))
            inv = lax.rsqrt(ms + NORM_EPS)
            for pj, gp in enumerate(gated):
                pb = g * (hpg // 2) + pj
                y_scr[A_WIDTH + pb * LANE:A_WIDTH + (pb + 1) * LANE, ycols] = \
                    (gp * inv * bnw_ref[pb * LANE:(pb + 1) * LANE, :]).astype(y_scr.dtype)

        cos = cs_ref[:C_HD // 2, ycols]
        sin = cs_ref[C_HD // 2:, ycols]
        ppg = C_HEADS // 2 // C_KV
        shared = {}

        def swa_keys():
            k_t = _norm_rope_t(u_cur[OFF_CK:OFF_CK + LANE, cols], knw_ref[...], cos, sin)
            v_t = u_cur[OFF_CV:OFF_CV + LANE, cols]
            k_p = k_t.T
            k_sw = pltpu.roll(k_p, HALF, axis=1)
            k_cur = [[jnp.where(lo, k_p, 0.0).astype(MXU_DTYPE), jnp.where(lo, 0.0, k_sw).astype(MXU_DTYPE)],
                     [jnp.where(lo, k_sw, 0.0).astype(MXU_DTYPE), jnp.where(lo, 0.0, k_p).astype(MXU_DTYPE)]]
            first = jnp.logical_and(t_id == 0, c == 0) if sub == 0 else False
            shift = jnp.where(first, 2 * L, 0)
            valid = ((krow < L) & (krow > klane + shift)) | ((krow >= L) & (krow - L <= klane))
            shared.update(k_t=k_t, v_t=v_t, k_cur=k_cur, valid=valid)

        def swa_pair(pb):
            g = pb // ppg
            k_cur, v_t, valid = shared["k_cur"], shared["v_t"], shared["valid"]
            q_t = _norm_rope_t(u_cur[OFF_CQ + pb * LANE:OFF_CQ + (pb + 1) * LANE, cols], qnw_ref[...], cos, sin)
            keys = jnp.concatenate([kprev_scr[2 * g], k_cur[g][0], kprev_scr[2 * g + 1], k_cur[g][1]], axis=0)
            sc = _dot(keys, q_t.astype(MXU_DTYPE)) * (C_HD ** -0.5)
            v_g = jnp.concatenate([vt_ref[g * C_HD:(g + 1) * C_HD, :], v_t[g * C_HD:(g + 1) * C_HD, :]],
                                  axis=1).astype(MXU_DTYPE)
            yield
            outs = []
            for e in range(2):
                s = jnp.where(valid, sc[e * 2 * L:(e + 1) * 2 * L, :], neg_inf)
                sink = sinks_ref[2 * pb + e]
                m = jnp.maximum(jnp.max(s, axis=0, keepdims=True), sink)
                ex = jnp.exp(s - m)
                p = ex * (1.0 / (jnp.sum(ex, axis=0, keepdims=True) + jnp.exp(sink - m)))
                outs.append(_dot(v_g, p.astype(MXU_DTYPE)))
            yield
            cz = u_cur[OFF_CZ + pb * LANE:OFF_CZ + (pb + 1) * LANE, cols]
            y_scr[A_WIDTH + B_WIDTH + pb * LANE:A_WIDTH + B_WIDTH + (pb + 1) * LANE, ycols] = \
                (jnp.concatenate(outs, axis=0) * _silu(cz)).astype(y_scr.dtype)

        def items():
            for h in range(A_HEADS):
                if h % 2 == 0:
                    pb = h // 2
                    q_t = u_cur[OFF_AQ + pb * LANE:OFF_AQ + (pb + 1) * LANE, cols].astype(MXU_DTYPE)
                    k_p = (u_cur[OFF_AK + pb * LANE:OFF_AK + (pb + 1) * LANE, cols] * (A_QK ** -0.5)).T
                yield mlstm_head(h, q_t, k_p)
                yield piece_item(1 + h)
            xbc = conv_silu()
            for g in range(B_GROUPS):
                yield ssd_group(g, xbc)
                yield piece_item(1 + A_HEADS + g)
            swa_keys()
            for pb in range(C_HEADS // 2):
                yield swa_pair(pb)
                if pb == 1:
                    yield piece_item(1 + A_HEADS + B_GROUPS)

        next_block_piece(0)
        _trace_interleaved(items(), PROMPT_ITEMS_IN_FLIGHT)
        for g in range(C_KV):
            for e in range(2):
                kprev_scr[2 * g + e] = shared["k_cur"][g][e]
        kt_ref[...] = shared["k_t"]
        vt_ref[...] = shared["v_t"]
        return carry

    u_bufs = (u0_scr, u1_scr)
    for sub in range(OUT_PAIR):
        x_after = x_ref[(sub + 1) * tb:(sub + 2) * tb, :] if sub + 1 < OUT_PAIR else xnext_ref[...]
        xn_scr[...] = _rmsnorm_rows(x_after, nw_ref[...]).astype(MXU_DTYPE)
        u_cur, u_next = u_bufs[sub % 2], u_bufs[(sub + 1) % 2]
        lax.fori_loop(0, nch, lambda c, carry: chunk(c, carry), 0)

    o_ref[...] = x_ref[...] + _dot_tn(y_scr[...], wout_ref[...])


def _prompt_layer(x, lw, l, cs_t, tb):
    n_seq, t_len, _ = x.shape
    assert t_len % (OUT_PAIR * tb) == 0 and tb % CHUNK == 0 and tb == PROMPT_BLOCK and OUT_PAIR % 2 == 0
    n_blocks = t_len // tb
    grid = (n_seq, n_blocks // OUT_PAIR)

    def per_seq(shape):
        nd = len(shape)
        return pl.BlockSpec((None,) + shape, lambda n, t, _nd=nd: (n,) + (0,) * _nd)

    def per_layer(shape, **kw):
        nd = len(shape)
        return pl.BlockSpec((None,) + shape, lambda n, t, _nd=nd: (l,) + (0,) * _nd, **kw)

    in_specs = [
        pl.BlockSpec(memory_space=pltpu.SMEM),
        pl.BlockSpec((None, OUT_PAIR * tb, D_MODEL), lambda n, t: (n, t, 0)),
        pl.BlockSpec((None, tb, D_MODEL),
                     lambda n, t: (n, jnp.minimum(OUT_PAIR * (t + 1), n_blocks - 1), 0)),
        per_layer((1, D_MODEL)),
        per_layer((D_INT, D_MODEL), pipeline_mode=pl.Buffered(1)),
        per_layer((D_MIX, D_MODEL), pipeline_mode=pl.Buffered(1)),
        per_layer((PK_ROWS, LANE), pipeline_mode=pl.Buffered(1)),
        pl.BlockSpec((C_HD, OUT_PAIR * tb), lambda n, t: (0, t)),
    ]
    out_shape = (
        jax.ShapeDtypeStruct((n_seq, t_len, D_MODEL), F32),
        jax.ShapeDtypeStruct((n_seq, ST_ROWS, LANE), F32),
    )
    out_specs = (
        pl.BlockSpec((None, OUT_PAIR * tb, D_MODEL), lambda n, t: (n, t, 0)),
        per_seq((ST_ROWS, LANE)),
    )
    scratch = [
        pltpu.VMEM((D_INT, tb), F32),
        pltpu.VMEM((D_INT, tb), F32),
        pltpu.VMEM((tb, D_MODEL), MXU_DTYPE),
        pltpu.VMEM((D_MIX, OUT_PAIR * tb), MXU_DTYPE),
        pltpu.VMEM((2 * C_KV, CHUNK, LANE), MXU_DTYPE),
        pltpu.VMEM((CONV_W - 1, B_CONV_DIM, LANE), F32),
    ]
    return pl.pallas_call(
        functools.partial(_prompt_layer_kernel, tb=tb),
        grid=grid, in_specs=in_specs, out_specs=out_specs, out_shape=out_shape, scratch_shapes=scratch,
        compiler_params=pltpu.CompilerParams(dimension_semantics=("arbitrary", "arbitrary"),
                                             vmem_limit_bytes=VMEM_LIMIT_BYTES),
        name="prompt_layer",
    )(lw["sinks"][l], x, x, lw["norm_w"], lw["w_t"], lw["w_out"], lw["packed"], cs_t)


def _unpad_prompt_states(st):
    lead = st.shape[:2]
    cst, mst, hst, convt, kt, vt = (st[:, :, o:o + n, :] for o, n in zip(ST_OFFSETS, ST_SIZES))
    cst = cst.reshape(lead + (A_HEADS, C_EXT, LANE))
    c_even, c_odd = cst[:, :, 0::2, :A_V, :HALF], cst[:, :, 1::2, :A_V, HALF:]
    c_state = jnp.stack([c_even, c_odd], axis=3).reshape(lead + (A_HEADS, A_V, A_QK))
    n_even, n_odd = cst[:, :, 0::2, A_V, :HALF], cst[:, :, 1::2, A_V, HALF:]
    n_state = jnp.stack([n_even, n_odd], axis=3).reshape(lead + (A_HEADS, A_QK))
    conv = jnp.swapaxes(convt[:, :, :, LANE - (CONV_W - 1):], 2, 3)
    k1 = jnp.transpose(kt.reshape(lead + (C_KV, C_HD, WINDOW)), (0, 1, 4, 2, 3))
    v1 = jnp.transpose(vt.reshape(lead + (C_KV, C_HD, WINDOW)), (0, 1, 4, 2, 3))
    return (c_state, n_state, mst[:, :, :A_HEADS, 0], hst.reshape(lead + (B_HEADS, B_P, B_STATE)), conv, k1, v1)


ITEMS_IN_FLIGHT = 16


def _trace_interleaved(item_iter, depth):
    active, exhausted = [], False
    while True:
        while not exhausted and len(active) < depth:
            nxt = next(item_iter, None)
            if nxt is None:
                exhausted = True
            else:
                active.append(nxt)
        if not active:
            return
        still = []
        for gen in active:
            try:
                next(gen)
                still.append(gen)
            except StopIteration:
                pass
        active = still


def _decode_kernel(
        x_ref, nw_ref, wt_ref, wout_ref, pk_ref, cw_ref, cb_ref, cs_ref,
        c_ref, n_ref, m_ref, s_ref, cv_ref, k_ref, v_ref,
        y_ref, co_ref, no_ref, mo_ref, so_ref, cvo_ref, ko_ref, vo_ref,
        hs_scr, ut_scr, vrow_scr, xbc_scr, xt_scr, ct_scr, qa_scr, knew_scr, rep_scr, h_scr, yt_scr, yall_scr):
    l = pl.program_id(0)
    j = pl.program_id(1)
    nb = x_ref.shape[0]
    hd_a = A_QK
    gb_ref, al_ref, anw_ref, dsk_ref, bnw_ref, qnw_ref, knw_ref, sink_ref, _, _ = _unpack_params(pk_ref)

    @pl.when(j == 0)
    def _layer_start():
        @pl.when(l == 0)
        def _():
            hs_scr[...] = x_ref[...]

        xn = _rmsnorm_rows(hs_scr[...], nw_ref[...]).astype(MXU_DTYPE)
        ut_scr[...] = _dot_nt(wt_ref[...], xn)
        yt_scr[...] = jnp.zeros_like(yt_scr)

        pre = jnp.concatenate([ut_scr[OFF_GA:OFF_GA + 2 * A_HEADS, :], ut_scr[OFF_GB:OFF_GB + B_HEADS, :]],
                              axis=0) + gb_ref[...]
        spl, lsg = _softplus_terms(pre)
        ig, lf, dt = pre[:A_HEADS], lsg[A_HEADS:2 * A_HEADS], spl[2 * A_HEADS:]
        d_a = jnp.exp(dt * (-jnp.exp(al_ref[2 * A_HEADS:, :])))
        m0 = m_ref[...]
        m_new = jnp.maximum(lf + m0, ig)
        sp = jnp.exp(lf + m0 - m_new)
        sl = jnp.exp(ig - m_new)
        mo_ref[...] = m_new
        inv_den = []
        for h in range(A_HEADS):
            k_t = ut_scr[OFF_AK + h * hd_a:OFF_AK + (h + 1) * hd_a, :] * (A_QK ** -0.5)
            q_t = ut_scr[OFF_AQ + h * hd_a:OFF_AQ + (h + 1) * hd_a, :]
            n_new = sp[h:h + 1] * n_ref[h] + sl[h:h + 1] * k_t
            no_ref[h] = n_new
            den = jnp.sum(n_new * q_t, axis=0, keepdims=True)
            inv_den.append(1.0 / jnp.maximum(jnp.abs(den), jnp.exp(-m_new[h:h + 1])))

        cos, sin = cs_ref[:C_HD // 2, :], cs_ref[C_HD // 2:, :]
        kn_t = _norm_rope_t(ut_scr[OFF_CK:OFF_CK + LANE, :], knw_ref[...], cos, sin)
        knew_scr[...] = kn_t
        s_new = []
        for pb in range(C_HEADS // 2):
            q_t = _norm_rope_t(ut_scr[OFF_CQ + pb * LANE:OFF_CQ + (pb + 1) * LANE, :], qnw_ref[...], cos, sin)
            qa_scr[pb * LANE:(pb + 1) * LANE, :] = q_t
            for e in range(2):
                g = (2 * pb + e) // (C_HEADS // C_KV)
                s_new.append(jnp.sum(q_t[e * C_HD:(e + 1) * C_HD, :] * kn_t[g * C_HD:(g + 1) * C_HD, :],
                                     axis=0, keepdims=True))

        table = jnp.concatenate([sp, sl, d_a, dt] + s_new + inv_den + [jnp.zeros((LANE - N_REP, nb), F32)], axis=0)
        table_t = table.T
        for r in range(N_REP):
            rep_scr[r] = jnp.broadcast_to(table_t[:, r:r + 1], (nb, LANE))

        ubx = ut_scr[OFF_BXBC:OFF_BXBC + B_CONV_DIM, :].T
        acc = cb_ref[...] + ubx * cw_ref[CONV_W - 1:CONV_W, :]
        for t in range(CONV_W - 1):
            acc = acc + cv_ref[t] * cw_ref[t:t + 1, :]
        xbc = _silu(acc)
        xbc_scr[...] = xbc
        for t in range(CONV_W - 2):
            cvo_ref[t] = cv_ref[t + 1]
        cvo_ref[CONV_W - 2] = ubx
        xt_scr[...] = xbc[:, :B_WIDTH].T
        for g in range(B_GROUPS):
            ct_scr[g] = xbc[:, B_WIDTH + (B_GROUPS + g) * B_STATE:B_WIDTH + (B_GROUPS + g + 1) * B_STATE].T.astype(ct_scr.dtype)
        vrow_scr[...] = ut_scr[OFF_AV:OFF_AV + A_WIDTH, :].T

    base = pl.multiple_of(j * SAMPLE_BLOCK, SAMPLE_BLOCK)
    rows = pl.ds(base, SAMPLE_BLOCK)
    lane_s = lax.broadcasted_iota(jnp.int32, (C_HD, nb), 1)
    lane_k = lax.broadcasted_iota(jnp.int32, (C_HD, WINDOW), 1)
    lane_r = lane_k[:1, :]
    v_rows = vrow_scr[rows, :]
    xbc_rows = xbc_scr[rows, :]
    reps = [rep_scr[r, rows, :] for r in range(N_REP)]
    hpk = C_HEADS // C_KV
    sels = [lane_s == base + i for i in range(SAMPLE_BLOCK)]

    def col(tile, i):
        return jnp.sum(jnp.where(sels[i], tile, 0.0), axis=1, keepdims=True)

    def rep(r, i):
        return reps[r][i:i + 1, :]

    acc = {}

    ones_m = jnp.ones((nb, LANE), MXU_DTYPE)

    def cols_mxu(tile):
        stacked = jnp.concatenate([jnp.where(sels[i], tile, 0.0) for i in range(SAMPLE_BLOCK)], axis=0)
        return _dot(stacked.astype(MXU_DTYPE), ones_m)

    def mlstm_item(h, i, kc, qc):
        yield
        v_row = v_rows[i:i + 1, h * A_V:(h + 1) * A_V]
        c_new = rep(REP_SP + h, i) * c_ref[i, h] + (rep(REP_SL + h, i) * v_row) * kc
        co_ref[i, h] = c_new
        acc[("h", h)].append(jnp.sum(c_new * qc, axis=0, keepdims=True) * rep(REP_IDEN + h, i))
        if i == SAMPLE_BLOCK - 1:
            h_scr[rows, h * A_V:(h + 1) * A_V] = jnp.concatenate(acc.pop(("h", h)), axis=0)

    def ssd_item(hd, i, xc):
        g = hd // (B_HEADS // B_GROUPS)
        yield
        b_row = xbc_rows[i:i + 1, B_WIDTH + g * B_STATE:B_WIDTH + (g + 1) * B_STATE]
        h_new = rep(REP_DA + hd, i) * s_ref[i, hd] + (rep(REP_DT + hd, i) * b_row) * xc
        so_ref[i, hd] = h_new
        acc[("y", hd)].append(h_new.astype(MXU_DTYPE))
        if i < SAMPLE_BLOCK - 1:
            return
        y_all = _dot(jnp.concatenate(acc.pop(("y", hd)), axis=0), ct_scr[g])
        yield
        y_blk = jnp.zeros((B_P, nb), F32)
        for ii in range(SAMPLE_BLOCK):
            y_blk = jnp.where(sels[ii], y_all[ii * B_P:(ii + 1) * B_P, :], y_blk)
        yt_scr[hd * B_P:(hd + 1) * B_P, :] = yt_scr[hd * B_P:(hd + 1) * B_P, :] + y_blk

    def attn_item(g, i, kn_all, vn_all, qcs, sink):
        kc_new, vc_new = col(kn_all, i), col(vn_all, i)
        yield
        k_t = k_ref[i, g]
        v_t = v_ref[i, g]
        ko_ref[i, g] = jnp.where(lane_k == WINDOW - 1, kc_new, pltpu.roll(k_t, WINDOW - 1, axis=1))
        vo_ref[i, g] = jnp.where(lane_k == WINDOW - 1, vc_new, pltpu.roll(v_t, WINDOW - 1, axis=1))
        s_rows = [jnp.where(lane_r == 0, rep(REP_SNEW + g * hpk + hh, i), jnp.sum(k_t * qcs[hh], axis=0, keepdims=True))
                  for hh in range(hpk)]
        s = jnp.concatenate(s_rows, axis=0) * (C_HD ** -0.5)
        m = jnp.maximum(jnp.max(s, axis=1, keepdims=True), sink)
        yield
        m_b = jnp.broadcast_to(m, s.shape)
        yield
        ex = jnp.exp(s - m_b)
        den = jnp.sum(ex, axis=1, keepdims=True) + jnp.exp(sink - m)
        yield
        inv_b = jnp.broadcast_to(1.0 / den, s.shape)
        yield
        p = ex * inv_b
        v_eff = jnp.where(lane_k == 0, vc_new, v_t)
        o_cols = [jnp.sum(v_eff * p[hh:hh + 1, :], axis=1, keepdims=True) for hh in range(hpk)]
        yield
        for hh in range(hpk):
            hd = g * hpk + hh
            acc[("o", hd)] = jnp.where(sels[i], o_cols[hh], acc[("o", hd)])
            if i == SAMPLE_BLOCK - 1:
                r0 = B_WIDTH + hd * C_HD
                yt_scr[r0:r0 + C_HD, :] = yt_scr[r0:r0 + C_HD, :] + acc.pop(("o", hd))

    def items():
        for h in range(A_HEADS):
            k_all = ut_scr[OFF_AK + h * hd_a:OFF_AK + (h + 1) * hd_a, :] * (A_QK ** -0.5)
            q_all = ut_scr[OFF_AQ + h * hd_a:OFF_AQ + (h + 1) * hd_a, :]
            acc[("h", h)] = []
            k_cols, q_cols = cols_mxu(k_all), cols_mxu(q_all)
            for i in range(SAMPLE_BLOCK):
                yield mlstm_item(h, i, k_cols[i * hd_a:(i + 1) * hd_a, :], q_cols[i * hd_a:(i + 1) * hd_a, :])
        for hd in range(B_HEADS):
            x_cols = cols_mxu(xt_scr[hd * B_P:(hd + 1) * B_P, :])
            acc[("y", hd)] = []
            for i in range(SAMPLE_BLOCK):
                yield ssd_item(hd, i, x_cols[i * B_P:(i + 1) * B_P, :])
        for g in range(C_KV):
            kn_all = knew_scr[g * C_HD:(g + 1) * C_HD, :]
            vn_all = ut_scr[OFF_CV + g * C_HD:OFF_CV + (g + 1) * C_HD, :]
            q_cols = [cols_mxu(qa_scr[(g * hpk + hh) * C_HD:(g * hpk + hh + 1) * C_HD, :]) for hh in range(hpk)]
            sink = sink_ref[g * hpk:(g + 1) * hpk, 0:1]
            for hh in range(hpk):
                acc[("o", g * hpk + hh)] = jnp.zeros((C_HD, nb), F32)
            for i in range(SAMPLE_BLOCK):
                yield attn_item(g, i, kn_all, vn_all, [qc[i * C_HD:(i + 1) * C_HD, :] for qc in q_cols], sink)

    _trace_interleaved(items(), ITEMS_IN_FLIGHT)

    @pl.when(j == pl.num_programs(1) - 1)
    def _layer_end():
        for h in range(A_HEADS):
            h_t = h_scr[:, h * A_V:(h + 1) * A_V].T
            hn = h_t * lax.rsqrt(jnp.mean(h_t * h_t, axis=0, keepdims=True) + NORM_EPS) * anw_ref[h * A_V:(h + 1) * A_V, :]
            ao = ut_scr[OFF_AO + h * A_V:OFF_AO + (h + 1) * A_V, :]
            az = ut_scr[OFF_AZ + h * A_V:OFF_AZ + (h + 1) * A_V, :]
            yall_scr[h * A_V:(h + 1) * A_V, :] = (hn * jax.nn.sigmoid(ao) * _silu(az)).astype(yall_scr.dtype)
        gw = B_WIDTH // B_GROUPS
        for g in range(B_GROUPS):
            r = slice(g * gw, (g + 1) * gw)
            y_g = yt_scr[r, :] + dsk_ref[r, :] * xt_scr[r, :]
            gated = y_g * _silu(ut_scr[OFF_BZ + g * gw:OFF_BZ + (g + 1) * gw, :])
            inv = lax.rsqrt(jnp.mean(gated * gated, axis=0, keepdims=True) + NORM_EPS)
            yall_scr[A_WIDTH + g * gw:A_WIDTH + (g + 1) * gw, :] = (gated * inv * bnw_ref[r, :]).astype(yall_scr.dtype)
        yall_scr[A_WIDTH + B_WIDTH:, :] = \
            (yt_scr[B_WIDTH:, :] * _silu(ut_scr[OFF_CZ:OFF_CZ + C_WIDTH, :])).astype(yall_scr.dtype)
        hs_new = hs_scr[...] + _dot_tn(yall_scr[...], wout_ref[...])
        hs_scr[...] = hs_new

        @pl.when(l == pl.num_programs(0) - 1)
        def _():
            y_ref[...] = hs_new


def _decode(x, lw, cs_s, c_v, n_v, m_v, s_v, conv_v, k_v, v_v):
    depth, nb = c_v.shape[0], x.shape[0]
    assert nb == LANE, "samples sit on the 128 lanes next to lane-replicated parameters"
    grid = (depth, nb // SAMPLE_BLOCK)

    def per_layer(shape):
        nd = len(shape)
        return pl.BlockSpec((None,) + shape, lambda l, j, _nd=nd: (l,) + (0,) * _nd)

    def per_block(shape):
        nd = len(shape)
        return pl.BlockSpec((None, SAMPLE_BLOCK) + shape, lambda l, j, _nd=nd: (l, j) + (0,) * _nd)

    def weight(shape):
        nd = len(shape)
        return pl.BlockSpec((None,) + shape, lambda l, j, _nd=nd: (l,) + (0,) * _nd, pipeline_mode=pl.Buffered(1))

    fixed = lambda shape: pl.BlockSpec(shape, lambda l, j, _nd=len(shape): (0,) * _nd)
    state_specs = [
        per_block((A_HEADS, A_QK, A_V)), per_layer((A_HEADS, A_QK, nb)), per_layer((A_HEADS, nb)),
        per_block((B_HEADS, B_P, B_STATE)), per_layer((CONV_W - 1, nb, B_CONV_DIM)),
        per_block((C_KV, C_HD, WINDOW)), per_block((C_KV, C_HD, WINDOW)),
    ]
    in_specs = [
        fixed((nb, D_MODEL)),
        per_layer((1, D_MODEL)), weight((D_INT, D_MODEL)), weight((D_MIX, D_MODEL)),
        weight((PK_ROWS, LANE)), per_layer((CONV_W, B_CONV_DIM)), per_layer((1, B_CONV_DIM)),
        fixed((C_HD, LANE)),
    ] + state_specs
    states = (c_v, n_v, m_v, s_v, conv_v, k_v, v_v)
    out_shape = (jax.ShapeDtypeStruct((nb, D_MODEL), F32),) + tuple(jax.ShapeDtypeStruct(a.shape, F32) for a in states)
    scratch = [
        pltpu.VMEM((nb, D_MODEL), F32),
        pltpu.VMEM((D_INT, nb), F32),
        pltpu.VMEM((nb, A_WIDTH), F32),
        pltpu.VMEM((nb, B_CONV_DIM), F32),
        pltpu.VMEM((B_WIDTH, nb), F32),
        pltpu.VMEM((B_GROUPS, B_STATE, nb), MXU_DTYPE),
        pltpu.VMEM((C_WIDTH, nb), F32),
        pltpu.VMEM((C_KV * C_HD, nb), F32),
        pltpu.VMEM((N_REP, nb, LANE), F32),
        pltpu.VMEM((nb, A_WIDTH), F32),
        pltpu.VMEM((B_WIDTH + C_WIDTH, nb), F32),
        pltpu.VMEM((D_MIX, nb), MXU_DTYPE),
    ]
    return pl.pallas_call(
        _decode_kernel,
        grid=grid, in_specs=in_specs, out_specs=(fixed((nb, D_MODEL)),) + tuple(state_specs),
        out_shape=out_shape, scratch_shapes=scratch,
        compiler_params=pltpu.CompilerParams(dimension_semantics=("arbitrary", "arbitrary"),
                                             vmem_limit_bytes=VMEM_LIMIT_BYTES),
        name="decode",
    )(x, lw["norm_w"], lw["w_t"], lw["w_out"], lw["packed"], lw["conv_w"], lw["conv_b"], cs_s, *states)


def _prep_weights(norm_w, w_in, a_igate_b, a_fgate_b, a_norm_w, b_conv_w, b_conv_b, b_dt_bias, b_A_log, b_D,
                  b_norm_w, c_qnorm_w, c_knorm_w, c_sinks, w_out):
    depth = w_in.shape[0]
    w_t = jnp.swapaxes(w_in, 1, 2).astype(MXU_DTYPE)
    gbias = jnp.concatenate([a_igate_b, a_fgate_b, b_dt_bias], axis=-1)
    alog = jnp.concatenate([jnp.zeros((depth, 2 * A_HEADS), b_A_log.dtype), b_A_log], axis=-1)

    rows = jnp.concatenate([
        gbias, alog, a_norm_w, jnp.repeat(b_D, B_P, axis=-1), b_norm_w,
        jnp.tile(c_qnorm_w, (1, LANE // C_HD)), jnp.tile(c_knorm_w, (1, LANE // C_HD)), c_sinks,
        b_conv_b, b_conv_w.reshape(depth, CONV_W * B_CONV_DIM)], axis=-1).astype(F32)
    assert rows.shape == (depth, PK_ROWS)
    return {
        "w_t": w_t, "w_out": w_out.astype(MXU_DTYPE),
        "norm_w": norm_w.astype(F32)[:, None, :],
        "packed": jnp.broadcast_to(rows[..., None], (depth, PK_ROWS, LANE)),
        "conv_w": b_conv_w.astype(F32), "conv_b": b_conv_b.astype(F32)[:, None, :],
        "sinks": c_sinks.astype(F32),
    }


def _rope_tables_t(pos, width=None):
    half = C_HD // 2
    inv = ROPE_THETA ** (-jnp.arange(half, dtype=F32) / half)
    ang = inv[:, None] * pos.astype(F32)[None, :]
    table = jnp.concatenate([jnp.cos(ang), jnp.sin(ang)], axis=0)
    if width is not None:
        table = jnp.broadcast_to(table, (C_HD, width))
    return table


def _kernel_impl(x_prompt, x_sample, state_mlstm_C, state_mlstm_n, state_mlstm_m, state_ssm, state_conv,
                 cache_k, cache_v, norm_w, w_in, a_igate_b, a_fgate_b, a_norm_w, b_conv_w, b_conv_b,
                 b_dt_bias, b_A_log, b_D, b_norm_w, c_qnorm_w, c_knorm_w, c_sinks, w_out, *, tb):
    depth = w_in.shape[0]
    t_len = x_prompt.shape[1]
    lw = _prep_weights(norm_w, w_in, a_igate_b, a_fgate_b, a_norm_w, b_conv_w, b_conv_b, b_dt_bias, b_A_log,
                       b_D, b_norm_w, c_qnorm_w, c_knorm_w, c_sinks, w_out)
    cs_p = _rope_tables_t(jnp.arange(t_len, dtype=jnp.int32))
    hp = x_prompt
    st_prompt = []
    for l in range(depth):
        res = _prompt_layer(hp, lw, l, cs_p, tb)
        hp = res[0]
        st_prompt.append(res[1])
    p_states = _unpad_prompt_states(jnp.stack(st_prompt))

    assert x_sample.shape[1] == 1
    cs_s = _rope_tables_t(PAST_LEN + jnp.arange(1, dtype=jnp.int32), LANE)
    outs = _decode(
        x_sample[:, 0, :], lw, cs_s,
        jnp.transpose(state_mlstm_C, (0, 1, 2, 4, 3)), jnp.transpose(state_mlstm_n, (0, 2, 3, 1)),
        jnp.transpose(state_mlstm_m, (0, 2, 1)), state_ssm, jnp.transpose(state_conv, (0, 2, 1, 3)),
        jnp.transpose(cache_k, (0, 1, 3, 4, 2)), jnp.transpose(cache_v, (0, 1, 3, 4, 2)))
    hs, c_o, n_o, m_o, s_o, conv_o, k_o, v_o = outs
    s_states = (jnp.transpose(c_o, (0, 1, 2, 4, 3)), jnp.transpose(n_o, (0, 3, 1, 2)), jnp.transpose(m_o, (0, 2, 1)),
                s_o, jnp.transpose(conv_o, (0, 2, 1, 3)),
                jnp.transpose(k_o, (0, 1, 4, 2, 3)), jnp.transpose(v_o, (0, 1, 4, 2, 3)))
    return (hp, hs[:, None, :], *p_states, *s_states)


def kernel(x_prompt, x_sample, state_mlstm_C, state_mlstm_n, state_mlstm_m, state_ssm, state_conv, cache_k, cache_v, norm_w, w_in, a_igate_b, a_fgate_b, a_norm_w, b_conv_w, b_conv_b, b_dt_bias, b_A_log, b_D, b_norm_w, c_qnorm_w, c_knorm_w, c_sinks, w_out):
    return _kernel_impl(x_prompt, x_sample, state_mlstm_C, state_mlstm_n, state_mlstm_m, state_ssm, state_conv,
                        cache_k, cache_v, norm_w, w_in, a_igate_b, a_fgate_b, a_norm_w, b_conv_w, b_conv_b,
                        b_dt_bias, b_A_log, b_D, b_norm_w, c_qnorm_w, c_knorm_w, c_sinks, w_out, tb=PROMPT_BLOCK)
```

```python
import functools

import jax
import jax.numpy as jnp
import numpy as np
from jax import lax
from jax.experimental import pallas as pl
from jax.experimental.pallas import tpu as pltpu

F32 = jnp.float32
MXU_DTYPE = jnp.bfloat16

D_MODEL = 1024
A_HEADS, A_QK, A_V = 4, 64, 128
A_WIDTH = A_HEADS * A_V
B_HEADS, B_P, B_GROUPS, B_STATE = 8, 64, 2, 128
B_WIDTH = B_HEADS * B_P
CONV_W = 4
B_CONV_DIM = B_WIDTH + 2 * B_GROUPS * B_STATE
C_HEADS, C_KV, C_HD = 8, 2, 64
C_WIDTH = C_HEADS * C_HD
WINDOW = 128
ROPE_THETA = 10000.0
D_MIX = A_WIDTH + B_WIDTH + C_WIDTH
NORM_EPS = 1e-6
PAST_LEN = 8192

LANE = 128
SUBLANE = 8
HALF = LANE // 2

CHUNK = 128
PROMPT_BLOCK = 256
SAMPLE_BLOCK = SUBLANE
N_GATES = 2 * A_HEADS + B_HEADS

_SRC_SIZES = (A_HEADS * A_QK, A_HEADS * A_QK, A_WIDTH, A_WIDTH, A_WIDTH, A_HEADS, A_HEADS,
              B_WIDTH, B_CONV_DIM, B_HEADS, C_WIDTH, C_KV * C_HD, C_KV * C_HD, C_WIDTH)
(OFF_AQ, OFF_AK, OFF_AV, OFF_AO, OFF_AZ, OFF_GA, _, OFF_BZ, OFF_BXBC, OFF_GB, OFF_CQ, OFF_CK, OFF_CV, OFF_CZ,
 D_INT) = np.concatenate([[0], np.cumsum(_SRC_SIZES)]).tolist()
assert all(o % SUBLANE == 0 for o in (OFF_GA, OFF_BZ, OFF_BXBC, OFF_GB, OFF_CQ, OFF_CK, OFF_CV, OFF_CZ))
ITEM_ORDER = ("m0", "a0", "p0", "m1", "a1", "s0", "m2", "a2", "p1", "m3", "a3", "s1")
INPROJ_PIECES = sum(tok[0] == "p" for tok in ITEM_ORDER)
PROMPT_ITEMS_IN_FLIGHT = 2
_PIECES_PER_BLOCK = INPROJ_PIECES * (PROMPT_BLOCK // CHUNK)
PIECE_ROWS = -(-D_INT // (_PIECES_PER_BLOCK * 2 * SUBLANE)) * 2 * SUBLANE
LAST_PIECE_START = D_INT - PIECE_ROWS
assert LAST_PIECE_START % (2 * SUBLANE) == 0
OUT_PAIR = 2

PK_GB = 0
PK_AL = PK_GB + N_GATES
PK_ANW = PK_AL + N_GATES
PK_DSK = PK_ANW + A_WIDTH
PK_BNW = PK_DSK + B_WIDTH
PK_QNW = PK_BNW + B_WIDTH
PK_KNW = PK_QNW + LANE
PK_SINK = PK_KNW + LANE
PK_CB = PK_SINK + C_HEADS
PK_CW = PK_CB + B_CONV_DIM
PK_ROWS = PK_CW + CONV_W * B_CONV_DIM


def _unpack_params(pk_ref):
    view = lambda start, n: pk_ref.at[pl.ds(start, n)]
    return (view(PK_GB, N_GATES), view(PK_AL, N_GATES), view(PK_ANW, A_WIDTH), view(PK_DSK, B_WIDTH),
            view(PK_BNW, B_WIDTH), view(PK_QNW, LANE), view(PK_KNW, LANE), view(PK_SINK, C_HEADS),
            view(PK_CB, B_CONV_DIM), view(PK_CW, CONV_W * B_CONV_DIM))


C_EXT = A_V + 2 * SUBLANE
ST_SIZES = (A_HEADS * C_EXT, SUBLANE, B_HEADS * B_P, B_CONV_DIM, C_KV * C_HD, C_KV * C_HD)
ST_OFFSETS = tuple(int(o) for o in np.cumsum((0,) + ST_SIZES[:-1]))
ST_ROWS = sum(ST_SIZES)
VMEM_LIMIT_BYTES =56 * 1024 * 1024

REP_SP, REP_SL = 0, A_HEADS
REP_DA, REP_DT = 2 * A_HEADS, 2 * A_HEADS + B_HEADS
REP_SNEW = 2 * A_HEADS + 2 * B_HEADS
REP_IDEN = REP_SNEW + C_HEADS
N_REP = REP_IDEN + A_HEADS


def _dot(a, b):
    return jnp.dot(a, b, preferred_element_type=F32)


def _dot_nt(a, b):
    return lax.dot_general(a, b, (((1,), (1,)), ((), ())), preferred_element_type=F32)


def _dot_tn(a, b):
    return lax.dot_general(a, b, (((0,), (0,)), ((), ())), preferred_element_type=F32)


def _split3(z):
    hi = z.astype(MXU_DTYPE)
    r1 = z - hi.astype(F32)
    mid = r1.astype(MXU_DTYPE)
    lo = (r1 - mid.astype(F32)).astype(MXU_DTYPE)
    return hi, mid, lo


def _softplus_terms(x):
    t = jnp.log1p(jnp.exp(-jnp.abs(x)))
    return jnp.maximum(x, 0.0) + t, jnp.minimum(x, 0.0) - t


def _silu(x):
    return x * jax.nn.sigmoid(x)


def _rmsnorm_rows(x, w):
    return x * lax.rsqrt(jnp.mean(x * x, axis=-1, keepdims=True) + NORM_EPS) * w


def _norm_rope_t(x_t, w_rep, cos, sin):
    half = C_HD // 2
    out = []
    for hh in range(LANE // C_HD):
        xh = x_t[hh * C_HD:(hh + 1) * C_HD, :]
        ms = jnp.mean(xh * xh, axis=0, keepdims=True)
        xn = xh * lax.rsqrt(ms + NORM_EPS) * w_rep[hh * C_HD:(hh + 1) * C_HD, :]
        x1, x2 = xn[:half, :], xn[half:, :]
        out += [x1 * cos - x2 * sin, x2 * cos + x1 * sin]
    return jnp.concatenate(out, axis=0)


def _prompt_layer_kernel(
        sinks_ref,
        x_ref, xnext_ref, nw_ref, wt_ref, wout_ref, pk_ref, cs_ref,
        o_ref, st_ref,
        u0_scr, u1_scr, xn_scr, y_scr, kprev_scr, roll_scr, *, tb):
    L = CHUNK
    nch = tb // L
    t_id = pl.program_id(1)
    sub = 0
    gb_ref, al_ref, anw_ref, dsk_ref, bnw_ref, qnw_ref, knw_ref, _, cb_ref, cw_ref = _unpack_params(pk_ref)
    conv_tap = lambda j: cw_ref[j * B_CONV_DIM:(j + 1) * B_CONV_DIM, :]
    cst_ref, mst_ref, hst_ref, convt_ref, kt_ref, vt_ref = (
        st_ref.at[pl.ds(start, n)] for start, n in zip(ST_OFFSETS, ST_SIZES))

    @pl.when(t_id == 0)
    def _():
        st_ref[...] = jnp.zeros_like(st_ref)
        kprev_scr[...] = jnp.zeros_like(kprev_scr)
        roll_scr[...] = jnp.zeros_like(roll_scr)
        u0_scr[...] = _dot_nt(wt_ref[...], _rmsnorm_rows(x_ref[:tb, :], nw_ref[...]).astype(MXU_DTYPE))

    u_cur, u_next = u0_scr, u1_scr

    row = lax.broadcasted_iota(jnp.int32, (L, L), 0)
    lane = lax.broadcasted_iota(jnp.int32, (L, L), 1)
    lo = lane < HALF
    top = row < B_P
    src_le_t = row <= lane
    tri = jnp.where(lane <= row, 1.0, 0.0).astype(MXU_DTYPE)
    grow = lax.broadcasted_iota(jnp.int32, (N_GATES, L), 0)
    krow = lax.broadcasted_iota(jnp.int32, (2 * L, L), 0)
    klane = lax.broadcasted_iota(jnp.int32, (2 * L, L), 1)
    neg_inf = -jnp.inf
    a_neg = -jnp.exp(al_ref[...])

    def chunk(c, carry):
        cols = pl.ds(pl.multiple_of(c * L, L), L)
        ycols = pl.ds(pl.multiple_of(sub * tb + c * L, L), L)

        def piece_item(k):
            start = jnp.minimum((c * INPROJ_PIECES + k) * PIECE_ROWS, LAST_PIECE_START)
            wrows = pl.ds(pl.multiple_of(start, 2 * SUBLANE), PIECE_ROWS)
            u_next[wrows, :] = _dot_nt(wt_ref[wrows, :], xn_scr[...])
            return
            yield

        pre_r = jnp.concatenate([u_cur[OFF_GA:OFF_GA + 2 * A_HEADS, cols], u_cur[OFF_GB:OFF_GB + B_HEADS, cols]],
                                axis=0) + gb_ref[...]
        sp_r, ls_r = _softplus_terms(pre_r)
        z_r = jnp.where((grow >= A_HEADS) & (grow < 2 * A_HEADS), ls_r,
                        jnp.where(grow >= 2 * A_HEADS, sp_r * a_neg, 0.0))
        zp = _split3(z_r)
        cum_r = _dot_nt(zp[0], tri) + _dot_nt(zp[1], tri) + _dot_nt(zp[2], tri)
        colf = jnp.concatenate([pre_r, cum_r, jnp.zeros((L - 2 * N_GATES, L), F32)], axis=0).T

        def mlstm_head(h, q_t, k_p):
            e = h % 2
            k_m = jnp.where(lo if e == 0 else jnp.logical_not(lo), k_p, 0.0).astype(MXU_DTYPE)
            v_t = u_cur[OFF_AV + h * LANE:OFF_AV + (h + 1) * LANE, cols]
            b_row = cum_r[A_HEADS + h:A_HEADS + h + 1, :]
            i_row = pre_r[h:h + 1, :]
            c_col = colf[:, N_GATES + A_HEADS + h:N_GATES + A_HEADS + h + 1] - colf[:, h:h + 1]
            b_end = b_row[:, L - 1:L]
            m_prev = mst_ref[h:h + 1, 0:1]
            c_prev = cst_ref[h * C_EXT:(h + 1) * C_EXT, :]

            g_row = b_end - b_row + i_row
            m_loc = jnp.max(g_row, axis=1, keepdims=True)
            d_t = jnp.where(src_le_t, b_row - c_col, neg_inf)
            inter = b_row + m_prev
            m_t = jnp.maximum(inter, jnp.max(d_t, axis=0, keepdims=True))
            res = _dot(jnp.concatenate([k_m, c_prev.astype(MXU_DTYPE)], axis=0), q_t)
            w_row = jnp.exp(g_row - m_loc)
            v_ext = jnp.concatenate([v_t, jnp.ones((1, L), F32), jnp.zeros((C_EXT - A_V - 1, L), F32)], axis=0)
            c_loc = _dot((v_ext * w_row).astype(MXU_DTYPE), k_m)
            e_t = jnp.exp(d_t - m_t)
            si = jnp.exp(inter - m_t)
            yield
            s_t = e_t * res[:L, :]
            sv = _dot(v_t.astype(MXU_DTYPE), s_t.astype(MXU_DTYPE))
            yield
            num = sv + si * res[L:L + A_V, :]
            den = jnp.sum(s_t, axis=0, keepdims=True) + si * res[L + A_V:L + A_V + 1, :]
            hh = num * (1.0 / jnp.maximum(jnp.abs(den), jnp.exp(-m_t)))
            hn = hh * lax.rsqrt(jnp.mean(hh * hh, axis=0, keepdims=True) + NORM_EPS) * anw_ref[h * LANE:(h + 1) * LANE, :]
            ao = u_cur[OFF_AO + h * LANE:OFF_AO + (h + 1) * LANE, cols]
            az = u_cur[OFF_AZ + h * LANE:OFF_AZ + (h + 1) * LANE, cols]
            y_scr[h * LANE:(h + 1) * LANE, ycols] = (hn * jax.nn.sigmoid(ao) * _silu(az)).astype(y_scr.dtype)

            m_new = jnp.maximum(b_end + m_prev, m_loc)
            sp = jnp.exp(b_end + m_prev - m_new)
            sl = jnp.exp(m_loc - m_new)
            cst_ref[h * C_EXT:(h + 1) * C_EXT, :] = sp * c_prev + sl * c_loc
            mst_ref[h:h + 1, :] = jnp.broadcast_to(m_new, (1, LANE))

        def conv_silu():
            cur = u_cur[OFF_BXBC:OFF_BXBC + B_CONV_DIM, cols]
            acc = cb_ref[...] + cur * conv_tap(CONV_W - 1)
            for j in range(1, CONV_W):
                rolled = pltpu.roll(cur, j, axis=1)
                acc = acc + jnp.where(lane[:1, :] >= j, rolled, roll_scr[j - 1]) * conv_tap(CONV_W - 1 - j)
                roll_scr[j - 1] = rolled
            convt_ref[...] = cur
            return _silu(acc)

        def ssd_group(g, xbc):
            b_t = xbc[B_WIDTH + g * B_STATE:B_WIDTH + (g + 1) * B_STATE, :]
            c_t = xbc[B_WIDTH + (B_GROUPS + g) * B_STATE:B_WIDTH + (B_GROUPS + g + 1) * B_STATE, :]
            b_m = b_t.T.astype(MXU_DTYPE)
            hpg = B_HEADS // B_GROUPS
            h_prev = hst_ref[g * hpg * B_P:(g + 1) * hpg * B_P, :]
            yield
            res = _dot(jnp.concatenate([b_m, h_prev.astype(MXU_DTYPE)], axis=0), c_t.astype(MXU_DTYPE))
            x_ts, h_locs, rows_of = [], [], []
            for pj in range(hpg // 2):
                pb = g * (hpg // 2) + pj
                x_t = xbc[pb * LANE:(pb + 1) * LANE, :]
                a_ends, dec_rows, ea_rows, a_rows, a_cols, dt_rows = [], [], [], [], [], []
                for e in range(2):
                    gi = 2 * A_HEADS + 2 * pb + e
                    a_row = cum_r[gi:gi + 1, :]
                    dt_row = sp_r[gi:gi + 1, :]
                    a_end = a_row[:, L - 1:L]
                    a_rows.append(a_row)
                    a_cols.append(colf[:, N_GATES + gi:N_GATES + gi + 1])
                    dt_rows.append(dt_row)
                    a_ends.append(jnp.exp(a_end))
                    dec_rows.append(jnp.exp(a_end - a_row) * dt_row)
                    ea_rows.append(jnp.exp(a_row))
                h_locs.append(_dot((x_t * jnp.where(top, dec_rows[0], dec_rows[1])).astype(MXU_DTYPE), b_m))
                x_ts.append(x_t)
                rows_of.append((a_ends, ea_rows, a_rows, a_cols, dt_rows))
            yield
            cb_tt = res[:L, :]
            ys_all = []
            for pj in range(hpg // 2):
                a_ends, ea_rows, a_rows, a_cols, dt_rows = rows_of[pj]
                ys = []
                for e in range(2):
                    w_t = jnp.exp(jnp.where(src_le_t, a_rows[e] - a_cols[e], neg_inf)) * cb_tt
                    xdt = (x_ts[pj][e * B_P:(e + 1) * B_P, :] * dt_rows[e]).astype(MXU_DTYPE)
                    ys.append(_dot(xdt, w_t.astype(MXU_DTYPE)))
                ys_all.append(ys)
            yield
            gated = []
            for pj in range(hpg // 2):
                pb = g * (hpg // 2) + pj
                a_ends, ea_rows, a_rows, a_cols, dt_rows = rows_of[pj]
                inter = res[L + pj * LANE:L + (pj + 1) * LANE, :] * jnp.where(top, ea_rows[0], ea_rows[1])
                y_p = jnp.concatenate(ys_all[pj], axis=0) + inter + dsk_ref[pb * LANE:(pb + 1) * LANE, :] * x_ts[pj]
                hst_ref[pb * LANE:(pb + 1) * LANE, :] = \
                    jnp.where(top, a_ends[0], a_ends[1]) * h_prev[pj * LANE:(pj + 1) * LANE, :] + h_locs[pj]
                bz = u_cur[OFF_BZ + pb * LANE:OFF_BZ + (pb + 1) * LANE, cols]
                gated.append(y_p * _silu(bz))
            ms = sum(jnp.sum(gp * gp, axis=0, keepdims=True) for gp in gated) * (1.0 / (B_WIDTH // B_GROUPS))
            inv = lax.rsqrt(ms + NORM_EPS)
            for pj, gp in enumerate(gated):
                pb = g * (hpg // 2) + pj
                y_scr[A_WIDTH + pb * LANE:A_WIDTH + (pb + 1) * LANE, ycols] = \
                    (gp * inv * bnw_ref[pb * LANE:(pb + 1) * LANE, :]).astype(y_scr.dtype)

        cos = cs_ref[:C_HD // 2, ycols]
        sin = cs_ref[C_HD // 2:, ycols]
        ppg = C_HEADS // 2 // C_KV
        shared = {}

        def swa_keys():
            k_t = _norm_rope_t(u_cur[OFF_CK:OFF_CK + LANE, cols], knw_ref[...], cos, sin)
            v_t = u_cur[OFF_CV:OFF_CV + LANE, cols]
            k_p = k_t.T
            k_sw = pltpu.roll(k_p, HALF, axis=1)
            k_cur = [[jnp.where(lo, k_p, 0.0).astype(MXU_DTYPE), jnp.where(lo, 0.0, k_sw).astype(MXU_DTYPE)],
                     [jnp.where(lo, k_sw, 0.0).astype(MXU_DTYPE), jnp.where(lo, 0.0, k_p).astype(MXU_DTYPE)]]
            first = jnp.logical_and(t_id == 0, c == 0) if sub == 0 else False
            shift = jnp.where(first, 2 * L, 0)
            valid = ((krow < L) & (krow > klane + shift)) | ((krow >= L) & (krow - L <= klane))
            shared.update(k_t=k_t, v_t=v_t, k_cur=k_cur, valid=valid)

        def swa_pair(pb):
            g = pb // ppg
            k_cur, v_t, valid = shared["k_cur"], shared["v_t"], shared["valid"]
            q_t = _norm_rope_t(u_cur[OFF_CQ + pb * LANE:OFF_CQ + (pb + 1) * LANE, cols], qnw_ref[...], cos, sin)
            keys = jnp.concatenate([kprev_scr[2 * g], k_cur[g][0], kprev_scr[2 * g + 1], k_cur[g][1]], axis=0)
            sc = _dot(keys, q_t.astype(MXU_DTYPE)) * (C_HD ** -0.5)
            v_g = jnp.concatenate([vt_ref[g * C_HD:(g + 1) * C_HD, :], v_t[g * C_HD:(g + 1) * C_HD, :]],
                                  axis=1).astype(MXU_DTYPE)
            yield
            outs = []
            for e in range(2):
                s = jnp.where(valid, sc[e * 2 * L:(e + 1) * 2 * L, :], neg_inf)
                sink = sinks_ref[2 * pb + e]
                m = jnp.maximum(jnp.max(s, axis=0, keepdims=True), sink)
                ex = jnp.exp(s - m)
                p = ex * (1.0 / (jnp.sum(ex, axis=0, keepdims=True) + jnp.exp(sink - m)))
                outs.append(_dot(v_g, p.astype(MXU_DTYPE)))
            yield
            cz = u_cur[OFF_CZ + pb * LANE:OFF_CZ + (pb + 1) * LANE, cols]
            y_scr[A_WIDTH + B_WIDTH + pb * LANE:A_WIDTH + B_WIDTH + (pb + 1) * LANE, ycols] = \
                (jnp.concatenate(outs, axis=0) * _silu(cz)).astype(y_scr.dtype)

        def items():
            made = {}
            for tok in ITEM_ORDER:
                kind, idx = tok[0], int(tok[1:])
                if kind == "m":
                    pb = idx // 2
                    if ("qk", pb) not in made:
                        made[("qk", pb)] = (
                            u_cur[OFF_AQ + pb * LANE:OFF_AQ + (pb + 1) * LANE, cols].astype(MXU_DTYPE),
                            (u_cur[OFF_AK + pb * LANE:OFF_AK + (pb + 1) * LANE, cols] * (A_QK ** -0.5)).T)
                    yield mlstm_head(idx, *made[("qk", pb)])
                elif kind == "s":
                    if "xbc" not in made:
                        made["xbc"] = conv_silu()
                    yield ssd_group(idx, made["xbc"])
                elif kind == "a":
                    if "keys" not in made:
                        made["keys"] = swa_keys()
                    yield swa_pair(idx)
                else:
                    yield piece_item(idx)

        _trace_interleaved(items(), PROMPT_ITEMS_IN_FLIGHT)
        for g in range(C_KV):
            for e in range(2):
                kprev_scr[2 * g + e] = shared["k_cur"][g][e]
        kt_ref[...] = shared["k_t"]
        vt_ref[...] = shared["v_t"]
        return carry

    u_bufs = (u0_scr, u1_scr)
    for sub in range(OUT_PAIR):
        x_after = x_ref[(sub + 1) * tb:(sub + 2) * tb, :] if sub + 1 < OUT_PAIR else xnext_ref[...]
        xn_scr[...] = _rmsnorm_rows(x_after, nw_ref[...]).astype(MXU_DTYPE)
        u_cur, u_next = u_bufs[sub % 2], u_bufs[(sub + 1) % 2]
        lax.fori_loop(0, nch, lambda c, carry: chunk(c, carry), 0)

    o_ref[...] = x_ref[...] + _dot_tn(y_scr[...], wout_ref[...])


def _prompt_layer(x, lw, l, cs_t, tb):
    n_seq, t_len, _ = x.shape
    assert t_len % (OUT_PAIR * tb) == 0 and tb % CHUNK == 0 and tb == PROMPT_BLOCK and OUT_PAIR % 2 == 0
    n_blocks = t_len // tb
    grid = (n_seq, n_blocks // OUT_PAIR)

    def per_seq(shape):
        nd = len(shape)
        return pl.BlockSpec((None,) + shape, lambda n, t, _nd=nd: (n,) + (0,) * _nd)

    def per_layer(shape, **kw):
        nd = len(shape)
        return pl.BlockSpec((None,) + shape, lambda n, t, _nd=nd: (l,) + (0,) * _nd, **kw)

    in_specs = [
        pl.BlockSpec(memory_space=pltpu.SMEM),
        pl.BlockSpec((None, OUT_PAIR * tb, D_MODEL), lambda n, t: (n, t, 0)),
        pl.BlockSpec((None, tb, D_MODEL),
                     lambda n, t: (n, jnp.minimum(OUT_PAIR * (t + 1), n_blocks - 1), 0)),
        per_layer((1, D_MODEL)),
        per_layer((D_INT, D_MODEL), pipeline_mode=pl.Buffered(1)),
        per_layer((D_MIX, D_MODEL), pipeline_mode=pl.Buffered(1)),
        per_layer((PK_ROWS, LANE), pipeline_mode=pl.Buffered(1)),
        pl.BlockSpec((C_HD, OUT_PAIR * tb), lambda n, t: (0, t)),
    ]
    out_shape = (
        jax.ShapeDtypeStruct((n_seq, t_len, D_MODEL), F32),
        jax.ShapeDtypeStruct((n_seq, ST_ROWS, LANE), F32),
    )
    out_specs = (
        pl.BlockSpec((None, OUT_PAIR * tb, D_MODEL), lambda n, t: (n, t, 0)),
        per_seq((ST_ROWS, LANE)),
    )
    scratch = [
        pltpu.VMEM((D_INT, tb), F32),
        pltpu.VMEM((D_INT, tb), F32),
        pltpu.VMEM((tb, D_MODEL), MXU_DTYPE),
        pltpu.VMEM((D_MIX, OUT_PAIR * tb), MXU_DTYPE),
        pltpu.VMEM((2 * C_KV, CHUNK, LANE), MXU_DTYPE),
        pltpu.VMEM((CONV_W - 1, B_CONV_DIM, LANE), F32),
    ]
    return pl.pallas_call(
        functools.partial(_prompt_layer_kernel, tb=tb),
        grid=grid, in_specs=in_specs, out_specs=out_specs, out_shape=out_shape, scratch_shapes=scratch,
        compiler_params=pltpu.CompilerParams(dimension_semantics=("arbitrary", "arbitrary"),
                                             vmem_limit_bytes=VMEM_LIMIT_BYTES),
        name="prompt_layer",
    )(lw["sinks"][l], x, x, lw["norm_w"], lw["w_t"], lw["w_out"], lw["packed"], cs_t)


def _unpad_prompt_states(st):
    lead = st.shape[:2]
    cst, mst, hst, convt, kt, vt = (st[:, :, o:o + n, :] for o, n in zip(ST_OFFSETS, ST_SIZES))
    cst = cst.reshape(lead + (A_HEADS, C_EXT, LANE))
    c_even, c_odd = cst[:, :, 0::2, :A_V, :HALF], cst[:, :, 1::2, :A_V, HALF:]
    c_state = jnp.stack([c_even, c_odd], axis=3).reshape(lead + (A_HEADS, A_V, A_QK))
    n_even, n_odd = cst[:, :, 0::2, A_V, :HALF], cst[:, :, 1::2, A_V, HALF:]
    n_state = jnp.stack([n_even, n_odd], axis=3).reshape(lead + (A_HEADS, A_QK))
    conv = jnp.swapaxes(convt[:, :, :, LANE - (CONV_W - 1):], 2, 3)
    k1 = jnp.transpose(kt.reshape(lead + (C_KV, C_HD, WINDOW)), (0, 1, 4, 2, 3))
    v1 = jnp.transpose(vt.reshape(lead + (C_KV, C_HD, WINDOW)), (0, 1, 4, 2, 3))
    return (c_state, n_state, mst[:, :, :A_HEADS, 0], hst.reshape(lead + (B_HEADS, B_P, B_STATE)), conv, k1, v1)


ITEMS_IN_FLIGHT = 32


def _trace_interleaved(item_iter, depth):
    active, exhausted = [], False
    while True:
        while not exhausted and len(active) < depth:
            nxt = next(item_iter, None)
            if nxt is None:
                exhausted = True
            else:
                active.append(nxt)
        if not active:
            return
        still = []
        for gen in active:
            try:
                next(gen)
                still.append(gen)
            except StopIteration:
                pass
        active = still


def _decode_kernel(
        x_ref, nw_ref, wt_ref, wout_ref, pk_ref, cw_ref, cb_ref, cs_ref,
        c_ref, n_ref, m_ref, s_ref, cv_ref, k_ref, v_ref,
        y_ref, co_ref, no_ref, mo_ref, so_ref, cvo_ref, ko_ref, vo_ref,
        hs_scr, ut_scr, vrow_scr, xbc_scr, xt_scr, ct_scr, qa_scr, knew_scr, rep_scr, h_scr, yt_scr, yall_scr):
    l = pl.program_id(0)
    j = pl.program_id(1)
    nb = x_ref.shape[0]
    hd_a = A_QK
    gb_ref, al_ref, anw_ref, dsk_ref, bnw_ref, qnw_ref, knw_ref, sink_ref, _, _ = _unpack_params(pk_ref)

    @pl.when(j == 0)
    def _layer_start():
        @pl.when(l == 0)
        def _():
            hs_scr[...] = x_ref[...]

        xn = _rmsnorm_rows(hs_scr[...], nw_ref[...]).astype(MXU_DTYPE)
        ut_scr[...] = _dot_nt(wt_ref[...], xn)
        yt_scr[...] = jnp.zeros_like(yt_scr)

        pre = jnp.concatenate([ut_scr[OFF_GA:OFF_GA + 2 * A_HEADS, :], ut_scr[OFF_GB:OFF_GB + B_HEADS, :]],
                              axis=0) + gb_ref[...]
        spl, lsg = _softplus_terms(pre)
        ig, lf, dt = pre[:A_HEADS], lsg[A_HEADS:2 * A_HEADS], spl[2 * A_HEADS:]
        d_a = jnp.exp(dt * (-jnp.exp(al_ref[2 * A_HEADS:, :])))
        m0 = m_ref[...]
        m_new = jnp.maximum(lf + m0, ig)
        sp = jnp.exp(lf + m0 - m_new)
        sl = jnp.exp(ig - m_new)
        mo_ref[...] = m_new
        inv_den = []
        for h in range(A_HEADS):
            k_t = ut_scr[OFF_AK + h * hd_a:OFF_AK + (h + 1) * hd_a, :] * (A_QK ** -0.5)
            q_t = ut_scr[OFF_AQ + h * hd_a:OFF_AQ + (h + 1) * hd_a, :]
            n_new = sp[h:h + 1] * n_ref[h] + sl[h:h + 1] * k_t
            no_ref[h] = n_new
            den = jnp.sum(n_new * q_t, axis=0, keepdims=True)
            inv_den.append(1.0 / jnp.maximum(jnp.abs(den), jnp.exp(-m_new[h:h + 1])))

        cos, sin = cs_ref[:C_HD // 2, :], cs_ref[C_HD // 2:, :]
        kn_t = _norm_rope_t(ut_scr[OFF_CK:OFF_CK + LANE, :], knw_ref[...], cos, sin)
        knew_scr[...] = kn_t
        s_new = []
        for pb in range(C_HEADS // 2):
            q_t = _norm_rope_t(ut_scr[OFF_CQ + pb * LANE:OFF_CQ + (pb + 1) * LANE, :], qnw_ref[...], cos, sin)
            qa_scr[pb * LANE:(pb + 1) * LANE, :] = q_t
            for e in range(2):
                g = (2 * pb + e) // (C_HEADS // C_KV)
                s_new.append(jnp.sum(q_t[e * C_HD:(e + 1) * C_HD, :] * kn_t[g * C_HD:(g + 1) * C_HD, :],
                                     axis=0, keepdims=True))

        table = jnp.concatenate([sp, sl, d_a, dt] + s_new + inv_den + [jnp.zeros((LANE - N_REP, nb), F32)], axis=0)
        table_t = table.T
        for r in range(N_REP):
            rep_scr[r] = jnp.broadcast_to(table_t[:, r:r + 1], (nb, LANE))

        ubx = ut_scr[OFF_BXBC:OFF_BXBC + B_CONV_DIM, :].T
        acc = cb_ref[...] + ubx * cw_ref[CONV_W - 1:CONV_W, :]
        for t in range(CONV_W - 1):
            acc = acc + cv_ref[t] * cw_ref[t:t + 1, :]
        xbc = _silu(acc)
        xbc_scr[...] = xbc
        for t in range(CONV_W - 2):
            cvo_ref[t] = cv_ref[t + 1]
        cvo_ref[CONV_W - 2] = ubx
        xt_scr[...] = xbc[:, :B_WIDTH].T
        for g in range(B_GROUPS):
            ct_scr[g] = xbc[:, B_WIDTH + (B_GROUPS + g) * B_STATE:B_WIDTH + (B_GROUPS + g + 1) * B_STATE].T.astype(ct_scr.dtype)
        vrow_scr[...] = ut_scr[OFF_AV:OFF_AV + A_WIDTH, :].T

    base = pl.multiple_of(j * SAMPLE_BLOCK, SAMPLE_BLOCK)
    rows = pl.ds(base, SAMPLE_BLOCK)
    lane_s = lax.broadcasted_iota(jnp.int32, (C_HD, nb), 1)
    lane_k = lax.broadcasted_iota(jnp.int32, (C_HD, WINDOW), 1)
    lane_r = lane_k[:1, :]
    v_rows = vrow_scr[rows, :]
    xbc_rows = xbc_scr[rows, :]
    reps = [rep_scr[r, rows, :] for r in range(N_REP)]
    hpk = C_HEADS // C_KV
    sels = [lane_s == base + i for i in range(SAMPLE_BLOCK)]

    def col(tile, i):
        return jnp.sum(jnp.where(sels[i], tile, 0.0), axis=1, keepdims=True)

    def rep(r, i):
        return reps[r][i:i + 1, :]

    acc = {}

    ones_m = jnp.ones((nb, LANE), MXU_DTYPE)

    def cols_mxu(tile):
        stacked = jnp.concatenate([jnp.where(sels[i], tile, 0.0) for i in range(SAMPLE_BLOCK)], axis=0)
        return _dot(stacked.astype(MXU_DTYPE), ones_m)

    def mlstm_item(h, i, kc, qc):
        yield
        v_row = v_rows[i:i + 1, h * A_V:(h + 1) * A_V]
        c_new = rep(REP_SP + h, i) * c_ref[i, h] + (rep(REP_SL + h, i) * v_row) * kc
        co_ref[i, h] = c_new
        acc[("h", h)].append(jnp.sum(c_new * qc, axis=0, keepdims=True) * rep(REP_IDEN + h, i))
        if i == SAMPLE_BLOCK - 1:
            h_scr[rows, h * A_V:(h + 1) * A_V] = jnp.concatenate(acc.pop(("h", h)), axis=0)

    def ssd_item(hd, i, xc):
        g = hd // (B_HEADS // B_GROUPS)
        yield
        b_row = xbc_rows[i:i + 1, B_WIDTH + g * B_STATE:B_WIDTH + (g + 1) * B_STATE]
        h_new = rep(REP_DA + hd, i) * s_ref[i, hd] + (rep(REP_DT + hd, i) * b_row) * xc
        so_ref[i, hd] = h_new
        acc[("y", hd)].append(h_new.astype(MXU_DTYPE))
        if i < SAMPLE_BLOCK - 1:
            return
        y_all = _dot(jnp.concatenate(acc.pop(("y", hd)), axis=0), ct_scr[g])
        yield
        y_blk = jnp.zeros((B_P, nb), F32)
        for ii in range(SAMPLE_BLOCK):
            y_blk = jnp.where(sels[ii], y_all[ii * B_P:(ii + 1) * B_P, :], y_blk)
        yt_scr[hd * B_P:(hd + 1) * B_P, :] = yt_scr[hd * B_P:(hd + 1) * B_P, :] + y_blk

    def attn_item(g, i, kn_all, vn_all, qcs, sink):
        kc_new, vc_new = col(kn_all, i), col(vn_all, i)
        yield
        k_t = k_ref[i, g]
        v_t = v_ref[i, g]
        ko_ref[i, g] = jnp.where(lane_k == WINDOW - 1, kc_new, pltpu.roll(k_t, WINDOW - 1, axis=1))
        vo_ref[i, g] = jnp.where(lane_k == WINDOW - 1, vc_new, pltpu.roll(v_t, WINDOW - 1, axis=1))
        s_rows = [jnp.where(lane_r == 0, rep(REP_SNEW + g * hpk + hh, i), jnp.sum(k_t * qcs[hh], axis=0, keepdims=True))
                  for hh in range(hpk)]
        s = jnp.concatenate(s_rows, axis=0) * (C_HD ** -0.5)
        m = jnp.maximum(jnp.max(s, axis=1, keepdims=True), sink)
        yield
        m_b = jnp.broadcast_to(m, s.shape)
        yield
        ex = jnp.exp(s - m_b)
        den = jnp.sum(ex, axis=1, keepdims=True) + jnp.exp(sink - m)
        yield
        inv_b = jnp.broadcast_to(1.0 / den, s.shape)
        yield
        p = ex * inv_b
        v_eff = jnp.where(lane_k == 0, vc_new, v_t)
        o_cols = [jnp.sum(v_eff * p[hh:hh + 1, :], axis=1, keepdims=True) for hh in range(hpk)]
        yield
        for hh in range(hpk):
            hd = g * hpk + hh
            acc[("o", hd)] = jnp.where(sels[i], o_cols[hh], acc[("o", hd)])
            if i == SAMPLE_BLOCK - 1:
                r0 = B_WIDTH + hd * C_HD
                yt_scr[r0:r0 + C_HD, :] = yt_scr[r0:r0 + C_HD, :] + acc.pop(("o", hd))

    def items():
        for h in range(A_HEADS):
            k_all = ut_scr[OFF_AK + h * hd_a:OFF_AK + (h + 1) * hd_a, :] * (A_QK ** -0.5)
            q_all = ut_scr[OFF_AQ + h * hd_a:OFF_AQ + (h + 1) * hd_a, :]
            acc[("h", h)] = []
            k_cols, q_cols = cols_mxu(k_all), cols_mxu(q_all)
            for i in range(SAMPLE_BLOCK):
                yield mlstm_item(h, i, k_cols[i * hd_a:(i + 1) * hd_a, :], q_cols[i * hd_a:(i + 1) * hd_a, :])
        for hd in range(B_HEADS):
            x_cols = cols_mxu(xt_scr[hd * B_P:(hd + 1) * B_P, :])
            acc[("y", hd)] = []
            for i in range(SAMPLE_BLOCK):
                yield ssd_item(hd, i, x_cols[i * B_P:(i + 1) * B_P, :])
        for g in range(C_KV):
            kn_all = knew_scr[g * C_HD:(g + 1) * C_HD, :]
            vn_all = ut_scr[OFF_CV + g * C_HD:OFF_CV + (g + 1) * C_HD, :]
            q_cols = [cols_mxu(qa_scr[(g * hpk + hh) * C_HD:(g * hpk + hh + 1) * C_HD, :]) for hh in range(hpk)]
            sink = sink_ref[g * hpk:(g + 1) * hpk, 0:1]
            for hh in range(hpk):
                acc[("o", g * hpk + hh)] = jnp.zeros((C_HD, nb), F32)
            for i in range(SAMPLE_BLOCK):
                yield attn_item(g, i, kn_all, vn_all, [qc[i * C_HD:(i + 1) * C_HD, :] for qc in q_cols], sink)

    _trace_interleaved(items(), ITEMS_IN_FLIGHT)

    @pl.when(j == pl.num_programs(1) - 1)
    def _layer_end():
        for h in range(A_HEADS):
            h_t = h_scr[:, h * A_V:(h + 1) * A_V].T
            hn = h_t * lax.rsqrt(jnp.mean(h_t * h_t, axis=0, keepdims=True) + NORM_EPS) * anw_ref[h * A_V:(h + 1) * A_V, :]
            ao = ut_scr[OFF_AO + h * A_V:OFF_AO + (h + 1) * A_V, :]
            az = ut_scr[OFF_AZ + h * A_V:OFF_AZ + (h + 1) * A_V, :]
            yall_scr[h * A_V:(h + 1) * A_V, :] = (hn * jax.nn.sigmoid(ao) * _silu(az)).astype(yall_scr.dtype)
        gw = B_WIDTH // B_GROUPS
        for g in range(B_GROUPS):
            r = slice(g * gw, (g + 1) * gw)
            y_g = yt_scr[r, :] + dsk_ref[r, :] * xt_scr[r, :]
            gated = y_g * _silu(ut_scr[OFF_BZ + g * gw:OFF_BZ + (g + 1) * gw, :])
            inv = lax.rsqrt(jnp.mean(gated * gated, axis=0, keepdims=True) + NORM_EPS)
            yall_scr[A_WIDTH + g * gw:A_WIDTH + (g + 1) * gw, :] = (gated * inv * bnw_ref[r, :]).astype(yall_scr.dtype)
        yall_scr[A_WIDTH + B_WIDTH:, :] = \
            (yt_scr[B_WIDTH:, :] * _silu(ut_scr[OFF_CZ:OFF_CZ + C_WIDTH, :])).astype(yall_scr.dtype)
        hs_new = hs_scr[...] + _dot_tn(yall_scr[...], wout_ref[...])
        hs_scr[...] = hs_new

        @pl.when(l == pl.num_programs(0) - 1)
        def _():
            y_ref[...] = hs_new


def _decode(x, lw, cs_s, c_v, n_v, m_v, s_v, conv_v, k_v, v_v):
    depth, nb = c_v.shape[0], x.shape[0]
    assert nb == LANE, "samples sit on the 128 lanes next to lane-replicated parameters"
    grid = (depth, nb // SAMPLE_BLOCK)

    def per_layer(shape):
        nd = len(shape)
        return pl.BlockSpec((None,) + shape, lambda l, j, _nd=nd: (l,) + (0,) * _nd)

    def per_block(shape):
        nd = len(shape)
        return pl.BlockSpec((None, SAMPLE_BLOCK) + shape, lambda l, j, _nd=nd: (l, j) + (0,) * _nd)

    def weight(shape):
        nd = len(shape)
        return pl.BlockSpec((None,) + shape, lambda l, j, _nd=nd: (l,) + (0,) * _nd, pipeline_mode=pl.Buffered(1))

    fixed = lambda shape: pl.BlockSpec(shape, lambda l, j, _nd=len(shape): (0,) * _nd)
    state_specs = [
        per_block((A_HEADS, A_QK, A_V)), per_layer((A_HEADS, A_QK, nb)), per_layer((A_HEADS, nb)),
        per_block((B_HEADS, B_P, B_STATE)), per_layer((CONV_W - 1, nb, B_CONV_DIM)),
        per_block((C_KV, C_HD, WINDOW)), per_block((C_KV, C_HD, WINDOW)),
    ]
    in_specs = [
        fixed((nb, D_MODEL)),
        per_layer((1, D_MODEL)), weight((D_INT, D_MODEL)), weight((D_MIX, D_MODEL)),
        weight((PK_ROWS, LANE)), per_layer((CONV_W, B_CONV_DIM)), per_layer((1, B_CONV_DIM)),
        fixed((C_HD, LANE)),
    ] + state_specs
    states = (c_v, n_v, m_v, s_v, conv_v, k_v, v_v)
    out_shape = (jax.ShapeDtypeStruct((nb, D_MODEL), F32),) + tuple(jax.ShapeDtypeStruct(a.shape, F32) for a in states)
    scratch = [
        pltpu.VMEM((nb, D_MODEL), F32),
        pltpu.VMEM((D_INT, nb), F32),
        pltpu.VMEM((nb, A_WIDTH), F32),
        pltpu.VMEM((nb, B_CONV_DIM), F32),
        pltpu.VMEM((B_WIDTH, nb), F32),
        pltpu.VMEM((B_GROUPS, B_STATE, nb), MXU_DTYPE),
        pltpu.VMEM((C_WIDTH, nb), F32),
        pltpu.VMEM((C_KV * C_HD, nb), F32),
        pltpu.VMEM((N_REP, nb, LANE), F32),
        pltpu.VMEM((nb, A_WIDTH), F32),
        pltpu.VMEM((B_WIDTH + C_WIDTH, nb), F32),
        pltpu.VMEM((D_MIX, nb), MXU_DTYPE),
    ]
    return pl.pallas_call(
        _decode_kernel,
        grid=grid, in_specs=in_specs, out_specs=(fixed((nb, D_MODEL)),) + tuple(state_specs),
        out_shape=out_shape, scratch_shapes=scratch,
        compiler_params=pltpu.CompilerParams(dimension_semantics=("arbitrary", "arbitrary"),
                                             vmem_limit_bytes=VMEM_LIMIT_BYTES),
        name="decode",
    )(x, lw["norm_w"], lw["w_t"], lw["w_out"], lw["packed"], lw["conv_w"], lw["conv_b"], cs_s, *states)


def _prep_weights(norm_w, w_in, a_igate_b, a_fgate_b, a_norm_w, b_conv_w, b_conv_b, b_dt_bias, b_A_log, b_D,
                  b_norm_w, c_qnorm_w, c_knorm_w, c_sinks, w_out):
    depth = w_in.shape[0]
    w_t = jnp.swapaxes(w_in, 1, 2).astype(MXU_DTYPE)
    gbias = jnp.concatenate([a_igate_b, a_fgate_b, b_dt_bias], axis=-1)
    alog = jnp.concatenate([jnp.zeros((depth, 2 * A_HEADS), b_A_log.dtype), b_A_log], axis=-1)

    rows = jnp.concatenate([
        gbias, alog, a_norm_w, jnp.repeat(b_D, B_P, axis=-1), b_norm_w,
        jnp.tile(c_qnorm_w, (1, LANE // C_HD)), jnp.tile(c_knorm_w, (1, LANE // C_HD)), c_sinks,
        b_conv_b, b_conv_w.reshape(depth, CONV_W * B_CONV_DIM)], axis=-1).astype(F32)
    assert rows.shape == (depth, PK_ROWS)
    return {
        "w_t": w_t, "w_out": w_out.astype(MXU_DTYPE),
        "norm_w": norm_w.astype(F32)[:, None, :],
        "packed": jnp.broadcast_to(rows[..., None], (depth, PK_ROWS, LANE)),
        "conv_w": b_conv_w.astype(F32), "conv_b": b_conv_b.astype(F32)[:, None, :],
        "sinks": c_sinks.astype(F32),
    }


def _rope_tables_t(pos, width=None):
    half = C_HD // 2
    inv = ROPE_THETA ** (-jnp.arange(half, dtype=F32) / half)
    ang = inv[:, None] * pos.astype(F32)[None, :]
    table = jnp.concatenate([jnp.cos(ang), jnp.sin(ang)], axis=0)
    if width is not None:
        table = jnp.broadcast_to(table, (C_HD, width))
    return table


def _kernel_impl(x_prompt, x_sample, state_mlstm_C, state_mlstm_n, state_mlstm_m, state_ssm, state_conv,
                 cache_k, cache_v, norm_w, w_in, a_igate_b, a_fgate_b, a_norm_w, b_conv_w, b_conv_b,
                 b_dt_bias, b_A_log, b_D, b_norm_w, c_qnorm_w, c_knorm_w, c_sinks, w_out, *, tb):
    depth = w_in.shape[0]
    t_len = x_prompt.shape[1]
    lw = _prep_weights(norm_w, w_in, a_igate_b, a_fgate_b, a_norm_w, b_conv_w, b_conv_b, b_dt_bias, b_A_log,
                       b_D, b_norm_w, c_qnorm_w, c_knorm_w, c_sinks, w_out)
    cs_p = _rope_tables_t(jnp.arange(t_len, dtype=jnp.int32))
    hp = x_prompt
    st_prompt = []
    for l in range(depth):
        res = _prompt_layer(hp, lw, l, cs_p, tb)
        hp = res[0]
        st_prompt.append(res[1])
    p_states = _unpad_prompt_states(jnp.stack(st_prompt))

    assert x_sample.shape[1] == 1
    cs_s = _rope_tables_t(PAST_LEN + jnp.arange(1, dtype=jnp.int32), LANE)
    outs = _decode(
        x_sample[:, 0, :], lw, cs_s,
        jnp.transpose(state_mlstm_C, (0, 1, 2, 4, 3)), jnp.transpose(state_mlstm_n, (0, 2, 3, 1)),
        jnp.transpose(state_mlstm_m, (0, 2, 1)), state_ssm, jnp.transpose(state_conv, (0, 2, 1, 3)),
        jnp.transpose(cache_k, (0, 1, 3, 4, 2)), jnp.transpose(cache_v, (0, 1, 3, 4, 2)))
    hs, c_o, n_o, m_o, s_o, conv_o, k_o, v_o = outs
    s_states = (jnp.transpose(c_o, (0, 1, 2, 4, 3)), jnp.transpose(n_o, (0, 3, 1, 2)), jnp.transpose(m_o, (0, 2, 1)),
                s_o, jnp.transpose(conv_o, (0, 2, 1, 3)),
                jnp.transpose(k_o, (0, 1, 4, 2, 3)), jnp.transpose(v_o, (0, 1, 4, 2, 3)))
    return (hp, hs[:, None, :], *p_states, *s_states)


def kernel(x_prompt, x_sample, state_mlstm_C, state_mlstm_n, state_mlstm_m, state_ssm, state_conv, cache_k, cache_v, norm_w, w_in, a_igate_b, a_fgate_b, a_norm_w, b_conv_w, b_conv_b, b_dt_bias, b_A_log, b_D, b_norm_w, c_qnorm_w, c_knorm_w, c_sinks, w_out):
    return _kernel_impl(x_prompt, x_sample, state_mlstm_C, state_mlstm_n, state_mlstm_m, state_ssm, state_conv,
                        cache_k, cache_v, norm_w, w_in, a_igate_b, a_fgate_b, a_norm_w, b_conv_w, b_conv_b,
                        b_dt_bias, b_A_log, b_D, b_norm_w, c_qnorm_w, c_knorm_w, c_sinks, w_out, tb=PROMPT_BLOCK)
```

```python
import functools

import jax
import jax.numpy as jnp
import numpy as np
from jax import lax
from jax.experimental import pallas as pl
from jax.experimental.pallas import tpu as pltpu

F32 = jnp.float32
MXU_DTYPE = jnp.bfloat16

D_MODEL = 1024
A_HEADS, A_QK, A_V = 4, 64, 128
A_WIDTH = A_HEADS * A_V
B_HEADS, B_P, B_GROUPS, B_STATE = 8, 64, 2, 128
B_WIDTH = B_HEADS * B_P
CONV_W = 4
B_CONV_DIM = B_WIDTH + 2 * B_GROUPS * B_STATE
C_HEADS, C_KV, C_HD = 8, 2, 64
C_WIDTH = C_HEADS * C_HD
WINDOW = 128
ROPE_THETA = 10000.0
D_MIX = A_WIDTH + B_WIDTH + C_WIDTH
NORM_EPS = 1e-6
PAST_LEN = 8192

LANE = 128
SUBLANE = 8
HALF = LANE // 2

CHUNK = 128
PROMPT_BLOCK = 256
SAMPLE_BLOCK = SUBLANE
N_GATES = 2 * A_HEADS + B_HEADS

_SRC_SIZES = (A_HEADS * A_QK, A_HEADS * A_QK, A_WIDTH, A_WIDTH, A_WIDTH, A_HEADS, A_HEADS,
              B_WIDTH, B_CONV_DIM, B_HEADS, C_WIDTH, C_KV * C_HD, C_KV * C_HD, C_WIDTH)
(OFF_AQ, OFF_AK, OFF_AV, OFF_AO, OFF_AZ, OFF_GA, _, OFF_BZ, OFF_BXBC, OFF_GB, OFF_CQ, OFF_CK, OFF_CV, OFF_CZ,
 D_INT) = np.concatenate([[0], np.cumsum(_SRC_SIZES)]).tolist()
assert all(o % SUBLANE == 0 for o in (OFF_GA, OFF_BZ, OFF_BXBC, OFF_GB, OFF_CQ, OFF_CK, OFF_CV, OFF_CZ))
ITEM_ORDER = ("m0", "a0", "p0", "s0", "m1", "a1", "m2", "a2", "p1", "s1", "m3", "a3")
INPROJ_PIECES = sum(tok[0] == "p" for tok in ITEM_ORDER)
PROMPT_ITEMS_IN_FLIGHT = 2
_PIECES_PER_BLOCK = INPROJ_PIECES * (PROMPT_BLOCK // CHUNK)
PIECE_ROWS = -(-D_INT // (_PIECES_PER_BLOCK * 2 * SUBLANE)) * 2 * SUBLANE
LAST_PIECE_START = D_INT - PIECE_ROWS
assert LAST_PIECE_START % (2 * SUBLANE) == 0
OUT_PAIR = 2

PK_GB = 0
PK_AL = PK_GB + N_GATES
PK_ANW = PK_AL + N_GATES
PK_DSK = PK_ANW + A_WIDTH
PK_BNW = PK_DSK + B_WIDTH
PK_QNW = PK_BNW + B_WIDTH
PK_KNW = PK_QNW + LANE
PK_SINK = PK_KNW + LANE
PK_CB = PK_SINK + C_HEADS
PK_CW = PK_CB + B_CONV_DIM
PK_ROWS = PK_CW + CONV_W * B_CONV_DIM


def _unpack_params(pk_ref):
    view = lambda start, n: pk_ref.at[pl.ds(start, n)]
    return (view(PK_GB, N_GATES), view(PK_AL, N_GATES), view(PK_ANW, A_WIDTH), view(PK_DSK, B_WIDTH),
            view(PK_BNW, B_WIDTH), view(PK_QNW, LANE), view(PK_KNW, LANE), view(PK_SINK, C_HEADS),
            view(PK_CB, B_CONV_DIM), view(PK_CW, CONV_W * B_CONV_DIM))


C_EXT = A_V + 2 * SUBLANE
ST_SIZES = (A_HEADS * C_EXT, SUBLANE, B_HEADS * B_P, B_CONV_DIM, C_KV * C_HD, C_KV * C_HD)
ST_OFFSETS = tuple(int(o) for o in np.cumsum((0,) + ST_SIZES[:-1]))
ST_ROWS = sum(ST_SIZES)
VMEM_LIMIT_BYTES =56 * 1024 * 1024

REP_SP, REP_SL = 0, A_HEADS
REP_DA, REP_DT = 2 * A_HEADS, 2 * A_HEADS + B_HEADS
REP_SNEW = 2 * A_HEADS + 2 * B_HEADS
REP_IDEN = REP_SNEW + C_HEADS
N_REP = REP_IDEN + A_HEADS


def _dot(a, b):
    return jnp.dot(a, b, preferred_element_type=F32)


def _dot_nt(a, b):
    return lax.dot_general(a, b, (((1,), (1,)), ((), ())), preferred_element_type=F32)


def _dot_tn(a, b):
    return lax.dot_general(a, b, (((0,), (0,)), ((), ())), preferred_element_type=F32)


def _split3(z):
    hi = z.astype(MXU_DTYPE)
    r1 = z - hi.astype(F32)
    mid = r1.astype(MXU_DTYPE)
    lo = (r1 - mid.astype(F32)).astype(MXU_DTYPE)
    return hi, mid, lo


def _softplus_terms(x):
    t = jnp.log1p(jnp.exp(-jnp.abs(x)))
    return jnp.maximum(x, 0.0) + t, jnp.minimum(x, 0.0) - t


def _silu(x):
    return x * jax.nn.sigmoid(x)


def _rmsnorm_rows(x, w):
    return x * lax.rsqrt(jnp.mean(x * x, axis=-1, keepdims=True) + NORM_EPS) * w


def _norm_rope_t(x_t, w_rep, cos, sin):
    half = C_HD // 2
    out = []
    for hh in range(LANE // C_HD):
        xh = x_t[hh * C_HD:(hh + 1) * C_HD, :]
        ms = jnp.mean(xh * xh, axis=0, keepdims=True)
        xn = xh * lax.rsqrt(ms + NORM_EPS) * w_rep[hh * C_HD:(hh + 1) * C_HD, :]
        x1, x2 = xn[:half, :], xn[half:, :]
        out += [x1 * cos - x2 * sin, x2 * cos + x1 * sin]
    return jnp.concatenate(out, axis=0)


def _prompt_layer_kernel(
        sinks_ref,
        x_ref, xnext_ref, nw_ref, wt_ref, wout_ref, pk_ref, cs_ref,
        o_ref, st_ref,
        u0_scr, u1_scr, xn_scr, y_scr, kprev_scr, roll_scr, *, tb):
    L = CHUNK
    nch = tb // L
    t_id = pl.program_id(1)
    sub = 0
    gb_ref, al_ref, anw_ref, dsk_ref, bnw_ref, qnw_ref, knw_ref, _, cb_ref, cw_ref = _unpack_params(pk_ref)
    conv_tap = lambda j: cw_ref[j * B_CONV_DIM:(j + 1) * B_CONV_DIM, :]
    cst_ref, mst_ref, hst_ref, convt_ref, kt_ref, vt_ref = (
        st_ref.at[pl.ds(start, n)] for start, n in zip(ST_OFFSETS, ST_SIZES))

    @pl.when(t_id == 0)
    def _():
        st_ref[...] = jnp.zeros_like(st_ref)
        kprev_scr[...] = jnp.zeros_like(kprev_scr)
        roll_scr[...] = jnp.zeros_like(roll_scr)
        u0_scr[...] = _dot_nt(wt_ref[...], _rmsnorm_rows(x_ref[:tb, :], nw_ref[...]).astype(MXU_DTYPE))

    u_cur, u_next = u0_scr, u1_scr

    row = lax.broadcasted_iota(jnp.int32, (L, L), 0)
    lane = lax.broadcasted_iota(jnp.int32, (L, L), 1)
    lo = lane < HALF
    top = row < B_P
    src_le_t = row <= lane
    tri = jnp.where(lane <= row, 1.0, 0.0).astype(MXU_DTYPE)
    grow = lax.broadcasted_iota(jnp.int32, (N_GATES, L), 0)
    krow = lax.broadcasted_iota(jnp.int32, (2 * L, L), 0)
    klane = lax.broadcasted_iota(jnp.int32, (2 * L, L), 1)
    neg_inf = -jnp.inf
    a_neg = -jnp.exp(al_ref[...])

    def chunk(c, carry):
        cols = pl.ds(pl.multiple_of(c * L, L), L)
        ycols = pl.ds(pl.multiple_of(sub * tb + c * L, L), L)

        def piece_item(k):
            start = jnp.minimum((c * INPROJ_PIECES + k) * PIECE_ROWS, LAST_PIECE_START)
            wrows = pl.ds(pl.multiple_of(start, 2 * SUBLANE), PIECE_ROWS)
            u_next[wrows, :] = _dot_nt(wt_ref[wrows, :], xn_scr[...])
            return
            yield

        pre_r = jnp.concatenate([u_cur[OFF_GA:OFF_GA + 2 * A_HEADS, cols], u_cur[OFF_GB:OFF_GB + B_HEADS, cols]],
                                axis=0) + gb_ref[...]
        sp_r, ls_r = _softplus_terms(pre_r)
        z_r = jnp.where((grow >= A_HEADS) & (grow < 2 * A_HEADS), ls_r,
                        jnp.where(grow >= 2 * A_HEADS, sp_r * a_neg, 0.0))
        zp = _split3(z_r)
        cum_r = _dot_nt(zp[0], tri) + _dot_nt(zp[1], tri) + _dot_nt(zp[2], tri)
        colf = jnp.concatenate([pre_r, cum_r, jnp.zeros((L - 2 * N_GATES, L), F32)], axis=0).T

        def mlstm_head(h, q_t, k_p):
            e = h % 2
            k_m = jnp.where(lo if e == 0 else jnp.logical_not(lo), k_p, 0.0).astype(MXU_DTYPE)
            v_t = u_cur[OFF_AV + h * LANE:OFF_AV + (h + 1) * LANE, cols]
            b_row = cum_r[A_HEADS + h:A_HEADS + h + 1, :]
            i_row = pre_r[h:h + 1, :]
            c_col = colf[:, N_GATES + A_HEADS + h:N_GATES + A_HEADS + h + 1] - colf[:, h:h + 1]
            b_end = b_row[:, L - 1:L]
            m_prev = mst_ref[h:h + 1, 0:1]
            c_prev = cst_ref[h * C_EXT:(h + 1) * C_EXT, :]

            g_row = b_end - b_row + i_row
            m_loc = jnp.max(g_row, axis=1, keepdims=True)
            d_t = jnp.where(src_le_t, b_row - c_col, neg_inf)
            inter = b_row + m_prev
            m_t = jnp.maximum(inter, jnp.max(d_t, axis=0, keepdims=True))
            res = _dot(jnp.concatenate([k_m, c_prev.astype(MXU_DTYPE)], axis=0), q_t)
            w_row = jnp.exp(g_row - m_loc)
            v_ext = jnp.concatenate([v_t, jnp.ones((1, L), F32), jnp.zeros((C_EXT - A_V - 1, L), F32)], axis=0)
            c_loc = _dot((v_ext * w_row).astype(MXU_DTYPE), k_m)
            e_t = jnp.exp(d_t - m_t)
            si = jnp.exp(inter - m_t)
            yield
            s_t = e_t * res[:L, :]
            sv = _dot(v_t.astype(MXU_DTYPE), s_t.astype(MXU_DTYPE))
            yield
            num = sv + si * res[L:L + A_V, :]
            den = jnp.sum(s_t, axis=0, keepdims=True) + si * res[L + A_V:L + A_V + 1, :]
            hh = num * (1.0 / jnp.maximum(jnp.abs(den), jnp.exp(-m_t)))
            hn = hh * lax.rsqrt(jnp.mean(hh * hh, axis=0, keepdims=True) + NORM_EPS) * anw_ref[h * LANE:(h + 1) * LANE, :]
            ao = u_cur[OFF_AO + h * LANE:OFF_AO + (h + 1) * LANE, cols]
            az = u_cur[OFF_AZ + h * LANE:OFF_AZ + (h + 1) * LANE, cols]
            y_scr[h * LANE:(h + 1) * LANE, ycols] = (hn * jax.nn.sigmoid(ao) * _silu(az)).astype(y_scr.dtype)

            m_new = jnp.maximum(b_end + m_prev, m_loc)
            sp = jnp.exp(b_end + m_prev - m_new)
            sl = jnp.exp(m_loc - m_new)
            cst_ref[h * C_EXT:(h + 1) * C_EXT, :] = sp * c_prev + sl * c_loc
            mst_ref[h:h + 1, :] = jnp.broadcast_to(m_new, (1, LANE))

        def conv_silu():
            cur = u_cur[OFF_BXBC:OFF_BXBC + B_CONV_DIM, cols]
            acc = cb_ref[...] + cur * conv_tap(CONV_W - 1)
            for j in range(1, CONV_W):
                rolled = pltpu.roll(cur, j, axis=1)
                acc = acc + jnp.where(lane[:1, :] >= j, rolled, roll_scr[j - 1]) * conv_tap(CONV_W - 1 - j)
                roll_scr[j - 1] = rolled
            convt_ref[...] = cur
            return _silu(acc)

        def ssd_group(g, xbc):
            b_t = xbc[B_WIDTH + g * B_STATE:B_WIDTH + (g + 1) * B_STATE, :]
            c_t = xbc[B_WIDTH + (B_GROUPS + g) * B_STATE:B_WIDTH + (B_GROUPS + g + 1) * B_STATE, :]
            b_m = b_t.T.astype(MXU_DTYPE)
            hpg = B_HEADS // B_GROUPS
            h_prev = hst_ref[g * hpg * B_P:(g + 1) * hpg * B_P, :]
            yield
            res = _dot(jnp.concatenate([b_m, h_prev.astype(MXU_DTYPE)], axis=0), c_t.astype(MXU_DTYPE))
            x_ts, h_locs, rows_of = [], [], []
            for pj in range(hpg // 2):
                pb = g * (hpg // 2) + pj
                x_t = xbc[pb * LANE:(pb + 1) * LANE, :]
                a_ends, dec_rows, ea_rows, a_rows, a_cols, dt_rows = [], [], [], [], [], []
                for e in range(2):
                    gi = 2 * A_HEADS + 2 * pb + e
                    a_row = cum_r[gi:gi + 1, :]
                    dt_row = sp_r[gi:gi + 1, :]
                    a_end = a_row[:, L - 1:L]
                    a_rows.append(a_row)
                    a_cols.append(colf[:, N_GATES + gi:N_GATES + gi + 1])
                    dt_rows.append(dt_row)
                    a_ends.append(jnp.exp(a_end))
                    dec_rows.append(jnp.exp(a_end - a_row) * dt_row)
                    ea_rows.append(jnp.exp(a_row))
                h_locs.append(_dot((x_t * jnp.where(top, dec_rows[0], dec_rows[1])).astype(MXU_DTYPE), b_m))
                x_ts.append(x_t)
                rows_of.append((a_ends, ea_rows, a_rows, a_cols, dt_rows))
            yield
            cb_tt = res[:L, :]
            ys_all = []
            for pj in range(hpg // 2):
                a_ends, ea_rows, a_rows, a_cols, dt_rows = rows_of[pj]
                ys = []
                for e in range(2):
                    w_t = jnp.exp(jnp.where(src_le_t, a_rows[e] - a_cols[e], neg_inf)) * cb_tt
                    xdt = (x_ts[pj][e * B_P:(e + 1) * B_P, :] * dt_rows[e]).astype(MXU_DTYPE)
                    ys.append(_dot(xdt, w_t.astype(MXU_DTYPE)))
                ys_all.append(ys)
            yield
            gated = []
            for pj in range(hpg // 2):
                pb = g * (hpg // 2) + pj
                a_ends, ea_rows, a_rows, a_cols, dt_rows = rows_of[pj]
                inter = res[L + pj * LANE:L + (pj + 1) * LANE, :] * jnp.where(top, ea_rows[0], ea_rows[1])
                y_p = jnp.concatenate(ys_all[pj], axis=0) + inter + dsk_ref[pb * LANE:(pb + 1) * LANE, :] * x_ts[pj]
                hst_ref[pb * LANE:(pb + 1) * LANE, :] = \
                    jnp.where(top, a_ends[0], a_ends[1]) * h_prev[pj * LANE:(pj + 1) * LANE, :] + h_locs[pj]
                bz = u_cur[OFF_BZ + pb * LANE:OFF_BZ + (pb + 1) * LANE, cols]
                gated.append(y_p * _silu(bz))
            ms = sum(jnp.sum(gp * gp, axis=0, keepdims=True) for gp in gated) * (1.0 / (B_WIDTH // B_GROUPS))
            inv = lax.rsqrt(ms + NORM_EPS)
            for pj, gp in enumerate(gated):
                pb = g * (hpg // 2) + pj
                y_scr[A_WIDTH + pb * LANE:A_WIDTH + (pb + 1) * LANE, ycols] = \
                    (gp * inv * bnw_ref[pb * LANE:(pb + 1) * LANE, :]).astype(y_scr.dtype)

        cos = cs_ref[:C_HD // 2, ycols]
        sin = cs_ref[C_HD // 2:, ycols]
        ppg = C_HEADS // 2 // C_KV
        shared = {}

        def swa_keys():
            k_t = _norm_rope_t(u_cur[OFF_CK:OFF_CK + LANE, cols], knw_ref[...], cos, sin)
            v_t = u_cur[OFF_CV:OFF_CV + LANE, cols]
            k_p = k_t.T
            k_sw = pltpu.roll(k_p, HALF, axis=1)
            k_cur = [[jnp.where(lo, k_p, 0.0).astype(MXU_DTYPE), jnp.where(lo, 0.0, k_sw).astype(MXU_DTYPE)],
                     [jnp.where(lo, k_sw, 0.0).astype(MXU_DTYPE), jnp.where(lo, 0.0, k_p).astype(MXU_DTYPE)]]
            first = jnp.logical_and(t_id == 0, c == 0) if sub == 0 else False
            shift = jnp.where(first, 2 * L, 0)
            valid = ((krow < L) & (krow > klane + shift)) | ((krow >= L) & (krow - L <= klane))
            shared.update(k_t=k_t, v_t=v_t, k_cur=k_cur, valid=valid)

        def swa_pair(pb):
            g = pb // ppg
            k_cur, v_t, valid = shared["k_cur"], shared["v_t"], shared["valid"]
            q_t = _norm_rope_t(u_cur[OFF_CQ + pb * LANE:OFF_CQ + (pb + 1) * LANE, cols], qnw_ref[...], cos, sin)
            keys = jnp.concatenate([kprev_scr[2 * g], k_cur[g][0], kprev_scr[2 * g + 1], k_cur[g][1]], axis=0)
            sc = _dot(keys, q_t.astype(MXU_DTYPE)) * (C_HD ** -0.5)
            v_g = jnp.concatenate([vt_ref[g * C_HD:(g + 1) * C_HD, :], v_t[g * C_HD:(g + 1) * C_HD, :]],
                                  axis=1).astype(MXU_DTYPE)
            yield
            outs = []
            for e in range(2):
                s = jnp.where(valid, sc[e * 2 * L:(e + 1) * 2 * L, :], neg_inf)
                sink = sinks_ref[2 * pb + e]
                m = jnp.maximum(jnp.max(s, axis=0, keepdims=True), sink)
                ex = jnp.exp(s - m)
                p = ex * (1.0 / (jnp.sum(ex, axis=0, keepdims=True) + jnp.exp(sink - m)))
                outs.append(_dot(v_g, p.astype(MXU_DTYPE)))
            yield
            cz = u_cur[OFF_CZ + pb * LANE:OFF_CZ + (pb + 1) * LANE, cols]
            y_scr[A_WIDTH + B_WIDTH + pb * LANE:A_WIDTH + B_WIDTH + (pb + 1) * LANE, ycols] = \
                (jnp.concatenate(outs, axis=0) * _silu(cz)).astype(y_scr.dtype)

        def items():
            made = {}
            for tok in ITEM_ORDER:
                kind, idx = tok[0], int(tok[1:])
                if kind == "m":
                    pb = idx // 2
                    if ("qk", pb) not in made:
                        made[("qk", pb)] = (
                            u_cur[OFF_AQ + pb * LANE:OFF_AQ + (pb + 1) * LANE, cols].astype(MXU_DTYPE),
                            (u_cur[OFF_AK + pb * LANE:OFF_AK + (pb + 1) * LANE, cols] * (A_QK ** -0.5)).T)
                    yield mlstm_head(idx, *made[("qk", pb)])
                elif kind == "s":
                    if "xbc" not in made:
                        made["xbc"] = conv_silu()
                    yield ssd_group(idx, made["xbc"])
                elif kind == "a":
                    if "keys" not in made:
                        made["keys"] = swa_keys()
                    yield swa_pair(idx)
                else:
                    yield piece_item(idx)

        _trace_interleaved(items(), PROMPT_ITEMS_IN_FLIGHT)
        for g in range(C_KV):
            for e in range(2):
                kprev_scr[2 * g + e] = shared["k_cur"][g][e]
        kt_ref[...] = shared["k_t"]
        vt_ref[...] = shared["v_t"]
        return carry

    u_bufs = (u0_scr, u1_scr)
    for sub in range(OUT_PAIR):
        x_after = x_ref[(sub + 1) * tb:(sub + 2) * tb, :] if sub + 1 < OUT_PAIR else xnext_ref[...]
        xn_scr[...] = _rmsnorm_rows(x_after, nw_ref[...]).astype(MXU_DTYPE)
        u_cur, u_next = u_bufs[sub % 2], u_bufs[(sub + 1) % 2]
        lax.fori_loop(0, nch, lambda c, carry: chunk(c, carry), 0)

    o_ref[...] = x_ref[...] + _dot_tn(y_scr[...], wout_ref[...])


def _prompt_layer(x, lw, l, cs_t, tb):
    n_seq, t_len, _ = x.shape
    assert t_len % (OUT_PAIR * tb) == 0 and tb % CHUNK == 0 and tb == PROMPT_BLOCK and OUT_PAIR % 2 == 0
    n_blocks = t_len // tb
    grid = (n_seq, n_blocks // OUT_PAIR)

    def per_seq(shape):
        nd = len(shape)
        return pl.BlockSpec((None,) + shape, lambda n, t, _nd=nd: (n,) + (0,) * _nd)

    def per_layer(shape, **kw):
        nd = len(shape)
        return pl.BlockSpec((None,) + shape, lambda n, t, _nd=nd: (l,) + (0,) * _nd, **kw)

    in_specs = [
        pl.BlockSpec(memory_space=pltpu.SMEM),
        pl.BlockSpec((None, OUT_PAIR * tb, D_MODEL), lambda n, t: (n, t, 0)),
        pl.BlockSpec((None, tb, D_MODEL),
                     lambda n, t: (n, jnp.minimum(OUT_PAIR * (t + 1), n_blocks - 1), 0)),
        per_layer((1, D_MODEL)),
        per_layer((D_INT, D_MODEL), pipeline_mode=pl.Buffered(1)),
        per_layer((D_MIX, D_MODEL), pipeline_mode=pl.Buffered(1)),
        per_layer((PK_ROWS, LANE), pipeline_mode=pl.Buffered(1)),
        pl.BlockSpec((C_HD, OUT_PAIR * tb), lambda n, t: (0, t)),
    ]
    out_shape = (
        jax.ShapeDtypeStruct((n_seq, t_len, D_MODEL), F32),
        jax.ShapeDtypeStruct((n_seq, ST_ROWS, LANE), F32),
    )
    out_specs = (
        pl.BlockSpec((None, OUT_PAIR * tb, D_MODEL), lambda n, t: (n, t, 0)),
        per_seq((ST_ROWS, LANE)),
    )
    scratch = [
        pltpu.VMEM((D_INT, tb), F32),
        pltpu.VMEM((D_INT, tb), F32),
        pltpu.VMEM((tb, D_MODEL), MXU_DTYPE),
        pltpu.VMEM((D_MIX, OUT_PAIR * tb), MXU_DTYPE),
        pltpu.VMEM((2 * C_KV, CHUNK, LANE), MXU_DTYPE),
        pltpu.VMEM((CONV_W - 1, B_CONV_DIM, LANE), F32),
    ]
    return pl.pallas_call(
        functools.partial(_prompt_layer_kernel, tb=tb),
        grid=grid, in_specs=in_specs, out_specs=out_specs, out_shape=out_shape, scratch_shapes=scratch,
        compiler_params=pltpu.CompilerParams(dimension_semantics=("arbitrary", "arbitrary"),
                                             vmem_limit_bytes=VMEM_LIMIT_BYTES),
        name="prompt_layer",
    )(lw["sinks"][l], x, x, lw["norm_w"], lw["w_t"], lw["w_out"], lw["packed"], cs_t)


def _unpad_prompt_states(st):
    lead = st.shape[:2]
    cst, mst, hst, convt, kt, vt = (st[:, :, o:o + n, :] for o, n in zip(ST_OFFSETS, ST_SIZES))
    cst = cst.reshape(lead + (A_HEADS, C_EXT, LANE))
    c_even, c_odd = cst[:, :, 0::2, :A_V, :HALF], cst[:, :, 1::2, :A_V, HALF:]
    c_state = jnp.stack([c_even, c_odd], axis=3).reshape(lead + (A_HEADS, A_V, A_QK))
    n_even, n_odd = cst[:, :, 0::2, A_V, :HALF], cst[:, :, 1::2, A_V, HALF:]
    n_state = jnp.stack([n_even, n_odd], axis=3).reshape(lead + (A_HEADS, A_QK))
    conv = jnp.swapaxes(convt[:, :, :, LANE - (CONV_W - 1):], 2, 3)
    k1 = jnp.transpose(kt.reshape(lead + (C_KV, C_HD, WINDOW)), (0, 1, 4, 2, 3))
    v1 = jnp.transpose(vt.reshape(lead + (C_KV, C_HD, WINDOW)), (0, 1, 4, 2, 3))
    return (c_state, n_state, mst[:, :, :A_HEADS, 0], hst.reshape(lead + (B_HEADS, B_P, B_STATE)), conv, k1, v1)


ITEMS_IN_FLIGHT = 32


def _trace_interleaved(item_iter, depth):
    active, exhausted = [], False
    while True:
        while not exhausted and len(active) < depth:
            nxt = next(item_iter, None)
            if nxt is None:
                exhausted = True
            else:
                active.append(nxt)
        if not active:
            return
        still = []
        for gen in active:
            try:
                next(gen)
                still.append(gen)
            except StopIteration:
                pass
        active = still


def _decode_kernel(
        x_ref, nw_ref, wt_ref, wout_ref, pk_ref, cw_ref, cb_ref, cs_ref,
        c_ref, n_ref, m_ref, s_ref, cv_ref, k_ref, v_ref,
        y_ref, co_ref, no_ref, mo_ref, so_ref, cvo_ref, ko_ref, vo_ref,
        hs_scr, ut_scr, vrow_scr, xbc_scr, xt_scr, ct_scr, qa_scr, knew_scr, rep_scr, h_scr, yt_scr, yall_scr):
    l = pl.program_id(0)
    j = pl.program_id(1)
    nb = x_ref.shape[0]
    hd_a = A_QK
    gb_ref, al_ref, anw_ref, dsk_ref, bnw_ref, qnw_ref, knw_ref, sink_ref, _, _ = _unpack_params(pk_ref)

    @pl.when(j == 0)
    def _layer_start():
        @pl.when(l == 0)
        def _():
            hs_scr[...] = x_ref[...]

        xn = _rmsnorm_rows(hs_scr[...], nw_ref[...]).astype(MXU_DTYPE)
        ut_scr[...] = _dot_nt(wt_ref[...], xn)
        yt_scr[...] = jnp.zeros_like(yt_scr)

        pre = jnp.concatenate([ut_scr[OFF_GA:OFF_GA + 2 * A_HEADS, :], ut_scr[OFF_GB:OFF_GB + B_HEADS, :]],
                              axis=0) + gb_ref[...]
        spl, lsg = _softplus_terms(pre)
        ig, lf, dt = pre[:A_HEADS], lsg[A_HEADS:2 * A_HEADS], spl[2 * A_HEADS:]
        d_a = jnp.exp(dt * (-jnp.exp(al_ref[2 * A_HEADS:, :])))
        m0 = m_ref[...]
        m_new = jnp.maximum(lf + m0, ig)
        sp = jnp.exp(lf + m0 - m_new)
        sl = jnp.exp(ig - m_new)
        mo_ref[...] = m_new
        inv_den = []
        for h in range(A_HEADS):
            k_t = ut_scr[OFF_AK + h * hd_a:OFF_AK + (h + 1) * hd_a, :] * (A_QK ** -0.5)
            q_t = ut_scr[OFF_AQ + h * hd_a:OFF_AQ + (h + 1) * hd_a, :]
            n_new = sp[h:h + 1] * n_ref[h] + sl[h:h + 1] * k_t
            no_ref[h] = n_new
            den = jnp.sum(n_new * q_t, axis=0, keepdims=True)
            inv_den.append(1.0 / jnp.maximum(jnp.abs(den), jnp.exp(-m_new[h:h + 1])))

        cos, sin = cs_ref[:C_HD // 2, :], cs_ref[C_HD // 2:, :]
        kn_t = _norm_rope_t(ut_scr[OFF_CK:OFF_CK + LANE, :], knw_ref[...], cos, sin)
        knew_scr[...] = kn_t
        s_new = []
        for pb in range(C_HEADS // 2):
            q_t = _norm_rope_t(ut_scr[OFF_CQ + pb * LANE:OFF_CQ + (pb + 1) * LANE, :], qnw_ref[...], cos, sin)
            qa_scr[pb * LANE:(pb + 1) * LANE, :] = q_t
            for e in range(2):
                g = (2 * pb + e) // (C_HEADS // C_KV)
                s_new.append(jnp.sum(q_t[e * C_HD:(e + 1) * C_HD, :] * kn_t[g * C_HD:(g + 1) * C_HD, :],
                                     axis=0, keepdims=True))

        table = jnp.concatenate([sp, sl, d_a, dt] + s_new + inv_den + [jnp.zeros((LANE - N_REP, nb), F32)], axis=0)
        table_t = table.T
        for r in range(N_REP):
            rep_scr[r] = jnp.broadcast_to(table_t[:, r:r + 1], (nb, LANE))

        ubx = ut_scr[OFF_BXBC:OFF_BXBC + B_CONV_DIM, :].T
        acc = cb_ref[...] + ubx * cw_ref[CONV_W - 1:CONV_W, :]
        for t in range(CONV_W - 1):
            acc = acc + cv_ref[t] * cw_ref[t:t + 1, :]
        xbc = _silu(acc)
        xbc_scr[...] = xbc
        for t in range(CONV_W - 2):
            cvo_ref[t] = cv_ref[t + 1]
        cvo_ref[CONV_W - 2] = ubx
        xt_scr[...] = xbc[:, :B_WIDTH].T
        for g in range(B_GROUPS):
            ct_scr[g] = xbc[:, B_WIDTH + (B_GROUPS + g) * B_STATE:B_WIDTH + (B_GROUPS + g + 1) * B_STATE].T.astype(ct_scr.dtype)
        vrow_scr[...] = ut_scr[OFF_AV:OFF_AV + A_WIDTH, :].T

    base = pl.multiple_of(j * SAMPLE_BLOCK, SAMPLE_BLOCK)
    rows = pl.ds(base, SAMPLE_BLOCK)
    lane_s = lax.broadcasted_iota(jnp.int32, (C_HD, nb), 1)
    lane_k = lax.broadcasted_iota(jnp.int32, (C_HD, WINDOW), 1)
    lane_r = lane_k[:1, :]
    v_rows = vrow_scr[rows, :]
    xbc_rows = xbc_scr[rows, :]
    reps = [rep_scr[r, rows, :] for r in range(N_REP)]
    hpk = C_HEADS // C_KV
    sels = [lane_s == base + i for i in range(SAMPLE_BLOCK)]

    def col(tile, i):
        return jnp.sum(jnp.where(sels[i], tile, 0.0), axis=1, keepdims=True)

    def rep(r, i):
        return reps[r][i:i + 1, :]

    acc = {}

    ones_m = jnp.ones((nb, LANE), MXU_DTYPE)

    def cols_mxu(tile):
        stacked = jnp.concatenate([jnp.where(sels[i], tile, 0.0) for i in range(SAMPLE_BLOCK)], axis=0)
        return _dot(stacked.astype(MXU_DTYPE), ones_m)

    def mlstm_item(h, i, kc, qc):
        yield
        v_row = v_rows[i:i + 1, h * A_V:(h + 1) * A_V]
        c_new = rep(REP_SP + h, i) * c_ref[i, h] + (rep(REP_SL + h, i) * v_row) * kc
        co_ref[i, h] = c_new
        acc[("h", h)].append(jnp.sum(c_new * qc, axis=0, keepdims=True) * rep(REP_IDEN + h, i))
        if i == SAMPLE_BLOCK - 1:
            h_scr[rows, h * A_V:(h + 1) * A_V] = jnp.concatenate(acc.pop(("h", h)), axis=0)

    def ssd_item(hd, i, xc):
        g = hd // (B_HEADS // B_GROUPS)
        yield
        b_row = xbc_rows[i:i + 1, B_WIDTH + g * B_STATE:B_WIDTH + (g + 1) * B_STATE]
        h_new = rep(REP_DA + hd, i) * s_ref[i, hd] + (rep(REP_DT + hd, i) * b_row) * xc
        so_ref[i, hd] = h_new
        acc[("y", hd)].append(h_new.astype(MXU_DTYPE))
        if i < SAMPLE_BLOCK - 1:
            return
        y_all = _dot(jnp.concatenate(acc.pop(("y", hd)), axis=0), ct_scr[g])
        yield
        y_blk = jnp.zeros((B_P, nb), F32)
        for ii in range(SAMPLE_BLOCK):
            y_blk = jnp.where(sels[ii], y_all[ii * B_P:(ii + 1) * B_P, :], y_blk)
        yt_scr[hd * B_P:(hd + 1) * B_P, :] = yt_scr[hd * B_P:(hd + 1) * B_P, :] + y_blk

    def attn_item(g, i, kn_all, vn_all, qcs, sink):
        kc_new, vc_new = col(kn_all, i), col(vn_all, i)
        yield
        k_t = k_ref[i, g]
        v_t = v_ref[i, g]
        ko_ref[i, g] = jnp.where(lane_k == WINDOW - 1, kc_new, pltpu.roll(k_t, WINDOW - 1, axis=1))
        vo_ref[i, g] = jnp.where(lane_k == WINDOW - 1, vc_new, pltpu.roll(v_t, WINDOW - 1, axis=1))
        s_rows = [jnp.where(lane_r == 0, rep(REP_SNEW + g * hpk + hh, i), jnp.sum(k_t * qcs[hh], axis=0, keepdims=True))
                  for hh in range(hpk)]
        s = jnp.concatenate(s_rows, axis=0) * (C_HD ** -0.5)
        m = jnp.maximum(jnp.max(s, axis=1, keepdims=True), sink)
        yield
        m_b = jnp.broadcast_to(m, s.shape)
        yield
        ex = jnp.exp(s - m_b)
        den = jnp.sum(ex, axis=1, keepdims=True) + jnp.exp(sink - m)
        yield
        inv_b = jnp.broadcast_to(1.0 / den, s.shape)
        yield
        p = ex * inv_b
        v_eff = jnp.where(lane_k == 0, vc_new, v_t)
        o_cols = [jnp.sum(v_eff * p[hh:hh + 1, :], axis=1, keepdims=True) for hh in range(hpk)]
        yield
        for hh in range(hpk):
            hd = g * hpk + hh
            acc[("o", hd)] = jnp.where(sels[i], o_cols[hh], acc[("o", hd)])
            if i == SAMPLE_BLOCK - 1:
                r0 = B_WIDTH + hd * C_HD
                yt_scr[r0:r0 + C_HD, :] = yt_scr[r0:r0 + C_HD, :] + acc.pop(("o", hd))

    def items():
        for h in range(A_HEADS):
            k_all = ut_scr[OFF_AK + h * hd_a:OFF_AK + (h + 1) * hd_a, :] * (A_QK ** -0.5)
            q_all = ut_scr[OFF_AQ + h * hd_a:OFF_AQ + (h + 1) * hd_a, :]
            acc[("h", h)] = []
            k_cols, q_cols = cols_mxu(k_all), cols_mxu(q_all)
            for i in range(SAMPLE_BLOCK):
                yield mlstm_item(h, i, k_cols[i * hd_a:(i + 1) * hd_a, :], q_cols[i * hd_a:(i + 1) * hd_a, :])
        for hd in range(B_HEADS):
            x_cols = cols_mxu(xt_scr[hd * B_P:(hd + 1) * B_P, :])
            acc[("y", hd)] = []
            for i in range(SAMPLE_BLOCK):
                yield ssd_item(hd, i, x_cols[i * B_P:(i + 1) * B_P, :])
        for g in range(C_KV):
            kn_all = knew_scr[g * C_HD:(g + 1) * C_HD, :]
            vn_all = ut_scr[OFF_CV + g * C_HD:OFF_CV + (g + 1) * C_HD, :]
            q_cols = [cols_mxu(qa_scr[(g * hpk + hh) * C_HD:(g * hpk + hh + 1) * C_HD, :]) for hh in range(hpk)]
            sink = sink_ref[g * hpk:(g + 1) * hpk, 0:1]
            for hh in range(hpk):
                acc[("o", g * hpk + hh)] = jnp.zeros((C_HD, nb), F32)
            for i in range(SAMPLE_BLOCK):
                yield attn_item(g, i, kn_all, vn_all, [qc[i * C_HD:(i + 1) * C_HD, :] for qc in q_cols], sink)

    _trace_interleaved(items(), ITEMS_IN_FLIGHT)

    @pl.when(j == pl.num_programs(1) - 1)
    def _layer_end():
        for h in range(A_HEADS):
            h_t = h_scr[:, h * A_V:(h + 1) * A_V].T
            hn = h_t * lax.rsqrt(jnp.mean(h_t * h_t, axis=0, keepdims=True) + NORM_EPS) * anw_ref[h * A_V:(h + 1) * A_V, :]
            ao = ut_scr[OFF_AO + h * A_V:OFF_AO + (h + 1) * A_V, :]
            az = ut_scr[OFF_AZ + h * A_V:OFF_AZ + (h + 1) * A_V, :]
            yall_scr[h * A_V:(h + 1) * A_V, :] = (hn * jax.nn.sigmoid(ao) * _silu(az)).astype(yall_scr.dtype)
        gw = B_WIDTH // B_GROUPS
        for g in range(B_GROUPS):
            r = slice(g * gw, (g + 1) * gw)
            y_g = yt_scr[r, :] + dsk_ref[r, :] * xt_scr[r, :]
            gated = y_g * _silu(ut_scr[OFF_BZ + g * gw:OFF_BZ + (g + 1) * gw, :])
            inv = lax.rsqrt(jnp.mean(gated * gated, axis=0, keepdims=True) + NORM_EPS)
            yall_scr[A_WIDTH + g * gw:A_WIDTH + (g + 1) * gw, :] = (gated * inv * bnw_ref[r, :]).astype(yall_scr.dtype)
        yall_scr[A_WIDTH + B_WIDTH:, :] = \
            (yt_scr[B_WIDTH:, :] * _silu(ut_scr[OFF_CZ:OFF_CZ + C_WIDTH, :])).astype(yall_scr.dtype)
        hs_new = hs_scr[...] + _dot_tn(yall_scr[...], wout_ref[...])
        hs_scr[...] = hs_new

        @pl.when(l == pl.num_programs(0) - 1)
        def _():
            y_ref[...] = hs_new


def _decode(x, lw, cs_s, c_v, n_v, m_v, s_v, conv_v, k_v, v_v):
    depth, nb = c_v.shape[0], x.shape[0]
    assert nb == LANE, "samples sit on the 128 lanes next to lane-replicated parameters"
    grid = (depth, nb // SAMPLE_BLOCK)

    def per_layer(shape):
        nd = len(shape)
        return pl.BlockSpec((None,) + shape, lambda l, j, _nd=nd: (l,) + (0,) * _nd)

    def per_block(shape):
        nd = len(shape)
        return pl.BlockSpec((None, SAMPLE_BLOCK) + shape, lambda l, j, _nd=nd: (l, j) + (0,) * _nd)

    def weight(shape):
        nd = len(shape)
        return pl.BlockSpec((None,) + shape, lambda l, j, _nd=nd: (l,) + (0,) * _nd, pipeline_mode=pl.Buffered(1))

    fixed = lambda shape: pl.BlockSpec(shape, lambda l, j, _nd=len(shape): (0,) * _nd)
    state_specs = [
        per_block((A_HEADS, A_QK, A_V)), per_layer((A_HEADS, A_QK, nb)), per_layer((A_HEADS, nb)),
        per_block((B_HEADS, B_P, B_STATE)), per_layer((CONV_W - 1, nb, B_CONV_DIM)),
        per_block((C_KV, C_HD, WINDOW)), per_block((C_KV, C_HD, WINDOW)),
    ]
    in_specs = [
        fixed((nb, D_MODEL)),
        per_layer((1, D_MODEL)), weight((D_INT, D_MODEL)), weight((D_MIX, D_MODEL)),
        weight((PK_ROWS, LANE)), per_layer((CONV_W, B_CONV_DIM)), per_layer((1, B_CONV_DIM)),
        fixed((C_HD, LANE)),
    ] + state_specs
    states = (c_v, n_v, m_v, s_v, conv_v, k_v, v_v)
    out_shape = (jax.ShapeDtypeStruct((nb, D_MODEL), F32),) + tuple(jax.ShapeDtypeStruct(a.shape, F32) for a in states)
    scratch = [
        pltpu.VMEM((nb, D_MODEL), F32),
        pltpu.VMEM((D_INT, nb), F32),
        pltpu.VMEM((nb, A_WIDTH), F32),
        pltpu.VMEM((nb, B_CONV_DIM), F32),
        pltpu.VMEM((B_WIDTH, nb), F32),
        pltpu.VMEM((B_GROUPS, B_STATE, nb), MXU_DTYPE),
        pltpu.VMEM((C_WIDTH, nb), F32),
        pltpu.VMEM((C_KV * C_HD, nb), F32),
        pltpu.VMEM((N_REP, nb, LANE), F32),
        pltpu.VMEM((nb, A_WIDTH), F32),
        pltpu.VMEM((B_WIDTH + C_WIDTH, nb), F32),
        pltpu.VMEM((D_MIX, nb), MXU_DTYPE),
    ]
    return pl.pallas_call(
        _decode_kernel,
        grid=grid, in_specs=in_specs, out_specs=(fixed((nb, D_MODEL)),) + tuple(state_specs),
        out_shape=out_shape, scratch_shapes=scratch,
        compiler_params=pltpu.CompilerParams(dimension_semantics=("arbitrary", "arbitrary"),
                                             vmem_limit_bytes=VMEM_LIMIT_BYTES),
        name="decode",
    )(x, lw["norm_w"], lw["w_t"], lw["w_out"], lw["packed"], lw["conv_w"], lw["conv_b"], cs_s, *states)


def _prep_weights(norm_w, w_in, a_igate_b, a_fgate_b, a_norm_w, b_conv_w, b_conv_b, b_dt_bias, b_A_log, b_D,
                  b_norm_w, c_qnorm_w, c_knorm_w, c_sinks, w_out):
    depth = w_in.shape[0]
    w_t = jnp.swapaxes(w_in, 1, 2).astype(MXU_DTYPE)
    gbias = jnp.concatenate([a_igate_b, a_fgate_b, b_dt_bias], axis=-1)
    alog = jnp.concatenate([jnp.zeros((depth, 2 * A_HEADS), b_A_log.dtype), b_A_log], axis=-1)

    rows = jnp.concatenate([
        gbias, alog, a_norm_w, jnp.repeat(b_D, B_P, axis=-1), b_norm_w,
        jnp.tile(c_qnorm_w, (1, LANE // C_HD)), jnp.tile(c_knorm_w, (1, LANE // C_HD)), c_sinks,
        b_conv_b, b_conv_w.reshape(depth, CONV_W * B_CONV_DIM)], axis=-1).astype(F32)
    assert rows.shape == (depth, PK_ROWS)
    return {
        "w_t": w_t, "w_out": w_out.astype(MXU_DTYPE),
        "norm_w": norm_w.astype(F32)[:, None, :],
        "packed": jnp.broadcast_to(rows[..., None], (depth, PK_ROWS, LANE)),
        "conv_w": b_conv_w.astype(F32), "conv_b": b_conv_b.astype(F32)[:, None, :],
        "sinks": c_sinks.astype(F32),
    }


def _rope_tables_t(pos, width=None):
    half = C_HD // 2
    inv = ROPE_THETA ** (-jnp.arange(half, dtype=F32) / half)
    ang = inv[:, None] * pos.astype(F32)[None, :]
    table = jnp.concatenate([jnp.cos(ang), jnp.sin(ang)], axis=0)
    if width is not None:
        table = jnp.broadcast_to(table, (C_HD, width))
    return table


def _kernel_impl(x_prompt, x_sample, state_mlstm_C, state_mlstm_n, state_mlstm_m, state_ssm, state_conv,
                 cache_k, cache_v, norm_w, w_in, a_igate_b, a_fgate_b, a_norm_w, b_conv_w, b_conv_b,
                 b_dt_bias, b_A_log, b_D, b_norm_w, c_qnorm_w, c_knorm_w, c_sinks, w_out, *, tb):
    depth = w_in.shape[0]
    t_len = x_prompt.shape[1]
    lw = _prep_weights(norm_w, w_in, a_igate_b, a_fgate_b, a_norm_w, b_conv_w, b_conv_b, b_dt_bias, b_A_log,
                       b_D, b_norm_w, c_qnorm_w, c_knorm_w, c_sinks, w_out)
    cs_p = _rope_tables_t(jnp.arange(t_len, dtype=jnp.int32))
    hp = x_prompt
    st_prompt = []
    for l in range(depth):
        res = _prompt_layer(hp, lw, l, cs_p, tb)
        hp = res[0]
        st_prompt.append(res[1])
    p_states = _unpad_prompt_states(jnp.stack(st_prompt))

    assert x_sample.shape[1] == 1
    cs_s = _rope_tables_t(PAST_LEN + jnp.arange(1, dtype=jnp.int32), LANE)
    outs = _decode(
        x_sample[:, 0, :], lw, cs_s,
        jnp.transpose(state_mlstm_C, (0, 1, 2, 4, 3)), jnp.transpose(state_mlstm_n, (0, 2, 3, 1)),
        jnp.transpose(state_mlstm_m, (0, 2, 1)), state_ssm, jnp.transpose(state_conv, (0, 2, 1, 3)),
        jnp.transpose(cache_k, (0, 1, 3, 4, 2)), jnp.transpose(cache_v, (0, 1, 3, 4, 2)))
    hs, c_o, n_o, m_o, s_o, conv_o, k_o, v_o = outs
    s_states = (jnp.transpose(c_o, (0, 1, 2, 4, 3)), jnp.transpose(n_o, (0, 3, 1, 2)), jnp.transpose(m_o, (0, 2, 1)),
                s_o, jnp.transpose(conv_o, (0, 2, 1, 3)),
                jnp.transpose(k_o, (0, 1, 4, 2, 3)), jnp.transpose(v_o, (0, 1, 4, 2, 3)))
    return (hp, hs[:, None, :], *p_states, *s_states)


def kernel(x_prompt, x_sample, state_mlstm_C, state_mlstm_n, state_mlstm_m, state_ssm, state_conv, cache_k, cache_v, norm_w, w_in, a_igate_b, a_fgate_b, a_norm_w, b_conv_w, b_conv_b, b_dt_bias, b_A_log, b_D, b_norm_w, c_qnorm_w, c_knorm_w, c_sinks, w_out):
    return _kernel_impl(x_prompt, x_sample, state_mlstm_C, state_mlstm_n, state_mlstm_m, state_ssm, state_conv,
                        cache_k, cache_v, norm_w, w_in, a_igate_b, a_fgate_b, a_norm_w, b_conv_w, b_conv_b,
                        b_dt_bias, b_A_log, b_D, b_norm_w, c_qnorm_w, c_knorm_w, c_sinks, w_out, tb=PROMPT_BLOCK)
```

```python
import functools

import jax
import jax.numpy as jnp
import numpy as np
from jax import lax
from jax.experimental import pallas as pl
from jax.experimental.pallas import tpu as pltpu

F32 = jnp.float32
MXU_DTYPE = jnp.bfloat16

D_MODEL = 1024
A_HEADS, A_QK, A_V = 4, 64, 128
A_WIDTH = A_HEADS * A_V
B_HEADS, B_P, B_GROUPS, B_STATE = 8, 64, 2, 128
B_WIDTH = B_HEADS * B_P
CONV_W = 4
B_CONV_DIM = B_WIDTH + 2 * B_GROUPS * B_STATE
C_HEADS, C_KV, C_HD = 8, 2, 64
C_WIDTH = C_HEADS * C_HD
WINDOW = 128
ROPE_THETA = 10000.0
D_MIX = A_WIDTH + B_WIDTH + C_WIDTH
NORM_EPS = 1e-6
PAST_LEN = 8192

LANE = 128
SUBLANE = 8
HALF = LANE // 2

CHUNK = 128
PROMPT_BLOCK = 256
SAMPLE_BLOCK = SUBLANE
N_GATES = 2 * A_HEADS + B_HEADS

_SRC_SIZES = (A_HEADS * A_QK, A_HEADS * A_QK, A_WIDTH, A_WIDTH, A_WIDTH, A_HEADS, A_HEADS,
              B_WIDTH, B_CONV_DIM, B_HEADS, C_WIDTH, C_KV * C_HD, C_KV * C_HD, C_WIDTH)
(OFF_AQ, OFF_AK, OFF_AV, OFF_AO, OFF_AZ, OFF_GA, _, OFF_BZ, OFF_BXBC, OFF_GB, OFF_CQ, OFF_CK, OFF_CV, OFF_CZ,
 D_INT) = np.concatenate([[0], np.cumsum(_SRC_SIZES)]).tolist()
assert all(o % SUBLANE == 0 for o in (OFF_GA, OFF_BZ, OFF_BXBC, OFF_GB, OFF_CQ, OFF_CK, OFF_CV, OFF_CZ))
ITEM_ORDER = ("m0", "a0", "p0", "s0", "m1", "a1", "m2", "a2", "p1", "s1", "m3", "a3")
INPROJ_PIECES = sum(tok[0] == "p" for tok in ITEM_ORDER)
PROMPT_ITEMS_IN_FLIGHT = 2
_PIECES_PER_BLOCK = INPROJ_PIECES * (PROMPT_BLOCK // CHUNK)
PIECE_ROWS = -(-D_INT // (_PIECES_PER_BLOCK * 2 * SUBLANE)) * 2 * SUBLANE
LAST_PIECE_START = D_INT - PIECE_ROWS
assert LAST_PIECE_START % (2 * SUBLANE) == 0
OUT_PAIR = 2

PK_GB = 0
PK_AL = PK_GB + N_GATES
PK_ANW = PK_AL + N_GATES
PK_DSK = PK_ANW + A_WIDTH
PK_BNW = PK_DSK + B_WIDTH
PK_QNW = PK_BNW + B_WIDTH
PK_KNW = PK_QNW + LANE
PK_SINK = PK_KNW + LANE
PK_CB = PK_SINK + C_HEADS
PK_CW = PK_CB + B_CONV_DIM
PK_ROWS = PK_CW + CONV_W * B_CONV_DIM


def _unpack_params(pk_ref):
    view = lambda start, n: pk_ref.at[pl.ds(start, n)]
    return (view(PK_GB, N_GATES), view(PK_AL, N_GATES), view(PK_ANW, A_WIDTH), view(PK_DSK, B_WIDTH),
            view(PK_BNW, B_WIDTH), view(PK_QNW, LANE), view(PK_KNW, LANE), view(PK_SINK, C_HEADS),
            view(PK_CB, B_CONV_DIM), view(PK_CW, CONV_W * B_CONV_DIM))


C_EXT = A_V + 2 * SUBLANE
ST_SIZES = (A_HEADS * C_EXT, SUBLANE, B_HEADS * B_P, B_CONV_DIM, C_KV * C_HD, C_KV * C_HD)
ST_OFFSETS = tuple(int(o) for o in np.cumsum((0,) + ST_SIZES[:-1]))
ST_ROWS = sum(ST_SIZES)
VMEM_LIMIT_BYTES =56 * 1024 * 1024

REP_SP, REP_SL = 0, A_HEADS
REP_DA, REP_DT = 2 * A_HEADS, 2 * A_HEADS + B_HEADS
REP_SNEW = 2 * A_HEADS + 2 * B_HEADS
REP_IDEN = REP_SNEW + C_HEADS
N_REP = REP_IDEN + A_HEADS


def _dot(a, b):
    return jnp.dot(a, b, preferred_element_type=F32)


def _dot_nt(a, b):
    return lax.dot_general(a, b, (((1,), (1,)), ((), ())), preferred_element_type=F32)


def _dot_tn(a, b):
    return lax.dot_general(a, b, (((0,), (0,)), ((), ())), preferred_element_type=F32)


def _split3(z):
    hi = z.astype(MXU_DTYPE)
    r1 = z - hi.astype(F32)
    mid = r1.astype(MXU_DTYPE)
    lo = (r1 - mid.astype(F32)).astype(MXU_DTYPE)
    return hi, mid, lo


def _softplus_terms(x):
    t = jnp.log1p(jnp.exp(-jnp.abs(x)))
    return jnp.maximum(x, 0.0) + t, jnp.minimum(x, 0.0) - t


def _silu(x):
    return x * jax.nn.sigmoid(x)


def _rmsnorm_rows(x, w):
    return x * lax.rsqrt(jnp.mean(x * x, axis=-1, keepdims=True) + NORM_EPS) * w


def _norm_rope_t(x_t, w_rep, cos, sin):
    half = C_HD // 2
    out = []
    for hh in range(LANE // C_HD):
        xh = x_t[hh * C_HD:(hh + 1) * C_HD, :]
        ms = jnp.mean(xh * xh, axis=0, keepdims=True)
        xn = xh * lax.rsqrt(ms + NORM_EPS) * w_rep[hh * C_HD:(hh + 1) * C_HD, :]
        x1, x2 = xn[:half, :], xn[half:, :]
        out += [x1 * cos - x2 * sin, x2 * cos + x1 * sin]
    return jnp.concatenate(out, axis=0)


def _prompt_layer_kernel(
        sinks_ref,
        x_ref, xnext_ref, nw_ref, wt_ref, wout_ref, pk_ref, cs_ref,
        o_ref, st_ref,
        u0_scr, u1_scr, xn_scr, y_scr, kprev_scr, roll_scr, *, tb):
    L = CHUNK
    nch = tb // L
    t_id = pl.program_id(1)
    sub = 0
    gb_ref, al_ref, anw_ref, dsk_ref, bnw_ref, qnw_ref, knw_ref, _, cb_ref, cw_ref = _unpack_params(pk_ref)
    conv_tap = lambda j: cw_ref[j * B_CONV_DIM:(j + 1) * B_CONV_DIM, :]
    cst_ref, mst_ref, hst_ref, convt_ref, kt_ref, vt_ref = (
        st_ref.at[pl.ds(start, n)] for start, n in zip(ST_OFFSETS, ST_SIZES))

    @pl.when(t_id == 0)
    def _():
        st_ref[...] = jnp.zeros_like(st_ref)
        kprev_scr[...] = jnp.zeros_like(kprev_scr)
        roll_scr[...] = jnp.zeros_like(roll_scr)
        u0_scr[...] = _dot_nt(wt_ref[...], _rmsnorm_rows(x_ref[:tb, :], nw_ref[...]).astype(MXU_DTYPE))

    u_cur, u_next = u0_scr, u1_scr

    row = lax.broadcasted_iota(jnp.int32, (L, L), 0)
    lane = lax.broadcasted_iota(jnp.int32, (L, L), 1)
    lo = lane < HALF
    top = row < B_P
    src_le_t = row <= lane
    tri = jnp.where(lane <= row, 1.0, 0.0).astype(MXU_DTYPE)
    grow = lax.broadcasted_iota(jnp.int32, (N_GATES, L), 0)
    krow = lax.broadcasted_iota(jnp.int32, (2 * L, L), 0)
    klane = lax.broadcasted_iota(jnp.int32, (2 * L, L), 1)
    neg_inf = -jnp.inf
    a_neg = -jnp.exp(al_ref[...])

    def chunk(c, carry):
        cols = pl.ds(pl.multiple_of(c * L, L), L)
        ycols = pl.ds(pl.multiple_of(sub * tb + c * L, L), L)

        def piece_item(k):
            start = jnp.minimum((c * INPROJ_PIECES + k) * PIECE_ROWS, LAST_PIECE_START)
            wrows = pl.ds(pl.multiple_of(start, 2 * SUBLANE), PIECE_ROWS)
            u_next[wrows, :] = _dot_nt(wt_ref[wrows, :], xn_scr[...])
            return
            yield

        pre_r = jnp.concatenate([u_cur[OFF_GA:OFF_GA + 2 * A_HEADS, cols], u_cur[OFF_GB:OFF_GB + B_HEADS, cols]],
                                axis=0) + gb_ref[...]
        sp_r, ls_r = _softplus_terms(pre_r)
        z_r = jnp.where((grow >= A_HEADS) & (grow < 2 * A_HEADS), ls_r,
                        jnp.where(grow >= 2 * A_HEADS, sp_r * a_neg, 0.0))
        zp = _split3(z_r)
        cum_r = _dot_nt(zp[0], tri) + _dot_nt(zp[1], tri) + _dot_nt(zp[2], tri)
        colf = jnp.concatenate([pre_r, cum_r, jnp.zeros((L - 2 * N_GATES, L), F32)], axis=0).T

        def mlstm_head(h, q_t, k_p):
            e = h % 2
            k_m = jnp.where(lo if e == 0 else jnp.logical_not(lo), k_p, 0.0).astype(MXU_DTYPE)
            v_t = u_cur[OFF_AV + h * LANE:OFF_AV + (h + 1) * LANE, cols]
            b_row = cum_r[A_HEADS + h:A_HEADS + h + 1, :]
            i_row = pre_r[h:h + 1, :]
            c_col = colf[:, N_GATES + A_HEADS + h:N_GATES + A_HEADS + h + 1] - colf[:, h:h + 1]
            b_end = b_row[:, L - 1:L]
            m_prev = mst_ref[h:h + 1, 0:1]
            c_prev = cst_ref[h * C_EXT:(h + 1) * C_EXT, :]

            g_row = b_end - b_row + i_row
            m_loc = jnp.max(g_row, axis=1, keepdims=True)
            d_t = jnp.where(src_le_t, b_row - c_col, neg_inf)
            inter = b_row + m_prev
            m_t = jnp.maximum(inter, jnp.max(d_t, axis=0, keepdims=True))
            res = _dot(jnp.concatenate([k_m, c_prev.astype(MXU_DTYPE)], axis=0), q_t)
            w_row = jnp.exp(g_row - m_loc)
            v_ext = jnp.concatenate([v_t, jnp.ones((1, L), F32), jnp.zeros((C_EXT - A_V - 1, L), F32)], axis=0)
            c_loc = _dot((v_ext * w_row).astype(MXU_DTYPE), k_m)
            e_t = jnp.exp(d_t - m_t)
            si = jnp.exp(inter - m_t)
            yield
            s_t = e_t * res[:L, :]
            sv = _dot(v_t.astype(MXU_DTYPE), s_t.astype(MXU_DTYPE))
            yield
            num = sv + si * res[L:L + A_V, :]
            den = jnp.sum(s_t, axis=0, keepdims=True) + si * res[L + A_V:L + A_V + 1, :]
            hh = num * (1.0 / jnp.maximum(jnp.abs(den), jnp.exp(-m_t)))
            hn = hh * lax.rsqrt(jnp.mean(hh * hh, axis=0, keepdims=True) + NORM_EPS) * anw_ref[h * LANE:(h + 1) * LANE, :]
            ao = u_cur[OFF_AO + h * LANE:OFF_AO + (h + 1) * LANE, cols]
            az = u_cur[OFF_AZ + h * LANE:OFF_AZ + (h + 1) * LANE, cols]
            y_scr[h * LANE:(h + 1) * LANE, ycols] = (hn * jax.nn.sigmoid(ao) * _silu(az)).astype(y_scr.dtype)

            m_new = jnp.maximum(b_end + m_prev, m_loc)
            sp = jnp.exp(b_end + m_prev - m_new)
            sl = jnp.exp(m_loc - m_new)
            cst_ref[h * C_EXT:(h + 1) * C_EXT, :] = sp * c_prev + sl * c_loc
            mst_ref[h:h + 1, :] = jnp.broadcast_to(m_new, (1, LANE))

        def conv_silu():
            cur = u_cur[OFF_BXBC:OFF_BXBC + B_CONV_DIM, cols]
            acc = cb_ref[...] + cur * conv_tap(CONV_W - 1)
            for j in range(1, CONV_W):
                rolled = pltpu.roll(cur, j, axis=1)
                acc = acc + jnp.where(lane[:1, :] >= j, rolled, roll_scr[j - 1]) * conv_tap(CONV_W - 1 - j)
                roll_scr[j - 1] = rolled
            convt_ref[...] = cur
            return _silu(acc)

        def ssd_group(g, xbc):
            b_t = xbc[B_WIDTH + g * B_STATE:B_WIDTH + (g + 1) * B_STATE, :]
            c_t = xbc[B_WIDTH + (B_GROUPS + g) * B_STATE:B_WIDTH + (B_GROUPS + g + 1) * B_STATE, :]
            b_m = b_t.T.astype(MXU_DTYPE)
            hpg = B_HEADS // B_GROUPS
            h_prev = hst_ref[g * hpg * B_P:(g + 1) * hpg * B_P, :]
            yield
            res = _dot(jnp.concatenate([b_m, h_prev.astype(MXU_DTYPE)], axis=0), c_t.astype(MXU_DTYPE))
            x_ts, h_locs, rows_of = [], [], []
            for pj in range(hpg // 2):
                pb = g * (hpg // 2) + pj
                x_t = xbc[pb * LANE:(pb + 1) * LANE, :]
                a_ends, dec_rows, ea_rows, a_rows, a_cols, dt_rows = [], [], [], [], [], []
                for e in range(2):
                    gi = 2 * A_HEADS + 2 * pb + e
                    a_row = cum_r[gi:gi + 1, :]
                    dt_row = sp_r[gi:gi + 1, :]
                    a_end = a_row[:, L - 1:L]
                    a_rows.append(a_row)
                    a_cols.append(colf[:, N_GATES + gi:N_GATES + gi + 1])
                    dt_rows.append(dt_row)
                    a_ends.append(jnp.exp(a_end))
                    dec_rows.append(jnp.exp(a_end - a_row) * dt_row)
                    ea_rows.append(jnp.exp(a_row))
                h_locs.append(_dot((x_t * jnp.where(top, dec_rows[0], dec_rows[1])).astype(MXU_DTYPE), b_m))
                x_ts.append(x_t)
                rows_of.append((a_ends, ea_rows, a_rows, a_cols, dt_rows))
            yield
            cb_tt = res[:L, :]
            ys_all = []
            for pj in range(hpg // 2):
                a_ends, ea_rows, a_rows, a_cols, dt_rows = rows_of[pj]
                ys = []
                for e in range(2):
                    w_t = jnp.exp(jnp.where(src_le_t, a_rows[e] - a_cols[e], neg_inf)) * cb_tt
                    xdt = (x_ts[pj][e * B_P:(e + 1) * B_P, :] * dt_rows[e]).astype(MXU_DTYPE)
                    ys.append(_dot(xdt, w_t.astype(MXU_DTYPE)))
                ys_all.append(ys)
            yield
            gated = []
            for pj in range(hpg // 2):
                pb = g * (hpg // 2) + pj
                a_ends, ea_rows, a_rows, a_cols, dt_rows = rows_of[pj]
                inter = res[L + pj * LANE:L + (pj + 1) * LANE, :] * jnp.where(top, ea_rows[0], ea_rows[1])
                y_p = jnp.concatenate(ys_all[pj], axis=0) + inter + dsk_ref[pb * LANE:(pb + 1) * LANE, :] * x_ts[pj]
                hst_ref[pb * LANE:(pb + 1) * LANE, :] = \
                    jnp.where(top, a_ends[0], a_ends[1]) * h_prev[pj * LANE:(pj + 1) * LANE, :] + h_locs[pj]
                bz = u_cur[OFF_BZ + pb * LANE:OFF_BZ + (pb + 1) * LANE, cols]
                gated.append(y_p * _silu(bz))
            ms = sum(jnp.sum(gp * gp, axis=0, keepdims=True) for gp in gated) * (1.0 / (B_WIDTH // B_GROUPS))
            inv = lax.rsqrt(ms + NORM_EPS)
            for pj, gp in enumerate(gated):
                pb = g * (hpg // 2) + pj
                y_scr[A_WIDTH + pb * LANE:A_WIDTH + (pb + 1) * LANE, ycols] = \
                    (gp * inv * bnw_ref[pb * LANE:(pb + 1) * LANE, :]).astype(y_scr.dtype)

        cos = cs_ref[:C_HD // 2, ycols]
        sin = cs_ref[C_HD // 2:, ycols]
        ppg = C_HEADS // 2 // C_KV
        shared = {}

        def swa_keys():
            k_t = _norm_rope_t(u_cur[OFF_CK:OFF_CK + LANE, cols], knw_ref[...], cos, sin)
            v_t = u_cur[OFF_CV:OFF_CV + LANE, cols]
            k_p = k_t.T
            k_sw = pltpu.roll(k_p, HALF, axis=1)
            k_cur = [[jnp.where(lo, k_p, 0.0).astype(MXU_DTYPE), jnp.where(lo, 0.0, k_sw).astype(MXU_DTYPE)],
                     [jnp.where(lo, k_sw, 0.0).astype(MXU_DTYPE), jnp.where(lo, 0.0, k_p).astype(MXU_DTYPE)]]
            first = jnp.logical_and(t_id == 0, c == 0) if sub == 0 else False
            shift = jnp.where(first, 2 * L, 0)
            valid = ((krow < L) & (krow > klane + shift)) | ((krow >= L) & (krow - L <= klane))
            shared.update(k_t=k_t, v_t=v_t, k_cur=k_cur, valid=valid)

        def swa_pair(pb):
            g = pb // ppg
            k_cur, v_t, valid = shared["k_cur"], shared["v_t"], shared["valid"]
            q_t = _norm_rope_t(u_cur[OFF_CQ + pb * LANE:OFF_CQ + (pb + 1) * LANE, cols], qnw_ref[...], cos, sin)
            keys = jnp.concatenate([kprev_scr[2 * g], k_cur[g][0], kprev_scr[2 * g + 1], k_cur[g][1]], axis=0)
            sc = _dot(keys, q_t.astype(MXU_DTYPE)) * (C_HD ** -0.5)
            v_g = jnp.concatenate([vt_ref[g * C_HD:(g + 1) * C_HD, :], v_t[g * C_HD:(g + 1) * C_HD, :]],
                                  axis=1).astype(MXU_DTYPE)
            yield
            outs = []
            for e in range(2):
                s = jnp.where(valid, sc[e * 2 * L:(e + 1) * 2 * L, :], neg_inf)
                sink = sinks_ref[2 * pb + e]
                m = jnp.maximum(jnp.max(s, axis=0, keepdims=True), sink)
                ex = jnp.exp(s - m)
                p = ex * (1.0 / (jnp.sum(ex, axis=0, keepdims=True) + jnp.exp(sink - m)))
                outs.append(_dot(v_g, p.astype(MXU_DTYPE)))
            yield
            cz = u_cur[OFF_CZ + pb * LANE:OFF_CZ + (pb + 1) * LANE, cols]
            y_scr[A_WIDTH + B_WIDTH + pb * LANE:A_WIDTH + B_WIDTH + (pb + 1) * LANE, ycols] = \
                (jnp.concatenate(outs, axis=0) * _silu(cz)).astype(y_scr.dtype)

        def items():
            made = {}
            for tok in ITEM_ORDER:
                kind, idx = tok[0], int(tok[1:])
                if kind == "m":
                    pb = idx // 2
                    if ("qk", pb) not in made:
                        made[("qk", pb)] = (
                            u_cur[OFF_AQ + pb * LANE:OFF_AQ + (pb + 1) * LANE, cols].astype(MXU_DTYPE),
                            (u_cur[OFF_AK + pb * LANE:OFF_AK + (pb + 1) * LANE, cols] * (A_QK ** -0.5)).T)
                    yield mlstm_head(idx, *made[("qk", pb)])
                elif kind == "s":
                    if "xbc" not in made:
                        made["xbc"] = conv_silu()
                    yield ssd_group(idx, made["xbc"])
                elif kind == "a":
                    if "keys" not in made:
                        made["keys"] = swa_keys()
                    yield swa_pair(idx)
                else:
                    yield piece_item(idx)

        _trace_interleaved(items(), PROMPT_ITEMS_IN_FLIGHT)
        for g in range(C_KV):
            for e in range(2):
                kprev_scr[2 * g + e] = shared["k_cur"][g][e]
        kt_ref[...] = shared["k_t"]
        vt_ref[...] = shared["v_t"]
        return carry

    u_bufs = (u0_scr, u1_scr)
    for sub in range(OUT_PAIR):
        x_after = x_ref[(sub + 1) * tb:(sub + 2) * tb, :] if sub + 1 < OUT_PAIR else xnext_ref[...]
        xn_scr[...] = _rmsnorm_rows(x_after, nw_ref[...]).astype(MXU_DTYPE)
        u_cur, u_next = u_bufs[sub % 2], u_bufs[(sub + 1) % 2]
        lax.fori_loop(0, nch, lambda c, carry: chunk(c, carry), 0)

    o_ref[...] = x_ref[...] + _dot_tn(y_scr[...], wout_ref[...])


def _prompt_layer(x, lw, l, cs_t, tb):
    n_seq, t_len, _ = x.shape
    assert t_len % (OUT_PAIR * tb) == 0 and tb % CHUNK == 0 and tb == PROMPT_BLOCK and OUT_PAIR % 2 == 0
    n_blocks = t_len // tb
    grid = (n_seq, n_blocks // OUT_PAIR)

    def per_seq(shape):
        nd = len(shape)
        return pl.BlockSpec((None,) + shape, lambda n, t, _nd=nd: (n,) + (0,) * _nd)

    def per_layer(shape, **kw):
        nd = len(shape)
        return pl.BlockSpec((None,) + shape, lambda n, t, _nd=nd: (l,) + (0,) * _nd, **kw)

    in_specs = [
        pl.BlockSpec(memory_space=pltpu.SMEM),
        pl.BlockSpec((None, OUT_PAIR * tb, D_MODEL), lambda n, t: (n, t, 0)),
        pl.BlockSpec((None, tb, D_MODEL),
                     lambda n, t: (n, jnp.minimum(OUT_PAIR * (t + 1), n_blocks - 1), 0)),
        per_layer((1, D_MODEL)),
        per_layer((D_INT, D_MODEL), pipeline_mode=pl.Buffered(1)),
        per_layer((D_MIX, D_MODEL), pipeline_mode=pl.Buffered(1)),
        per_layer((PK_ROWS, LANE), pipeline_mode=pl.Buffered(1)),
        pl.BlockSpec((C_HD, OUT_PAIR * tb), lambda n, t: (0, t)),
    ]
    out_shape = (
        jax.ShapeDtypeStruct((n_seq, t_len, D_MODEL), F32),
        jax.ShapeDtypeStruct((n_seq, ST_ROWS, LANE), F32),
    )
    out_specs = (
        pl.BlockSpec((None, OUT_PAIR * tb, D_MODEL), lambda n, t: (n, t, 0)),
        per_seq((ST_ROWS, LANE)),
    )
    scratch = [
        pltpu.VMEM((D_INT, tb), F32),
        pltpu.VMEM((D_INT, tb), F32),
        pltpu.VMEM((tb, D_MODEL), MXU_DTYPE),
        pltpu.VMEM((D_MIX, OUT_PAIR * tb), MXU_DTYPE),
        pltpu.VMEM((2 * C_KV, CHUNK, LANE), MXU_DTYPE),
        pltpu.VMEM((CONV_W - 1, B_CONV_DIM, LANE), F32),
    ]
    return pl.pallas_call(
        functools.partial(_prompt_layer_kernel, tb=tb),
        grid=grid, in_specs=in_specs, out_specs=out_specs, out_shape=out_shape, scratch_shapes=scratch,
        compiler_params=pltpu.CompilerParams(dimension_semantics=("arbitrary", "arbitrary"),
                                             vmem_limit_bytes=VMEM_LIMIT_BYTES),
        name="prompt_layer",
    )(lw["sinks"][l], x, x, lw["norm_w"], lw["w_t"], lw["w_out"], lw["packed"], cs_t)


def _unpad_prompt_states(st):
    lead = st.shape[:2]
    cst, mst, hst, convt, kt, vt = (st[:, :, o:o + n, :] for o, n in zip(ST_OFFSETS, ST_SIZES))
    cst = cst.reshape(lead + (A_HEADS, C_EXT, LANE))
    c_even, c_odd = cst[:, :, 0::2, :A_V, :HALF], cst[:, :, 1::2, :A_V, HALF:]
    c_state = jnp.stack([c_even, c_odd], axis=3).reshape(lead + (A_HEADS, A_V, A_QK))
    n_even, n_odd = cst[:, :, 0::2, A_V, :HALF], cst[:, :, 1::2, A_V, HALF:]
    n_state = jnp.stack([n_even, n_odd], axis=3).reshape(lead + (A_HEADS, A_QK))
    conv = jnp.swapaxes(convt[:, :, :, LANE - (CONV_W - 1):], 2, 3)
    k1 = jnp.transpose(kt.reshape(lead + (C_KV, C_HD, WINDOW)), (0, 1, 4, 2, 3))
    v1 = jnp.transpose(vt.reshape(lead + (C_KV, C_HD, WINDOW)), (0, 1, 4, 2, 3))
    return (c_state, n_state, mst[:, :, :A_HEADS, 0], hst.reshape(lead + (B_HEADS, B_P, B_STATE)), conv, k1, v1)


ITEMS_IN_FLIGHT = 32


def _trace_interleaved(item_iter, depth):
    active, exhausted = [], False
    while True:
        while not exhausted and len(active) < depth:
            nxt = next(item_iter, None)
            if nxt is None:
                exhausted = True
            else:
                active.append(nxt)
        if not active:
            return
        still = []
        for gen in active:
            try:
                next(gen)
                still.append(gen)
            except StopIteration:
                pass
        active = still


def _decode_kernel(
        x_ref, nw_ref, wt_ref, wout_ref, pk_ref, cw_ref, cb_ref, cs_ref,
        c_ref, n_ref, m_ref, s_ref, cv_ref, k_ref, v_ref,
        y_ref, co_ref, no_ref, mo_ref, so_ref, cvo_ref, ko_ref, vo_ref,
        hs_scr, ut_scr, vrow_scr, xbc_scr, xt_scr, ct_scr, qa_scr, knew_scr, rep_scr, h_scr, yt_scr, yall_scr):
    l = pl.program_id(0)
    j = pl.program_id(1)
    nb = x_ref.shape[0]
    hd_a = A_QK
    gb_ref, al_ref, anw_ref, dsk_ref, bnw_ref, qnw_ref, knw_ref, sink_ref, _, _ = _unpack_params(pk_ref)

    @pl.when(j == 0)
    def _layer_start():
        @pl.when(l == 0)
        def _():
            hs_scr[...] = x_ref[...]

        xn = _rmsnorm_rows(hs_scr[...], nw_ref[...]).astype(MXU_DTYPE)
        ut_scr[...] = _dot_nt(wt_ref[...], xn)
        yt_scr[...] = jnp.zeros_like(yt_scr)

        pre = jnp.concatenate([ut_scr[OFF_GA:OFF_GA + 2 * A_HEADS, :], ut_scr[OFF_GB:OFF_GB + B_HEADS, :]],
                              axis=0) + gb_ref[...]
        spl, lsg = _softplus_terms(pre)
        ig, lf, dt = pre[:A_HEADS], lsg[A_HEADS:2 * A_HEADS], spl[2 * A_HEADS:]
        d_a = jnp.exp(dt * (-jnp.exp(al_ref[2 * A_HEADS:, :])))
        m0 = m_ref[...]
        m_new = jnp.maximum(lf + m0, ig)
        sp = jnp.exp(lf + m0 - m_new)
        sl = jnp.exp(ig - m_new)
        mo_ref[...] = m_new
        inv_den = []
        for h in range(A_HEADS):
            k_t = ut_scr[OFF_AK + h * hd_a:OFF_AK + (h + 1) * hd_a, :] * (A_QK ** -0.5)
            q_t = ut_scr[OFF_AQ + h * hd_a:OFF_AQ + (h + 1) * hd_a, :]
            n_new = sp[h:h + 1] * n_ref[h] + sl[h:h + 1] * k_t
            no_ref[h] = n_new
            den = jnp.sum(n_new * q_t, axis=0, keepdims=True)
            inv_den.append(1.0 / jnp.maximum(jnp.abs(den), jnp.exp(-m_new[h:h + 1])))

        cos, sin = cs_ref[:C_HD // 2, :], cs_ref[C_HD // 2:, :]
        kn_t = _norm_rope_t(ut_scr[OFF_CK:OFF_CK + LANE, :], knw_ref[...], cos, sin)
        knew_scr[...] = kn_t
        s_new = []
        for pb in range(C_HEADS // 2):
            q_t = _norm_rope_t(ut_scr[OFF_CQ + pb * LANE:OFF_CQ + (pb + 1) * LANE, :], qnw_ref[...], cos, sin)
            qa_scr[pb * LANE:(pb + 1) * LANE, :] = q_t
            for e in range(2):
                g = (2 * pb + e) // (C_HEADS // C_KV)
                s_new.append(jnp.sum(q_t[e * C_HD:(e + 1) * C_HD, :] * kn_t[g * C_HD:(g + 1) * C_HD, :],
                                     axis=0, keepdims=True))

        table = jnp.concatenate([sp, sl, d_a, dt] + s_new + inv_den + [jnp.zeros((LANE - N_REP, nb), F32)], axis=0)
        table_t = table.T
        for r in range(N_REP):
            rep_scr[r] = jnp.broadcast_to(table_t[:, r:r + 1], (nb, LANE))

        ubx = ut_scr[OFF_BXBC:OFF_BXBC + B_CONV_DIM, :].T
        acc = cb_ref[...] + ubx * cw_ref[CONV_W - 1:CONV_W, :]
        for t in range(CONV_W - 1):
            acc = acc + cv_ref[t] * cw_ref[t:t + 1, :]
        xbc = _silu(acc)
        xbc_scr[...] = xbc
        for t in range(CONV_W - 2):
            cvo_ref[t] = cv_ref[t + 1]
        cvo_ref[CONV_W - 2] = ubx
        xt_scr[...] = xbc[:, :B_WIDTH].T
        for g in range(B_GROUPS):
            ct_scr[g] = xbc[:, B_WIDTH + (B_GROUPS + g) * B_STATE:B_WIDTH + (B_GROUPS + g + 1) * B_STATE].T.astype(ct_scr.dtype)
        vrow_scr[...] = ut_scr[OFF_AV:OFF_AV + A_WIDTH, :].T

    base = pl.multiple_of(j * SAMPLE_BLOCK, SAMPLE_BLOCK)
    rows = pl.ds(base, SAMPLE_BLOCK)
    lane_s = lax.broadcasted_iota(jnp.int32, (C_HD, nb), 1)
    lane_k = lax.broadcasted_iota(jnp.int32, (C_HD, WINDOW), 1)
    lane_r = lane_k[:1, :]
    v_rows = vrow_scr[rows, :]
    xbc_rows = xbc_scr[rows, :]
    reps = [rep_scr[r, rows, :] for r in range(N_REP)]
    hpk = C_HEADS // C_KV
    sels = [lane_s == base + i for i in range(SAMPLE_BLOCK)]

    def col(tile, i):
        return jnp.sum(jnp.where(sels[i], tile, 0.0), axis=1, keepdims=True)

    def rep(r, i):
        return reps[r][i:i + 1, :]

    acc = {}

    ones_m = jnp.ones((nb, LANE), MXU_DTYPE)

    def cols_mxu(tile):
        stacked = jnp.concatenate([jnp.where(sels[i], tile, 0.0) for i in range(SAMPLE_BLOCK)], axis=0)
        return _dot(stacked.astype(MXU_DTYPE), ones_m)

    def mlstm_item(h, i, kc, qc):
        yield
        v_row = v_rows[i:i + 1, h * A_V:(h + 1) * A_V]
        c_new = rep(REP_SP + h, i) * c_ref[i, h] + (rep(REP_SL + h, i) * v_row) * kc
        co_ref[i, h] = c_new
        acc[("h", h)].append(jnp.sum(c_new * qc, axis=0, keepdims=True) * rep(REP_IDEN + h, i))
        if i == SAMPLE_BLOCK - 1:
            h_scr[rows, h * A_V:(h + 1) * A_V] = jnp.concatenate(acc.pop(("h", h)), axis=0)

    def ssd_item(hd, i, xc):
        g = hd // (B_HEADS // B_GROUPS)
        yield
        b_row = xbc_rows[i:i + 1, B_WIDTH + g * B_STATE:B_WIDTH + (g + 1) * B_STATE]
        h_new = rep(REP_DA + hd, i) * s_ref[i, hd] + (rep(REP_DT + hd, i) * b_row) * xc
        so_ref[i, hd] = h_new
        acc[("y", hd)].append(h_new.astype(MXU_DTYPE))
        if i < SAMPLE_BLOCK - 1:
            return
        y_all = _dot(jnp.concatenate(acc.pop(("y", hd)), axis=0), ct_scr[g])
        yield
        y_blk = jnp.zeros((B_P, nb), F32)
        for ii in range(SAMPLE_BLOCK):
            y_blk = jnp.where(sels[ii], y_all[ii * B_P:(ii + 1) * B_P, :], y_blk)
        yt_scr[hd * B_P:(hd + 1) * B_P, :] = yt_scr[hd * B_P:(hd + 1) * B_P, :] + y_blk

    def attn_item(g, i, kn_all, vn_all, qcs, sink):
        kc_new, vc_new = col(kn_all, i), col(vn_all, i)
        yield
        k_t = k_ref[i, g]
        v_t = v_ref[i, g]
        ko_ref[i, g] = jnp.where(lane_k == WINDOW - 1, kc_new, pltpu.roll(k_t, WINDOW - 1, axis=1))
        vo_ref[i, g] = jnp.where(lane_k == WINDOW - 1, vc_new, pltpu.roll(v_t, WINDOW - 1, axis=1))
        s_rows = [jnp.where(lane_r == 0, rep(REP_SNEW + g * hpk + hh, i), jnp.sum(k_t * qcs[hh], axis=0, keepdims=True))
                  for hh in range(hpk)]
        s = jnp.concatenate(s_rows, axis=0) * (C_HD ** -0.5)
        m = jnp.maximum(jnp.max(s, axis=1, keepdims=True), sink)
        yield
        m_b = jnp.broadcast_to(m, s.shape)
        yield
        ex = jnp.exp(s - m_b)
        den = jnp.sum(ex, axis=1, keepdims=True) + jnp.exp(sink - m)
        yield
        inv_b = jnp.broadcast_to(1.0 / den, s.shape)
        yield
        p = ex * inv_b
        v_eff = jnp.where(lane_k == 0, vc_new, v_t)
        o_cols = [jnp.sum(v_eff * p[hh:hh + 1, :], axis=1, keepdims=True) for hh in range(hpk)]
        yield
        for hh in range(hpk):
            hd = g * hpk + hh
            acc[("o", hd)] = jnp.where(sels[i], o_cols[hh], acc[("o", hd)])
            if i == SAMPLE_BLOCK - 1:
                r0 = B_WIDTH + hd * C_HD
                yt_scr[r0:r0 + C_HD, :] = yt_scr[r0:r0 + C_HD, :] + acc.pop(("o", hd))

    def items():
        for g in range(C_KV):
            kn_all = knew_scr[g * C_HD:(g + 1) * C_HD, :]
            vn_all = ut_scr[OFF_CV + g * C_HD:OFF_CV + (g + 1) * C_HD, :]
            q_cols = [cols_mxu(qa_scr[(g * hpk + hh) * C_HD:(g * hpk + hh + 1) * C_HD, :]) for hh in range(hpk)]
            sink = sink_ref[g * hpk:(g + 1) * hpk, 0:1]
            for hh in range(hpk):
                acc[("o", g * hpk + hh)] = jnp.zeros((C_HD, nb), F32)
            for i in range(SAMPLE_BLOCK):
                yield attn_item(g, i, kn_all, vn_all, [qc[i * C_HD:(i + 1) * C_HD, :] for qc in q_cols], sink)
        for h in range(A_HEADS):
            k_all = ut_scr[OFF_AK + h * hd_a:OFF_AK + (h + 1) * hd_a, :] * (A_QK ** -0.5)
            q_all = ut_scr[OFF_AQ + h * hd_a:OFF_AQ + (h + 1) * hd_a, :]
            acc[("h", h)] = []
            k_cols, q_cols = cols_mxu(k_all), cols_mxu(q_all)
            for i in range(SAMPLE_BLOCK):
                yield mlstm_item(h, i, k_cols[i * hd_a:(i + 1) * hd_a, :], q_cols[i * hd_a:(i + 1) * hd_a, :])
        for hd in range(B_HEADS):
            x_cols = cols_mxu(xt_scr[hd * B_P:(hd + 1) * B_P, :])
            acc[("y", hd)] = []
            for i in range(SAMPLE_BLOCK):
                yield ssd_item(hd, i, x_cols[i * B_P:(i + 1) * B_P, :])

    _trace_interleaved(items(), ITEMS_IN_FLIGHT)

    @pl.when(j == pl.num_programs(1) - 1)
    def _layer_end():
        for h in range(A_HEADS):
            h_t = h_scr[:, h * A_V:(h + 1) * A_V].T
            hn = h_t * lax.rsqrt(jnp.mean(h_t * h_t, axis=0, keepdims=True) + NORM_EPS) * anw_ref[h * A_V:(h + 1) * A_V, :]
            ao = ut_scr[OFF_AO + h * A_V:OFF_AO + (h + 1) * A_V, :]
            az = ut_scr[OFF_AZ + h * A_V:OFF_AZ + (h + 1) * A_V, :]
            yall_scr[h * A_V:(h + 1) * A_V, :] = (hn * jax.nn.sigmoid(ao) * _silu(az)).astype(yall_scr.dtype)
        gw = B_WIDTH // B_GROUPS
        for g in range(B_GROUPS):
            r = slice(g * gw, (g + 1) * gw)
            y_g = yt_scr[r, :] + dsk_ref[r, :] * xt_scr[r, :]
            gated = y_g * _silu(ut_scr[OFF_BZ + g * gw:OFF_BZ + (g + 1) * gw, :])
            inv = lax.rsqrt(jnp.mean(gated * gated, axis=0, keepdims=True) + NORM_EPS)
            yall_scr[A_WIDTH + g * gw:A_WIDTH + (g + 1) * gw, :] = (gated * inv * bnw_ref[r, :]).astype(yall_scr.dtype)
        yall_scr[A_WIDTH + B_WIDTH:, :] = \
            (yt_scr[B_WIDTH:, :] * _silu(ut_scr[OFF_CZ:OFF_CZ + C_WIDTH, :])).astype(yall_scr.dtype)
        hs_new = hs_scr[...] + _dot_tn(yall_scr[...], wout_ref[...])
        hs_scr[...] = hs_new

        @pl.when(l == pl.num_programs(0) - 1)
        def _():
            y_ref[...] = hs_new


def _decode(x, lw, cs_s, c_v, n_v, m_v, s_v, conv_v, k_v, v_v):
    depth, nb = c_v.shape[0], x.shape[0]
    assert nb == LANE, "samples sit on the 128 lanes next to lane-replicated parameters"
    grid = (depth, nb // SAMPLE_BLOCK)

    def per_layer(shape):
        nd = len(shape)
        return pl.BlockSpec((None,) + shape, lambda l, j, _nd=nd: (l,) + (0,) * _nd)

    def per_block(shape):
        nd = len(shape)
        return pl.BlockSpec((None, SAMPLE_BLOCK) + shape, lambda l, j, _nd=nd: (l, j) + (0,) * _nd)

    def weight(shape):
        nd = len(shape)
        return pl.BlockSpec((None,) + shape, lambda l, j, _nd=nd: (l,) + (0,) * _nd, pipeline_mode=pl.Buffered(1))

    fixed = lambda shape: pl.BlockSpec(shape, lambda l, j, _nd=len(shape): (0,) * _nd)
    state_specs = [
        per_block((A_HEADS, A_QK, A_V)), per_layer((A_HEADS, A_QK, nb)), per_layer((A_HEADS, nb)),
        per_block((B_HEADS, B_P, B_STATE)), per_layer((CONV_W - 1, nb, B_CONV_DIM)),
        per_block((C_KV, C_HD, WINDOW)), per_block((C_KV, C_HD, WINDOW)),
    ]
    in_specs = [
        fixed((nb, D_MODEL)),
        per_layer((1, D_MODEL)), weight((D_INT, D_MODEL)), weight((D_MIX, D_MODEL)),
        weight((PK_ROWS, LANE)), per_layer((CONV_W, B_CONV_DIM)), per_layer((1, B_CONV_DIM)),
        fixed((C_HD, LANE)),
    ] + state_specs
    states = (c_v, n_v, m_v, s_v, conv_v, k_v, v_v)
    out_shape = (jax.ShapeDtypeStruct((nb, D_MODEL), F32),) + tuple(jax.ShapeDtypeStruct(a.shape, F32) for a in states)
    scratch = [
        pltpu.VMEM((nb, D_MODEL), F32),
        pltpu.VMEM((D_INT, nb), F32),
        pltpu.VMEM((nb, A_WIDTH), F32),
        pltpu.VMEM((nb, B_CONV_DIM), F32),
        pltpu.VMEM((B_WIDTH, nb), F32),
        pltpu.VMEM((B_GROUPS, B_STATE, nb), MXU_DTYPE),
        pltpu.VMEM((C_WIDTH, nb), F32),
        pltpu.VMEM((C_KV * C_HD, nb), F32),
        pltpu.VMEM((N_REP, nb, LANE), F32),
        pltpu.VMEM((nb, A_WIDTH), F32),
        pltpu.VMEM((B_WIDTH + C_WIDTH, nb), F32),
        pltpu.VMEM((D_MIX, nb), MXU_DTYPE),
    ]
    return pl.pallas_call(
        _decode_kernel,
        grid=grid, in_specs=in_specs, out_specs=(fixed((nb, D_MODEL)),) + tuple(state_specs),
        out_shape=out_shape, scratch_shapes=scratch,
        compiler_params=pltpu.CompilerParams(dimension_semantics=("arbitrary", "arbitrary"),
                                             vmem_limit_bytes=VMEM_LIMIT_BYTES),
        name="decode",
    )(x, lw["norm_w"], lw["w_t"], lw["w_out"], lw["packed"], lw["conv_w"], lw["conv_b"], cs_s, *states)


def _prep_weights(norm_w, w_in, a_igate_b, a_fgate_b, a_norm_w, b_conv_w, b_conv_b, b_dt_bias, b_A_log, b_D,
                  b_norm_w, c_qnorm_w, c_knorm_w, c_sinks, w_out):
    depth = w_in.shape[0]
    w_t = jnp.swapaxes(w_in, 1, 2).astype(MXU_DTYPE)
    gbias = jnp.concatenate([a_igate_b, a_fgate_b, b_dt_bias], axis=-1)
    alog = jnp.concatenate([jnp.zeros((depth, 2 * A_HEADS), b_A_log.dtype), b_A_log], axis=-1)

    rows = jnp.concatenate([
        gbias, alog, a_norm_w, jnp.repeat(b_D, B_P, axis=-1), b_norm_w,
        jnp.tile(c_qnorm_w, (1, LANE // C_HD)), jnp.tile(c_knorm_w, (1, LANE // C_HD)), c_sinks,
        b_conv_b, b_conv_w.reshape(depth, CONV_W * B_CONV_DIM)], axis=-1).astype(F32)
    assert rows.shape == (depth, PK_ROWS)
    return {
        "w_t": w_t, "w_out": w_out.astype(MXU_DTYPE),
        "norm_w": norm_w.astype(F32)[:, None, :],
        "packed": jnp.broadcast_to(rows[..., None], (depth, PK_ROWS, LANE)),
        "conv_w": b_conv_w.astype(F32), "conv_b": b_conv_b.astype(F32)[:, None, :],
        "sinks": c_sinks.astype(F32),
    }


def _rope_tables_t(pos, width=None):
    half = C_HD // 2
    inv = ROPE_THETA ** (-jnp.arange(half, dtype=F32) / half)
    ang = inv[:, None] * pos.astype(F32)[None, :]
    table = jnp.concatenate([jnp.cos(ang), jnp.sin(ang)], axis=0)
    if width is not None:
        table = jnp.broadcast_to(table, (C_HD, width))
    return table


def _kernel_impl(x_prompt, x_sample, state_mlstm_C, state_mlstm_n, state_mlstm_m, state_ssm, state_conv,
                 cache_k, cache_v, norm_w, w_in, a_igate_b, a_fgate_b, a_norm_w, b_conv_w, b_conv_b,
                 b_dt_bias, b_A_log, b_D, b_norm_w, c_qnorm_w, c_knorm_w, c_sinks, w_out, *, tb):
    depth = w_in.shape[0]
    t_len = x_prompt.shape[1]
    lw = _prep_weights(norm_w, w_in, a_igate_b, a_fgate_b, a_norm_w, b_conv_w, b_conv_b, b_dt_bias, b_A_log,
                       b_D, b_norm_w, c_qnorm_w, c_knorm_w, c_sinks, w_out)
    cs_p = _rope_tables_t(jnp.arange(t_len, dtype=jnp.int32))
    hp = x_prompt
    st_prompt = []
    for l in range(depth):
        res = _prompt_layer(hp, lw, l, cs_p, tb)
        hp = res[0]
        st_prompt.append(res[1])
    p_states = _unpad_prompt_states(jnp.stack(st_prompt))

    assert x_sample.shape[1] == 1
    cs_s = _rope_tables_t(PAST_LEN + jnp.arange(1, dtype=jnp.int32), LANE)
    outs = _decode(
        x_sample[:, 0, :], lw, cs_s,
        jnp.transpose(state_mlstm_C, (0, 1, 2, 4, 3)), jnp.transpose(state_mlstm_n, (0, 2, 3, 1)),
        jnp.transpose(state_mlstm_m, (0, 2, 1)), state_ssm, jnp.transpose(state_conv, (0, 2, 1, 3)),
        jnp.transpose(cache_k, (0, 1, 3, 4, 2)), jnp.transpose(cache_v, (0, 1, 3, 4, 2)))
    hs, c_o, n_o, m_o, s_o, conv_o, k_o, v_o = outs
    s_states = (jnp.transpose(c_o, (0, 1, 2, 4, 3)), jnp.transpose(n_o, (0, 3, 1, 2)), jnp.transpose(m_o, (0, 2, 1)),
                s_o, jnp.transpose(conv_o, (0, 2, 1, 3)),
                jnp.transpose(k_o, (0, 1, 4, 2, 3)), jnp.transpose(v_o, (0, 1, 4, 2, 3)))
    return (hp, hs[:, None, :], *p_states, *s_states)


def kernel(x_prompt, x_sample, state_mlstm_C, state_mlstm_n, state_mlstm_m, state_ssm, state_conv, cache_k, cache_v, norm_w, w_in, a_igate_b, a_fgate_b, a_norm_w, b_conv_w, b_conv_b, b_dt_bias, b_A_log, b_D, b_norm_w, c_qnorm_w, c_knorm_w, c_sinks, w_out):
    return _kernel_impl(x_prompt, x_sample, state_mlstm_C, state_mlstm_n, state_mlstm_m, state_ssm, state_conv,
                        cache_k, cache_v, norm_w, w_in, a_igate_b, a_fgate_b, a_norm_w, b_conv_w, b_conv_b,
                        b_dt_bias, b_A_log, b_D, b_norm_w, c_qnorm_w, c_knorm_w, c_sinks, w_out, tb=PROMPT_BLOCK)
```

```python
import functools

import jax
import jax.numpy as jnp
import numpy as np
from jax import lax
from jax.experimental import pallas as pl
from jax.experimental.pallas import tpu as pltpu

F32 = jnp.float32
MXU_DTYPE = jnp.bfloat16

D_MODEL = 1024
A_HEADS, A_QK, A_V = 4, 64, 128
A_WIDTH = A_HEADS * A_V
B_HEADS, B_P, B_GROUPS, B_STATE = 8, 64, 2, 128
B_WIDTH = B_HEADS * B_P
CONV_W = 4
B_CONV_DIM = B_WIDTH + 2 * B_GROUPS * B_STATE
C_HEADS, C_KV, C_HD = 8, 2, 64
C_WIDTH = C_HEADS * C_HD
WINDOW = 128
ROPE_THETA = 10000.0
D_MIX = A_WIDTH + B_WIDTH + C_WIDTH
NORM_EPS = 1e-6
PAST_LEN = 8192

LANE = 128
SUBLANE = 8
HALF = LANE // 2

CHUNK = 128
PROMPT_BLOCK = 256
SAMPLE_BLOCK = SUBLANE
N_GATES = 2 * A_HEADS + B_HEADS

_SRC_SIZES = (A_HEADS * A_QK, A_HEADS * A_QK, A_WIDTH, A_WIDTH, A_WIDTH, A_HEADS, A_HEADS,
              B_WIDTH, B_CONV_DIM, B_HEADS, C_WIDTH, C_KV * C_HD, C_KV * C_HD, C_WIDTH)
(OFF_AQ, OFF_AK, OFF_AV, OFF_AO, OFF_AZ, OFF_GA, _, OFF_BZ, OFF_BXBC, OFF_GB, OFF_CQ, OFF_CK, OFF_CV, OFF_CZ,
 D_INT) = np.concatenate([[0], np.cumsum(_SRC_SIZES)]).tolist()
assert all(o % SUBLANE == 0 for o in (OFF_GA, OFF_BZ, OFF_BXBC, OFF_GB, OFF_CQ, OFF_CK, OFF_CV, OFF_CZ))
ITEM_ORDER = ("m0", "a0", "p0", "s0", "m1", "a1", "m2", "a2", "p1", "s1", "m3", "a3")
INPROJ_PIECES = sum(tok[0] == "p" for tok in ITEM_ORDER)
PROMPT_ITEMS_IN_FLIGHT = 2
_PIECES_PER_BLOCK = INPROJ_PIECES * (PROMPT_BLOCK // CHUNK)
PIECE_ROWS = -(-D_INT // (_PIECES_PER_BLOCK * 2 * SUBLANE)) * 2 * SUBLANE
LAST_PIECE_START = D_INT - PIECE_ROWS
assert LAST_PIECE_START % (2 * SUBLANE) == 0
OUT_PAIR = 2

PK_GB = 0
PK_AL = PK_GB + N_GATES
PK_ANW = PK_AL + N_GATES
PK_DSK = PK_ANW + A_WIDTH
PK_BNW = PK_DSK + B_WIDTH
PK_QNW = PK_BNW + B_WIDTH
PK_KNW = PK_QNW + LANE
PK_SINK = PK_KNW + LANE
PK_CB = PK_SINK + C_HEADS
PK_CW = PK_CB + B_CONV_DIM
PK_ROWS = PK_CW + CONV_W * B_CONV_DIM


def _unpack_params(pk_ref):
    view = lambda start, n: pk_ref.at[pl.ds(start, n)]
    return (view(PK_GB, N_GATES), view(PK_AL, N_GATES), view(PK_ANW, A_WIDTH), view(PK_DSK, B_WIDTH),
            view(PK_BNW, B_WIDTH), view(PK_QNW, LANE), view(PK_KNW, LANE), view(PK_SINK, C_HEADS),
            view(PK_CB, B_CONV_DIM), view(PK_CW, CONV_W * B_CONV_DIM))


C_EXT = A_V + 2 * SUBLANE
ST_SIZES = (A_HEADS * C_EXT, SUBLANE, B_HEADS * B_P, B_CONV_DIM, C_KV * C_HD, C_KV * C_HD)
ST_OFFSETS = tuple(int(o) for o in np.cumsum((0,) + ST_SIZES[:-1]))
ST_ROWS = sum(ST_SIZES)
VMEM_LIMIT_BYTES =56 * 1024 * 1024

REP_SP, REP_SL = 0, A_HEADS
REP_DA, REP_DT = 2 * A_HEADS, 2 * A_HEADS + B_HEADS
REP_SNEW = 2 * A_HEADS + 2 * B_HEADS
REP_IDEN = REP_SNEW + C_HEADS
N_REP = REP_IDEN + A_HEADS


def _dot(a, b):
    return jnp.dot(a, b, preferred_element_type=F32)


def _dot_nt(a, b):
    return lax.dot_general(a, b, (((1,), (1,)), ((), ())), preferred_element_type=F32)


def _dot_tn(a, b):
    return lax.dot_general(a, b, (((0,), (0,)), ((), ())), preferred_element_type=F32)


def _split3(z):
    hi = z.astype(MXU_DTYPE)
    r1 = z - hi.astype(F32)
    mid = r1.astype(MXU_DTYPE)
    lo = (r1 - mid.astype(F32)).astype(MXU_DTYPE)
    return hi, mid, lo


def _softplus_terms(x):
    t = jnp.log1p(jnp.exp(-jnp.abs(x)))
    return jnp.maximum(x, 0.0) + t, jnp.minimum(x, 0.0) - t


def _silu(x):
    return x * jax.nn.sigmoid(x)


def _rmsnorm_rows(x, w):
    return x * lax.rsqrt(jnp.mean(x * x, axis=-1, keepdims=True) + NORM_EPS) * w


def _norm_rope_t(x_t, w_rep, cos, sin):
    half = C_HD // 2
    out = []
    for hh in range(LANE // C_HD):
        xh = x_t[hh * C_HD:(hh + 1) * C_HD, :]
        ms = jnp.mean(xh * xh, axis=0, keepdims=True)
        xn = xh * lax.rsqrt(ms + NORM_EPS) * w_rep[hh * C_HD:(hh + 1) * C_HD, :]
        x1, x2 = xn[:half, :], xn[half:, :]
        out += [x1 * cos - x2 * sin, x2 * cos + x1 * sin]
    return jnp.concatenate(out, axis=0)


def _prompt_layer_kernel(
        sinks_ref,
        x_ref, xnext_ref, nw_ref, wt_ref, wout_ref, pk_ref, cs_ref,
        o_ref, st_ref,
        u0_scr, u1_scr, xn_scr, y_scr, kprev_scr, roll_scr, *, tb):
    L = CHUNK
    nch = tb // L
    t_id = pl.program_id(1)
    sub = 0
    gb_ref, al_ref, anw_ref, dsk_ref, bnw_ref, qnw_ref, knw_ref, _, cb_ref, cw_ref = _unpack_params(pk_ref)
    conv_tap = lambda j: cw_ref[j * B_CONV_DIM:(j + 1) * B_CONV_DIM, :]
    cst_ref, mst_ref, hst_ref, convt_ref, kt_ref, vt_ref = (
        st_ref.at[pl.ds(start, n)] for start, n in zip(ST_OFFSETS, ST_SIZES))

    @pl.when(t_id == 0)
    def _():
        st_ref[...] = jnp.zeros_like(st_ref)
        kprev_scr[...] = jnp.zeros_like(kprev_scr)
        roll_scr[...] = jnp.zeros_like(roll_scr)
        u0_scr[...] = _dot_nt(wt_ref[...], _rmsnorm_rows(x_ref[:tb, :], nw_ref[...]).astype(MXU_DTYPE))

    u_cur, u_next = u0_scr, u1_scr

    row = lax.broadcasted_iota(jnp.int32, (L, L), 0)
    lane = lax.broadcasted_iota(jnp.int32, (L, L), 1)
    lo = lane < HALF
    top = row < B_P
    src_le_t = row <= lane
    tri = jnp.where(lane <= row, 1.0, 0.0).astype(MXU_DTYPE)
    grow = lax.broadcasted_iota(jnp.int32, (N_GATES, L), 0)
    krow = lax.broadcasted_iota(jnp.int32, (2 * L, L), 0)
    klane = lax.broadcasted_iota(jnp.int32, (2 * L, L), 1)
    neg_inf = -jnp.inf
    a_neg = -jnp.exp(al_ref[...])

    def chunk(c, carry):
        cols = pl.ds(pl.multiple_of(c * L, L), L)
        ycols = pl.ds(pl.multiple_of(sub * tb + c * L, L), L)

        def piece_item(k):
            start = jnp.minimum((c * INPROJ_PIECES + k) * PIECE_ROWS, LAST_PIECE_START)
            wrows = pl.ds(pl.multiple_of(start, 2 * SUBLANE), PIECE_ROWS)
            u_next[wrows, :] = _dot_nt(wt_ref[wrows, :], xn_scr[...])
            return
            yield

        pre_r = jnp.concatenate([u_cur[OFF_GA:OFF_GA + 2 * A_HEADS, cols], u_cur[OFF_GB:OFF_GB + B_HEADS, cols]],
                                axis=0) + gb_ref[...]
        sp_r, ls_r = _softplus_terms(pre_r)
        z_r = jnp.where((grow >= A_HEADS) & (grow < 2 * A_HEADS), ls_r,
                        jnp.where(grow >= 2 * A_HEADS, sp_r * a_neg, 0.0))
        zp = _split3(z_r)
        cum_r = _dot_nt(zp[0], tri) + _dot_nt(zp[1], tri) + _dot_nt(zp[2], tri)
        colf = jnp.concatenate([pre_r, cum_r, jnp.zeros((L - 2 * N_GATES, L), F32)], axis=0).T

        def mlstm_head(h, q_t, k_p):
            e = h % 2
            k_m = jnp.where(lo if e == 0 else jnp.logical_not(lo), k_p, 0.0).astype(MXU_DTYPE)
            v_t = u_cur[OFF_AV + h * LANE:OFF_AV + (h + 1) * LANE, cols]
            b_row = cum_r[A_HEADS + h:A_HEADS + h + 1, :]
            i_row = pre_r[h:h + 1, :]
            c_col = colf[:, N_GATES + A_HEADS + h:N_GATES + A_HEADS + h + 1] - colf[:, h:h + 1]
            b_end = b_row[:, L - 1:L]
            m_prev = mst_ref[h:h + 1, 0:1]
            c_prev = cst_ref[h * C_EXT:(h + 1) * C_EXT, :]

            g_row = b_end - b_row + i_row
            m_loc = jnp.max(g_row, axis=1, keepdims=True)
            d_t = jnp.where(src_le_t, b_row - c_col, neg_inf)
            inter = b_row + m_prev
            m_t = jnp.maximum(inter, jnp.max(d_t, axis=0, keepdims=True))
            res = _dot(jnp.concatenate([k_m, c_prev.astype(MXU_DTYPE)], axis=0), q_t)
            w_row = jnp.exp(g_row - m_loc)
            v_ext = jnp.concatenate([v_t, jnp.ones((1, L), F32), jnp.zeros((C_EXT - A_V - 1, L), F32)], axis=0)
            c_loc = _dot((v_ext * w_row).astype(MXU_DTYPE), k_m)
            e_t = jnp.exp(d_t - m_t)
            si = jnp.exp(inter - m_t)
            yield
            s_t = e_t * res[:L, :]
            sv = _dot(v_t.astype(MXU_DTYPE), s_t.astype(MXU_DTYPE))
            yield
            num = sv + si * res[L:L + A_V, :]
            den = jnp.sum(s_t, axis=0, keepdims=True) + si * res[L + A_V:L + A_V + 1, :]
            hh = num * (1.0 / jnp.maximum(jnp.abs(den), jnp.exp(-m_t)))
            hn = hh * lax.rsqrt(jnp.mean(hh * hh, axis=0, keepdims=True) + NORM_EPS) * anw_ref[h * LANE:(h + 1) * LANE, :]
            ao = u_cur[OFF_AO + h * LANE:OFF_AO + (h + 1) * LANE, cols]
            az = u_cur[OFF_AZ + h * LANE:OFF_AZ + (h + 1) * LANE, cols]
            y_scr[h * LANE:(h + 1) * LANE, ycols] = (hn * jax.nn.sigmoid(ao) * _silu(az)).astype(y_scr.dtype)

            m_new = jnp.maximum(b_end + m_prev, m_loc)
            sp = jnp.exp(b_end + m_prev - m_new)
            sl = jnp.exp(m_loc - m_new)
            cst_ref[h * C_EXT:(h + 1) * C_EXT, :] = sp * c_prev + sl * c_loc
            mst_ref[h:h + 1, :] = jnp.broadcast_to(m_new, (1, LANE))

        def conv_silu():
            cur = u_cur[OFF_BXBC:OFF_BXBC + B_CONV_DIM, cols]
            acc = cb_ref[...] + cur * conv_tap(CONV_W - 1)
            for j in range(1, CONV_W):
                rolled = pltpu.roll(cur, j, axis=1)
                acc = acc + jnp.where(lane[:1, :] >= j, rolled, roll_scr[j - 1]) * conv_tap(CONV_W - 1 - j)
                roll_scr[j - 1] = rolled
            convt_ref[...] = cur
            return _silu(acc)

        def ssd_group(g, xbc):
            b_t = xbc[B_WIDTH + g * B_STATE:B_WIDTH + (g + 1) * B_STATE, :]
            c_t = xbc[B_WIDTH + (B_GROUPS + g) * B_STATE:B_WIDTH + (B_GROUPS + g + 1) * B_STATE, :]
            b_m = b_t.T.astype(MXU_DTYPE)
            hpg = B_HEADS // B_GROUPS
            h_prev = hst_ref[g * hpg * B_P:(g + 1) * hpg * B_P, :]
            yield
            res = _dot(jnp.concatenate([b_m, h_prev.astype(MXU_DTYPE)], axis=0), c_t.astype(MXU_DTYPE))
            x_ts, h_locs, rows_of = [], [], []
            for pj in range(hpg // 2):
                pb = g * (hpg // 2) + pj
                x_t = xbc[pb * LANE:(pb + 1) * LANE, :]
                a_ends, dec_rows, ea_rows, a_rows, a_cols, dt_rows = [], [], [], [], [], []
                for e in range(2):
                    gi = 2 * A_HEADS + 2 * pb + e
                    a_row = cum_r[gi:gi + 1, :]
                    dt_row = sp_r[gi:gi + 1, :]
                    a_end = a_row[:, L - 1:L]
                    a_rows.append(a_row)
                    a_cols.append(colf[:, N_GATES + gi:N_GATES + gi + 1])
                    dt_rows.append(dt_row)
                    a_ends.append(jnp.exp(a_end))
                    dec_rows.append(jnp.exp(a_end - a_row) * dt_row)
                    ea_rows.append(jnp.exp(a_row))
                h_locs.append(_dot((x_t * jnp.where(top, dec_rows[0], dec_rows[1])).astype(MXU_DTYPE), b_m))
                x_ts.append(x_t)
                rows_of.append((a_ends, ea_rows, a_rows, a_cols, dt_rows))
            yield
            cb_tt = res[:L, :]
            ys_all = []
            for pj in range(hpg // 2):
                a_ends, ea_rows, a_rows, a_cols, dt_rows = rows_of[pj]
                ys = []
                for e in range(2):
                    w_t = jnp.exp(jnp.where(src_le_t, a_rows[e] - a_cols[e], neg_inf)) * cb_tt
                    xdt = (x_ts[pj][e * B_P:(e + 1) * B_P, :] * dt_rows[e]).astype(MXU_DTYPE)
                    ys.append(_dot(xdt, w_t.astype(MXU_DTYPE)))
                ys_all.append(ys)
            yield
            gated = []
            for pj in range(hpg // 2):
                pb = g * (hpg // 2) + pj
                a_ends, ea_rows, a_rows, a_cols, dt_rows = rows_of[pj]
                inter = res[L + pj * LANE:L + (pj + 1) * LANE, :] * jnp.where(top, ea_rows[0], ea_rows[1])
                y_p = jnp.concatenate(ys_all[pj], axis=0) + inter + dsk_ref[pb * LANE:(pb + 1) * LANE, :] * x_ts[pj]
                hst_ref[pb * LANE:(pb + 1) * LANE, :] = \
                    jnp.where(top, a_ends[0], a_ends[1]) * h_prev[pj * LANE:(pj + 1) * LANE, :] + h_locs[pj]
                bz = u_cur[OFF_BZ + pb * LANE:OFF_BZ + (pb + 1) * LANE, cols]
                gated.append(y_p * _silu(bz))
            ms = sum(jnp.sum(gp * gp, axis=0, keepdims=True) for gp in gated) * (1.0 / (B_WIDTH // B_GROUPS))
            inv = lax.rsqrt(ms + NORM_EPS)
            for pj, gp in enumerate(gated):
                pb = g * (hpg // 2) + pj
                y_scr[A_WIDTH + pb * LANE:A_WIDTH + (pb + 1) * LANE, ycols] = \
                    (gp * inv * bnw_ref[pb * LANE:(pb + 1) * LANE, :]).astype(y_scr.dtype)

        cos = cs_ref[:C_HD // 2, ycols]
        sin = cs_ref[C_HD // 2:, ycols]
        ppg = C_HEADS // 2 // C_KV
        shared = {}

        def swa_keys():
            k_t = _norm_rope_t(u_cur[OFF_CK:OFF_CK + LANE, cols], knw_ref[...], cos, sin)
            v_t = u_cur[OFF_CV:OFF_CV + LANE, cols]
            k_p = k_t.T
            k_sw = pltpu.roll(k_p, HALF, axis=1)
            k_cur = [[jnp.where(lo, k_p, 0.0).astype(MXU_DTYPE), jnp.where(lo, 0.0, k_sw).astype(MXU_DTYPE)],
                     [jnp.where(lo, k_sw, 0.0).astype(MXU_DTYPE), jnp.where(lo, 0.0, k_p).astype(MXU_DTYPE)]]
            first = jnp.logical_and(t_id == 0, c == 0) if sub == 0 else False
            shift = jnp.where(first, 2 * L, 0)
            valid = ((krow < L) & (krow > klane + shift)) | ((krow >= L) & (krow - L <= klane))
            shared.update(k_t=k_t, v_t=v_t, k_cur=k_cur, valid=valid)

        def swa_pair(pb):
            g = pb // ppg
            k_cur, v_t, valid = shared["k_cur"], shared["v_t"], shared["valid"]
            q_t = _norm_rope_t(u_cur[OFF_CQ + pb * LANE:OFF_CQ + (pb + 1) * LANE, cols], qnw_ref[...], cos, sin)
            keys = jnp.concatenate([kprev_scr[2 * g], k_cur[g][0], kprev_scr[2 * g + 1], k_cur[g][1]], axis=0)
            sc = _dot(keys, q_t.astype(MXU_DTYPE)) * (C_HD ** -0.5)
            v_g = jnp.concatenate([vt_ref[g * C_HD:(g + 1) * C_HD, :], v_t[g * C_HD:(g + 1) * C_HD, :]],
                                  axis=1).astype(MXU_DTYPE)
            yield
            outs = []
            for e in range(2):
                s = jnp.where(valid, sc[e * 2 * L:(e + 1) * 2 * L, :], neg_inf)
                sink = sinks_ref[2 * pb + e]
                m = jnp.maximum(jnp.max(s, axis=0, keepdims=True), sink)
                ex = jnp.exp(s - m)
                p = ex * (1.0 / (jnp.sum(ex, axis=0, keepdims=True) + jnp.exp(sink - m)))
                outs.append(_dot(v_g, p.astype(MXU_DTYPE)))
            yield
            cz = u_cur[OFF_CZ + pb * LANE:OFF_CZ + (pb + 1) * LANE, cols]
            y_scr[A_WIDTH + B_WIDTH + pb * LANE:A_WIDTH + B_WIDTH + (pb + 1) * LANE, ycols] = \
                (jnp.concatenate(outs, axis=0) * _silu(cz)).astype(y_scr.dtype)

        def items():
            made = {}
            for tok in ITEM_ORDER:
                kind, idx = tok[0], int(tok[1:])
                if kind == "m":
                    pb = idx // 2
                    if ("qk", pb) not in made:
                        made[("qk", pb)] = (
                            u_cur[OFF_AQ + pb * LANE:OFF_AQ + (pb + 1) * LANE, cols].astype(MXU_DTYPE),
                            (u_cur[OFF_AK + pb * LANE:OFF_AK + (pb + 1) * LANE, cols] * (A_QK ** -0.5)).T)
                    yield mlstm_head(idx, *made[("qk", pb)])
                elif kind == "s":
                    if "xbc" not in made:
                        made["xbc"] = conv_silu()
                    yield ssd_group(idx, made["xbc"])
                elif kind == "a":
                    if "keys" not in made:
                        made["keys"] = swa_keys()
                    yield swa_pair(idx)
                else:
                    yield piece_item(idx)

        _trace_interleaved(items(), PROMPT_ITEMS_IN_FLIGHT)
        for g in range(C_KV):
            for e in range(2):
                kprev_scr[2 * g + e] = shared["k_cur"][g][e]
        kt_ref[...] = shared["k_t"]
        vt_ref[...] = shared["v_t"]
        return carry

    u_bufs = (u0_scr, u1_scr)
    for sub in range(OUT_PAIR):
        x_after = x_ref[(sub + 1) * tb:(sub + 2) * tb, :] if sub + 1 < OUT_PAIR else xnext_ref[...]
        xn_scr[...] = _rmsnorm_rows(x_after, nw_ref[...]).astype(MXU_DTYPE)
        u_cur, u_next = u_bufs[sub % 2], u_bufs[(sub + 1) % 2]
        lax.fori_loop(0, nch, lambda c, carry: chunk(c, carry), 0)

    o_ref[...] = x_ref[...] + _dot_tn(y_scr[...], wout_ref[...])


def _prompt_layer(x, lw, l, cs_t, tb):
    n_seq, t_len, _ = x.shape
    assert t_len % (OUT_PAIR * tb) == 0 and tb % CHUNK == 0 and tb == PROMPT_BLOCK and OUT_PAIR % 2 == 0
    n_blocks = t_len // tb
    grid = (n_seq, n_blocks // OUT_PAIR)

    def per_seq(shape):
        nd = len(shape)
        return pl.BlockSpec((None,) + shape, lambda n, t, _nd=nd: (n,) + (0,) * _nd)

    def per_layer(shape, **kw):
        nd = len(shape)
        return pl.BlockSpec((None,) + shape, lambda n, t, _nd=nd: (l,) + (0,) * _nd, **kw)

    in_specs = [
        pl.BlockSpec(memory_space=pltpu.SMEM),
        pl.BlockSpec((None, OUT_PAIR * tb, D_MODEL), lambda n, t: (n, t, 0)),
        pl.BlockSpec((None, tb, D_MODEL),
                     lambda n, t: (n, jnp.minimum(OUT_PAIR * (t + 1), n_blocks - 1), 0)),
        per_layer((1, D_MODEL)),
        per_layer((D_INT, D_MODEL), pipeline_mode=pl.Buffered(1)),
        per_layer((D_MIX, D_MODEL), pipeline_mode=pl.Buffered(1)),
        per_layer((PK_ROWS, LANE), pipeline_mode=pl.Buffered(1)),
        pl.BlockSpec((C_HD, OUT_PAIR * tb), lambda n, t: (0, t)),
    ]
    out_shape = (
        jax.ShapeDtypeStruct((n_seq, t_len, D_MODEL), F32),
        jax.ShapeDtypeStruct((n_seq, ST_ROWS, LANE), F32),
    )
    out_specs = (
        pl.BlockSpec((None, OUT_PAIR * tb, D_MODEL), lambda n, t: (n, t, 0)),
        per_seq((ST_ROWS, LANE)),
    )
    scratch = [
        pltpu.VMEM((D_INT, tb), F32),
        pltpu.VMEM((D_INT, tb), F32),
        pltpu.VMEM((tb, D_MODEL), MXU_DTYPE),
        pltpu.VMEM((D_MIX, OUT_PAIR * tb), MXU_DTYPE),
        pltpu.VMEM((2 * C_KV, CHUNK, LANE), MXU_DTYPE),
        pltpu.VMEM((CONV_W - 1, B_CONV_DIM, LANE), F32),
    ]
    return pl.pallas_call(
        functools.partial(_prompt_layer_kernel, tb=tb),
        grid=grid, in_specs=in_specs, out_specs=out_specs, out_shape=out_shape, scratch_shapes=scratch,
        compiler_params=pltpu.CompilerParams(dimension_semantics=("arbitrary", "arbitrary"),
                                             vmem_limit_bytes=VMEM_LIMIT_BYTES),
        name="prompt_layer",
    )(lw["sinks"][l], x, x, lw["norm_w"], lw["w_t"], lw["w_out"], lw["packed"], cs_t)


def _unpad_prompt_states(st):
    lead = st.shape[:2]
    cst, mst, hst, convt, kt, vt = (st[:, :, o:o + n, :] for o, n in zip(ST_OFFSETS, ST_SIZES))
    cst = cst.reshape(lead + (A_HEADS, C_EXT, LANE))
    c_even, c_odd = cst[:, :, 0::2, :A_V, :HALF], cst[:, :, 1::2, :A_V, HALF:]
    c_state = jnp.stack([c_even, c_odd], axis=3).reshape(lead + (A_HEADS, A_V, A_QK))
    n_even, n_odd = cst[:, :, 0::2, A_V, :HALF], cst[:, :, 1::2, A_V, HALF:]
    n_state = jnp.stack([n_even, n_odd], axis=3).reshape(lead + (A_HEADS, A_QK))
    conv = jnp.swapaxes(convt[:, :, :, LANE - (CONV_W - 1):], 2, 3)
    k1 = jnp.transpose(kt.reshape(lead + (C_KV, C_HD, WINDOW)), (0, 1, 4, 2, 3))
    v1 = jnp.transpose(vt.reshape(lead + (C_KV, C_HD, WINDOW)), (0, 1, 4, 2, 3))
    return (c_state, n_state, mst[:, :, :A_HEADS, 0], hst.reshape(lead + (B_HEADS, B_P, B_STATE)), conv, k1, v1)


DECODE_VMEM_LIMIT_BYTES = 62 * 1024 * 1024
ITEMS_IN_FLIGHT = 32


def _trace_interleaved(item_iter, depth):
    active, exhausted = [], False
    while True:
        while not exhausted and len(active) < depth:
            nxt = next(item_iter, None)
            if nxt is None:
                exhausted = True
            else:
                active.append(nxt)
        if not active:
            return
        still = []
        for gen in active:
            try:
                next(gen)
                still.append(gen)
            except StopIteration:
                pass
        active = still


def _decode_kernel(
        x_ref, nw_ref, wt_ref, wout_ref, pk_ref, cw_ref, cb_ref, cs_ref,
        c_ref, n_ref, m_ref, s_ref, cv_ref, k_ref, v_ref,
        y_ref, co_ref, no_ref, mo_ref, so_ref, cvo_ref, ko_ref, vo_ref,
        hs_scr, ut_scr, vrow_scr, xbc_scr, xt_scr, ct_scr, qa_scr, knew_scr, rep_scr, h_scr, yt_scr, yall_scr):
    l = pl.program_id(0)
    j = pl.program_id(1)
    nb = x_ref.shape[0]
    hd_a = A_QK
    gb_ref, al_ref, anw_ref, dsk_ref, bnw_ref, qnw_ref, knw_ref, sink_ref, _, _ = _unpack_params(pk_ref)

    @pl.when(j == 0)
    def _layer_start():
        @pl.when(l == 0)
        def _():
            hs_scr[...] = x_ref[...]

        xn = _rmsnorm_rows(hs_scr[...], nw_ref[...]).astype(MXU_DTYPE)
        ut_scr[...] = _dot_nt(wt_ref[...], xn)
        yt_scr[...] = jnp.zeros_like(yt_scr)

        pre = jnp.concatenate([ut_scr[OFF_GA:OFF_GA + 2 * A_HEADS, :], ut_scr[OFF_GB:OFF_GB + B_HEADS, :]],
                              axis=0) + gb_ref[...]
        spl, lsg = _softplus_terms(pre)
        ig, lf, dt = pre[:A_HEADS], lsg[A_HEADS:2 * A_HEADS], spl[2 * A_HEADS:]
        d_a = jnp.exp(dt * (-jnp.exp(al_ref[2 * A_HEADS:, :])))
        m0 = m_ref[...]
        m_new = jnp.maximum(lf + m0, ig)
        sp = jnp.exp(lf + m0 - m_new)
        sl = jnp.exp(ig - m_new)
        mo_ref[...] = m_new
        inv_den = []
        for h in range(A_HEADS):
            k_t = ut_scr[OFF_AK + h * hd_a:OFF_AK + (h + 1) * hd_a, :] * (A_QK ** -0.5)
            q_t = ut_scr[OFF_AQ + h * hd_a:OFF_AQ + (h + 1) * hd_a, :]
            n_new = sp[h:h + 1] * n_ref[h] + sl[h:h + 1] * k_t
            no_ref[h] = n_new
            den = jnp.sum(n_new * q_t, axis=0, keepdims=True)
            inv_den.append(1.0 / jnp.maximum(jnp.abs(den), jnp.exp(-m_new[h:h + 1])))

        cos, sin = cs_ref[:C_HD // 2, :], cs_ref[C_HD // 2:, :]
        kn_t = _norm_rope_t(ut_scr[OFF_CK:OFF_CK + LANE, :], knw_ref[...], cos, sin)
        knew_scr[...] = kn_t
        s_new = []
        for pb in range(C_HEADS // 2):
            q_t = _norm_rope_t(ut_scr[OFF_CQ + pb * LANE:OFF_CQ + (pb + 1) * LANE, :], qnw_ref[...], cos, sin)
            qa_scr[pb * LANE:(pb + 1) * LANE, :] = q_t
            for e in range(2):
                g = (2 * pb + e) // (C_HEADS // C_KV)
                s_new.append(jnp.sum(q_t[e * C_HD:(e + 1) * C_HD, :] * kn_t[g * C_HD:(g + 1) * C_HD, :],
                                     axis=0, keepdims=True))

        table = jnp.concatenate([sp, sl, d_a, dt] + s_new + inv_den + [jnp.zeros((LANE - N_REP, nb), F32)], axis=0)
        table_t = table.T
        for r in range(N_REP):
            rep_scr[r] = jnp.broadcast_to(table_t[:, r:r + 1], (nb, LANE))

        ubx = ut_scr[OFF_BXBC:OFF_BXBC + B_CONV_DIM, :].T
        acc = cb_ref[...] + ubx * cw_ref[CONV_W - 1:CONV_W, :]
        for t in range(CONV_W - 1):
            acc = acc + cv_ref[t] * cw_ref[t:t + 1, :]
        xbc = _silu(acc)
        xbc_scr[...] = xbc
        for t in range(CONV_W - 2):
            cvo_ref[t] = cv_ref[t + 1]
        cvo_ref[CONV_W - 2] = ubx
        xt_scr[...] = xbc[:, :B_WIDTH].T
        for g in range(B_GROUPS):
            ct_scr[g] = xbc[:, B_WIDTH + (B_GROUPS + g) * B_STATE:B_WIDTH + (B_GROUPS + g + 1) * B_STATE].T.astype(ct_scr.dtype)
        vrow_scr[...] = ut_scr[OFF_AV:OFF_AV + A_WIDTH, :].T

    base = pl.multiple_of(j * SAMPLE_BLOCK, SAMPLE_BLOCK)
    rows = pl.ds(base, SAMPLE_BLOCK)
    lane_s = lax.broadcasted_iota(jnp.int32, (C_HD, nb), 1)
    lane_k = lax.broadcasted_iota(jnp.int32, (C_HD, WINDOW), 1)
    lane_r = lane_k[:1, :]
    v_rows = vrow_scr[rows, :]
    xbc_rows = xbc_scr[rows, :]
    reps = [rep_scr[r, rows, :] for r in range(N_REP)]
    hpk = C_HEADS // C_KV
    sels = [lane_s == base + i for i in range(SAMPLE_BLOCK)]

    def col(tile, i):
        return jnp.sum(jnp.where(sels[i], tile, 0.0), axis=1, keepdims=True)

    def rep(r, i):
        return reps[r][i:i + 1, :]

    acc = {}

    ones_m = jnp.ones((nb, LANE), MXU_DTYPE)

    def cols_mxu(tile):
        stacked = jnp.concatenate([jnp.where(sels[i], tile, 0.0) for i in range(SAMPLE_BLOCK)], axis=0)
        return _dot(stacked.astype(MXU_DTYPE), ones_m)

    def mlstm_item(h, i, kc, qc):
        yield
        v_row = v_rows[i:i + 1, h * A_V:(h + 1) * A_V]
        c_new = rep(REP_SP + h, i) * c_ref[i, h] + (rep(REP_SL + h, i) * v_row) * kc
        co_ref[i, h] = c_new
        acc[("h", h)].append(jnp.sum(c_new * qc, axis=0, keepdims=True) * rep(REP_IDEN + h, i))
        if i == SAMPLE_BLOCK - 1:
            h_scr[rows, h * A_V:(h + 1) * A_V] = jnp.concatenate(acc.pop(("h", h)), axis=0)

    def ssd_item(hd, i, xc):
        g = hd // (B_HEADS // B_GROUPS)
        yield
        b_row = xbc_rows[i:i + 1, B_WIDTH + g * B_STATE:B_WIDTH + (g + 1) * B_STATE]
        h_new = rep(REP_DA + hd, i) * s_ref[i, hd] + (rep(REP_DT + hd, i) * b_row) * xc
        so_ref[i, hd] = h_new
        acc[("y", hd)].append(h_new.astype(MXU_DTYPE))
        if i < SAMPLE_BLOCK - 1:
            return
        y_all = _dot(jnp.concatenate(acc.pop(("y", hd)), axis=0), ct_scr[g])
        yield
        y_blk = jnp.zeros((B_P, nb), F32)
        for ii in range(SAMPLE_BLOCK):
            y_blk = jnp.where(sels[ii], y_all[ii * B_P:(ii + 1) * B_P, :], y_blk)
        yt_scr[hd * B_P:(hd + 1) * B_P, :] = yt_scr[hd * B_P:(hd + 1) * B_P, :] + y_blk

    def attn_item(g, i, kn_all, vn_all, qcs, sink):
        kc_new, vc_new = col(kn_all, i), col(vn_all, i)
        yield
        k_t = k_ref[i, g]
        v_t = v_ref[i, g]
        ko_ref[i, g] = jnp.where(lane_k == WINDOW - 1, kc_new, pltpu.roll(k_t, WINDOW - 1, axis=1))
        vo_ref[i, g] = jnp.where(lane_k == WINDOW - 1, vc_new, pltpu.roll(v_t, WINDOW - 1, axis=1))
        s_rows = [jnp.where(lane_r == 0, rep(REP_SNEW + g * hpk + hh, i), jnp.sum(k_t * qcs[hh], axis=0, keepdims=True))
                  for hh in range(hpk)]
        s = jnp.concatenate(s_rows, axis=0) * (C_HD ** -0.5)
        m = jnp.maximum(jnp.max(s, axis=1, keepdims=True), sink)
        yield
        m_b = jnp.broadcast_to(m, s.shape)
        yield
        ex = jnp.exp(s - m_b)
        den = jnp.sum(ex, axis=1, keepdims=True) + jnp.exp(sink - m)
        yield
        inv_b = jnp.broadcast_to(1.0 / den, s.shape)
        yield
        p = ex * inv_b
        v_eff = jnp.where(lane_k == 0, vc_new, v_t)
        o_cols = [jnp.sum(v_eff * p[hh:hh + 1, :], axis=1, keepdims=True) for hh in range(hpk)]
        yield
        for hh in range(hpk):
            hd = g * hpk + hh
            acc[("o", hd)] = jnp.where(sels[i], o_cols[hh], acc[("o", hd)])
            if i == SAMPLE_BLOCK - 1:
                r0 = B_WIDTH + hd * C_HD
                yt_scr[r0:r0 + C_HD, :] = yt_scr[r0:r0 + C_HD, :] + acc.pop(("o", hd))

    def items():
        for g in range(C_KV):
            kn_all = knew_scr[g * C_HD:(g + 1) * C_HD, :]
            vn_all = ut_scr[OFF_CV + g * C_HD:OFF_CV + (g + 1) * C_HD, :]
            q_cols = [cols_mxu(qa_scr[(g * hpk + hh) * C_HD:(g * hpk + hh + 1) * C_HD, :]) for hh in range(hpk)]
            sink = sink_ref[g * hpk:(g + 1) * hpk, 0:1]
            for hh in range(hpk):
                acc[("o", g * hpk + hh)] = jnp.zeros((C_HD, nb), F32)
            for i in range(SAMPLE_BLOCK):
                yield attn_item(g, i, kn_all, vn_all, [qc[i * C_HD:(i + 1) * C_HD, :] for qc in q_cols], sink)
        for h in range(A_HEADS):
            k_all = ut_scr[OFF_AK + h * hd_a:OFF_AK + (h + 1) * hd_a, :] * (A_QK ** -0.5)
            q_all = ut_scr[OFF_AQ + h * hd_a:OFF_AQ + (h + 1) * hd_a, :]
            acc[("h", h)] = []
            k_cols, q_cols = cols_mxu(k_all), cols_mxu(q_all)
            for i in range(SAMPLE_BLOCK):
                yield mlstm_item(h, i, k_cols[i * hd_a:(i + 1) * hd_a, :], q_cols[i * hd_a:(i + 1) * hd_a, :])
        for hd in range(B_HEADS):
            x_cols = cols_mxu(xt_scr[hd * B_P:(hd + 1) * B_P, :])
            acc[("y", hd)] = []
            for i in range(SAMPLE_BLOCK):
                yield ssd_item(hd, i, x_cols[i * B_P:(i + 1) * B_P, :])

    _trace_interleaved(items(), ITEMS_IN_FLIGHT)

    @pl.when(j == pl.num_programs(1) - 1)
    def _layer_end():
        for h in range(A_HEADS):
            h_t = h_scr[:, h * A_V:(h + 1) * A_V].T
            hn = h_t * lax.rsqrt(jnp.mean(h_t * h_t, axis=0, keepdims=True) + NORM_EPS) * anw_ref[h * A_V:(h + 1) * A_V, :]
            ao = ut_scr[OFF_AO + h * A_V:OFF_AO + (h + 1) * A_V, :]
            az = ut_scr[OFF_AZ + h * A_V:OFF_AZ + (h + 1) * A_V, :]
            yall_scr[h * A_V:(h + 1) * A_V, :] = (hn * jax.nn.sigmoid(ao) * _silu(az)).astype(yall_scr.dtype)
        gw = B_WIDTH // B_GROUPS
        for g in range(B_GROUPS):
            r = slice(g * gw, (g + 1) * gw)
            y_g = yt_scr[r, :] + dsk_ref[r, :] * xt_scr[r, :]
            gated = y_g * _silu(ut_scr[OFF_BZ + g * gw:OFF_BZ + (g + 1) * gw, :])
            inv = lax.rsqrt(jnp.mean(gated * gated, axis=0, keepdims=True) + NORM_EPS)
            yall_scr[A_WIDTH + g * gw:A_WIDTH + (g + 1) * gw, :] = (gated * inv * bnw_ref[r, :]).astype(yall_scr.dtype)
        yall_scr[A_WIDTH + B_WIDTH:, :] = \
            (yt_scr[B_WIDTH:, :] * _silu(ut_scr[OFF_CZ:OFF_CZ + C_WIDTH, :])).astype(yall_scr.dtype)
        hs_new = hs_scr[...] + _dot_tn(yall_scr[...], wout_ref[...])
        hs_scr[...] = hs_new

        @pl.when(l == pl.num_programs(0) - 1)
        def _():
            y_ref[...] = hs_new


def _decode(x, lw, cs_s, c_v, n_v, m_v, s_v, conv_v, k_v, v_v):
    depth, nb = c_v.shape[0], x.shape[0]
    assert nb == LANE, "samples sit on the 128 lanes next to lane-replicated parameters"
    grid = (depth, nb // SAMPLE_BLOCK)

    def per_layer(shape):
        nd = len(shape)
        return pl.BlockSpec((None,) + shape, lambda l, j, _nd=nd: (l,) + (0,) * _nd)

    def per_block(shape):
        nd = len(shape)
        return pl.BlockSpec((None, SAMPLE_BLOCK) + shape, lambda l, j, _nd=nd: (l, j) + (0,) * _nd)

    def weight(shape):
        nd = len(shape)
        return pl.BlockSpec((None,) + shape, lambda l, j, _nd=nd: (l,) + (0,) * _nd, pipeline_mode=pl.Buffered(1))

    fixed = lambda shape: pl.BlockSpec(shape, lambda l, j, _nd=len(shape): (0,) * _nd)
    state_specs = [
        per_block((A_HEADS, A_QK, A_V)), per_layer((A_HEADS, A_QK, nb)), per_layer((A_HEADS, nb)),
        per_block((B_HEADS, B_P, B_STATE)), per_layer((CONV_W - 1, nb, B_CONV_DIM)),
        per_block((C_KV, C_HD, WINDOW)), per_block((C_KV, C_HD, WINDOW)),
    ]
    in_specs = [
        fixed((nb, D_MODEL)),
        per_layer((1, D_MODEL)), per_layer((D_INT, D_MODEL)), weight((D_MIX, D_MODEL)),
        weight((PK_ROWS, LANE)), per_layer((CONV_W, B_CONV_DIM)), per_layer((1, B_CONV_DIM)),
        fixed((C_HD, LANE)),
    ] + state_specs
    states = (c_v, n_v, m_v, s_v, conv_v, k_v, v_v)
    out_shape = (jax.ShapeDtypeStruct((nb, D_MODEL), F32),) + tuple(jax.ShapeDtypeStruct(a.shape, F32) for a in states)
    scratch = [
        pltpu.VMEM((nb, D_MODEL), F32),
        pltpu.VMEM((D_INT, nb), F32),
        pltpu.VMEM((nb, A_WIDTH), F32),
        pltpu.VMEM((nb, B_CONV_DIM), F32),
        pltpu.VMEM((B_WIDTH, nb), F32),
        pltpu.VMEM((B_GROUPS, B_STATE, nb), MXU_DTYPE),
        pltpu.VMEM((C_WIDTH, nb), F32),
        pltpu.VMEM((C_KV * C_HD, nb), F32),
        pltpu.VMEM((N_REP, nb, LANE), F32),
        pltpu.VMEM((nb, A_WIDTH), F32),
        pltpu.VMEM((B_WIDTH + C_WIDTH, nb), F32),
        pltpu.VMEM((D_MIX, nb), MXU_DTYPE),
    ]
    return pl.pallas_call(
        _decode_kernel,
        grid=grid, in_specs=in_specs, out_specs=(fixed((nb, D_MODEL)),) + tuple(state_specs),
        out_shape=out_shape, scratch_shapes=scratch,
        compiler_params=pltpu.CompilerParams(dimension_semantics=("arbitrary", "arbitrary"),
                                             vmem_limit_bytes=DECODE_VMEM_LIMIT_BYTES),
        name="decode",
    )(x, lw["norm_w"], lw["w_t"], lw["w_out"], lw["packed"], lw["conv_w"], lw["conv_b"], cs_s, *states)


def _prep_weights(norm_w, w_in, a_igate_b, a_fgate_b, a_norm_w, b_conv_w, b_conv_b, b_dt_bias, b_A_log, b_D,
                  b_norm_w, c_qnorm_w, c_knorm_w, c_sinks, w_out):
    depth = w_in.shape[0]
    w_t = jnp.swapaxes(w_in, 1, 2).astype(MXU_DTYPE)
    gbias = jnp.concatenate([a_igate_b, a_fgate_b, b_dt_bias], axis=-1)
    alog = jnp.concatenate([jnp.zeros((depth, 2 * A_HEADS), b_A_log.dtype), b_A_log], axis=-1)

    rows = jnp.concatenate([
        gbias, alog, a_norm_w, jnp.repeat(b_D, B_P, axis=-1), b_norm_w,
        jnp.tile(c_qnorm_w, (1, LANE // C_HD)), jnp.tile(c_knorm_w, (1, LANE // C_HD)), c_sinks,
        b_conv_b, b_conv_w.reshape(depth, CONV_W * B_CONV_DIM)], axis=-1).astype(F32)
    assert rows.shape == (depth, PK_ROWS)
    return {
        "w_t": w_t, "w_out": w_out.astype(MXU_DTYPE),
        "norm_w": norm_w.astype(F32)[:, None, :],
        "packed": jnp.broadcast_to(rows[..., None], (depth, PK_ROWS, LANE)),
        "conv_w": b_conv_w.astype(F32), "conv_b": b_conv_b.astype(F32)[:, None, :],
        "sinks": c_sinks.astype(F32),
    }


def _rope_tables_t(pos, width=None):
    half = C_HD // 2
    inv = ROPE_THETA ** (-jnp.arange(half, dtype=F32) / half)
    ang = inv[:, None] * pos.astype(F32)[None, :]
    table = jnp.concatenate([jnp.cos(ang), jnp.sin(ang)], axis=0)
    if width is not None:
        table = jnp.broadcast_to(table, (C_HD, width))
    return table


def _kernel_impl(x_prompt, x_sample, state_mlstm_C, state_mlstm_n, state_mlstm_m, state_ssm, state_conv,
                 cache_k, cache_v, norm_w, w_in, a_igate_b, a_fgate_b, a_norm_w, b_conv_w, b_conv_b,
                 b_dt_bias, b_A_log, b_D, b_norm_w, c_qnorm_w, c_knorm_w, c_sinks, w_out, *, tb):
    depth = w_in.shape[0]
    t_len = x_prompt.shape[1]
    lw = _prep_weights(norm_w, w_in, a_igate_b, a_fgate_b, a_norm_w, b_conv_w, b_conv_b, b_dt_bias, b_A_log,
                       b_D, b_norm_w, c_qnorm_w, c_knorm_w, c_sinks, w_out)
    cs_p = _rope_tables_t(jnp.arange(t_len, dtype=jnp.int32))
    hp = x_prompt
    st_prompt = []
    for l in range(depth):
        res = _prompt_layer(hp, lw, l, cs_p, tb)
        hp = res[0]
        st_prompt.append(res[1])
    p_states = _unpad_prompt_states(jnp.stack(st_prompt))

    assert x_sample.shape[1] == 1
    cs_s = _rope_tables_t(PAST_LEN + jnp.arange(1, dtype=jnp.int32), LANE)
    outs = _decode(
        x_sample[:, 0, :], lw, cs_s,
        jnp.transpose(state_mlstm_C, (0, 1, 2, 4, 3)), jnp.transpose(state_mlstm_n, (0, 2, 3, 1)),
        jnp.transpose(state_mlstm_m, (0, 2, 1)), state_ssm, jnp.transpose(state_conv, (0, 2, 1, 3)),
        jnp.transpose(cache_k, (0, 1, 3, 4, 2)), jnp.transpose(cache_v, (0, 1, 3, 4, 2)))
    hs, c_o, n_o, m_o, s_o, conv_o, k_o, v_o = outs
    s_states = (jnp.transpose(c_o, (0, 1, 2, 4, 3)), jnp.transpose(n_o, (0, 3, 1, 2)), jnp.transpose(m_o, (0, 2, 1)),
                s_o, jnp.transpose(conv_o, (0, 2, 1, 3)),
                jnp.transpose(k_o, (0, 1, 4, 2, 3)), jnp.transpose(v_o, (0, 1, 4, 2, 3)))
    return (hp, hs[:, None, :], *p_states, *s_states)


def kernel(x_prompt, x_sample, state_mlstm_C, state_mlstm_n, state_mlstm_m, state_ssm, state_conv, cache_k, cache_v, norm_w, w_in, a_igate_b, a_fgate_b, a_norm_w, b_conv_w, b_conv_b, b_dt_bias, b_A_log, b_D, b_norm_w, c_qnorm_w, c_knorm_w, c_sinks, w_out):
    return _kernel_impl(x_prompt, x_sample, state_mlstm_C, state_mlstm_n, state_mlstm_m, state_ssm, state_conv,
                        cache_k, cache_v, norm_w, w_in, a_igate_b, a_fgate_b, a_norm_w, b_conv_w, b_conv_b,
                        b_dt_bias, b_A_log, b_D, b_norm_w, c_qnorm_w, c_knorm_w, c_sinks, w_out, tb=PROMPT_BLOCK)
```
